```python
import jax
import jax.numpy as jnp
from jax import lax
import numpy as np

D_MODEL = 2048
BATCH = 4
SEQ = 8192
DEPTH = 1
DEC_BATCH = 32
DEC_SEQ = 64
PAST_LEN = 1024

CHUNK = 64
RET_HEADS = 8
RET_DK = 128
RET_DV = 128
RET_QK_W = RET_HEADS * RET_DK
RET_V_W = RET_HEADS * RET_DV
ROPE_BASE = 10000.0
SWA_Q_HEADS = 16
SWA_KV_HEADS = 2
SWA_GROUP = SWA_Q_HEADS // SWA_KV_HEADS
SWA_HEAD_DIM = 64
SWA_Q_W = SWA_Q_HEADS * SWA_HEAD_DIM
SWA_KV_W = SWA_KV_HEADS * SWA_HEAD_DIM
WINDOW = 128
WINDOW_CHUNKS = WINDOW // CHUNK
N_BRANCHES = 2
IN_SPLITS = (RET_QK_W, RET_QK_W, RET_V_W, RET_V_W, SWA_Q_W, SWA_KV_W, SWA_KV_W, N_BRANCHES * D_MODEL)
IN_W = sum(IN_SPLITS)
N_GROUPS = 4
EXPERTS_PER_GROUP = 4
N_EXPERTS = N_GROUPS * EXPERTS_PER_GROUP
TOP_K = 2
D_EXPERT = 512
N_MOD = 6
EPS = 1e-6
NEG_INF = -1e30

kernel_name = "hybrid_retention_sinkswa_hmoe_stream_step"

F32 = jnp.float32


def rmsnorm(x, g):
    xf = x.astype(F32)
    y = xf * lax.rsqrt(jnp.mean(xf * xf, axis=-1, keepdims=True) + EPS)
    return (y * g.astype(F32)).astype(x.dtype)


def adaln(c, w, b):
    m = jax.nn.silu(c) @ w + b
    return jnp.split(m.reshape(c.shape[0], N_MOD, D_MODEL), N_MOD, axis=1)


def modulate(h, shift, scale):
    return h * (1 + scale) + shift


def rope(x, pos):
    half = x.shape[-1] // 2
    inv = ROPE_BASE ** (-jnp.arange(half, dtype=F32) / half)
    ang = pos.astype(F32)[:, None] * inv[None, :]
    cos = jnp.cos(ang)[None, :, None, :]
    sin = jnp.sin(ang)[None, :, None, :]
    x1 = x[..., :half].astype(F32)
    x2 = x[..., half:].astype(F32)
    return jnp.concatenate([x1 * cos - x2 * sin, x1 * sin + x2 * cos], axis=-1)


def project_in(h, w_in, pos):
    B, T, _ = h.shape
    points = tuple(int(p) for p in np.cumsum(IN_SPLITS)[:-1])
    rq, rk, rv, rg, sq, sk, sv, bg = jnp.split(h @ w_in, points, axis=-1)
    rq = rope(rq.reshape(B, T, RET_HEADS, RET_DK), pos)
    rk = rope(rk.reshape(B, T, RET_HEADS, RET_DK), pos) * (RET_DK ** -0.5)
    rv = rv.reshape(B, T, RET_HEADS, RET_DV).astype(F32)
    sq = sq.reshape(B, T, SWA_Q_HEADS, SWA_HEAD_DIM)
    sk = sk.reshape(B, T, SWA_KV_HEADS, SWA_HEAD_DIM)
    sv = sv.reshape(B, T, SWA_KV_HEADS, SWA_HEAD_DIM)
    return rq, rk, rv, rg, sq, sk, sv, bg


def ret_log_gamma():
    return jnp.log1p(-jnp.power(2.0, -5.0 - jnp.arange(RET_HEADS, dtype=F32)))


def retention_block(S, q, k, v):
    lg = ret_log_gamma()
    L = q.shape[1]
    idx = jnp.arange(L)
    diff = idx[:, None] - idx[None, :]
    causal = diff >= 0
    decay = jnp.where(causal[None], jnp.exp(jnp.where(causal, diff, 0)[None].astype(F32) * lg[:, None, None]), 0.0)
    s = jnp.einsum('blhd,bmhd->bhlm', q, k) * decay[None]
    intra = jnp.einsum('bhlm,bmhe->blhe', s, v)
    q_dec = jnp.exp((idx + 1).astype(F32)[:, None] * lg[None, :])
    cross = jnp.einsum('blhd,bhde->blhe', q, S) * q_dec[None, :, :, None]
    k_dec = jnp.exp((L - 1 - idx).astype(F32)[:, None] * lg[None, :])
    S_new = jnp.exp(L * lg)[None, :, None, None] * S + jnp.einsum('blhd,blhe->bhde', k * k_dec[None, :, :, None], v)
    return intra + cross, S_new


def retention_prompt(q, k, v):
    B, T = q.shape[:2]
    nc = T // CHUNK

    def to_chunks(a):
        return a.reshape(B, nc, CHUNK, *a.shape[2:]).swapaxes(0, 1)

    def step(S, xs):
        o, S2 = retention_block(S, *xs)
        return S2, o

    S0 = jnp.zeros((B, RET_HEADS, RET_DK, RET_DV), F32)
    S_fin, o = lax.scan(step, S0, (to_chunks(q), to_chunks(k), to_chunks(v)))
    return o.swapaxes(0, 1).reshape(B, T, RET_HEADS, RET_DV), S_fin


def retention_norm_gate(o, g, gn_g):
    B, T = o.shape[:2]
    mu = jnp.mean(o, axis=-1, keepdims=True)
    var = jnp.mean(jnp.square(o - mu), axis=-1, keepdims=True)
    on = ((o - mu) * lax.rsqrt(var + EPS)).reshape(B, T, RET_V_W) * gn_g.astype(F32)
    return on * jax.nn.silu(g.astype(F32))


def sink_attention(q, k, v, mask, sinks):
    s = jnp.einsum('bnqhgd,bnkhd->bnhgqk', q.astype(F32), k.astype(F32)) * (SWA_HEAD_DIM ** -0.5)
    s = jnp.where(mask[None, :, None, None, None, :], s, NEG_INF)
    sink = sinks.astype(F32).reshape(1, 1, SWA_KV_HEADS, SWA_GROUP, 1, 1)
    m = jnp.maximum(jnp.max(s, axis=-1, keepdims=True), sink)
    p = jnp.exp(s - m)
    w = p / (jnp.sum(p, axis=-1, keepdims=True) + jnp.exp(sink - m))
    return jnp.einsum('bnhgqk,bnkhd->bnqhgd', w, v.astype(F32))


def swa_prompt(q, k, v, sinks):
    B, T = q.shape[:2]
    nc = T // CHUNK
    qc = q.reshape(B, nc, CHUNK, SWA_KV_HEADS, SWA_GROUP, SWA_HEAD_DIM)

    def band(a):
        ac = a.reshape(B, nc, CHUNK, SWA_KV_HEADS, SWA_HEAD_DIM)
        ap = jnp.pad(ac, ((0, 0), (WINDOW_CHUNKS, 0), (0, 0), (0, 0), (0, 0)))
        return jnp.concatenate([ap[:, j:j + nc] for j in range(WINDOW_CHUNKS + 1)], axis=2)

    chunk_ok = (jnp.arange(nc)[:, None] - WINDOW_CHUNKS + jnp.arange(WINDOW_CHUNKS + 1)[None, :]) >= 0
    mask = jnp.repeat(chunk_ok, CHUNK, axis=1)
    o = sink_attention(qc, band(k), band(v), mask, sinks)
    return o.reshape(B, T, SWA_Q_W)


def swa_sample(q, k, v, cache_k, cache_v, sinks):
    B, L = q.shape[:2]
    k_all = jnp.concatenate([cache_k.astype(k.dtype), k], axis=1)
    v_all = jnp.concatenate([cache_v.astype(v.dtype), v], axis=1)
    qc = q.reshape(B, 1, L, SWA_KV_HEADS, SWA_GROUP, SWA_HEAD_DIM)
    mask = jnp.ones((1, k_all.shape[1]), dtype=bool)
    o = sink_attention(qc, k_all[:, None], v_all[:, None], mask, sinks)
    return o.reshape(B, L, SWA_Q_W), k_all[:, -WINDOW:], v_all[:, -WINDOW:]


def merge_branches(h, o_ret, rg, ret_gn_g, o_swa, bg, w_ret_branch, w_swa_branch, w_out):
    dt = h.dtype
    r = retention_norm_gate(o_ret, rg, ret_gn_g).astype(dt)
    g_r, g_s = jnp.split(jax.nn.sigmoid(bg.astype(F32)).astype(dt), N_BRANCHES, axis=-1)
    merged = g_r * (r @ w_ret_branch) + g_s * (o_swa.astype(dt) @ w_swa_branch)
    return merged @ w_out


def mixer_prompt(h, w_in, ret_gn_g, swa_sinks, w_ret_branch, w_swa_branch, w_out):
    pos = jnp.arange(h.shape[1])
    rq, rk, rv, rg, sq, sk, sv, bg = project_in(h, w_in, pos)
    o_ret, S = retention_prompt(rq, rk, rv)
    o_swa = swa_prompt(sq, sk, sv, swa_sinks)
    out = merge_branches(h, o_ret, rg, ret_gn_g, o_swa, bg, w_ret_branch, w_swa_branch, w_out)
    return out, (S, sk[:, -WINDOW:], sv[:, -WINDOW:])


def mixer_sample(h, ret_state, cache_k, cache_v, w_in, ret_gn_g, swa_sinks, w_ret_branch, w_swa_branch, w_out):
    pos = PAST_LEN + jnp.arange(h.shape[1])
    rq, rk, rv, rg, sq, sk, sv, bg = project_in(h, w_in, pos)
    o_ret, S = retention_block(ret_state.astype(F32), rq, rk, rv)
    o_swa, nk, nv = swa_sample(sq, sk, sv, cache_k, cache_v, swa_sinks)
    out = merge_branches(h, o_ret, rg, ret_gn_g, o_swa, bg, w_ret_branch, w_swa_branch, w_out)
    return out, (S, nk, nv)


def hier_moe(h, rg_w, rg_b, re_w, re_b, w1, w3, w2):
    B, T, D = h.shape
    t = h.reshape(B * T, D)
    g_logits = (t @ rg_w).astype(F32) + rg_b.astype(F32)
    g_prob = jax.nn.softmax(g_logits, axis=-1)
    g_idx = jnp.argmax(g_logits, axis=-1)
    g_w = jnp.take_along_axis(g_prob, g_idx[:, None], axis=-1)
    e_logits = ((t @ re_w).astype(F32) + re_b.astype(F32)).reshape(-1, N_GROUPS, EXPERTS_PER_GROUP)
    e_logits = jnp.take_along_axis(e_logits, g_idx[:, None, None], axis=1)[:, 0]
    top_v, top_i = lax.top_k(e_logits, TOP_K)
    top_w = jax.nn.softmax(top_v, axis=-1) * g_w
    eid = g_idx[:, None] * EXPERTS_PER_GROUP + top_i
    gates = jnp.sum(jax.nn.one_hot(eid, N_EXPERTS, dtype=F32) * top_w[..., None], axis=1).astype(t.dtype)
    y = jnp.zeros_like(t)
    for e in range(N_EXPERTS):
        a = jax.nn.silu(t @ w1[e]) * (t @ w3[e])
        y = y + gates[:, e:e + 1] * (a @ w2[e])
    return y.reshape(B, T, D)


def layer(x, c, mix_fn, norm1_g, norm2_g, ada_w, ada_b, rg_w, rg_b, re_w, re_b, w1, w3, w2):
    sh1, sc1, gt1, sh2, sc2, gt2 = adaln(c, ada_w, ada_b)
    mix, state = mix_fn(modulate(rmsnorm(x, norm1_g), sh1, sc1))
    x = x + gt1 * mix
    x = x + gt2 * hier_moe(modulate(rmsnorm(x, norm2_g), sh2, sc2), rg_w, rg_b, re_w, re_b, w1, w3, w2)
    return x, state


def setup_inputs(seed: int = 0) -> dict:
    key = jax.random.key(seed)
    ks = jax.random.split(key, 26)

    def nrm(k, shape, scale):
        return jax.random.normal(k, shape, F32) * scale

    swa_len = min(WINDOW, PAST_LEN)
    return {
        "x_prompt": nrm(ks[0], (BATCH, SEQ, D_MODEL), 1.0),
        "x_sample": nrm(ks[1], (DEC_BATCH, DEC_SEQ, D_MODEL), 1.0),
        "cache_ret_state": nrm(ks[2], (DEPTH, DEC_BATCH, RET_HEADS, RET_DK, RET_DV), 0.1),
        "cache_swa_k": nrm(ks[3], (DEPTH, DEC_BATCH, swa_len, SWA_KV_HEADS, SWA_HEAD_DIM), 1.0),
        "cache_swa_v": nrm(ks[4], (DEPTH, DEC_BATCH, swa_len, SWA_KV_HEADS, SWA_HEAD_DIM), 1.0),
        "c_prompt": nrm(ks[5], (BATCH, D_MODEL), 1.0),
        "c_sample": nrm(ks[6], (DEC_BATCH, D_MODEL), 1.0),
        "norm1_g": 1.0 + nrm(ks[7], (DEPTH, D_MODEL), 0.05),
        "norm2_g": 1.0 + nrm(ks[8], (DEPTH, D_MODEL), 0.05),
        "ada_w": nrm(ks[9], (DEPTH, D_MODEL, N_MOD * D_MODEL), 0.5 * D_MODEL ** -0.5),
        "ada_b": nrm(ks[10], (DEPTH, N_MOD * D_MODEL), 0.02),
        "w_in": nrm(ks[11], (DEPTH, D_MODEL, IN_W), D_MODEL ** -0.5),
        "ret_gn_g": 1.0 + nrm(ks[12], (DEPTH, RET_V_W), 0.05),
        "swa_sinks": nrm(ks[13], (DEPTH, SWA_Q_HEADS), 0.5),
        "w_ret_branch": nrm(ks[14], (DEPTH, RET_V_W, D_MODEL), RET_V_W ** -0.5),
        "w_swa_branch": nrm(ks[15], (DEPTH, SWA_Q_W, D_MODEL), SWA_Q_W ** -0.5),
        "w_out": nrm(ks[16], (DEPTH, D_MODEL, D_MODEL), D_MODEL ** -0.5),
        "router_group_w": nrm(ks[17], (DEPTH, D_MODEL, N_GROUPS), D_MODEL ** -0.5),
        "router_group_b": nrm(ks[18], (DEPTH, N_GROUPS), 0.01),
        "router_expert_w": nrm(ks[19], (DEPTH, D_MODEL, N_EXPERTS), D_MODEL ** -0.5),
        "router_expert_b": nrm(ks[20], (DEPTH, N_EXPERTS), 0.01),
        "expert_w1": nrm(ks[21], (DEPTH, N_EXPERTS, D_MODEL, D_EXPERT), D_MODEL ** -0.5),
        "expert_w3": nrm(ks[22], (DEPTH, N_EXPERTS, D_MODEL, D_EXPERT), D_MODEL ** -0.5),
        "expert_w2": nrm(ks[23], (DEPTH, N_EXPERTS, D_EXPERT, D_MODEL), D_EXPERT ** -0.5),
        "final_norm_g": 1.0 + nrm(ks[24], (D_MODEL,), 0.05),
    }


def reference(x_prompt, x_sample, cache_ret_state, cache_swa_k, cache_swa_v, c_prompt, c_sample,
              norm1_g, norm2_g, ada_w, ada_b, w_in, ret_gn_g, swa_sinks, w_ret_branch, w_swa_branch, w_out,
              router_group_w, router_group_b, router_expert_w, router_expert_b,
              expert_w1, expert_w3, expert_w2, final_norm_g):
    xp, xs = x_prompt, x_sample
    ret_p, kp, vp, ret_s, ksl, vsl = [], [], [], [], [], []
    for l in range(DEPTH):
        moe_args = (router_group_w[l], router_group_b[l], router_expert_w[l], router_expert_b[l],
                    expert_w1[l], expert_w3[l], expert_w2[l])
        mix_args = (w_in[l], ret_gn_g[l], swa_sinks[l], w_ret_branch[l], w_swa_branch[l], w_out[l])
        xp, (S_p, nk_p, nv_p) = layer(
            xp, c_prompt, lambda h: mixer_prompt(h, *mix_args),
            norm1_g[l], norm2_g[l], ada_w[l], ada_b[l], *moe_args)
        xs, (S_s, nk_s, nv_s) = layer(
            xs, c_sample, lambda h: mixer_sample(h, cache_ret_state[l], cache_swa_k[l], cache_swa_v[l], *mix_args),
            norm1_g[l], norm2_g[l], ada_w[l], ada_b[l], *moe_args)
        ret_p.append(S_p)
        kp.append(nk_p)
        vp.append(nv_p)
        ret_s.append(S_s)
        ksl.append(nk_s)
        vsl.append(nv_s)
    y_prompt = rmsnorm(xp, final_norm_g)
    y_sample = rmsnorm(xs, final_norm_g)
    return (y_prompt, y_sample, jnp.stack(ret_p), jnp.stack(kp), jnp.stack(vp),
            jnp.stack(ret_s), jnp.stack(ksl), jnp.stack(vsl))
```

```python
import functools
import math

import numpy as np
import jax
import jax.numpy as jnp
from jax import lax
from jax.experimental import pallas as pl
from jax.experimental.pallas import tpu as pltpu

F32 = jnp.float32
BF16 = jnp.bfloat16

D_MODEL = 2048
CHUNK = 64
RET_HEADS = 8
RET_DK = 128
RET_DV = 128
RET_W = RET_HEADS * RET_DK
ROPE_BASE = 10000.0
SWA_Q_HEADS = 16
SWA_KV_HEADS = 2
SWA_GROUP = SWA_Q_HEADS // SWA_KV_HEADS
SWA_HEAD_DIM = 64
SWA_Q_W = SWA_Q_HEADS * SWA_HEAD_DIM
SWA_KV_W = SWA_KV_HEADS * SWA_HEAD_DIM
WINDOW = 128
N_GROUPS = 4
EXPERTS_PER_GROUP = 4
N_EXPERTS = 16
D_EXPERT = 512
N_MOD = 6
EPS = 1e-6
NEG_INF = -1e30
N_PAIRS = 6
N_CLASSES = N_GROUPS * N_PAIRS

LANES = 128
VMEM_LIMIT = 56 * 1024 * 1024

RET_LOG_GAMMA = tuple(math.log1p(-(2.0 ** (-5.0 - h))) for h in range(RET_HEADS))


def _cparams(sem):
    return pltpu.CompilerParams(dimension_semantics=sem, vmem_limit_bytes=VMEM_LIMIT)


def _resident(shape):
    nd = len(shape)
    return pl.BlockSpec(shape, lambda *_: (0,) * nd, pipeline_mode=pl.Buffered(1))


def _ada_kernel(c_ref, w_ref, b_ref, o_ref):
    c = c_ref[...]
    a = c * jax.nn.sigmoid(c)
    o_ref[...] = jnp.dot(a, w_ref[...], preferred_element_type=F32,
                         precision=lax.Precision.HIGHEST) + b_ref[...]


def _ada(c_all, ada_w, ada_b):
    nb = c_all.shape[0]
    n_out = ada_w.shape[1]
    tn = 1024
    return pl.pallas_call(
        _ada_kernel,
        grid=(n_out // tn,),
        in_specs=[pl.BlockSpec((nb, D_MODEL), lambda j: (0, 0)),
                  pl.BlockSpec((D_MODEL, tn), lambda j: (0, j)),
                  pl.BlockSpec((1, tn), lambda j: (0, j))],
        out_specs=pl.BlockSpec((nb, tn), lambda j: (0, j)),
        out_shape=jax.ShapeDtypeStruct((nb, n_out), F32),
        compiler_params=_cparams(("arbitrary",)),
        name="ada",
    )(c_all, ada_w, ada_b.reshape(1, n_out))


def _modulated_norm(x, g, shift, scale, n_seq):
    tm = x.shape[0]
    ms = jnp.mean(x * x, axis=-1, keepdims=True)
    y = x * lax.rsqrt(ms + EPS) * g
    y3 = y.reshape(n_seq, tm // n_seq, D_MODEL)
    h = y3 * (1.0 + scale)[:, None, :] + shift[:, None, :]
    return h.reshape(tm, D_MODEL)


def _inproj_kernel(x_ref, mod_ref, g1_ref, cos_ref, sin_ref, wret_ref, wsq_ref, wkv_ref, wbg_ref,
                   ret_ref, sq_ref, kv_ref, gate_ref, h_scr, *, n_seq):
    j = pl.program_id(1)

    @pl.when(j == 0)
    def _():
        h = _modulated_norm(x_ref[...], g1_ref[...], mod_ref[:, 0, :], mod_ref[:, 1, :], n_seq)
        h_scr[...] = h.astype(BF16)

    @pl.when(j < 2)
    def _():
        acc = jnp.dot(h_scr[...], wret_ref[...], preferred_element_type=F32)
        cos = cos_ref[...]
        sin = sin_ref[...]
        scale = jnp.where(j == 1, RET_DK ** -0.5, 1.0).astype(F32)
        for hh in range(RET_HEADS):
            a = acc[:, hh * RET_DK:(hh + 1) * RET_DK]
            r = a * cos + pltpu.roll(a, RET_DK // 2, 1) * sin
            ret_ref[:, hh * RET_DK:(hh + 1) * RET_DK] = (r * scale).astype(BF16)

    @pl.when((j >= 2) & (j < 4))
    def _():
        ret_ref[...] = jnp.dot(h_scr[...], wret_ref[...], preferred_element_type=F32).astype(BF16)

    @pl.when(j == 4)
    def _():
        h = h_scr[...]
        sq_ref[...] = jnp.dot(h, wsq_ref[...], preferred_element_type=F32).astype(BF16)
        kv_ref[...] = jnp.dot(h, wkv_ref[...], preferred_element_type=F32)

    @pl.when(j >= 5)
    def _():
        acc = jnp.dot(h_scr[...], wbg_ref[...], preferred_element_type=F32)
        gate_ref[...] = jax.nn.sigmoid(acc).astype(BF16)


def _inproj(x2d, mod3, g1, cos_t, sin_t, wret, wsq, wkv, wbg, *, seq_len, tm):
    R = x2d.shape[0]
    if seq_len >= tm:
        n_seq, tps = 1, seq_len // tm
        mod_map = lambda i, j: (i // tps, 0, 0)
        tab_map = lambda i, j: (i % tps, 0)
    else:
        n_seq = tm // seq_len
        mod_map = lambda i, j: (i, 0, 0)
        tab_map = lambda i, j: (0, 0)
    tn = 1024
    n_ret = wret.shape[1] // tn
    n_bg = wbg.shape[1] // tn
    nj = n_ret + 1 + n_bg
    ret_map = lambda i, j: (i, jnp.minimum(j, n_ret - 1))
    bg_map = lambda i, j: (i, jnp.clip(j - (n_ret + 1), 0, n_bg - 1))
    return pl.pallas_call(
        functools.partial(_inproj_kernel, n_seq=n_seq),
        grid=(R // tm, nj),
        in_specs=[
            pl.BlockSpec((tm, D_MODEL), lambda i, j: (i, 0)),
            pl.BlockSpec((n_seq, N_MOD, D_MODEL), mod_map),
            _resident((1, D_MODEL)),
            pl.BlockSpec((tm, LANES), tab_map),
            pl.BlockSpec((tm, LANES), tab_map),
            pl.BlockSpec((D_MODEL, tn), lambda i, j: (0, jnp.minimum(j, n_ret - 1))),
            _resident((D_MODEL, SWA_Q_W)),
            _resident((D_MODEL, 2 * SWA_KV_W)),
            pl.BlockSpec((D_MODEL, tn), lambda i, j: (0, jnp.clip(j - (n_ret + 1), 0, n_bg - 1))),
        ],
        out_specs=[
            pl.BlockSpec((tm, tn), ret_map),
            pl.BlockSpec((tm, SWA_Q_W), lambda i, j: (i, 0)),
            pl.BlockSpec((tm, 2 * SWA_KV_W), lambda i, j: (i, 0)),
            pl.BlockSpec((tm, tn), bg_map),
        ],
        out_shape=[
            jax.ShapeDtypeStruct((R, 4 * RET_W), BF16),
            jax.ShapeDtypeStruct((R, SWA_Q_W), BF16),
            jax.ShapeDtypeStruct((R, 2 * SWA_KV_W), F32),
            jax.ShapeDtypeStruct((R, 2 * D_MODEL), BF16),
        ],
        scratch_shapes=[pltpu.VMEM((tm, D_MODEL), BF16)],
        compiler_params=_cparams(("parallel", "arbitrary")),
        name="inproj",
    )(x2d, mod3, g1, cos_t, sin_t, wret, wsq, wkv, wbg)


def _ret_kernel(blk_ref, s0_ref, gn_ref, r_ref, sout_ref, s_scr, dec_scr, qd_scr, kd_scr, *, lc):
    c = pl.program_id(1)

    @pl.when(c == 0)
    def _():
        s_scr[...] = s0_ref[0]
        li = lax.broadcasted_iota(jnp.int32, (lc, lc), 0)
        mi = lax.broadcasted_iota(jnp.int32, (lc, lc), 1)
        diff = li - mi
        l1 = lax.broadcasted_iota(jnp.int32, (lc, RET_DK), 0)
        for h in range(RET_HEADS):
            lg = RET_LOG_GAMMA[h]
            dec_scr[h] = jnp.where(diff >= 0, jnp.exp(jnp.maximum(diff, 0).astype(F32) * lg), 0.0)
            qd_scr[h] = jnp.exp((l1 + 1).astype(F32) * lg)
            kd_scr[h] = jnp.exp((lc - 1 - l1).astype(F32) * lg)

    nt = (((1,), (1,)), ((), ()))
    tn = (((0,), (0,)), ((), ()))
    for h in range(RET_HEADS):
        lo, hi = h * RET_DK, (h + 1) * RET_DK
        q = blk_ref[:, lo:hi]
        k = blk_ref[:, RET_W + lo:RET_W + hi]
        v = blk_ref[:, 2 * RET_W + lo:2 * RET_W + hi]
        g = blk_ref[:, 3 * RET_W + lo:3 * RET_W + hi].astype(F32)
        s_prev = s_scr[h]
        s = lax.dot_general(q, k, nt, preferred_element_type=F32) * dec_scr[h]
        intra = jnp.dot(s.astype(BF16), v, preferred_element_type=F32)
        cross = jnp.dot(q, s_prev.astype(BF16), preferred_element_type=F32) * qd_scr[h]
        o = intra + cross
        kd = (k.astype(F32) * kd_scr[h]).astype(BF16)
        s_scr[h] = math.exp(lc * RET_LOG_GAMMA[h]) * s_prev + lax.dot_general(
            kd, v, tn, preferred_element_type=F32)
        mu = jnp.mean(o, axis=-1, keepdims=True)
        d = o - mu
        var = jnp.mean(d * d, axis=-1, keepdims=True)
        on = d * lax.rsqrt(var + EPS) * gn_ref[:, lo:hi]
        r_ref[:, lo:hi] = (on * (g * jax.nn.sigmoid(g))).astype(BF16)

    @pl.when(c == pl.num_programs(1) - 1)
    def _():
        sout_ref[0] = s_scr[...]


def _retention(ret_all, s0, gn_g, *, n_seq, seq_len, lc):
    R = ret_all.shape[0]
    nc = seq_len // lc
    st_spec = pl.BlockSpec((1, RET_HEADS, RET_DK, RET_DV), lambda b, c: (b, 0, 0, 0))
    return pl.pallas_call(
        functools.partial(_ret_kernel, lc=lc),
        grid=(n_seq, nc),
        in_specs=[pl.BlockSpec((lc, 4 * RET_W), lambda b, c: (b * nc + c, 0)),
                  st_spec,
                  _resident((1, RET_W))],
        out_specs=[pl.BlockSpec((lc, RET_W), lambda b, c: (b * nc + c, 0)), st_spec],
        out_shape=[jax.ShapeDtypeStruct((R, RET_W), BF16),
                   jax.ShapeDtypeStruct((n_seq, RET_HEADS, RET_DK, RET_DV), F32)],
        scratch_shapes=[pltpu.VMEM((RET_HEADS, RET_DK, RET_DV), F32),
                        pltpu.VMEM((RET_HEADS, lc, lc), F32),
                        pltpu.VMEM((RET_HEADS, lc, RET_DK), F32),
                        pltpu.VMEM((RET_HEADS, lc, RET_DK), F32)],
        compiler_params=_cparams(("parallel", "arbitrary")),
        name="retention",
    )(ret_all, s0, gn_g)


KEYS = WINDOW + CHUNK
KPAD = 256


def _swa_kernel(sink_ref, q_ref, k2_ref, k1_ref, k0_ref, v2_ref, v1_ref, v0_ref, o_ref, *, masked):
    c = pl.program_id(1)
    lane = lax.broadcasted_iota(jnp.int32, (KEYS, LANES), 1)
    zpad = jnp.zeros((KPAD - KEYS, LANES), BF16)

    def block_diag(win, h):
        rolled = pltpu.roll(win, SWA_HEAD_DIM, 1)
        lo_src, hi_src = (win, rolled) if h == 0 else (rolled, win)
        a = jnp.where(lane < SWA_HEAD_DIM, lo_src, 0.0).astype(BF16)
        b = jnp.where(lane >= SWA_HEAD_DIM, hi_src, 0.0).astype(BF16)
        return jnp.concatenate([a, zpad, b, zpad], axis=0)

    kwin = jnp.concatenate([k2_ref[...], k1_ref[...], k0_ref[...]], axis=0)
    vwin = jnp.concatenate([v2_ref[...], v1_ref[...], v0_ref[...]], axis=0)

    col = lax.broadcasted_iota(jnp.int32, (1, KPAD), 1)
    if masked:
        first_ok = jnp.where(c >= 2, 0, jnp.where(c == 1, CHUNK, 2 * CHUNK))
        ok = (col >= first_ok) & (col < KEYS)
    else:
        ok = col < KEYS
    n_pairs = SWA_GROUP // 2
    rows = n_pairs * CHUNK
    row = lax.broadcasted_iota(jnp.int32, (rows, 1), 0)
    out_lane = lax.broadcasted_iota(jnp.int32, (rows, LANES), 1)
    nt = (((1,), (1,)), ((), ()))
    for h in range(SWA_KV_HEADS):
        kk = block_diag(kwin, h)
        vv = block_diag(vwin, h)
        base = h * SWA_GROUP * SWA_HEAD_DIM
        q4 = jnp.concatenate([q_ref[:, base + p * LANES: base + (p + 1) * LANES] for p in range(n_pairs)],
                             axis=0)
        s = lax.dot_general(q4, kk, nt, preferred_element_type=F32) * (SWA_HEAD_DIM ** -0.5)
        ps, invs = [], []
        for half in range(2):
            sh = jnp.where(ok, s[:, half * KPAD:(half + 1) * KPAD], NEG_INF)
            sink = jnp.zeros((rows, 1), F32)
            for p in range(n_pairs):
                sink = jnp.where(row // CHUNK == p, sink_ref[h * SWA_GROUP + 2 * p + half], sink)
            m = jnp.maximum(jnp.max(sh, axis=-1, keepdims=True), sink)
            p_half = jnp.exp(sh - m)
            den = jnp.sum(p_half, axis=-1, keepdims=True) + jnp.exp(sink - m)
            ps.append(p_half.astype(BF16))
            invs.append(1.0 / den)
        pv = jnp.dot(jnp.concatenate(ps, axis=1), vv, preferred_element_type=F32)
        o = pv * jnp.where(out_lane < SWA_HEAD_DIM, invs[0], invs[1])
        for p in range(n_pairs):
            o_ref[:, base + p * LANES: base + (p + 1) * LANES] = o[p * CHUNK:(p + 1) * CHUNK].astype(BF16)


def _swa(sinks, sq, k_arrs, v_arrs, k_maps, v_maps, *, n_seq, nc, masked):
    R = sq.shape[0]
    kv_specs = [pl.BlockSpec((CHUNK, SWA_KV_W), m) for m in (*k_maps, *v_maps)]
    return pl.pallas_call(
        functools.partial(_swa_kernel, masked=masked),
        grid=(n_seq, nc),
        in_specs=[pl.BlockSpec(memory_space=pltpu.SMEM),
                  pl.BlockSpec((CHUNK, SWA_Q_W), lambda b, c: (b * nc + c, 0)),
                  *kv_specs],
        out_specs=pl.BlockSpec((CHUNK, SWA_Q_W), lambda b, c: (b * nc + c, 0)),
        out_shape=jax.ShapeDtypeStruct((R, SWA_Q_W), BF16),
        compiler_params=_cparams(("parallel", "arbitrary")),
        name="swa",
    )(sinks, sq, *k_arrs, *v_arrs)


def _route(logits):
    tm = logits.shape[0]
    lane = lax.broadcasted_iota(jnp.int32, (tm, LANES), 1)
    is_g = lane < N_GROUPS
    gl = jnp.where(is_g, logits, NEG_INF)
    gmax = jnp.max(gl, axis=-1, keepdims=True)
    gidx = jnp.min(jnp.where(gl == gmax, lane, LANES), axis=-1, keepdims=True)
    gsum = jnp.sum(jnp.where(is_g, jnp.exp(gl - gmax), 0.0), axis=-1, keepdims=True)
    g_w = 1.0 / gsum
    base = N_GROUPS + EXPERTS_PER_GROUP * gidx
    el = jnp.where((lane >= base) & (lane < base + EXPERTS_PER_GROUP), logits, NEG_INF)
    v1 = jnp.max(el, axis=-1, keepdims=True)
    i1 = jnp.min(jnp.where(el == v1, lane, LANES), axis=-1, keepdims=True)
    el2 = jnp.where(lane == i1, NEG_INF, el)
    v2 = jnp.max(el2, axis=-1, keepdims=True)
    i2 = jnp.min(jnp.where(el2 == v2, lane, LANES), axis=-1, keepdims=True)
    e2 = jnp.exp(v2 - v1)
    den = 1.0 + e2
    w1 = g_w / den
    w2 = g_w * e2 / den
    l1 = i1 - base
    l2 = i2 - base
    first_lo = l1 < l2
    la = jnp.where(first_lo, l1, l2)
    lb = jnp.where(first_lo, l2, l1)
    wa = jnp.where(first_lo, w1, w2)
    wb = jnp.where(first_lo, w2, w1)
    pair = jnp.where(la == 0, 0, jnp.where(la == 1, 3, 5)) + (lb - la - 1)
    cls = (gidx * N_PAIRS + pair).astype(F32)
    return jnp.where(lane == 0, wa, jnp.where(lane == 1, wb, jnp.where(lane == 2, cls, 0.0)))


def _merge_kernel(x_ref, r_ref, o_ref, gate_ref, mod_ref, g2_ref, wrb_ref, wsb_ref, wout_ref, wr_ref, br_ref,
                  x1_ref, text_ref, meta_ref, *, n_seq):
    tm = x_ref.shape[0]
    g_r = gate_ref[:, :D_MODEL].astype(F32)
    g_s = gate_ref[:, D_MODEL:].astype(F32)
    merged = (g_r * jnp.dot(r_ref[...], wrb_ref[...], preferred_element_type=F32)
              + g_s * jnp.dot(o_ref[...], wsb_ref[...], preferred_element_type=F32))
    mix = jnp.dot(merged.astype(BF16), wout_ref[...], preferred_element_type=F32)
    gt1 = mod_ref[:, 2, :]
    x1 = (x_ref[...].reshape(n_seq, tm // n_seq, D_MODEL) + gt1[:, None, :]
          * mix.reshape(n_seq, tm // n_seq, D_MODEL)).reshape(tm, D_MODEL)
    x1_ref[...] = x1
    t = _modulated_norm(x1, g2_ref[...], mod_ref[:, 3, :], mod_ref[:, 4, :], n_seq)
    logits = jnp.dot(t.astype(BF16), wr_ref[...], preferred_element_type=F32) + br_ref[...]
    meta = _route(logits)
    text_ref[:, :D_MODEL] = t
    text_ref[:, D_MODEL:] = meta
    meta_ref[...] = meta


def _merge(x2d, r, o_swa, gates, mod3, g2, wrb, wsb, wout, wr, br, *, seq_len, tm):
    R = x2d.shape[0]
    if seq_len >= tm:
        n_seq, tps = 1, seq_len // tm
        mod_map = lambda i: (i // tps, 0, 0)
    else:
        n_seq = tm // seq_len
        mod_map = lambda i: (i, 0, 0)
    row = lambda w: pl.BlockSpec((tm, w), lambda i: (i, 0))
    return pl.pallas_call(
        functools.partial(_merge_kernel, n_seq=n_seq),
        grid=(R // tm,),
        in_specs=[row(D_MODEL), row(RET_W), row(SWA_Q_W), row(2 * D_MODEL),
                  pl.BlockSpec((n_seq, N_MOD, D_MODEL), mod_map),
                  _resident((1, D_MODEL)),
                  _resident((RET_W, D_MODEL)), _resident((SWA_Q_W, D_MODEL)), _resident((D_MODEL, D_MODEL)),
                  _resident((D_MODEL, LANES)), _resident((1, LANES))],
        out_specs=[row(D_MODEL), row(D_MODEL + LANES), row(LANES)],
        out_shape=[jax.ShapeDtypeStruct((R, D_MODEL), F32),
                   jax.ShapeDtypeStruct((R, D_MODEL + LANES), F32),
                   jax.ShapeDtypeStruct((R, LANES), F32)],
        compiler_params=_cparams(("parallel",)),
        name="merge",
    )(x2d, r, o_swa, gates, mod3, g2, wrb, wsb, wout, wr, br)


def _moe_kernel(ea_ref, eb_ref, valid_ref, xs_ref, w1a_ref, w3a_ref, w2a_ref, w1b_ref, w3b_ref, w2b_ref, y_ref):
    i = pl.program_id(0)

    @pl.when(valid_ref[i] > 0)
    def _():
        x = xs_ref[:, :D_MODEL].astype(BF16)
        wa = xs_ref[:, D_MODEL:D_MODEL + 1]
        wb = xs_ref[:, D_MODEL + 1:D_MODEL + 2]

        def hidden(w1_ref, w3_ref, gate):
            a = jnp.dot(x, w1_ref[0], preferred_element_type=F32)
            b = jnp.dot(x, w3_ref[0], preferred_element_type=F32)
            return (a * jax.nn.sigmoid(a) * b * gate).astype(BF16)

        ha = hidden(w1a_ref, w3a_ref, wa)
        hb = hidden(w1b_ref, w3b_ref, wb)
        y_ref[...] = (jnp.dot(ha, w2a_ref[0], preferred_element_type=F32)
                      + jnp.dot(hb, w2b_ref[0], preferred_element_type=F32))

    @pl.when(valid_ref[i] == 0)
    def _():
        y_ref[...] = jnp.zeros_like(y_ref)


def _moe(tile_ea, tile_eb, tile_valid, xs, w1, w3, w2, *, tm):
    NP = xs.shape[0]
    n_tiles = NP // tm
    wa_map = lambda i, ea, eb, va: (ea[i], 0, 0)
    wb_map = lambda i, ea, eb, va: (eb[i], 0, 0)
    up = (1, D_MODEL, D_EXPERT)
    down = (1, D_EXPERT, D_MODEL)
    grid_spec = pltpu.PrefetchScalarGridSpec(
        num_scalar_prefetch=3,
        grid=(n_tiles,),
        in_specs=[pl.BlockSpec((tm, D_MODEL + LANES), lambda i, ea, eb, va: (i, 0)),
                  pl.BlockSpec(up, wa_map), pl.BlockSpec(up, wa_map), pl.BlockSpec(down, wa_map),
                  pl.BlockSpec(up, wb_map), pl.BlockSpec(up, wb_map), pl.BlockSpec(down, wb_map)],
        out_specs=pl.BlockSpec((tm, D_MODEL), lambda i, ea, eb, va: (i, 0)),
    )
    return pl.pallas_call(
        _moe_kernel,
        grid_spec=grid_spec,
        out_shape=jax.ShapeDtypeStruct((NP, D_MODEL), F32),
        compiler_params=_cparams(("arbitrary",)),
        name="moe",
    )(tile_ea, tile_eb, tile_valid, xs, w1, w3, w2, w1, w3, w2)


def _final_kernel(x1_ref, y_ref, mod_ref, g_ref, o_ref, *, n_seq):
    tm = x1_ref.shape[0]
    gt2 = mod_ref[:, 5, :]
    x2 = (x1_ref[...].reshape(n_seq, tm // n_seq, D_MODEL)
          + gt2[:, None, :] * y_ref[...].reshape(n_seq, tm // n_seq, D_MODEL)).reshape(tm, D_MODEL)
    ms = jnp.mean(x2 * x2, axis=-1, keepdims=True)
    o_ref[...] = x2 * lax.rsqrt(ms + EPS) * g_ref[...]


def _final(x1, y, mod3, gf, *, seq_len, tm):
    R = x1.shape[0]
    if seq_len >= tm:
        n_seq, tps = 1, seq_len // tm
        mod_map = lambda i: (i // tps, 0, 0)
    else:
        n_seq = tm // seq_len
        mod_map = lambda i: (i, 0, 0)
    row = pl.BlockSpec((tm, D_MODEL), lambda i: (i, 0))
    return pl.pallas_call(
        functools.partial(_final_kernel, n_seq=n_seq),
        grid=(R // tm,),
        in_specs=[row, row, pl.BlockSpec((n_seq, N_MOD, D_MODEL), mod_map), _resident((1, D_MODEL))],
        out_specs=row,
        out_shape=jax.ShapeDtypeStruct((R, D_MODEL), F32),
        compiler_params=_cparams(("parallel",)),
        name="final",
    )(x1, y, mod3, gf)


def _rope_tables(pos):
    half = RET_DK // 2
    inv = ROPE_BASE ** (-jnp.arange(half, dtype=F32) / half)
    ang = pos.astype(F32)[:, None] * inv[None, :]
    cos = jnp.cos(ang)
    sin = jnp.sin(ang)
    return jnp.concatenate([cos, cos], axis=-1), jnp.concatenate([-sin, sin], axis=-1)


def _pair_tables():
    la, lb = [], []
    for a in range(EXPERTS_PER_GROUP):
        for b in range(a + 1, EXPERTS_PER_GROUP):
            la.append(a)
            lb.append(b)
    cls = np.arange(N_CLASSES)
    ea = (cls // N_PAIRS) * EXPERTS_PER_GROUP + np.asarray(la)[cls % N_PAIRS]
    eb = (cls // N_PAIRS) * EXPERTS_PER_GROUP + np.asarray(lb)[cls % N_PAIRS]
    return jnp.asarray(ea, jnp.int32), jnp.asarray(eb, jnp.int32)


def _plan(cls, tm):
    n = cls.shape[0]
    n_tiles = n // tm + N_CLASSES
    counts = jnp.sum(cls[:, None] == jnp.arange(N_CLASSES)[None, :], axis=0).astype(jnp.int32)
    padded = ((counts + tm - 1) // tm) * tm
    ends = jnp.cumsum(padded)
    offs = ends - padded
    starts = jnp.cumsum(counts) - counts
    order = jnp.argsort(cls, stable=True).astype(jnp.int32)
    cs = cls[order]
    pos_sorted = offs[cs] + jnp.arange(n, dtype=jnp.int32) - starts[cs]
    pos = jnp.zeros((n,), jnp.int32).at[order].set(pos_sorted)
    tok_of_pos = jnp.zeros((n_tiles * tm,), jnp.int32).at[pos].set(jnp.arange(n, dtype=jnp.int32))
    tile_start = jnp.arange(n_tiles, dtype=jnp.int32) * tm
    tile_cls = jnp.minimum(jnp.searchsorted(ends, tile_start, side="right"), N_CLASSES - 1).astype(jnp.int32)
    tile_valid = (tile_start < ends[-1]).astype(jnp.int32)
    ea_t, eb_t = _pair_tables()
    return pos, tok_of_pos, ea_t[tile_cls], eb_t[tile_cls], tile_valid


def kernel(x_prompt, x_sample, cache_ret_state, cache_swa_k, cache_swa_v, c_prompt, c_sample,
           norm1_g, norm2_g, ada_w, ada_b, w_in, ret_gn_g, swa_sinks, w_ret_branch, w_swa_branch, w_out,
           router_group_w, router_group_b, router_expert_w, router_expert_b,
           expert_w1, expert_w3, expert_w2, final_norm_g):
    depth = w_in.shape[0]
    assert depth == 1
    bp, tp, _ = x_prompt.shape
    bs, ts, _ = x_sample.shape
    past = WINDOW
    assert cache_swa_k.shape[2] == past and ts == CHUNK and tp % 512 == 0
    past_len = 1024
    tm = 512

    l = 0
    wi = w_in[l].astype(BF16)
    wret = wi[:, :4 * RET_W]
    wsq = wi[:, 4 * RET_W:4 * RET_W + SWA_Q_W]
    wkv = wi[:, 4 * RET_W + SWA_Q_W:4 * RET_W + SWA_Q_W + 2 * SWA_KV_W]
    wbg = wi[:, 4 * RET_W + SWA_Q_W + 2 * SWA_KV_W:]
    wrb = w_ret_branch[l].astype(BF16)
    wsb = w_swa_branch[l].astype(BF16)
    wo = w_out[l].astype(BF16)
    n_r = N_GROUPS + N_EXPERTS
    wr = jnp.zeros((D_MODEL, LANES), F32).at[:, :N_GROUPS].set(router_group_w[l]).at[:, N_GROUPS:n_r].set(
        router_expert_w[l]).astype(BF16)
    br = jnp.zeros((1, LANES), F32).at[0, :N_GROUPS].set(router_group_b[l]).at[0, N_GROUPS:n_r].set(
        router_expert_b[l])
    w1 = expert_w1[l].astype(BF16)
    w3 = expert_w3[l].astype(BF16)
    w2 = expert_w2[l].astype(BF16)
    g1 = norm1_g[l].reshape(1, D_MODEL)
    g2 = norm2_g[l].reshape(1, D_MODEL)
    gn = ret_gn_g[l].reshape(1, RET_W)
    gf = final_norm_g.reshape(1, D_MODEL)
    sinks = swa_sinks[l]

    c_all = jnp.concatenate([c_prompt, c_sample], axis=0)
    mod = _ada(c_all, ada_w[l], ada_b[l]).reshape(bp + bs, N_MOD, D_MODEL)
    mod_p, mod_s = mod[:bp], mod[bp:]

    cos_p, sin_p = _rope_tables(jnp.arange(tp))
    cos_s, sin_s = _rope_tables(past_len + jnp.arange(ts))
    rep = tm // ts
    cos_s, sin_s = jnp.tile(cos_s, (rep, 1)), jnp.tile(sin_s, (rep, 1))

    xp = x_prompt.reshape(bp * tp, D_MODEL)
    xs = x_sample.reshape(bs * ts, D_MODEL)

    ret_p, sq_p, kv_p, gate_p = _inproj(xp, mod_p, g1, cos_p, sin_p, wret, wsq, wkv, wbg, seq_len=tp, tm=tm)
    ret_s, sq_s, kv_s, gate_s = _inproj(xs, mod_s, g1, cos_s, sin_s, wret, wsq, wkv, wbg, seq_len=ts, tm=tm)

    lc_p = 128
    s0_p = jnp.zeros((bp, RET_HEADS, RET_DK, RET_DV), F32)
    r_p, state_p = _retention(ret_p, s0_p, gn, n_seq=bp, seq_len=tp, lc=lc_p)
    r_s, state_s = _retention(ret_s, cache_ret_state[l].astype(F32), gn, n_seq=bs, seq_len=ts, lc=ts)

    nc_p = tp // CHUNK
    kmap = lambda back, colblk: (lambda b, c: (b * nc_p + jnp.maximum(c - back, 0), colblk))
    o_p = _swa(sinks, sq_p, [kv_p] * 3, [kv_p] * 3,
               [kmap(2, 0), kmap(1, 0), kmap(0, 0)], [kmap(2, 1), kmap(1, 1), kmap(0, 1)],
               n_seq=bp, nc=nc_p, masked=True)
    ck = cache_swa_k[l].reshape(bs * past, SWA_KV_W)
    cv = cache_swa_v[l].reshape(bs * past, SWA_KV_W)
    cmap = lambda blk: (lambda b, c: (2 * b + blk, 0))
    o_s = _swa(sinks, sq_s, [ck, ck, kv_s], [cv, cv, kv_s],
               [cmap(0), cmap(1), lambda b, c: (b, 0)], [cmap(0), cmap(1), lambda b, c: (b, 1)],
               n_seq=bs, nc=1, masked=False)

    tm_m = 256
    x1_p, text_p, meta_p = _merge(xp, r_p, o_p, gate_p, mod_p, g2, wrb, wsb, wo, wr, br, seq_len=tp, tm=tm_m)
    x1_s, text_s, meta_s = _merge(xs, r_s, o_s, gate_s, mod_s, g2, wrb, wsb, wo, wr, br, seq_len=ts, tm=tm_m)

    tm_e = 256
    cls = jnp.concatenate([meta_p[:, 2], meta_s[:, 2]], axis=0).astype(jnp.int32)
    pos, tok_of_pos, tile_ea, tile_eb, tile_valid = _plan(cls, tm_e)
    text = jnp.concatenate([text_p, text_s], axis=0)
    xsorted = jnp.take(text, tok_of_pos, axis=0)
    y_sorted = _moe(tile_ea, tile_eb, tile_valid, xsorted, w1, w3, w2, tm=tm_e)
    y = jnp.take(y_sorted, pos, axis=0)
    n_p = bp * tp

    out_p = _final(x1_p, y[:n_p], mod_p, gf, seq_len=tp, tm=tm)
    out_s = _final(x1_s, y[n_p:], mod_s, gf, seq_len=ts, tm=tm)

    y_prompt = out_p.reshape(bp, tp, D_MODEL)
    y_sample = out_s.reshape(bs, ts, D_MODEL)
    kvp = kv_p.reshape(bp, tp, 2, SWA_KV_HEADS, SWA_HEAD_DIM)[:, tp - WINDOW:]
    kvs = kv_s.reshape(bs, ts, 2, SWA_KV_HEADS, SWA_HEAD_DIM)
    k_s = jnp.concatenate([cache_swa_k[l].astype(F32), kvs[:, :, 0]], axis=1)[:, -WINDOW:]
    v_s = jnp.concatenate([cache_swa_v[l].astype(F32), kvs[:, :, 1]], axis=1)[:, -WINDOW:]
    return (y_prompt, y_sample, state_p[None], kvp[:, :, 0][None], kvp[:, :, 1][None],
            state_s[None], k_s[None], v_s[None])
```

```python
import functools
import math

import jax
import jax.numpy as jnp
from jax import lax
from jax.experimental import pallas as pl
from jax.experimental.pallas import tpu as pltpu

F32 = jnp.float32
BF16 = jnp.bfloat16

D_MODEL = 2048
CHUNK = 64
RET_HEADS = 8
RET_DK = 128
RET_DV = 128
RET_W = RET_HEADS * RET_DK
ROPE_BASE = 10000.0
SWA_Q_HEADS = 16
SWA_KV_HEADS = 2
SWA_GROUP = SWA_Q_HEADS // SWA_KV_HEADS
SWA_HEAD_DIM = 64
SWA_Q_W = SWA_Q_HEADS * SWA_HEAD_DIM
SWA_KV_W = SWA_KV_HEADS * SWA_HEAD_DIM
WINDOW = 128
PAST_LEN = 1024
N_GROUPS = 4
EXPERTS_PER_GROUP = 4
N_EXPERTS = 16
D_EXPERT = 512
N_MOD = 6
EPS = 1e-6
NEG_INF = -1e30
N_PAIRS = 6
N_CLASSES = N_GROUPS * N_PAIRS

LANES = 128
VMEM_LIMIT = 56 * 1024 * 1024

RET_LOG_GAMMA = tuple(math.log1p(-(2.0 ** (-5.0 - h))) for h in range(RET_HEADS))


def _cparams(sem):
    return pltpu.CompilerParams(dimension_semantics=sem, vmem_limit_bytes=VMEM_LIMIT)


def _resident(shape):
    nd = len(shape)
    return pl.BlockSpec(shape, lambda *_: (0,) * nd, pipeline_mode=pl.Buffered(1))


def _ada_kernel(c_ref, w_ref, b_ref, o_ref):
    c = c_ref[...]
    a = c * jax.nn.sigmoid(c)
    o_ref[...] = jnp.dot(a, w_ref[...], preferred_element_type=F32,
                         precision=lax.Precision.HIGHEST) + b_ref[...]


def _ada(c_all, ada_w, ada_b):
    nb = c_all.shape[0]
    n_out = ada_w.shape[1]
    tn = 1024
    return pl.pallas_call(
        _ada_kernel,
        grid=(n_out // tn,),
        in_specs=[pl.BlockSpec((nb, D_MODEL), lambda j: (0, 0)),
                  pl.BlockSpec((D_MODEL, tn), lambda j: (0, j)),
                  pl.BlockSpec((1, tn), lambda j: (0, j))],
        out_specs=pl.BlockSpec((nb, tn), lambda j: (0, j)),
        out_shape=jax.ShapeDtypeStruct((nb, n_out), F32),
        compiler_params=_cparams(("arbitrary",)),
        name="ada",
    )(c_all, ada_w, ada_b.reshape(1, n_out))


def _modulated_norm(x, g, shift, scale, n_seq):
    tm = x.shape[0]
    ms = jnp.mean(x * x, axis=-1, keepdims=True)
    y = x * lax.rsqrt(ms + EPS) * g
    y3 = y.reshape(n_seq, tm // n_seq, D_MODEL)
    h = y3 * (1.0 + scale)[:, None, :] + shift[:, None, :]
    return h.reshape(tm, D_MODEL)


def _inproj_kernel(x_ref, mod_ref, g1_ref, cos_ref, sin_ref, wret_ref, wsq_ref, wkv_ref, wbg_ref,
                   ret_ref, sq_ref, kv_ref, gate_ref, h_scr, *, n_seq):
    j = pl.program_id(1)

    @pl.when(j == 0)
    def _():
        h = _modulated_norm(x_ref[...], g1_ref[...], mod_ref[:, 0, :], mod_ref[:, 1, :], n_seq)
        h_scr[...] = h.astype(BF16)

    @pl.when(j < 2)
    def _():
        acc = jnp.dot(h_scr[...], wret_ref[...], preferred_element_type=F32)
        cos = cos_ref[...]
        sin = sin_ref[...]
        scale = jnp.where(j == 1, RET_DK ** -0.5, 1.0).astype(F32)
        for hh in range(RET_HEADS):
            a = acc[:, hh * RET_DK:(hh + 1) * RET_DK]
            r = a * cos + pltpu.roll(a, RET_DK // 2, 1) * sin
            ret_ref[:, hh * RET_DK:(hh + 1) * RET_DK] = (r * scale).astype(BF16)

    @pl.when((j >= 2) & (j < 4))
    def _():
        ret_ref[...] = jnp.dot(h_scr[...], wret_ref[...], preferred_element_type=F32).astype(BF16)

    @pl.when(j == 4)
    def _():
        h = h_scr[...]
        sq_ref[...] = jnp.dot(h, wsq_ref[...], preferred_element_type=F32).astype(BF16)
        kv_ref[...] = jnp.dot(h, wkv_ref[...], preferred_element_type=F32)

    @pl.when(j >= 5)
    def _():
        acc = jnp.dot(h_scr[...], wbg_ref[...], preferred_element_type=F32)
        gate_ref[...] = jax.nn.sigmoid(acc).astype(BF16)


def _inproj(x2d, mod3, g1, cos_t, sin_t, wret, wsq, wkv, wbg, *, seq_len, tm):
    R = x2d.shape[0]
    if seq_len >= tm:
        n_seq, tps = 1, seq_len // tm
        mod_map = lambda i, j: (i // tps, 0, 0)
        tab_map = lambda i, j: (i % tps, 0)
    else:
        n_seq = tm // seq_len
        mod_map = lambda i, j: (i, 0, 0)
        tab_map = lambda i, j: (0, 0)
    tn = 1024
    n_ret = wret.shape[1] // tn
    n_bg = wbg.shape[1] // tn
    nj = n_ret + 1 + n_bg
    ret_map = lambda i, j: (i, jnp.minimum(j, n_ret - 1))
    bg_map = lambda i, j: (i, jnp.clip(j - (n_ret + 1), 0, n_bg - 1))
    return pl.pallas_call(
        functools.partial(_inproj_kernel, n_seq=n_seq),
        grid=(R // tm, nj),
        in_specs=[
            pl.BlockSpec((tm, D_MODEL), lambda i, j: (i, 0)),
            pl.BlockSpec((n_seq, N_MOD, D_MODEL), mod_map),
            _resident((1, D_MODEL)),
            pl.BlockSpec((tm, LANES), tab_map),
            pl.BlockSpec((tm, LANES), tab_map),
            pl.BlockSpec((D_MODEL, tn), lambda i, j: (0, jnp.minimum(j, n_ret - 1))),
            _resident((D_MODEL, SWA_Q_W)),
            _resident((D_MODEL, 2 * SWA_KV_W)),
            pl.BlockSpec((D_MODEL, tn), lambda i, j: (0, jnp.clip(j - (n_ret + 1), 0, n_bg - 1))),
        ],
        out_specs=[
            pl.BlockSpec((tm, tn), ret_map),
            pl.BlockSpec((tm, SWA_Q_W), lambda i, j: (i, 0)),
            pl.BlockSpec((tm, 2 * SWA_KV_W), lambda i, j: (i, 0)),
            pl.BlockSpec((tm, tn), bg_map),
        ],
        out_shape=[
            jax.ShapeDtypeStruct((R, 4 * RET_W), BF16),
            jax.ShapeDtypeStruct((R, SWA_Q_W), BF16),
            jax.ShapeDtypeStruct((R, 2 * SWA_KV_W), F32),
            jax.ShapeDtypeStruct((R, 2 * D_MODEL), BF16),
        ],
        scratch_shapes=[pltpu.VMEM((tm, D_MODEL), BF16)],
        compiler_params=_cparams(("parallel", "arbitrary")),
        name="inproj",
    )(x2d, mod3, g1, cos_t, sin_t, wret, wsq, wkv, wbg)


def _ret_kernel(blk_ref, s0_ref, gn_ref, r_ref, sout_ref, s_scr, dec_scr, qd_scr, kd_scr, *, lc):
    c = pl.program_id(1)

    @pl.when(c == 0)
    def _():
        s_scr[...] = s0_ref[0]
        li = lax.broadcasted_iota(jnp.int32, (lc, lc), 0)
        mi = lax.broadcasted_iota(jnp.int32, (lc, lc), 1)
        diff = li - mi
        l1 = lax.broadcasted_iota(jnp.int32, (lc, RET_DK), 0)
        for h in range(RET_HEADS):
            lg = RET_LOG_GAMMA[h]
            dec_scr[h] = jnp.where(diff >= 0, jnp.exp(jnp.maximum(diff, 0).astype(F32) * lg), 0.0)
            qd_scr[h] = jnp.exp((l1 + 1).astype(F32) * lg)
            kd_scr[h] = jnp.exp((lc - 1 - l1).astype(F32) * lg)

    nt = (((1,), (1,)), ((), ()))
    tn = (((0,), (0,)), ((), ()))
    for h in range(RET_HEADS):
        lo, hi = h * RET_DK, (h + 1) * RET_DK
        q = blk_ref[:, lo:hi]
        k = blk_ref[:, RET_W + lo:RET_W + hi]
        v = blk_ref[:, 2 * RET_W + lo:2 * RET_W + hi]
        g = blk_ref[:, 3 * RET_W + lo:3 * RET_W + hi].astype(F32)
        s_prev = s_scr[h]
        s = lax.dot_general(q, k, nt, preferred_element_type=F32) * dec_scr[h]
        intra = jnp.dot(s.astype(BF16), v, preferred_element_type=F32)
        cross = jnp.dot(q, s_prev.astype(BF16), preferred_element_type=F32) * qd_scr[h]
        o = intra + cross
        kd = (k.astype(F32) * kd_scr[h]).astype(BF16)
        s_scr[h] = math.exp(lc * RET_LOG_GAMMA[h]) * s_prev + lax.dot_general(
            kd, v, tn, preferred_element_type=F32)
        mu = jnp.mean(o, axis=-1, keepdims=True)
        d = o - mu
        var = jnp.mean(d * d, axis=-1, keepdims=True)
        on = d * lax.rsqrt(var + EPS) * gn_ref[:, lo:hi]
        r_ref[:, lo:hi] = (on * (g * jax.nn.sigmoid(g))).astype(BF16)

    @pl.when(c == pl.num_programs(1) - 1)
    def _():
        sout_ref[0] = s_scr[...]


def _retention(ret_all, s0, gn_g, *, n_seq, seq_len, lc):
    R = ret_all.shape[0]
    nc = seq_len // lc
    st_spec = pl.BlockSpec((1, RET_HEADS, RET_DK, RET_DV), lambda b, c: (b, 0, 0, 0))
    return pl.pallas_call(
        functools.partial(_ret_kernel, lc=lc),
        grid=(n_seq, nc),
        in_specs=[pl.BlockSpec((lc, 4 * RET_W), lambda b, c: (b * nc + c, 0)),
                  st_spec,
                  _resident((1, RET_W))],
        out_specs=[pl.BlockSpec((lc, RET_W), lambda b, c: (b * nc + c, 0)), st_spec],
        out_shape=[jax.ShapeDtypeStruct((R, RET_W), BF16),
                   jax.ShapeDtypeStruct((n_seq, RET_HEADS, RET_DK, RET_DV), F32)],
        scratch_shapes=[pltpu.VMEM((RET_HEADS, RET_DK, RET_DV), F32),
                        pltpu.VMEM((RET_HEADS, lc, lc), F32),
                        pltpu.VMEM((RET_HEADS, lc, RET_DK), F32),
                        pltpu.VMEM((RET_HEADS, lc, RET_DK), F32)],
        compiler_params=_cparams(("parallel", "arbitrary")),
        name="retention",
    )(ret_all, s0, gn_g)


KEYS = WINDOW + CHUNK
KPAD = 256


def _swa_kernel(sink_ref, q_ref, k2_ref, k1_ref, k0_ref, v2_ref, v1_ref, v0_ref, o_ref, *, masked):
    c = pl.program_id(1)
    lane = lax.broadcasted_iota(jnp.int32, (KEYS, LANES), 1)
    zpad = jnp.zeros((KPAD - KEYS, LANES), BF16)

    def block_diag(win, h):
        rolled = pltpu.roll(win, SWA_HEAD_DIM, 1)
        lo_src, hi_src = (win, rolled) if h == 0 else (rolled, win)
        a = jnp.where(lane < SWA_HEAD_DIM, lo_src, 0.0).astype(BF16)
        b = jnp.where(lane >= SWA_HEAD_DIM, hi_src, 0.0).astype(BF16)
        return jnp.concatenate([a, zpad, b, zpad], axis=0)

    kwin = jnp.concatenate([k2_ref[...], k1_ref[...], k0_ref[...]], axis=0)
    vwin = jnp.concatenate([v2_ref[...], v1_ref[...], v0_ref[...]], axis=0)

    col = lax.broadcasted_iota(jnp.int32, (1, KPAD), 1)
    if masked:
        first_ok = jnp.where(c >= 2, 0, jnp.where(c == 1, CHUNK, 2 * CHUNK))
        ok = (col >= first_ok) & (col < KEYS)
    else:
        ok = col < KEYS
    n_pairs = SWA_GROUP // 2
    rows = n_pairs * CHUNK
    row = lax.broadcasted_iota(jnp.int32, (rows, 1), 0)
    out_lane = lax.broadcasted_iota(jnp.int32, (rows, LANES), 1)
    nt = (((1,), (1,)), ((), ()))
    for h in range(SWA_KV_HEADS):
        kk = block_diag(kwin, h)
        vv = block_diag(vwin, h)
        base = h * SWA_GROUP * SWA_HEAD_DIM
        q4 = jnp.concatenate([q_ref[:, base + p * LANES: base + (p + 1) * LANES] for p in range(n_pairs)],
                             axis=0)
        s = lax.dot_general(q4, kk, nt, preferred_element_type=F32) * (SWA_HEAD_DIM ** -0.5)
        ps, invs = [], []
        for half in range(2):
            sh = jnp.where(ok, s[:, half * KPAD:(half + 1) * KPAD], NEG_INF)
            sink = jnp.zeros((rows, 1), F32)
            for p in range(n_pairs):
                sink = jnp.where(row // CHUNK == p, sink_ref[h * SWA_GROUP + 2 * p + half], sink)
            m = jnp.maximum(jnp.max(sh, axis=-1, keepdims=True), sink)
            p_half = jnp.exp(sh - m)
            den = jnp.sum(p_half, axis=-1, keepdims=True) + jnp.exp(sink - m)
            ps.append(p_half.astype(BF16))
            invs.append(1.0 / den)
        pv = jnp.dot(jnp.concatenate(ps, axis=1), vv, preferred_element_type=F32)
        o = pv * jnp.where(out_lane < SWA_HEAD_DIM, invs[0], invs[1])
        for p in range(n_pairs):
            o_ref[:, base + p * LANES: base + (p + 1) * LANES] = o[p * CHUNK:(p + 1) * CHUNK].astype(BF16)


def _swa(sinks, sq, k_arrs, v_arrs, k_maps, v_maps, *, n_seq, nc, masked):
    R = sq.shape[0]
    kv_specs = [pl.BlockSpec((CHUNK, SWA_KV_W), m) for m in (*k_maps, *v_maps)]
    return pl.pallas_call(
        functools.partial(_swa_kernel, masked=masked),
        grid=(n_seq, nc),
        in_specs=[pl.BlockSpec(memory_space=pltpu.SMEM),
                  pl.BlockSpec((CHUNK, SWA_Q_W), lambda b, c: (b * nc + c, 0)),
                  *kv_specs],
        out_specs=pl.BlockSpec((CHUNK, SWA_Q_W), lambda b, c: (b * nc + c, 0)),
        out_shape=jax.ShapeDtypeStruct((R, SWA_Q_W), BF16),
        compiler_params=_cparams(("parallel", "arbitrary")),
        name="swa",
    )(sinks, sq, *k_arrs, *v_arrs)


def _route(logits):
    tm = logits.shape[0]
    lane = lax.broadcasted_iota(jnp.int32, (tm, LANES), 1)
    is_g = lane < N_GROUPS
    gl = jnp.where(is_g, logits, NEG_INF)
    gmax = jnp.max(gl, axis=-1, keepdims=True)
    gidx = jnp.min(jnp.where(gl == gmax, lane, LANES), axis=-1, keepdims=True)
    gsum = jnp.sum(jnp.where(is_g, jnp.exp(gl - gmax), 0.0), axis=-1, keepdims=True)
    g_w = 1.0 / gsum
    base = N_GROUPS + EXPERTS_PER_GROUP * gidx
    el = jnp.where((lane >= base) & (lane < base + EXPERTS_PER_GROUP), logits, NEG_INF)
    v1 = jnp.max(el, axis=-1, keepdims=True)
    i1 = jnp.min(jnp.where(el == v1, lane, LANES), axis=-1, keepdims=True)
    el2 = jnp.where(lane == i1, NEG_INF, el)
    v2 = jnp.max(el2, axis=-1, keepdims=True)
    i2 = jnp.min(jnp.where(el2 == v2, lane, LANES), axis=-1, keepdims=True)
    e2 = jnp.exp(v2 - v1)
    den = 1.0 + e2
    w1 = g_w / den
    w2 = g_w * e2 / den
    l1 = i1 - base
    l2 = i2 - base
    first_lo = l1 < l2
    la = jnp.where(first_lo, l1, l2)
    lb = jnp.where(first_lo, l2, l1)
    wa = jnp.where(first_lo, w1, w2)
    wb = jnp.where(first_lo, w2, w1)
    pair = jnp.where(la == 0, 0, jnp.where(la == 1, 3, 5)) + (lb - la - 1)
    cls = (gidx * N_PAIRS + pair).astype(F32)
    return jnp.where(lane == 0, wa, jnp.where(lane == 1, wb, jnp.where(lane == 2, cls, 0.0)))


def _merge_kernel(x_ref, r_ref, o_ref, gate_ref, mod_ref, g2_ref, wrb_ref, wsb_ref, wout_ref, wr_ref, br_ref,
                  x1_ref, text_ref, meta_ref, *, n_seq):
    tm = x_ref.shape[0]
    g_r = gate_ref[:, :D_MODEL].astype(F32)
    g_s = gate_ref[:, D_MODEL:].astype(F32)
    merged = (g_r * jnp.dot(r_ref[...], wrb_ref[...], preferred_element_type=F32)
              + g_s * jnp.dot(o_ref[...], wsb_ref[...], preferred_element_type=F32))
    mix = jnp.dot(merged.astype(BF16), wout_ref[...], preferred_element_type=F32)
    gt1 = mod_ref[:, 2, :]
    x1 = (x_ref[...].reshape(n_seq, tm // n_seq, D_MODEL) + gt1[:, None, :]
          * mix.reshape(n_seq, tm // n_seq, D_MODEL)).reshape(tm, D_MODEL)
    x1_ref[...] = x1
    t = _modulated_norm(x1, g2_ref[...], mod_ref[:, 3, :], mod_ref[:, 4, :], n_seq)
    logits = jnp.dot(t.astype(BF16), wr_ref[...], preferred_element_type=F32) + br_ref[...]
    meta = _route(logits)
    text_ref[:, :D_MODEL] = t
    text_ref[:, D_MODEL:] = meta
    meta_ref[...] = meta


def _merge(x2d, r, o_swa, gates, mod3, g2, wrb, wsb, wout, wr, br, *, seq_len, tm):
    R = x2d.shape[0]
    if seq_len >= tm:
        n_seq, tps = 1, seq_len // tm
        mod_map = lambda i: (i // tps, 0, 0)
    else:
        n_seq = tm // seq_len
        mod_map = lambda i: (i, 0, 0)
    row = lambda w: pl.BlockSpec((tm, w), lambda i: (i, 0))
    return pl.pallas_call(
        functools.partial(_merge_kernel, n_seq=n_seq),
        grid=(R // tm,),
        in_specs=[row(D_MODEL), row(RET_W), row(SWA_Q_W), row(2 * D_MODEL),
                  pl.BlockSpec((n_seq, N_MOD, D_MODEL), mod_map),
                  _resident((1, D_MODEL)),
                  _resident((RET_W, D_MODEL)), _resident((SWA_Q_W, D_MODEL)), _resident((D_MODEL, D_MODEL)),
                  _resident((D_MODEL, LANES)), _resident((1, LANES))],
        out_specs=[row(D_MODEL), row(D_MODEL + LANES), row(LANES)],
        out_shape=[jax.ShapeDtypeStruct((R, D_MODEL), F32),
                   jax.ShapeDtypeStruct((R, D_MODEL + LANES), F32),
                   jax.ShapeDtypeStruct((R, LANES), F32)],
        compiler_params=_cparams(("parallel",)),
        name="merge",
    )(x2d, r, o_swa, gates, mod3, g2, wrb, wsb, wout, wr, br)


PLAN_BLK = 512
TILE_ROWS = 256


def _plan_kernel(meta_a_ref, meta_b_ref, pos_ref, tile_ref, pad_ref, cnt_scr, offs_scr, carry_scr, tri_scr,
                 *, tm, nb_a):
    ph = pl.program_id(0)
    b = pl.program_id(1)
    blk = meta_a_ref.shape[0]
    lane = lax.broadcasted_iota(jnp.int32, (blk, LANES), 1)
    cls_col = jnp.where(b < nb_a, meta_a_ref[:, 2:3], meta_b_ref[:, 2:3])
    oh = jnp.where(cls_col == lane.astype(F32), 1.0, 0.0)

    @pl.when((ph == 0) & (b == 0))
    def _():
        cnt_scr[...] = jnp.zeros_like(cnt_scr)
        ri = lax.broadcasted_iota(jnp.int32, (blk, blk), 0)
        ci = lax.broadcasted_iota(jnp.int32, (blk, blk), 1)
        tri_scr[...] = jnp.where(ci <= ri, 1.0, 0.0).astype(BF16)

    @pl.when(ph == 0)
    def _():
        cnt_scr[...] += jnp.sum(oh, axis=0, keepdims=True)

    @pl.when((ph == 1) & (b == 0))
    def _():
        cnt = cnt_scr[...]
        ptiles = jnp.floor((cnt + (tm - 1)) * (1.0 / tm))
        ri = lax.broadcasted_iota(jnp.int32, (LANES, LANES), 0)
        ci = lax.broadcasted_iota(jnp.int32, (LANES, LANES), 1)
        before = jnp.where(ri < ci, 1.0, 0.0).astype(BF16)
        offs = jnp.dot(ptiles.astype(BF16), before, preferred_element_type=F32) * tm
        offs_scr[...] = offs
        carry_scr[...] = jnp.zeros_like(carry_scr)
        padded = ptiles * tm
        ends = offs + padded
        tl = lax.broadcasted_iota(jnp.int32, (TILE_ROWS, LANES), 1)
        tstart = lax.broadcasted_iota(jnp.int32, (TILE_ROWS, LANES), 0).astype(F32) * tm
        tcls = jnp.sum(jnp.where((ends[0:1, :] <= tstart) & (tl < N_CLASSES), 1.0, 0.0), axis=1, keepdims=True)
        tcls = jnp.minimum(tcls, N_CLASSES - 1.0)
        total = jnp.max(ends[0:1, :], axis=1, keepdims=True)
        grp = (jnp.where(tcls >= N_PAIRS, 1.0, 0.0) + jnp.where(tcls >= 2 * N_PAIRS, 1.0, 0.0)
               + jnp.where(tcls >= 3 * N_PAIRS, 1.0, 0.0))
        pair = tcls - N_PAIRS * grp
        la = jnp.where(pair >= 3, 1.0, 0.0) + jnp.where(pair >= 5, 1.0, 0.0)
        lb = jnp.where(pair == 0, 1.0, jnp.where((pair == 1) | (pair == 3), 2.0, 3.0))
        ea = EXPERTS_PER_GROUP * grp + la
        eb = EXPERTS_PER_GROUP * grp + lb
        n_used = total * (1.0 / tm)
        tile_ref[...] = jnp.where(tl == 0, ea, jnp.where(tl == 1, eb, jnp.where(tl == 2, n_used, 0.0))
                                  ).astype(jnp.int32)
        npad = padded - cnt
        pstart = jnp.dot(npad.astype(BF16), before, preferred_element_type=F32)
        n_class_pad = jnp.sum(npad[0:1, :], axis=1, keepdims=True)
        rows = pad_ref.shape[0]
        v = (lax.broadcasted_iota(jnp.int32, (rows, LANES), 0) * LANES
             + lax.broadcasted_iota(jnp.int32, (rows, LANES), 1)).astype(F32)
        slot = jnp.where(v >= n_class_pad, total - n_class_pad + v, 0.0)
        for c in range(N_CLASSES):
            ps = pstart[0:1, c:c + 1]
            inside = (v >= ps) & (v < ps + npad[0:1, c:c + 1])
            slot = jnp.where(inside, offs[0:1, c:c + 1] + cnt[0:1, c:c + 1] - ps + v, slot)
        pad_ref[...] = slot.astype(jnp.int32)

    @pl.when(ph == 1)
    def _():
        incl = jnp.dot(tri_scr[...], oh.astype(BF16), preferred_element_type=F32)
        base = offs_scr[0:1, :] + carry_scr[0:1, :]
        pos = jnp.sum(oh * (base + incl - oh), axis=1, keepdims=True)
        pos_ref[...] = jnp.broadcast_to(pos, (blk, LANES)).astype(jnp.int32)
        carry_scr[...] += jnp.sum(oh, axis=0, keepdims=True)


def _plan(meta_a, meta_b, *, tm):
    na, nb = meta_a.shape[0], meta_b.shape[0]
    n = na + nb
    n_tiles = n // tm + N_CLASSES
    n_free = N_CLASSES * tm
    assert n_tiles <= TILE_ROWS and na % PLAN_BLK == 0 and nb % PLAN_BLK == 0 and n_free % LANES == 0
    n_slots = n_tiles * tm
    nb_a = na // PLAN_BLK
    nb_b = nb // PLAN_BLK
    pos2d, tile2d, pad2d = pl.pallas_call(
        functools.partial(_plan_kernel, tm=tm, nb_a=nb_a),
        grid=(2, nb_a + nb_b),
        in_specs=[pl.BlockSpec((PLAN_BLK, LANES), lambda ph, b: (jnp.minimum(b, nb_a - 1), 0)),
                  pl.BlockSpec((PLAN_BLK, LANES), lambda ph, b: (jnp.maximum(b - nb_a, 0), 0))],
        out_specs=[pl.BlockSpec((PLAN_BLK, LANES), lambda ph, b: (b * ph, 0)),
                   pl.BlockSpec((TILE_ROWS, LANES), lambda ph, b: (0, 0)),
                   pl.BlockSpec((n_free // LANES, LANES), lambda ph, b: (0, 0))],
        out_shape=[jax.ShapeDtypeStruct((n, LANES), jnp.int32),
                   jax.ShapeDtypeStruct((TILE_ROWS, LANES), jnp.int32),
                   jax.ShapeDtypeStruct((n_free // LANES, LANES), jnp.int32)],
        scratch_shapes=[pltpu.VMEM((8, LANES), F32), pltpu.VMEM((8, LANES), F32), pltpu.VMEM((8, LANES), F32),
                        pltpu.VMEM((PLAN_BLK, PLAN_BLK), BF16)],
        compiler_params=_cparams(("arbitrary", "arbitrary")),
        name="plan",
    )(meta_a, meta_b)
    return pos2d[:, 0], pad2d.reshape(-1), tile2d[:n_tiles, 0], tile2d[:n_tiles, 1], tile2d[0:1, 2], n_slots


def _scatter_kernel(pos_ref, text_a_ref, text_b_ref, out_ref, zero_scr, sem, *, nb_a, nb_b):
    i = pl.program_id(0)
    tb = text_a_ref.shape[0]

    @pl.when(i == 0)
    def _():
        zero_scr[...] = jnp.zeros_like(zero_scr)

    def scatter_rows(src_ref):
        def body(r, carry):
            dst = pos_ref[0, 0, r]
            pltpu.make_async_copy(src_ref.at[pl.ds(r, 1), :], out_ref.at[pl.ds(dst, 1), :], sem).start()
            return carry

        lax.fori_loop(0, tb, body, 0, unroll=8)
        pltpu.make_async_copy(src_ref, out_ref.at[pl.ds(0, tb), :], sem).wait()

    @pl.when(i < nb_a)
    def _():
        scatter_rows(text_a_ref)

    @pl.when((i >= nb_a) & (i < nb_a + nb_b))
    def _():
        scatter_rows(text_b_ref)

    @pl.when(i >= nb_a + nb_b)
    def _():
        scatter_rows(zero_scr)


def _scatter_rows(text_a, text_b, pos_ext, *, tb):
    w = text_a.shape[1]
    nb_a = text_a.shape[0] // tb
    nb_b = text_b.shape[0] // tb
    n_steps = pos_ext.shape[0] // tb
    return pl.pallas_call(
        functools.partial(_scatter_kernel, nb_a=nb_a, nb_b=nb_b),
        grid=(n_steps,),
        in_specs=[pl.BlockSpec((1, 1, tb), lambda i: (i, 0, 0), memory_space=pltpu.SMEM),
                  pl.BlockSpec((tb, w), lambda i: (jnp.minimum(i, nb_a - 1), 0)),
                  pl.BlockSpec((tb, w), lambda i: (jnp.clip(i - nb_a, 0, nb_b - 1), 0))],
        out_specs=pl.BlockSpec(memory_space=pl.ANY),
        out_shape=jax.ShapeDtypeStruct((pos_ext.shape[0], w), text_a.dtype),
        scratch_shapes=[pltpu.VMEM((tb, w), text_a.dtype), pltpu.SemaphoreType.DMA(())],
        compiler_params=_cparams(("arbitrary",)),
        name="scatter_rows",
    )(pos_ext.reshape(n_steps, 1, tb), text_a, text_b)


def _moe_kernel(ea_ref, eb_ref, nused_ref, xs_ref, w1a_ref, w3a_ref, w2a_ref, w1b_ref, w3b_ref, w2b_ref, y_ref):
    i = pl.program_id(0)

    @pl.when(i >= nused_ref[0])
    def _():
        y_ref[...] = jnp.zeros_like(y_ref)

    @pl.when(i < nused_ref[0])
    def _():
        x = xs_ref[:, :D_MODEL].astype(BF16)
        wa = xs_ref[:, D_MODEL:D_MODEL + 1]
        wb = xs_ref[:, D_MODEL + 1:D_MODEL + 2]

        def hidden(w1_ref, w3_ref, gate):
            a = jnp.dot(x, w1_ref[0], preferred_element_type=F32)
            b = jnp.dot(x, w3_ref[0], preferred_element_type=F32)
            return (a * jax.nn.sigmoid(a) * b * gate).astype(BF16)

        ha = hidden(w1a_ref, w3a_ref, wa)
        hb = hidden(w1b_ref, w3b_ref, wb)
        y_ref[...] = (jnp.dot(ha, w2a_ref[0], preferred_element_type=F32)
                      + jnp.dot(hb, w2b_ref[0], preferred_element_type=F32))


def _moe(tile_ea, tile_eb, n_used, xs, w1, w3, w2, *, tm, n_tiles):
    last = lambda i, nu: jnp.minimum(i, nu[0] - 1)
    wa_map = lambda i, ea, eb, nu: (ea[last(i, nu)], 0, 0)
    wb_map = lambda i, ea, eb, nu: (eb[last(i, nu)], 0, 0)
    row_map = lambda i, ea, eb, nu: (last(i, nu), 0)
    up = (1, D_MODEL, D_EXPERT)
    down = (1, D_EXPERT, D_MODEL)
    grid_spec = pltpu.PrefetchScalarGridSpec(
        num_scalar_prefetch=3,
        grid=(n_tiles,),
        in_specs=[pl.BlockSpec((tm, D_MODEL + LANES), row_map),
                  pl.BlockSpec(up, wa_map), pl.BlockSpec(up, wa_map), pl.BlockSpec(down, wa_map),
                  pl.BlockSpec(up, wb_map), pl.BlockSpec(up, wb_map), pl.BlockSpec(down, wb_map)],
        out_specs=pl.BlockSpec((tm, D_MODEL), lambda i, ea, eb, nu: (i, 0)),
    )
    return pl.pallas_call(
        _moe_kernel,
        grid_spec=grid_spec,
        out_shape=jax.ShapeDtypeStruct((n_tiles * tm, D_MODEL), F32),
        compiler_params=_cparams(("arbitrary",)),
        name="moe",
    )(tile_ea, tile_eb, n_used, xs, w1, w3, w2, w1, w3, w2)


def _final_kernel(pos_ref, pos_next_ref, x1_ref, mod_ref, g_ref, ys_ref, o_ref, ybuf, sem, *, n_seq):
    i = pl.program_id(0)
    n = pl.num_programs(0)
    tm = x1_ref.shape[0]
    slot = i % 2

    def start_gather(idx_ref, s):
        def body(r, carry):
            src = idx_ref[0, 0, r]
            pltpu.make_async_copy(ys_ref.at[pl.ds(src, 1), :], ybuf.at[s, pl.ds(r, 1), :], sem.at[s]).start()
            return carry

        lax.fori_loop(0, tm, body, 0, unroll=8)

    @pl.when(i == 0)
    def _():
        start_gather(pos_ref, 0)

    @pl.when(i + 1 < n)
    def _():
        start_gather(pos_next_ref, 1 - slot)

    pltpu.make_async_copy(ys_ref.at[pl.ds(0, tm), :], ybuf.at[slot], sem.at[slot]).wait()
    gt2 = mod_ref[:, 5, :]
    x2 = (x1_ref[...].reshape(n_seq, tm // n_seq, D_MODEL)
          + gt2[:, None, :] * ybuf[slot].reshape(n_seq, tm // n_seq, D_MODEL)).reshape(tm, D_MODEL)
    ms = jnp.mean(x2 * x2, axis=-1, keepdims=True)
    o_ref[...] = x2 * lax.rsqrt(ms + EPS) * g_ref[...]


def _final(x1, y_sorted, pos, mod3, gf, *, seq_len, tm):
    R = x1.shape[0]
    if seq_len >= tm:
        n_seq, tps = 1, seq_len // tm
        mod_map = lambda i: (i // tps, 0, 0)
    else:
        n_seq = tm // seq_len
        mod_map = lambda i: (i, 0, 0)
    n = R // tm
    row = pl.BlockSpec((tm, D_MODEL), lambda i: (i, 0))
    pos3 = pos.reshape(n, 1, tm)
    return pl.pallas_call(
        functools.partial(_final_kernel, n_seq=n_seq),
        grid=(n,),
        in_specs=[pl.BlockSpec((1, 1, tm), lambda i: (i, 0, 0), memory_space=pltpu.SMEM),
                  pl.BlockSpec((1, 1, tm), lambda i: (jnp.minimum(i + 1, n - 1), 0, 0), memory_space=pltpu.SMEM),
                  row, pl.BlockSpec((n_seq, N_MOD, D_MODEL), mod_map), _resident((1, D_MODEL)),
                  pl.BlockSpec(memory_space=pl.ANY)],
        out_specs=row,
        out_shape=jax.ShapeDtypeStruct((R, D_MODEL), F32),
        scratch_shapes=[pltpu.VMEM((2, tm, D_MODEL), F32), pltpu.SemaphoreType.DMA((2,))],
        compiler_params=_cparams(("arbitrary",)),
        name="final",
    )(pos3, pos3, x1, mod3, gf, y_sorted)


def _rope_tables(pos):
    half = RET_DK // 2
    inv = ROPE_BASE ** (-jnp.arange(half, dtype=F32) / half)
    ang = pos.astype(F32)[:, None] * inv[None, :]
    cos = jnp.cos(ang)
    sin = jnp.sin(ang)
    return jnp.concatenate([cos, cos], axis=-1), jnp.concatenate([-sin, sin], axis=-1)


def kernel(x_prompt, x_sample, cache_ret_state, cache_swa_k, cache_swa_v, c_prompt, c_sample,
           norm1_g, norm2_g, ada_w, ada_b, w_in, ret_gn_g, swa_sinks, w_ret_branch, w_swa_branch, w_out,
           router_group_w, router_group_b, router_expert_w, router_expert_b,
           expert_w1, expert_w3, expert_w2, final_norm_g):
    depth = w_in.shape[0]
    assert depth == 1
    bp, tp, _ = x_prompt.shape
    bs, ts, _ = x_sample.shape
    past = WINDOW
    assert cache_swa_k.shape[2] == past and ts == CHUNK and tp % 512 == 0
    tm = 512

    l = 0
    wi = w_in[l].astype(BF16)
    wret = wi[:, :4 * RET_W]
    wsq = wi[:, 4 * RET_W:4 * RET_W + SWA_Q_W]
    wkv = wi[:, 4 * RET_W + SWA_Q_W:4 * RET_W + SWA_Q_W + 2 * SWA_KV_W]
    wbg = wi[:, 4 * RET_W + SWA_Q_W + 2 * SWA_KV_W:]
    wrb = w_ret_branch[l].astype(BF16)
    wsb = w_swa_branch[l].astype(BF16)
    wo = w_out[l].astype(BF16)
    n_r = N_GROUPS + N_EXPERTS
    wr = jnp.zeros((D_MODEL, LANES), F32).at[:, :N_GROUPS].set(router_group_w[l]).at[:, N_GROUPS:n_r].set(
        router_expert_w[l]).astype(BF16)
    br = jnp.zeros((1, LANES), F32).at[0, :N_GROUPS].set(router_group_b[l]).at[0, N_GROUPS:n_r].set(
        router_expert_b[l])
    w1 = expert_w1[l].astype(BF16)
    w3 = expert_w3[l].astype(BF16)
    w2 = expert_w2[l].astype(BF16)
    g1 = norm1_g[l].reshape(1, D_MODEL)
    g2 = norm2_g[l].reshape(1, D_MODEL)
    gn = ret_gn_g[l].reshape(1, RET_W)
    gf = final_norm_g.reshape(1, D_MODEL)
    sinks = swa_sinks[l]

    c_all = jnp.concatenate([c_prompt, c_sample], axis=0)
    mod = _ada(c_all, ada_w[l], ada_b[l]).reshape(bp + bs, N_MOD, D_MODEL)
    mod_p, mod_s = mod[:bp], mod[bp:]

    cos_p, sin_p = _rope_tables(jnp.arange(tp))
    cos_s, sin_s = _rope_tables(PAST_LEN + jnp.arange(ts))
    rep = tm // ts
    cos_s, sin_s = jnp.tile(cos_s, (rep, 1)), jnp.tile(sin_s, (rep, 1))

    xp = x_prompt.reshape(bp * tp, D_MODEL)
    xs = x_sample.reshape(bs * ts, D_MODEL)

    ret_p, sq_p, kv_p, gate_p = _inproj(xp, mod_p, g1, cos_p, sin_p, wret, wsq, wkv, wbg, seq_len=tp, tm=tm)
    ret_s, sq_s, kv_s, gate_s = _inproj(xs, mod_s, g1, cos_s, sin_s, wret, wsq, wkv, wbg, seq_len=ts, tm=tm)

    lc_p = 128
    s0_p = jnp.zeros((bp, RET_HEADS, RET_DK, RET_DV), F32)
    r_p, state_p = _retention(ret_p, s0_p, gn, n_seq=bp, seq_len=tp, lc=lc_p)
    r_s, state_s = _retention(ret_s, cache_ret_state[l].astype(F32), gn, n_seq=bs, seq_len=ts, lc=ts)

    nc_p = tp // CHUNK
    kmap = lambda back, colblk: (lambda b, c: (b * nc_p + jnp.maximum(c - back, 0), colblk))
    o_p = _swa(sinks, sq_p, [kv_p] * 3, [kv_p] * 3,
               [kmap(2, 0), kmap(1, 0), kmap(0, 0)], [kmap(2, 1), kmap(1, 1), kmap(0, 1)],
               n_seq=bp, nc=nc_p, masked=True)
    ck = cache_swa_k[l].reshape(bs * past, SWA_KV_W)
    cv = cache_swa_v[l].reshape(bs * past, SWA_KV_W)
    cmap = lambda blk: (lambda b, c: (2 * b + blk, 0))
    o_s = _swa(sinks, sq_s, [ck, ck, kv_s], [cv, cv, kv_s],
               [cmap(0), cmap(1), lambda b, c: (b, 0)], [cmap(0), cmap(1), lambda b, c: (b, 1)],
               n_seq=bs, nc=1, masked=False)

    tm_m = 256
    n_p = bp * tp
    x1_p, text_p, meta_p = _merge(xp, r_p, o_p, gate_p, mod_p, g2, wrb, wsb, wo, wr, br, seq_len=tp, tm=tm_m)
    x1_s, text_s, meta_s = _merge(xs, r_s, o_s, gate_s, mod_s, g2, wrb, wsb, wo, wr, br, seq_len=ts, tm=tm_m)

    tm_e = 256
    pos, free_slots, tile_ea, tile_eb, n_used, n_slots = _plan(meta_p, meta_s, tm=tm_e)
    pos_ext = jnp.concatenate([pos, free_slots], axis=0)
    xsorted = _scatter_rows(text_p, text_s, pos_ext, tb=tm_e)
    y_sorted = _moe(tile_ea, tile_eb, n_used, xsorted, w1, w3, w2, tm=tm_e, n_tiles=n_slots // tm_e)

    tm_f = 256
    out_p = _final(x1_p, y_sorted, pos[:n_p], mod_p, gf, seq_len=tp, tm=tm_f)
    out_s = _final(x1_s, y_sorted, pos[n_p:], mod_s, gf, seq_len=ts, tm=tm_f)

    y_prompt = out_p.reshape(bp, tp, D_MODEL)
    y_sample = out_s.reshape(bs, ts, D_MODEL)
    kvp = kv_p.reshape(bp, tp, 2, SWA_KV_HEADS, SWA_HEAD_DIM)[:, tp - WINDOW:]
    kvs = kv_s.reshape(bs, ts, 2, SWA_KV_HEADS, SWA_HEAD_DIM)
    k_s = jnp.concatenate([cache_swa_k[l].astype(F32), kvs[:, :, 0]], axis=1)[:, -WINDOW:]
    v_s = jnp.concatenate([cache_swa_v[l].astype(F32), kvs[:, :, 1]], axis=1)[:, -WINDOW:]
    return (y_prompt, y_sample, state_p[None], kvp[:, :, 0][None], kvp[:, :, 1][None],
            state_s[None], k_s[None], v_s[None])
```

```python
import functools
import math

import jax
import jax.numpy as jnp
from jax import lax
from jax.experimental import pallas as pl
from jax.experimental.pallas import tpu as pltpu

F32 = jnp.float32
BF16 = jnp.bfloat16

D_MODEL = 2048
CHUNK = 64
RET_HEADS = 8
RET_DK = 128
RET_DV = 128
RET_W = RET_HEADS * RET_DK
ROPE_BASE = 10000.0
SWA_Q_HEADS = 16
SWA_KV_HEADS = 2
SWA_GROUP = SWA_Q_HEADS // SWA_KV_HEADS
SWA_HEAD_DIM = 64
SWA_Q_W = SWA_Q_HEADS * SWA_HEAD_DIM
SWA_KV_W = SWA_KV_HEADS * SWA_HEAD_DIM
WINDOW = 128
PAST_LEN = 1024
N_GROUPS = 4
EXPERTS_PER_GROUP = 4
N_EXPERTS = 16
D_EXPERT = 512
N_MOD = 6
EPS = 1e-6
NEG_INF = -1e30
N_PAIRS = 6
N_CLASSES = N_GROUPS * N_PAIRS

LANES = 128
VMEM_LIMIT = 56 * 1024 * 1024

RET_LOG_GAMMA = tuple(math.log1p(-(2.0 ** (-5.0 - h))) for h in range(RET_HEADS))


def _cparams(sem):
    return pltpu.CompilerParams(dimension_semantics=sem, vmem_limit_bytes=VMEM_LIMIT)


def _resident(shape):
    nd = len(shape)
    return pl.BlockSpec(shape, lambda *_: (0,) * nd, pipeline_mode=pl.Buffered(1))


def _ada_kernel(c_ref, w_ref, b_ref, o_ref):
    c = c_ref[...]
    a = c * jax.nn.sigmoid(c)
    o_ref[...] = jnp.dot(a, w_ref[...], preferred_element_type=F32,
                         precision=lax.Precision.HIGHEST) + b_ref[...]


def _ada(c_all, ada_w, ada_b):
    nb = c_all.shape[0]
    n_out = ada_w.shape[1]
    tn = 1024
    return pl.pallas_call(
        _ada_kernel,
        grid=(n_out // tn,),
        in_specs=[pl.BlockSpec((nb, D_MODEL), lambda j: (0, 0)),
                  pl.BlockSpec((D_MODEL, tn), lambda j: (0, j)),
                  pl.BlockSpec((1, tn), lambda j: (0, j))],
        out_specs=pl.BlockSpec((nb, tn), lambda j: (0, j)),
        out_shape=jax.ShapeDtypeStruct((nb, n_out), F32),
        compiler_params=_cparams(("arbitrary",)),
        name="ada",
    )(c_all, ada_w, ada_b.reshape(1, n_out))


def _modulated_norm(x, g, shift, scale, n_seq):
    tm = x.shape[0]
    ms = jnp.mean(x * x, axis=-1, keepdims=True)
    y = x * lax.rsqrt(ms + EPS) * g
    y3 = y.reshape(n_seq, tm // n_seq, D_MODEL)
    h = y3 * (1.0 + scale)[:, None, :] + shift[:, None, :]
    return h.reshape(tm, D_MODEL)


COL_BLK = 1024


def _inproj_ret_kernel(x_ref, mod_ref, g1_ref, cos_ref, sin_ref, wret_ref, ret_ref, h_ref, *, n_seq):
    h = _modulated_norm(x_ref[...], g1_ref[...], mod_ref[:, 0, :], mod_ref[:, 1, :], n_seq)
    h_ref[...] = h.astype(BF16)
    cos = cos_ref[...]
    sin = sin_ref[...]
    for blk in range(4 * RET_W // COL_BLK):
        c0 = blk * COL_BLK
        acc = jnp.dot(h_ref[...], wret_ref[:, c0:c0 + COL_BLK], preferred_element_type=F32)
        if c0 >= 2 * RET_W:
            ret_ref[:, c0:c0 + COL_BLK] = acc.astype(BF16)
            continue
        scale = 1.0 if c0 < RET_W else RET_DK ** -0.5
        for hh in range(COL_BLK // RET_DK):
            a = acc[:, hh * RET_DK:(hh + 1) * RET_DK]
            r = a * cos + pltpu.roll(a, RET_DK // 2, 1) * sin
            ret_ref[:, c0 + hh * RET_DK:c0 + (hh + 1) * RET_DK] = (r * scale).astype(BF16)


def _inproj_rest_kernel(h_ref, wsq_ref, wkv_ref, wbg_ref, sq_ref, kv_ref, gate_ref):
    sq_ref[...] = jnp.dot(h_ref[...], wsq_ref[...], preferred_element_type=F32).astype(BF16)
    kv_ref[...] = jnp.dot(h_ref[...], wkv_ref[...], preferred_element_type=F32)
    for blk in range(2 * D_MODEL // COL_BLK):
        c0 = blk * COL_BLK
        acc = jnp.dot(h_ref[...], wbg_ref[:, c0:c0 + COL_BLK], preferred_element_type=F32)
        gate_ref[:, c0:c0 + COL_BLK] = jax.nn.sigmoid(acc).astype(BF16)


def _inproj(x2d, mod3, g1, cos_t, sin_t, wret, wsq, wkv, wbg, *, seq_len, tm):
    R = x2d.shape[0]
    if seq_len >= tm:
        n_seq, tps = 1, seq_len // tm
        mod_map = lambda i: (i // tps, 0, 0)
        tab_map = lambda i: (i % tps, 0)
    else:
        n_seq = tm // seq_len
        mod_map = lambda i: (i, 0, 0)
        tab_map = lambda i: (0, 0)
    row = lambda w: pl.BlockSpec((tm, w), lambda i: (i, 0))
    ret, h = pl.pallas_call(
        functools.partial(_inproj_ret_kernel, n_seq=n_seq),
        grid=(R // tm,),
        in_specs=[row(D_MODEL),
                  pl.BlockSpec((n_seq, N_MOD, D_MODEL), mod_map),
                  _resident((1, D_MODEL)),
                  pl.BlockSpec((tm, LANES), tab_map),
                  pl.BlockSpec((tm, LANES), tab_map),
                  _resident((D_MODEL, 4 * RET_W))],
        out_specs=[row(4 * RET_W), row(D_MODEL)],
        out_shape=[jax.ShapeDtypeStruct((R, 4 * RET_W), BF16),
                   jax.ShapeDtypeStruct((R, D_MODEL), BF16)],
        compiler_params=_cparams(("parallel",)),
        name="inproj_ret",
    )(x2d, mod3, g1, cos_t, sin_t, wret)
    sq, kv, gate = pl.pallas_call(
        _inproj_rest_kernel,
        grid=(R // tm,),
        in_specs=[row(D_MODEL),
                  _resident((D_MODEL, SWA_Q_W)),
                  _resident((D_MODEL, 2 * SWA_KV_W)),
                  _resident((D_MODEL, 2 * D_MODEL))],
        out_specs=[row(SWA_Q_W), row(2 * SWA_KV_W), row(2 * D_MODEL)],
        out_shape=[jax.ShapeDtypeStruct((R, SWA_Q_W), BF16),
                   jax.ShapeDtypeStruct((R, 2 * SWA_KV_W), F32),
                   jax.ShapeDtypeStruct((R, 2 * D_MODEL), BF16)],
        compiler_params=_cparams(("parallel",)),
        name="inproj_rest",
    )(h, wsq, wkv, wbg)
    return ret, sq, kv, gate


def _ret_kernel(blk_ref, s0_ref, gn_ref, r_ref, sout_ref, s_scr, dec_scr, qd_scr, kd_scr, *, lc):
    c = pl.program_id(1)

    @pl.when(c == 0)
    def _():
        s_scr[...] = s0_ref[0]
        li = lax.broadcasted_iota(jnp.int32, (lc, lc), 0)
        mi = lax.broadcasted_iota(jnp.int32, (lc, lc), 1)
        diff = li - mi
        l1 = lax.broadcasted_iota(jnp.int32, (lc, RET_DK), 0)
        for h in range(RET_HEADS):
            lg = RET_LOG_GAMMA[h]
            dec_scr[h] = jnp.where(diff >= 0, jnp.exp(jnp.maximum(diff, 0).astype(F32) * lg), 0.0)
            qd_scr[h] = jnp.exp((l1 + 1).astype(F32) * lg)
            kd_scr[h] = jnp.exp((lc - 1 - l1).astype(F32) * lg)

    nt = (((1,), (1,)), ((), ()))
    tn = (((0,), (0,)), ((), ()))
    for h in range(RET_HEADS):
        lo, hi = h * RET_DK, (h + 1) * RET_DK
        q = blk_ref[:, lo:hi]
        k = blk_ref[:, RET_W + lo:RET_W + hi]
        v = blk_ref[:, 2 * RET_W + lo:2 * RET_W + hi]
        g = blk_ref[:, 3 * RET_W + lo:3 * RET_W + hi].astype(F32)
        s_prev = s_scr[h]
        s = lax.dot_general(q, k, nt, preferred_element_type=F32) * dec_scr[h]
        intra = jnp.dot(s.astype(BF16), v, preferred_element_type=F32)
        cross = jnp.dot(q, s_prev.astype(BF16), preferred_element_type=F32) * qd_scr[h]
        o = intra + cross
        kd = (k.astype(F32) * kd_scr[h]).astype(BF16)
        s_scr[h] = math.exp(lc * RET_LOG_GAMMA[h]) * s_prev + lax.dot_general(
            kd, v, tn, preferred_element_type=F32)
        mu = jnp.mean(o, axis=-1, keepdims=True)
        d = o - mu
        var = jnp.mean(d * d, axis=-1, keepdims=True)
        on = d * lax.rsqrt(var + EPS) * gn_ref[:, lo:hi]
        r_ref[:, lo:hi] = (on * (g * jax.nn.sigmoid(g))).astype(BF16)

    @pl.when(c == pl.num_programs(1) - 1)
    def _():
        sout_ref[0] = s_scr[...]


def _retention(ret_all, s0, gn_g, *, n_seq, seq_len, lc):
    R = ret_all.shape[0]
    nc = seq_len // lc
    st_spec = pl.BlockSpec((1, RET_HEADS, RET_DK, RET_DV), lambda b, c: (b, 0, 0, 0))
    return pl.pallas_call(
        functools.partial(_ret_kernel, lc=lc),
        grid=(n_seq, nc),
        in_specs=[pl.BlockSpec((lc, 4 * RET_W), lambda b, c: (b * nc + c, 0)),
                  st_spec,
                  _resident((1, RET_W))],
        out_specs=[pl.BlockSpec((lc, RET_W), lambda b, c: (b * nc + c, 0)), st_spec],
        out_shape=[jax.ShapeDtypeStruct((R, RET_W), BF16),
                   jax.ShapeDtypeStruct((n_seq, RET_HEADS, RET_DK, RET_DV), F32)],
        scratch_shapes=[pltpu.VMEM((RET_HEADS, RET_DK, RET_DV), F32),
                        pltpu.VMEM((RET_HEADS, lc, lc), F32),
                        pltpu.VMEM((RET_HEADS, lc, RET_DK), F32),
                        pltpu.VMEM((RET_HEADS, lc, RET_DK), F32)],
        compiler_params=_cparams(("parallel", "arbitrary")),
        name="retention",
    )(ret_all, s0, gn_g)


KEYS = WINDOW + CHUNK
KPAD = 256


def _swa_kernel(sink_ref, q_ref, k2_ref, k1_ref, k0_ref, v2_ref, v1_ref, v0_ref, o_ref, *, masked):
    c = pl.program_id(1)
    lane = lax.broadcasted_iota(jnp.int32, (KEYS, LANES), 1)
    zpad = jnp.zeros((KPAD - KEYS, LANES), BF16)

    def block_diag(win, h):
        rolled = pltpu.roll(win, SWA_HEAD_DIM, 1)
        lo_src, hi_src = (win, rolled) if h == 0 else (rolled, win)
        a = jnp.where(lane < SWA_HEAD_DIM, lo_src, 0.0).astype(BF16)
        b = jnp.where(lane >= SWA_HEAD_DIM, hi_src, 0.0).astype(BF16)
        return jnp.concatenate([a, zpad, b, zpad], axis=0)

    kwin = jnp.concatenate([k2_ref[...], k1_ref[...], k0_ref[...]], axis=0)
    vwin = jnp.concatenate([v2_ref[...], v1_ref[...], v0_ref[...]], axis=0)

    col = lax.broadcasted_iota(jnp.int32, (1, KPAD), 1)
    if masked:
        first_ok = jnp.where(c >= 2, 0, jnp.where(c == 1, CHUNK, 2 * CHUNK))
        ok = (col >= first_ok) & (col < KEYS)
    else:
        ok = col < KEYS
    n_pairs = SWA_GROUP // 2
    rows = n_pairs * CHUNK
    row = lax.broadcasted_iota(jnp.int32, (rows, 1), 0)
    out_lane = lax.broadcasted_iota(jnp.int32, (rows, LANES), 1)
    nt = (((1,), (1,)), ((), ()))
    for h in range(SWA_KV_HEADS):
        kk = block_diag(kwin, h)
        vv = block_diag(vwin, h)
        base = h * SWA_GROUP * SWA_HEAD_DIM
        q4 = jnp.concatenate([q_ref[:, base + p * LANES: base + (p + 1) * LANES] for p in range(n_pairs)],
                             axis=0)
        s = lax.dot_general(q4, kk, nt, preferred_element_type=F32) * (SWA_HEAD_DIM ** -0.5)
        ps, invs = [], []
        for half in range(2):
            sh = jnp.where(ok, s[:, half * KPAD:(half + 1) * KPAD], NEG_INF)
            sink = jnp.zeros((rows, 1), F32)
            for p in range(n_pairs):
                sink = jnp.where(row // CHUNK == p, sink_ref[h * SWA_GROUP + 2 * p + half], sink)
            m = jnp.maximum(jnp.max(sh, axis=-1, keepdims=True), sink)
            p_half = jnp.exp(sh - m)
            den = jnp.sum(p_half, axis=-1, keepdims=True) + jnp.exp(sink - m)
            ps.append(p_half.astype(BF16))
            invs.append(1.0 / den)
        pv = jnp.dot(jnp.concatenate(ps, axis=1), vv, preferred_element_type=F32)
        o = pv * jnp.where(out_lane < SWA_HEAD_DIM, invs[0], invs[1])
        for p in range(n_pairs):
            o_ref[:, base + p * LANES: base + (p + 1) * LANES] = o[p * CHUNK:(p + 1) * CHUNK].astype(BF16)


def _swa(sinks, sq, k_arrs, v_arrs, k_maps, v_maps, *, n_seq, nc, masked):
    R = sq.shape[0]
    kv_specs = [pl.BlockSpec((CHUNK, SWA_KV_W), m) for m in (*k_maps, *v_maps)]
    return pl.pallas_call(
        functools.partial(_swa_kernel, masked=masked),
        grid=(n_seq, nc),
        in_specs=[pl.BlockSpec(memory_space=pltpu.SMEM),
                  pl.BlockSpec((CHUNK, SWA_Q_W), lambda b, c: (b * nc + c, 0)),
                  *kv_specs],
        out_specs=pl.BlockSpec((CHUNK, SWA_Q_W), lambda b, c: (b * nc + c, 0)),
        out_shape=jax.ShapeDtypeStruct((R, SWA_Q_W), BF16),
        compiler_params=_cparams(("parallel", "arbitrary")),
        name="swa",
    )(sinks, sq, *k_arrs, *v_arrs)


def _route(logits):
    tm = logits.shape[0]
    lane = lax.broadcasted_iota(jnp.int32, (tm, LANES), 1)
    is_g = lane < N_GROUPS
    gl = jnp.where(is_g, logits, NEG_INF)
    gmax = jnp.max(gl, axis=-1, keepdims=True)
    gidx = jnp.min(jnp.where(gl == gmax, lane, LANES), axis=-1, keepdims=True)
    gsum = jnp.sum(jnp.where(is_g, jnp.exp(gl - gmax), 0.0), axis=-1, keepdims=True)
    g_w = 1.0 / gsum
    base = N_GROUPS + EXPERTS_PER_GROUP * gidx
    el = jnp.where((lane >= base) & (lane < base + EXPERTS_PER_GROUP), logits, NEG_INF)
    v1 = jnp.max(el, axis=-1, keepdims=True)
    i1 = jnp.min(jnp.where(el == v1, lane, LANES), axis=-1, keepdims=True)
    el2 = jnp.where(lane == i1, NEG_INF, el)
    v2 = jnp.max(el2, axis=-1, keepdims=True)
    i2 = jnp.min(jnp.where(el2 == v2, lane, LANES), axis=-1, keepdims=True)
    e2 = jnp.exp(v2 - v1)
    den = 1.0 + e2
    w1 = g_w / den
    w2 = g_w * e2 / den
    l1 = i1 - base
    l2 = i2 - base
    first_lo = l1 < l2
    la = jnp.where(first_lo, l1, l2)
    lb = jnp.where(first_lo, l2, l1)
    wa = jnp.where(first_lo, w1, w2)
    wb = jnp.where(first_lo, w2, w1)
    pair = jnp.where(la == 0, 0, jnp.where(la == 1, 3, 5)) + (lb - la - 1)
    cls = (gidx * N_PAIRS + pair).astype(F32)
    return jnp.where(lane == 0, wa, jnp.where(lane == 1, wb, jnp.where(lane == 2, cls, 0.0)))


def _merge_kernel(x_ref, r_ref, o_ref, gate_ref, mod_ref, g2_ref, wrb_ref, wsb_ref, wout_ref, wr_ref, br_ref,
                  x1_ref, text_ref, meta_ref, *, n_seq):
    tm = x_ref.shape[0]
    g_r = gate_ref[:, :D_MODEL].astype(F32)
    g_s = gate_ref[:, D_MODEL:].astype(F32)
    merged = (g_r * jnp.dot(r_ref[...], wrb_ref[...], preferred_element_type=F32)
              + g_s * jnp.dot(o_ref[...], wsb_ref[...], preferred_element_type=F32))
    mix = jnp.dot(merged.astype(BF16), wout_ref[...], preferred_element_type=F32)
    gt1 = mod_ref[:, 2, :]
    x1 = (x_ref[...].reshape(n_seq, tm // n_seq, D_MODEL) + gt1[:, None, :]
          * mix.reshape(n_seq, tm // n_seq, D_MODEL)).reshape(tm, D_MODEL)
    x1_ref[...] = x1
    t = _modulated_norm(x1, g2_ref[...], mod_ref[:, 3, :], mod_ref[:, 4, :], n_seq)
    logits = jnp.dot(t.astype(BF16), wr_ref[...], preferred_element_type=F32) + br_ref[...]
    meta = _route(logits)
    text_ref[:, :D_MODEL] = t
    text_ref[:, D_MODEL:] = meta
    meta_ref[...] = meta


def _merge(x2d, r, o_swa, gates, mod3, g2, wrb, wsb, wout, wr, br, *, seq_len, tm):
    R = x2d.shape[0]
    if seq_len >= tm:
        n_seq, tps = 1, seq_len // tm
        mod_map = lambda i: (i // tps, 0, 0)
    else:
        n_seq = tm // seq_len
        mod_map = lambda i: (i, 0, 0)
    row = lambda w: pl.BlockSpec((tm, w), lambda i: (i, 0))
    return pl.pallas_call(
        functools.partial(_merge_kernel, n_seq=n_seq),
        grid=(R // tm,),
        in_specs=[row(D_MODEL), row(RET_W), row(SWA_Q_W), row(2 * D_MODEL),
                  pl.BlockSpec((n_seq, N_MOD, D_MODEL), mod_map),
                  _resident((1, D_MODEL)),
                  _resident((RET_W, D_MODEL)), _resident((SWA_Q_W, D_MODEL)), _resident((D_MODEL, D_MODEL)),
                  _resident((D_MODEL, LANES)), _resident((1, LANES))],
        out_specs=[row(D_MODEL), row(D_MODEL + LANES), row(LANES)],
        out_shape=[jax.ShapeDtypeStruct((R, D_MODEL), F32),
                   jax.ShapeDtypeStruct((R, D_MODEL + LANES), F32),
                   jax.ShapeDtypeStruct((R, LANES), F32)],
        compiler_params=_cparams(("parallel",)),
        name="merge",
    )(x2d, r, o_swa, gates, mod3, g2, wrb, wsb, wout, wr, br)


PLAN_BLK = 2048
TILE_ROWS = 256


def _plan_kernel(meta_a_ref, meta_b_ref, pos_ref, tile_ref, pad_ref, cnt_scr, offs_scr, carry_scr, tri_scr,
                 *, tm, nb_a):
    ph = pl.program_id(0)
    b = pl.program_id(1)
    blk = meta_a_ref.shape[0]
    lane = lax.broadcasted_iota(jnp.int32, (blk, LANES), 1)
    cls_col = jnp.where(b < nb_a, meta_a_ref[:, 2:3], meta_b_ref[:, 2:3])
    oh = jnp.where(cls_col == lane.astype(F32), 1.0, 0.0)

    @pl.when((ph == 0) & (b == 0))
    def _():
        cnt_scr[...] = jnp.zeros_like(cnt_scr)
        ri = lax.broadcasted_iota(jnp.int32, (blk, blk), 0)
        ci = lax.broadcasted_iota(jnp.int32, (blk, blk), 1)
        tri_scr[...] = jnp.where(ci <= ri, 1.0, 0.0).astype(BF16)

    @pl.when(ph == 0)
    def _():
        cnt_scr[...] += jnp.sum(oh, axis=0, keepdims=True)

    @pl.when((ph == 1) & (b == 0))
    def _():
        cnt = cnt_scr[...]
        ptiles = jnp.floor((cnt + (tm - 1)) * (1.0 / tm))
        ri = lax.broadcasted_iota(jnp.int32, (LANES, LANES), 0)
        ci = lax.broadcasted_iota(jnp.int32, (LANES, LANES), 1)
        before = jnp.where(ri < ci, 1.0, 0.0).astype(BF16)
        offs = jnp.dot(ptiles.astype(BF16), before, preferred_element_type=F32) * tm
        offs_scr[...] = offs
        carry_scr[...] = jnp.zeros_like(carry_scr)
        padded = ptiles * tm
        ends = offs + padded
        tl = lax.broadcasted_iota(jnp.int32, (TILE_ROWS, LANES), 1)
        tstart = lax.broadcasted_iota(jnp.int32, (TILE_ROWS, LANES), 0).astype(F32) * tm
        tcls = jnp.sum(jnp.where((ends[0:1, :] <= tstart) & (tl < N_CLASSES), 1.0, 0.0), axis=1, keepdims=True)
        tcls = jnp.minimum(tcls, N_CLASSES - 1.0)
        total = jnp.max(ends[0:1, :], axis=1, keepdims=True)
        grp = (jnp.where(tcls >= N_PAIRS, 1.0, 0.0) + jnp.where(tcls >= 2 * N_PAIRS, 1.0, 0.0)
               + jnp.where(tcls >= 3 * N_PAIRS, 1.0, 0.0))
        pair = tcls - N_PAIRS * grp
        la = jnp.where(pair >= 3, 1.0, 0.0) + jnp.where(pair >= 5, 1.0, 0.0)
        lb = jnp.where(pair == 0, 1.0, jnp.where((pair == 1) | (pair == 3), 2.0, 3.0))
        ea = EXPERTS_PER_GROUP * grp + la
        eb = EXPERTS_PER_GROUP * grp + lb
        n_used = total * (1.0 / tm)
        tile_ref[...] = jnp.where(tl == 0, ea, jnp.where(tl == 1, eb, jnp.where(tl == 2, n_used, 0.0))
                                  ).astype(jnp.int32)
        npad = padded - cnt
        pstart = jnp.dot(npad.astype(BF16), before, preferred_element_type=F32)
        n_class_pad = jnp.sum(npad[0:1, :], axis=1, keepdims=True)
        rows = pad_ref.shape[0]
        v = (lax.broadcasted_iota(jnp.int32, (rows, LANES), 0) * LANES
             + lax.broadcasted_iota(jnp.int32, (rows, LANES), 1)).astype(F32)
        slot = jnp.where(v >= n_class_pad, total - n_class_pad + v, 0.0)
        for c in range(N_CLASSES):
            ps = pstart[0:1, c:c + 1]
            inside = (v >= ps) & (v < ps + npad[0:1, c:c + 1])
            slot = jnp.where(inside, offs[0:1, c:c + 1] + cnt[0:1, c:c + 1] - ps + v, slot)
        pad_ref[...] = slot.astype(jnp.int32)

    @pl.when(ph == 1)
    def _():
        incl = jnp.dot(tri_scr[...], oh.astype(BF16), preferred_element_type=F32)
        base = offs_scr[0:1, :] + carry_scr[0:1, :]
        pos = jnp.sum(oh * (base + incl - oh), axis=1, keepdims=True)
        pos_ref[...] = jnp.broadcast_to(pos, (blk, LANES)).astype(jnp.int32)
        carry_scr[...] += jnp.sum(oh, axis=0, keepdims=True)


def _plan(meta_a, meta_b, *, tm):
    na, nb = meta_a.shape[0], meta_b.shape[0]
    n = na + nb
    n_tiles = n // tm + N_CLASSES
    n_free = N_CLASSES * tm
    assert n_tiles <= TILE_ROWS and na % PLAN_BLK == 0 and nb % PLAN_BLK == 0 and n_free % LANES == 0
    n_slots = n_tiles * tm
    nb_a = na // PLAN_BLK
    nb_b = nb // PLAN_BLK
    pos2d, tile2d, pad2d = pl.pallas_call(
        functools.partial(_plan_kernel, tm=tm, nb_a=nb_a),
        grid=(2, nb_a + nb_b),
        in_specs=[pl.BlockSpec((PLAN_BLK, LANES), lambda ph, b: (jnp.minimum(b, nb_a - 1), 0)),
                  pl.BlockSpec((PLAN_BLK, LANES), lambda ph, b: (jnp.maximum(b - nb_a, 0), 0))],
        out_specs=[pl.BlockSpec((PLAN_BLK, LANES), lambda ph, b: (b * ph, 0)),
                   pl.BlockSpec((TILE_ROWS, LANES), lambda ph, b: (0, 0)),
                   pl.BlockSpec((n_free // LANES, LANES), lambda ph, b: (0, 0))],
        out_shape=[jax.ShapeDtypeStruct((n, LANES), jnp.int32),
                   jax.ShapeDtypeStruct((TILE_ROWS, LANES), jnp.int32),
                   jax.ShapeDtypeStruct((n_free // LANES, LANES), jnp.int32)],
        scratch_shapes=[pltpu.VMEM((8, LANES), F32), pltpu.VMEM((8, LANES), F32), pltpu.VMEM((8, LANES), F32),
                        pltpu.VMEM((PLAN_BLK, PLAN_BLK), BF16)],
        compiler_params=_cparams(("arbitrary", "arbitrary")),
        name="plan",
    )(meta_a, meta_b)
    return pos2d[:, 0], pad2d.reshape(-1), tile2d[:n_tiles, 0], tile2d[:n_tiles, 1], tile2d[0:1, 2], n_slots


def _scatter_kernel(pos_ref, text_a_ref, text_b_ref, out_ref, zero_scr, sem, *, nb_a, nb_b):
    i = pl.program_id(0)
    tb = text_a_ref.shape[0]

    @pl.when(i == 0)
    def _():
        zero_scr[...] = jnp.zeros_like(zero_scr)

    def scatter_rows(src_ref):
        def body(r, carry):
            dst = pos_ref[0, 0, r]
            pltpu.make_async_copy(src_ref.at[pl.ds(r, 1), :], out_ref.at[pl.ds(dst, 1), :], sem).start()
            return carry

        lax.fori_loop(0, tb, body, 0, unroll=8)
        pltpu.make_async_copy(src_ref, out_ref.at[pl.ds(0, tb), :], sem).wait()

    @pl.when(i < nb_a)
    def _():
        scatter_rows(text_a_ref)

    @pl.when((i >= nb_a) & (i < nb_a + nb_b))
    def _():
        scatter_rows(text_b_ref)

    @pl.when(i >= nb_a + nb_b)
    def _():
        scatter_rows(zero_scr)


def _scatter_rows(text_a, text_b, pos_ext, *, tb):
    w = text_a.shape[1]
    nb_a = text_a.shape[0] // tb
    nb_b = text_b.shape[0] // tb
    n_steps = pos_ext.shape[0] // tb
    return pl.pallas_call(
        functools.partial(_scatter_kernel, nb_a=nb_a, nb_b=nb_b),
        grid=(n_steps,),
        in_specs=[pl.BlockSpec((1, 1, tb), lambda i: (i, 0, 0), memory_space=pltpu.SMEM),
                  pl.BlockSpec((tb, w), lambda i: (jnp.minimum(i, nb_a - 1), 0)),
                  pl.BlockSpec((tb, w), lambda i: (jnp.clip(i - nb_a, 0, nb_b - 1), 0))],
        out_specs=pl.BlockSpec(memory_space=pl.ANY),
        out_shape=jax.ShapeDtypeStruct((pos_ext.shape[0], w), text_a.dtype),
        scratch_shapes=[pltpu.VMEM((tb, w), text_a.dtype), pltpu.SemaphoreType.DMA(())],
        compiler_params=_cparams(("arbitrary",)),
        name="scatter_rows",
    )(pos_ext.reshape(n_steps, 1, tb), text_a, text_b)


def _moe_kernel(ea_ref, eb_ref, nused_ref, xs_ref, w1a_ref, w3a_ref, w2a_ref, w1b_ref, w3b_ref, w2b_ref, y_ref):
    i = pl.program_id(0)

    @pl.when(i >= nused_ref[0])
    def _():
        y_ref[...] = jnp.zeros_like(y_ref)

    @pl.when(i < nused_ref[0])
    def _():
        x = xs_ref[:, :D_MODEL].astype(BF16)
        wa = xs_ref[:, D_MODEL:D_MODEL + 1]
        wb = xs_ref[:, D_MODEL + 1:D_MODEL + 2]

        def hidden(w1_ref, w3_ref, gate):
            a = jnp.dot(x, w1_ref[0], preferred_element_type=F32)
            b = jnp.dot(x, w3_ref[0], preferred_element_type=F32)
            return (a * jax.nn.sigmoid(a) * b * gate).astype(BF16)

        ha = hidden(w1a_ref, w3a_ref, wa)
        hb = hidden(w1b_ref, w3b_ref, wb)
        y_ref[...] = (jnp.dot(ha, w2a_ref[0], preferred_element_type=F32)
                      + jnp.dot(hb, w2b_ref[0], preferred_element_type=F32))


def _moe(tile_ea, tile_eb, n_used, xs, w1, w3, w2, *, tm, n_tiles):
    last = lambda i, nu: jnp.minimum(i, nu[0] - 1)
    wa_map = lambda i, ea, eb, nu: (ea[last(i, nu)], 0, 0)
    wb_map = lambda i, ea, eb, nu: (eb[last(i, nu)], 0, 0)
    row_map = lambda i, ea, eb, nu: (last(i, nu), 0)
    up = (1, D_MODEL, D_EXPERT)
    down = (1, D_EXPERT, D_MODEL)
    grid_spec = pltpu.PrefetchScalarGridSpec(
        num_scalar_prefetch=3,
        grid=(n_tiles,),
        in_specs=[pl.BlockSpec((tm, D_MODEL + LANES), row_map),
                  pl.BlockSpec(up, wa_map), pl.BlockSpec(up, wa_map), pl.BlockSpec(down, wa_map),
                  pl.BlockSpec(up, wb_map), pl.BlockSpec(up, wb_map), pl.BlockSpec(down, wb_map)],
        out_specs=pl.BlockSpec((tm, D_MODEL), lambda i, ea, eb, nu: (i, 0)),
    )
    return pl.pallas_call(
        _moe_kernel,
        grid_spec=grid_spec,
        out_shape=jax.ShapeDtypeStruct((n_tiles * tm, D_MODEL), F32),
        compiler_params=_cparams(("arbitrary",)),
        name="moe",
    )(tile_ea, tile_eb, n_used, xs, w1, w3, w2, w1, w3, w2)


def _final_kernel(pos_ref, pos_next_ref, x1_ref, mod_ref, g_ref, ys_ref, o_ref, ybuf, sem, *, n_seq):
    i = pl.program_id(0)
    n = pl.num_programs(0)
    tm = x1_ref.shape[0]
    slot = i % 2

    def start_gather(idx_ref, s):
        def body(r, carry):
            src = idx_ref[0, 0, r]
            pltpu.make_async_copy(ys_ref.at[pl.ds(src, 1), :], ybuf.at[s, pl.ds(r, 1), :], sem.at[s]).start()
            return carry

        lax.fori_loop(0, tm, body, 0, unroll=8)

    @pl.when(i == 0)
    def _():
        start_gather(pos_ref, 0)

    @pl.when(i + 1 < n)
    def _():
        start_gather(pos_next_ref, 1 - slot)

    pltpu.make_async_copy(ys_ref.at[pl.ds(0, tm), :], ybuf.at[slot], sem.at[slot]).wait()
    gt2 = mod_ref[:, 5, :]
    x2 = (x1_ref[...].reshape(n_seq, tm // n_seq, D_MODEL)
          + gt2[:, None, :] * ybuf[slot].reshape(n_seq, tm // n_seq, D_MODEL)).reshape(tm, D_MODEL)
    ms = jnp.mean(x2 * x2, axis=-1, keepdims=True)
    o_ref[...] = x2 * lax.rsqrt(ms + EPS) * g_ref[...]


def _final(x1, y_sorted, pos, mod3, gf, *, seq_len, tm):
    R = x1.shape[0]
    if seq_len >= tm:
        n_seq, tps = 1, seq_len // tm
        mod_map = lambda i: (i // tps, 0, 0)
    else:
        n_seq = tm // seq_len
        mod_map = lambda i: (i, 0, 0)
    n = R // tm
    row = pl.BlockSpec((tm, D_MODEL), lambda i: (i, 0))
    pos3 = pos.reshape(n, 1, tm)
    return pl.pallas_call(
        functools.partial(_final_kernel, n_seq=n_seq),
        grid=(n,),
        in_specs=[pl.BlockSpec((1, 1, tm), lambda i: (i, 0, 0), memory_space=pltpu.SMEM),
                  pl.BlockSpec((1, 1, tm), lambda i: (jnp.minimum(i + 1, n - 1), 0, 0), memory_space=pltpu.SMEM),
                  row, pl.BlockSpec((n_seq, N_MOD, D_MODEL), mod_map), _resident((1, D_MODEL)),
                  pl.BlockSpec(memory_space=pl.ANY)],
        out_specs=row,
        out_shape=jax.ShapeDtypeStruct((R, D_MODEL), F32),
        scratch_shapes=[pltpu.VMEM((2, tm, D_MODEL), F32), pltpu.SemaphoreType.DMA((2,))],
        compiler_params=_cparams(("arbitrary",)),
        name="final",
    )(pos3, pos3, x1, mod3, gf, y_sorted)


def _rope_tables(pos):
    half = RET_DK // 2
    inv = ROPE_BASE ** (-jnp.arange(half, dtype=F32) / half)
    ang = pos.astype(F32)[:, None] * inv[None, :]
    cos = jnp.cos(ang)
    sin = jnp.sin(ang)
    return jnp.concatenate([cos, cos], axis=-1), jnp.concatenate([-sin, sin], axis=-1)


def kernel(x_prompt, x_sample, cache_ret_state, cache_swa_k, cache_swa_v, c_prompt, c_sample,
           norm1_g, norm2_g, ada_w, ada_b, w_in, ret_gn_g, swa_sinks, w_ret_branch, w_swa_branch, w_out,
           router_group_w, router_group_b, router_expert_w, router_expert_b,
           expert_w1, expert_w3, expert_w2, final_norm_g):
    depth = w_in.shape[0]
    assert depth == 1
    bp, tp, _ = x_prompt.shape
    bs, ts, _ = x_sample.shape
    past = WINDOW
    assert cache_swa_k.shape[2] == past and ts == CHUNK and tp % 512 == 0
    tm = 512

    l = 0
    wi = w_in[l].astype(BF16)
    wret = wi[:, :4 * RET_W]
    wsq = wi[:, 4 * RET_W:4 * RET_W + SWA_Q_W]
    wkv = wi[:, 4 * RET_W + SWA_Q_W:4 * RET_W + SWA_Q_W + 2 * SWA_KV_W]
    wbg = wi[:, 4 * RET_W + SWA_Q_W + 2 * SWA_KV_W:]
    wrb = w_ret_branch[l].astype(BF16)
    wsb = w_swa_branch[l].astype(BF16)
    wo = w_out[l].astype(BF16)
    n_r = N_GROUPS + N_EXPERTS
    wr = jnp.zeros((D_MODEL, LANES), F32).at[:, :N_GROUPS].set(router_group_w[l]).at[:, N_GROUPS:n_r].set(
        router_expert_w[l]).astype(BF16)
    br = jnp.zeros((1, LANES), F32).at[0, :N_GROUPS].set(router_group_b[l]).at[0, N_GROUPS:n_r].set(
        router_expert_b[l])
    w1 = expert_w1[l].astype(BF16)
    w3 = expert_w3[l].astype(BF16)
    w2 = expert_w2[l].astype(BF16)
    g1 = norm1_g[l].reshape(1, D_MODEL)
    g2 = norm2_g[l].reshape(1, D_MODEL)
    gn = ret_gn_g[l].reshape(1, RET_W)
    gf = final_norm_g.reshape(1, D_MODEL)
    sinks = swa_sinks[l]

    c_all = jnp.concatenate([c_prompt, c_sample], axis=0)
    mod = _ada(c_all, ada_w[l], ada_b[l]).reshape(bp + bs, N_MOD, D_MODEL)
    mod_p, mod_s = mod[:bp], mod[bp:]

    cos_p, sin_p = _rope_tables(jnp.arange(tp))
    cos_s, sin_s = _rope_tables(PAST_LEN + jnp.arange(ts))
    rep = tm // ts
    cos_s, sin_s = jnp.tile(cos_s, (rep, 1)), jnp.tile(sin_s, (rep, 1))

    xp = x_prompt.reshape(bp * tp, D_MODEL)
    xs = x_sample.reshape(bs * ts, D_MODEL)

    ret_p, sq_p, kv_p, gate_p = _inproj(xp, mod_p, g1, cos_p, sin_p, wret, wsq, wkv, wbg, seq_len=tp, tm=tm)
    ret_s, sq_s, kv_s, gate_s = _inproj(xs, mod_s, g1, cos_s, sin_s, wret, wsq, wkv, wbg, seq_len=ts, tm=tm)

    lc_p = 128
    s0_p = jnp.zeros((bp, RET_HEADS, RET_DK, RET_DV), F32)
    r_p, state_p = _retention(ret_p, s0_p, gn, n_seq=bp, seq_len=tp, lc=lc_p)
    r_s, state_s = _retention(ret_s, cache_ret_state[l].astype(F32), gn, n_seq=bs, seq_len=ts, lc=ts)

    nc_p = tp // CHUNK
    kmap = lambda back, colblk: (lambda b, c: (b * nc_p + jnp.maximum(c - back, 0), colblk))
    o_p = _swa(sinks, sq_p, [kv_p] * 3, [kv_p] * 3,
               [kmap(2, 0), kmap(1, 0), kmap(0, 0)], [kmap(2, 1), kmap(1, 1), kmap(0, 1)],
               n_seq=bp, nc=nc_p, masked=True)
    ck = cache_swa_k[l].reshape(bs * past, SWA_KV_W)
    cv = cache_swa_v[l].reshape(bs * past, SWA_KV_W)
    cmap = lambda blk: (lambda b, c: (2 * b + blk, 0))
    o_s = _swa(sinks, sq_s, [ck, ck, kv_s], [cv, cv, kv_s],
               [cmap(0), cmap(1), lambda b, c: (b, 0)], [cmap(0), cmap(1), lambda b, c: (b, 1)],
               n_seq=bs, nc=1, masked=False)

    tm_m = 256
    n_p = bp * tp
    x1_p, text_p, meta_p = _merge(xp, r_p, o_p, gate_p, mod_p, g2, wrb, wsb, wo, wr, br, seq_len=tp, tm=tm_m)
    x1_s, text_s, meta_s = _merge(xs, r_s, o_s, gate_s, mod_s, g2, wrb, wsb, wo, wr, br, seq_len=ts, tm=tm_m)

    tm_e = 256
    pos, free_slots, tile_ea, tile_eb, n_used, n_slots = _plan(meta_p, meta_s, tm=tm_e)
    pos_ext = jnp.concatenate([pos, free_slots], axis=0)
    xsorted = _scatter_rows(text_p, text_s, pos_ext, tb=512)
    y_sorted = _moe(tile_ea, tile_eb, n_used, xsorted, w1, w3, w2, tm=tm_e, n_tiles=n_slots // tm_e)

    tm_f = 512
    out_p = _final(x1_p, y_sorted, pos[:n_p], mod_p, gf, seq_len=tp, tm=tm_f)
    out_s = _final(x1_s, y_sorted, pos[n_p:], mod_s, gf, seq_len=ts, tm=tm_f)

    y_prompt = out_p.reshape(bp, tp, D_MODEL)
    y_sample = out_s.reshape(bs, ts, D_MODEL)
    kvp = kv_p.reshape(bp, tp, 2, SWA_KV_HEADS, SWA_HEAD_DIM)[:, tp - WINDOW:]
    kvs = kv_s.reshape(bs, ts, 2, SWA_KV_HEADS, SWA_HEAD_DIM)
    k_s = jnp.concatenate([cache_swa_k[l].astype(F32), kvs[:, :, 0]], axis=1)[:, -WINDOW:]
    v_s = jnp.concatenate([cache_swa_v[l].astype(F32), kvs[:, :, 1]], axis=1)[:, -WINDOW:]
    return (y_prompt, y_sample, state_p[None], kvp[:, :, 0][None], kvp[:, :, 1][None],
            state_s[None], k_s[None], v_s[None])
```

```python
import functools
import math

import jax
import jax.numpy as jnp
from jax import lax
from jax.experimental import pallas as pl
from jax.experimental.pallas import tpu as pltpu

F32 = jnp.float32
BF16 = jnp.bfloat16

D_MODEL = 2048
CHUNK = 64
RET_HEADS = 8
RET_DK = 128
RET_DV = 128
RET_W = RET_HEADS * RET_DK
ROPE_BASE = 10000.0
SWA_Q_HEADS = 16
SWA_KV_HEADS = 2
SWA_GROUP = SWA_Q_HEADS // SWA_KV_HEADS
SWA_HEAD_DIM = 64
SWA_Q_W = SWA_Q_HEADS * SWA_HEAD_DIM
SWA_KV_W = SWA_KV_HEADS * SWA_HEAD_DIM
WINDOW = 128
PAST_LEN = 1024
N_GROUPS = 4
EXPERTS_PER_GROUP = 4
N_EXPERTS = 16
D_EXPERT = 512
N_MOD = 6
EPS = 1e-6
NEG_INF = -1e30
N_PAIRS = 6
N_CLASSES = N_GROUPS * N_PAIRS

LANES = 128
VMEM_LIMIT = 56 * 1024 * 1024

RET_LOG_GAMMA = tuple(math.log1p(-(2.0 ** (-5.0 - h))) for h in range(RET_HEADS))


def _cparams(sem):
    return pltpu.CompilerParams(dimension_semantics=sem, vmem_limit_bytes=VMEM_LIMIT)


def _resident(shape):
    nd = len(shape)
    return pl.BlockSpec(shape, lambda *_: (0,) * nd, pipeline_mode=pl.Buffered(1))


def _ada_kernel(c_ref, w_ref, b_ref, o_ref):
    c = c_ref[...]
    a = c * jax.nn.sigmoid(c)
    o_ref[...] = jnp.dot(a, w_ref[...], preferred_element_type=F32,
                         precision=lax.Precision.HIGHEST) + b_ref[...]


def _ada(c_all, ada_w, ada_b):
    nb = c_all.shape[0]
    n_out = ada_w.shape[1]
    tn = 1024
    return pl.pallas_call(
        _ada_kernel,
        grid=(n_out // tn,),
        in_specs=[pl.BlockSpec((nb, D_MODEL), lambda j: (0, 0)),
                  pl.BlockSpec((D_MODEL, tn), lambda j: (0, j)),
                  pl.BlockSpec((1, tn), lambda j: (0, j))],
        out_specs=pl.BlockSpec((nb, tn), lambda j: (0, j)),
        out_shape=jax.ShapeDtypeStruct((nb, n_out), F32),
        compiler_params=_cparams(("arbitrary",)),
        name="ada",
    )(c_all, ada_w, ada_b.reshape(1, n_out))


def _modulated_norm(x, g, shift, scale, n_seq):
    tm = x.shape[0]
    ms = jnp.mean(x * x, axis=-1, keepdims=True)
    y = x * lax.rsqrt(ms + EPS) * g
    y3 = y.reshape(n_seq, tm // n_seq, D_MODEL)
    h = y3 * (1.0 + scale)[:, None, :] + shift[:, None, :]
    return h.reshape(tm, D_MODEL)


COL_BLK = 1024


def _inproj_ret_kernel(x_ref, mod_ref, g1_ref, cos_ref, sin_ref, wret_ref, ret_ref, h_ref, dq_scr, dk_scr,
                       *, n_seq, lc):
    tm = x_ref.shape[0]

    @pl.when(pl.program_id(0) == 0)
    def _():
        e = ((lax.broadcasted_iota(jnp.int32, (tm, RET_DK), 0) % lc) + 1).astype(F32)
        for hh in range(RET_HEADS):
            dq_scr[hh] = jnp.exp(e * RET_LOG_GAMMA[hh])
            dk_scr[hh] = jnp.exp(-e * RET_LOG_GAMMA[hh]) * (RET_DK ** -0.5)

    h = _modulated_norm(x_ref[...], g1_ref[...], mod_ref[:, 0, :], mod_ref[:, 1, :], n_seq)
    h_ref[...] = h.astype(BF16)
    cos = cos_ref[...]
    sin = sin_ref[...]
    assert COL_BLK == RET_W
    for blk in range(4):
        c0 = blk * COL_BLK
        acc = jnp.dot(h_ref[...], wret_ref[:, c0:c0 + COL_BLK], preferred_element_type=F32)
        if blk >= 2:
            ret_ref[:, c0:c0 + COL_BLK] = acc.astype(BF16)
            continue
        dec_scr = dq_scr if blk == 0 else dk_scr
        for hh in range(RET_HEADS):
            a = acc[:, hh * RET_DK:(hh + 1) * RET_DK]
            r = a * cos + pltpu.roll(a, RET_DK // 2, 1) * sin
            ret_ref[:, c0 + hh * RET_DK:c0 + (hh + 1) * RET_DK] = (r * dec_scr[hh]).astype(BF16)


def _inproj_rest_kernel(h_ref, wsq_ref, wkv_ref, wbg_ref, sq_ref, kv_ref, gate_ref):
    sq_ref[...] = jnp.dot(h_ref[...], wsq_ref[...], preferred_element_type=F32).astype(BF16)
    kv_ref[...] = jnp.dot(h_ref[...], wkv_ref[...], preferred_element_type=F32)
    for blk in range(2 * D_MODEL // COL_BLK):
        c0 = blk * COL_BLK
        acc = jnp.dot(h_ref[...], wbg_ref[:, c0:c0 + COL_BLK], preferred_element_type=F32)
        gate_ref[:, c0:c0 + COL_BLK] = jax.nn.sigmoid(acc).astype(BF16)


def _inproj(x2d, mod3, g1, cos_t, sin_t, wret, wsq, wkv, wbg, *, seq_len, tm, lc):
    R = x2d.shape[0]
    if seq_len >= tm:
        n_seq, tps = 1, seq_len // tm
        mod_map = lambda i: (i // tps, 0, 0)
        tab_map = lambda i: (i % tps, 0)
    else:
        n_seq = tm // seq_len
        mod_map = lambda i: (i, 0, 0)
        tab_map = lambda i: (0, 0)
    row = lambda w: pl.BlockSpec((tm, w), lambda i: (i, 0))
    assert tm % lc == 0
    ret, h = pl.pallas_call(
        functools.partial(_inproj_ret_kernel, n_seq=n_seq, lc=lc),
        grid=(R // tm,),
        in_specs=[row(D_MODEL),
                  pl.BlockSpec((n_seq, N_MOD, D_MODEL), mod_map),
                  _resident((1, D_MODEL)),
                  pl.BlockSpec((tm, LANES), tab_map),
                  pl.BlockSpec((tm, LANES), tab_map),
                  _resident((D_MODEL, 4 * RET_W))],
        out_specs=[row(4 * RET_W), row(D_MODEL)],
        out_shape=[jax.ShapeDtypeStruct((R, 4 * RET_W), BF16),
                   jax.ShapeDtypeStruct((R, D_MODEL), BF16)],
        scratch_shapes=[pltpu.VMEM((RET_HEADS, tm, RET_DK), F32), pltpu.VMEM((RET_HEADS, tm, RET_DK), F32)],
        compiler_params=_cparams(("arbitrary",)),
        name="inproj_ret",
    )(x2d, mod3, g1, cos_t, sin_t, wret)
    sq, kv, gate = pl.pallas_call(
        _inproj_rest_kernel,
        grid=(R // tm,),
        in_specs=[row(D_MODEL),
                  _resident((D_MODEL, SWA_Q_W)),
                  _resident((D_MODEL, 2 * SWA_KV_W)),
                  _resident((D_MODEL, 2 * D_MODEL))],
        out_specs=[row(SWA_Q_W), row(2 * SWA_KV_W), row(2 * D_MODEL)],
        out_shape=[jax.ShapeDtypeStruct((R, SWA_Q_W), BF16),
                   jax.ShapeDtypeStruct((R, 2 * SWA_KV_W), F32),
                   jax.ShapeDtypeStruct((R, 2 * D_MODEL), BF16)],
        compiler_params=_cparams(("parallel",)),
        name="inproj_rest",
    )(h, wsq, wkv, wbg)
    return ret, sq, kv, gate


def _ret_kernel(blk_ref, s0_ref, gn_ref, r_ref, sout_ref, s_scr, *, lc, n_sub):
    c = pl.program_id(1)

    @pl.when(c == 0)
    def _():
        s_scr[...] = s0_ref[0]

    causal = (lax.broadcasted_iota(jnp.int32, (lc, lc), 0) >= lax.broadcasted_iota(jnp.int32, (lc, lc), 1))
    nt = (((1,), (1,)), ((), ()))
    tn = (((0,), (0,)), ((), ()))
    for h in range(RET_HEADS):
        lo, hi = h * RET_DK, (h + 1) * RET_DK
        state = s_scr[h]
        for sub in range(n_sub):
            rows = slice(sub * lc, (sub + 1) * lc)
            q = blk_ref[rows, lo:hi]
            k = blk_ref[rows, RET_W + lo:RET_W + hi]
            v = blk_ref[rows, 2 * RET_W + lo:2 * RET_W + hi]
            g = blk_ref[rows, 3 * RET_W + lo:3 * RET_W + hi].astype(F32)
            s = jnp.where(causal, lax.dot_general(q, k, nt, preferred_element_type=F32), 0.0)
            o = (jnp.dot(s.astype(BF16), v, preferred_element_type=F32)
                 + jnp.dot(q, state.astype(BF16), preferred_element_type=F32))
            state = math.exp(lc * RET_LOG_GAMMA[h]) * (state + lax.dot_general(k, v, tn, preferred_element_type=F32))
            mu = jnp.mean(o, axis=-1, keepdims=True)
            d = o - mu
            var = jnp.mean(d * d, axis=-1, keepdims=True)
            on = d * lax.rsqrt(var + EPS) * gn_ref[:, lo:hi]
            r_ref[rows, lo:hi] = (on * (g * jax.nn.sigmoid(g))).astype(BF16)
        s_scr[h] = state

    @pl.when(c == pl.num_programs(1) - 1)
    def _():
        sout_ref[0] = s_scr[...]


def _retention(ret_all, s0, gn_g, *, n_seq, seq_len, lc, n_sub):
    R = ret_all.shape[0]
    rows = lc * n_sub
    nc = seq_len // rows
    st_spec = pl.BlockSpec((1, RET_HEADS, RET_DK, RET_DV), lambda b, c: (b, 0, 0, 0))
    return pl.pallas_call(
        functools.partial(_ret_kernel, lc=lc, n_sub=n_sub),
        grid=(n_seq, nc),
        in_specs=[pl.BlockSpec((rows, 4 * RET_W), lambda b, c: (b * nc + c, 0)),
                  st_spec,
                  _resident((1, RET_W))],
        out_specs=[pl.BlockSpec((rows, RET_W), lambda b, c: (b * nc + c, 0)), st_spec],
        out_shape=[jax.ShapeDtypeStruct((R, RET_W), BF16),
                   jax.ShapeDtypeStruct((n_seq, RET_HEADS, RET_DK, RET_DV), F32)],
        scratch_shapes=[pltpu.VMEM((RET_HEADS, RET_DK, RET_DV), F32)],
        compiler_params=_cparams(("parallel", "arbitrary")),
        name="retention",
    )(ret_all, s0, gn_g)


KEYS = WINDOW + CHUNK
KPAD = 256


def _swa_kernel(sink_ref, q_ref, k2_ref, k1_ref, k0_ref, v2_ref, v1_ref, v0_ref, o_ref, *, masked):
    c = pl.program_id(1)
    lane = lax.broadcasted_iota(jnp.int32, (KEYS, LANES), 1)
    zpad = jnp.zeros((KPAD - KEYS, LANES), BF16)

    def block_diag(win, h):
        rolled = pltpu.roll(win, SWA_HEAD_DIM, 1)
        lo_src, hi_src = (win, rolled) if h == 0 else (rolled, win)
        a = jnp.where(lane < SWA_HEAD_DIM, lo_src, 0.0).astype(BF16)
        b = jnp.where(lane >= SWA_HEAD_DIM, hi_src, 0.0).astype(BF16)
        return jnp.concatenate([a, zpad, b, zpad], axis=0)

    kwin = jnp.concatenate([k2_ref[...], k1_ref[...], k0_ref[...]], axis=0)
    vwin = jnp.concatenate([v2_ref[...], v1_ref[...], v0_ref[...]], axis=0)

    col = lax.broadcasted_iota(jnp.int32, (1, KPAD), 1)
    if masked:
        first_ok = jnp.where(c >= 2, 0, jnp.where(c == 1, CHUNK, 2 * CHUNK))
        ok = (col >= first_ok) & (col < KEYS)
    else:
        ok = col < KEYS
    n_pairs = SWA_GROUP // 2
    rows = n_pairs * CHUNK
    row = lax.broadcasted_iota(jnp.int32, (rows, 1), 0)
    out_lane = lax.broadcasted_iota(jnp.int32, (rows, LANES), 1)
    nt = (((1,), (1,)), ((), ()))
    for h in range(SWA_KV_HEADS):
        kk = block_diag(kwin, h)
        vv = block_diag(vwin, h)
        base = h * SWA_GROUP * SWA_HEAD_DIM
        q4 = jnp.concatenate([q_ref[:, base + p * LANES: base + (p + 1) * LANES] for p in range(n_pairs)],
                             axis=0)
        s = lax.dot_general(q4, kk, nt, preferred_element_type=F32) * (SWA_HEAD_DIM ** -0.5)
        ps, invs = [], []
        for half in range(2):
            sh = jnp.where(ok, s[:, half * KPAD:(half + 1) * KPAD], NEG_INF)
            sink = jnp.zeros((rows, 1), F32)
            for p in range(n_pairs):
                sink = jnp.where(row // CHUNK == p, sink_ref[h * SWA_GROUP + 2 * p + half], sink)
            m = jnp.maximum(jnp.max(sh, axis=-1, keepdims=True), sink)
            p_half = jnp.exp(sh - m)
            den = jnp.sum(p_half, axis=-1, keepdims=True) + jnp.exp(sink - m)
            ps.append(p_half.astype(BF16))
            invs.append(1.0 / den)
        pv = jnp.dot(jnp.concatenate(ps, axis=1), vv, preferred_element_type=F32)
        o = pv * jnp.where(out_lane < SWA_HEAD_DIM, invs[0], invs[1])
        for p in range(n_pairs):
            o_ref[:, base + p * LANES: base + (p + 1) * LANES] = o[p * CHUNK:(p + 1) * CHUNK].astype(BF16)


def _swa(sinks, sq, k_arrs, v_arrs, k_maps, v_maps, *, n_seq, nc, masked):
    R = sq.shape[0]
    kv_specs = [pl.BlockSpec((CHUNK, SWA_KV_W), m) for m in (*k_maps, *v_maps)]
    return pl.pallas_call(
        functools.partial(_swa_kernel, masked=masked),
        grid=(n_seq, nc),
        in_specs=[pl.BlockSpec(memory_space=pltpu.SMEM),
                  pl.BlockSpec((CHUNK, SWA_Q_W), lambda b, c: (b * nc + c, 0)),
                  *kv_specs],
        out_specs=pl.BlockSpec((CHUNK, SWA_Q_W), lambda b, c: (b * nc + c, 0)),
        out_shape=jax.ShapeDtypeStruct((R, SWA_Q_W), BF16),
        compiler_params=_cparams(("parallel", "arbitrary")),
        name="swa",
    )(sinks, sq, *k_arrs, *v_arrs)


def _route(logits):
    tm = logits.shape[0]
    lane = lax.broadcasted_iota(jnp.int32, (tm, LANES), 1)
    is_g = lane < N_GROUPS
    gl = jnp.where(is_g, logits, NEG_INF)
    gmax = jnp.max(gl, axis=-1, keepdims=True)
    gidx = jnp.min(jnp.where(gl == gmax, lane, LANES), axis=-1, keepdims=True)
    gsum = jnp.sum(jnp.where(is_g, jnp.exp(gl - gmax), 0.0), axis=-1, keepdims=True)
    g_w = 1.0 / gsum
    base = N_GROUPS + EXPERTS_PER_GROUP * gidx
    el = jnp.where((lane >= base) & (lane < base + EXPERTS_PER_GROUP), logits, NEG_INF)
    v1 = jnp.max(el, axis=-1, keepdims=True)
    i1 = jnp.min(jnp.where(el == v1, lane, LANES), axis=-1, keepdims=True)
    el2 = jnp.where(lane == i1, NEG_INF, el)
    v2 = jnp.max(el2, axis=-1, keepdims=True)
    i2 = jnp.min(jnp.where(el2 == v2, lane, LANES), axis=-1, keepdims=True)
    e2 = jnp.exp(v2 - v1)
    den = 1.0 + e2
    w1 = g_w / den
    w2 = g_w * e2 / den
    l1 = i1 - base
    l2 = i2 - base
    first_lo = l1 < l2
    la = jnp.where(first_lo, l1, l2)
    lb = jnp.where(first_lo, l2, l1)
    wa = jnp.where(first_lo, w1, w2)
    wb = jnp.where(first_lo, w2, w1)
    pair = jnp.where(la == 0, 0, jnp.where(la == 1, 3, 5)) + (lb - la - 1)
    cls = (gidx * N_PAIRS + pair).astype(F32)
    return jnp.where(lane == 0, wa, jnp.where(lane == 1, wb, jnp.where(lane == 2, cls, 0.0)))


def _merge_kernel(x_ref, r_ref, o_ref, gate_ref, mod_ref, g2_ref, wrb_ref, wsb_ref, wout_ref, wr_ref, br_ref,
                  x1_ref, meta_ref, *, n_seq):
    tm = x_ref.shape[0]
    g_r = gate_ref[:, :D_MODEL].astype(F32)
    g_s = gate_ref[:, D_MODEL:].astype(F32)
    merged = (g_r * jnp.dot(r_ref[...], wrb_ref[...], preferred_element_type=F32)
              + g_s * jnp.dot(o_ref[...], wsb_ref[...], preferred_element_type=F32))
    mix = jnp.dot(merged.astype(BF16), wout_ref[...], preferred_element_type=F32)
    gt1 = mod_ref[:, 2, :]
    x1 = (x_ref[...].reshape(n_seq, tm // n_seq, D_MODEL) + gt1[:, None, :]
          * mix.reshape(n_seq, tm // n_seq, D_MODEL)).reshape(tm, D_MODEL)
    x1_ref[...] = x1
    t = _modulated_norm(x1, g2_ref[...], mod_ref[:, 3, :], mod_ref[:, 4, :], n_seq)
    logits = jnp.dot(t.astype(BF16), wr_ref[...], preferred_element_type=F32) + br_ref[...]
    meta_ref[...] = _route(logits)


def _merge(x2d, r, o_swa, gates, mod3, g2, wrb, wsb, wout, wr, br, *, seq_len, tm):
    R = x2d.shape[0]
    if seq_len >= tm:
        n_seq, tps = 1, seq_len // tm
        mod_map = lambda i: (i // tps, 0, 0)
    else:
        n_seq = tm // seq_len
        mod_map = lambda i: (i, 0, 0)
    row = lambda w: pl.BlockSpec((tm, w), lambda i: (i, 0))
    return pl.pallas_call(
        functools.partial(_merge_kernel, n_seq=n_seq),
        grid=(R // tm,),
        in_specs=[row(D_MODEL), row(RET_W), row(SWA_Q_W), row(2 * D_MODEL),
                  pl.BlockSpec((n_seq, N_MOD, D_MODEL), mod_map),
                  _resident((1, D_MODEL)),
                  _resident((RET_W, D_MODEL)), _resident((SWA_Q_W, D_MODEL)), _resident((D_MODEL, D_MODEL)),
                  _resident((D_MODEL, LANES)), _resident((1, LANES))],
        out_specs=[row(D_MODEL), row(LANES)],
        out_shape=[jax.ShapeDtypeStruct((R, D_MODEL), F32),
                   jax.ShapeDtypeStruct((R, LANES), F32)],
        compiler_params=_cparams(("parallel",)),
        name="merge",
    )(x2d, r, o_swa, gates, mod3, g2, wrb, wsb, wout, wr, br)


PLAN_BLK = 2048
TILE_ROWS = 256


def _plan_kernel(meta_a_ref, meta_b_ref, pos_ref, tile_ref, pad_ref, cnt_scr, offs_scr, carry_scr, tri_scr,
                 *, tm, nb_a):
    ph = pl.program_id(0)
    b = pl.program_id(1)
    blk = meta_a_ref.shape[0]
    lane = lax.broadcasted_iota(jnp.int32, (blk, LANES), 1)
    cls_col = jnp.where(b < nb_a, meta_a_ref[:, 2:3], meta_b_ref[:, 2:3])
    oh = jnp.where(cls_col == lane.astype(F32), 1.0, 0.0)

    @pl.when((ph == 0) & (b == 0))
    def _():
        cnt_scr[...] = jnp.zeros_like(cnt_scr)
        ri = lax.broadcasted_iota(jnp.int32, (blk, blk), 0)
        ci = lax.broadcasted_iota(jnp.int32, (blk, blk), 1)
        tri_scr[...] = jnp.where(ci <= ri, 1.0, 0.0).astype(BF16)

    @pl.when(ph == 0)
    def _():
        cnt_scr[...] += jnp.sum(oh, axis=0, keepdims=True)

    @pl.when((ph == 1) & (b == 0))
    def _():
        cnt = cnt_scr[...]
        ptiles = jnp.floor((cnt + (tm - 1)) * (1.0 / tm))
        ri = lax.broadcasted_iota(jnp.int32, (LANES, LANES), 0)
        ci = lax.broadcasted_iota(jnp.int32, (LANES, LANES), 1)
        before = jnp.where(ri < ci, 1.0, 0.0).astype(BF16)
        offs = jnp.dot(ptiles.astype(BF16), before, preferred_element_type=F32) * tm
        offs_scr[...] = offs
        carry_scr[...] = jnp.zeros_like(carry_scr)
        padded = ptiles * tm
        ends = offs + padded
        tl = lax.broadcasted_iota(jnp.int32, (TILE_ROWS, LANES), 1)
        tstart = lax.broadcasted_iota(jnp.int32, (TILE_ROWS, LANES), 0).astype(F32) * tm
        tcls = jnp.sum(jnp.where((ends[0:1, :] <= tstart) & (tl < N_CLASSES), 1.0, 0.0), axis=1, keepdims=True)
        tcls = jnp.minimum(tcls, N_CLASSES - 1.0)
        total = jnp.max(ends[0:1, :], axis=1, keepdims=True)
        grp = (jnp.where(tcls >= N_PAIRS, 1.0, 0.0) + jnp.where(tcls >= 2 * N_PAIRS, 1.0, 0.0)
               + jnp.where(tcls >= 3 * N_PAIRS, 1.0, 0.0))
        pair = tcls - N_PAIRS * grp
        la = jnp.where(pair >= 3, 1.0, 0.0) + jnp.where(pair >= 5, 1.0, 0.0)
        lb = jnp.where(pair == 0, 1.0, jnp.where((pair == 1) | (pair == 3), 2.0, 3.0))
        ea = EXPERTS_PER_GROUP * grp + la
        eb = EXPERTS_PER_GROUP * grp + lb
        n_used = total * (1.0 / tm)
        tile_ref[...] = jnp.where(tl == 0, ea, jnp.where(tl == 1, eb, jnp.where(tl == 2, n_used, 0.0))
                                  ).astype(jnp.int32)
        npad = padded - cnt
        pstart = jnp.dot(npad.astype(BF16), before, preferred_element_type=F32)
        n_class_pad = jnp.sum(npad[0:1, :], axis=1, keepdims=True)
        rows = pad_ref.shape[0]
        v = (lax.broadcasted_iota(jnp.int32, (rows, LANES), 0) * LANES
             + lax.broadcasted_iota(jnp.int32, (rows, LANES), 1)).astype(F32)
        slot = jnp.where(v >= n_class_pad, total - n_class_pad + v, 0.0)
        for c in range(N_CLASSES):
            ps = pstart[0:1, c:c + 1]
            inside = (v >= ps) & (v < ps + npad[0:1, c:c + 1])
            slot = jnp.where(inside, offs[0:1, c:c + 1] + cnt[0:1, c:c + 1] - ps + v, slot)
        pad_ref[...] = slot.astype(jnp.int32)

    @pl.when(ph == 1)
    def _():
        incl = jnp.dot(tri_scr[...], oh.astype(BF16), preferred_element_type=F32)
        base = offs_scr[0:1, :] + carry_scr[0:1, :]
        pos = jnp.sum(oh * (base + incl - oh), axis=1, keepdims=True)
        pos_ref[...] = jnp.broadcast_to(pos, (blk, LANES)).astype(jnp.int32)
        carry_scr[...] += jnp.sum(oh, axis=0, keepdims=True)


def _plan(meta_a, meta_b, *, tm):
    na, nb = meta_a.shape[0], meta_b.shape[0]
    n = na + nb
    n_tiles = n // tm + N_CLASSES
    n_free = N_CLASSES * tm
    assert n_tiles <= TILE_ROWS and na % PLAN_BLK == 0 and nb % PLAN_BLK == 0 and n_free % LANES == 0
    n_slots = n_tiles * tm
    nb_a = na // PLAN_BLK
    nb_b = nb // PLAN_BLK
    pos2d, tile2d, pad2d = pl.pallas_call(
        functools.partial(_plan_kernel, tm=tm, nb_a=nb_a),
        grid=(2, nb_a + nb_b),
        in_specs=[pl.BlockSpec((PLAN_BLK, LANES), lambda ph, b: (jnp.minimum(b, nb_a - 1), 0)),
                  pl.BlockSpec((PLAN_BLK, LANES), lambda ph, b: (jnp.maximum(b - nb_a, 0), 0))],
        out_specs=[pl.BlockSpec((PLAN_BLK, LANES), lambda ph, b: (b * ph, 0)),
                   pl.BlockSpec((TILE_ROWS, LANES), lambda ph, b: (0, 0)),
                   pl.BlockSpec((n_free // LANES, LANES), lambda ph, b: (0, 0))],
        out_shape=[jax.ShapeDtypeStruct((n, LANES), jnp.int32),
                   jax.ShapeDtypeStruct((TILE_ROWS, LANES), jnp.int32),
                   jax.ShapeDtypeStruct((n_free // LANES, LANES), jnp.int32)],
        scratch_shapes=[pltpu.VMEM((8, LANES), F32), pltpu.VMEM((8, LANES), F32), pltpu.VMEM((8, LANES), F32),
                        pltpu.VMEM((PLAN_BLK, PLAN_BLK), BF16)],
        compiler_params=_cparams(("arbitrary", "arbitrary")),
        name="plan",
    )(meta_a, meta_b)
    return pos2d[:, 0], pad2d.reshape(-1), tile2d[:n_tiles, 0], tile2d[:n_tiles, 1], tile2d[0:1, 2], n_slots


ROW_W = D_MODEL + LANES


def _scatter_kernel(pos_ref, x1a_ref, meta_a_ref, mod_a_ref, x1b_ref, meta_b_ref, mod_b_ref, g2_ref, out_ref,
                    stage, zero_scr, sem, *, nb_a, nb_b, nseq_a, nseq_b):
    i = pl.program_id(0)
    n = pl.num_programs(0)
    tb = stage.shape[1]
    slot = i % 2

    def wait_slot(s):
        pltpu.make_async_copy(stage.at[s], out_ref.at[pl.ds(0, tb), :], sem.at[s]).wait()

    def scatter_rows(src_ref):
        def body(r, carry):
            dst = pos_ref[0, 0, r]
            pltpu.make_async_copy(src_ref.at[pl.ds(r, 1), :], out_ref.at[pl.ds(dst, 1), :], sem.at[slot]).start()
            return carry

        lax.fori_loop(0, tb, body, 0, unroll=8)

    def build_rows(x1_ref, meta_ref, mod_ref, n_seq):
        t = _modulated_norm(x1_ref[...], g2_ref[...], mod_ref[:, 3, :], mod_ref[:, 4, :], n_seq)
        stage[slot, :, :D_MODEL] = t
        stage[slot, :, D_MODEL:] = meta_ref[...]
        scatter_rows(stage.at[slot])

    @pl.when(i == 0)
    def _():
        zero_scr[...] = jnp.zeros_like(zero_scr)

    @pl.when(i >= 2)
    def _():
        wait_slot(slot)

    @pl.when(i < nb_a)
    def _():
        build_rows(x1a_ref, meta_a_ref, mod_a_ref, nseq_a)

    @pl.when((i >= nb_a) & (i < nb_a + nb_b))
    def _():
        build_rows(x1b_ref, meta_b_ref, mod_b_ref, nseq_b)

    @pl.when(i >= nb_a + nb_b)
    def _():
        scatter_rows(zero_scr)

    @pl.when(i == n - 1)
    def _():
        wait_slot(1 - slot)
        wait_slot(slot)


def _scatter_rows(x1_a, meta_a, mod_a, x1_b, meta_b, mod_b, g2, pos_ext, *, seq_a, seq_b, tb):
    nb_a = x1_a.shape[0] // tb
    nb_b = x1_b.shape[0] // tb
    n_steps = pos_ext.shape[0] // tb
    assert n_steps >= 2 and seq_a % tb == 0 and tb % seq_b == 0
    tps = seq_a // tb
    nseq_b = tb // seq_b
    blk_a = lambda i: jnp.minimum(i, nb_a - 1)
    blk_b = lambda i: jnp.clip(i - nb_a, 0, nb_b - 1)
    return pl.pallas_call(
        functools.partial(_scatter_kernel, nb_a=nb_a, nb_b=nb_b, nseq_a=1, nseq_b=nseq_b),
        grid=(n_steps,),
        in_specs=[pl.BlockSpec((1, 1, tb), lambda i: (i, 0, 0), memory_space=pltpu.SMEM),
                  pl.BlockSpec((tb, D_MODEL), lambda i: (blk_a(i), 0)),
                  pl.BlockSpec((tb, LANES), lambda i: (blk_a(i), 0)),
                  pl.BlockSpec((1, N_MOD, D_MODEL), lambda i: (blk_a(i) // tps, 0, 0)),
                  pl.BlockSpec((tb, D_MODEL), lambda i: (blk_b(i), 0)),
                  pl.BlockSpec((tb, LANES), lambda i: (blk_b(i), 0)),
                  pl.BlockSpec((nseq_b, N_MOD, D_MODEL), lambda i: (blk_b(i), 0, 0)),
                  _resident((1, D_MODEL))],
        out_specs=pl.BlockSpec(memory_space=pl.ANY),
        out_shape=jax.ShapeDtypeStruct((pos_ext.shape[0], ROW_W), F32),
        scratch_shapes=[pltpu.VMEM((2, tb, ROW_W), F32), pltpu.VMEM((tb, ROW_W), F32),
                        pltpu.SemaphoreType.DMA((2,))],
        compiler_params=_cparams(("arbitrary",)),
        name="scatter_rows",
    )(pos_ext.reshape(n_steps, 1, tb), x1_a, meta_a, mod_a, x1_b, meta_b, mod_b, g2)


def _moe_kernel(ea_ref, eb_ref, nused_ref, xs_ref, w1a_ref, w3a_ref, w2a_ref, w1b_ref, w3b_ref, w2b_ref, y_ref):
    i = pl.program_id(0)

    @pl.when(i >= nused_ref[0])
    def _():
        y_ref[...] = jnp.zeros_like(y_ref)

    @pl.when(i < nused_ref[0])
    def _():
        x = xs_ref[:, :D_MODEL].astype(BF16)
        wa = xs_ref[:, D_MODEL:D_MODEL + 1]
        wb = xs_ref[:, D_MODEL + 1:D_MODEL + 2]

        def hidden(w1_ref, w3_ref, gate):
            a = jnp.dot(x, w1_ref[0], preferred_element_type=F32)
            b = jnp.dot(x, w3_ref[0], preferred_element_type=F32)
            return (a * jax.nn.sigmoid(a) * b * gate).astype(BF16)

        ha = hidden(w1a_ref, w3a_ref, wa)
        hb = hidden(w1b_ref, w3b_ref, wb)
        y_ref[...] = (jnp.dot(ha, w2a_ref[0], preferred_element_type=F32)
                      + jnp.dot(hb, w2b_ref[0], preferred_element_type=F32))


def _moe(tile_ea, tile_eb, n_used, xs, w1, w3, w2, *, tm, n_tiles):
    last = lambda i, nu: jnp.minimum(i, nu[0] - 1)
    wa_map = lambda i, ea, eb, nu: (ea[last(i, nu)], 0, 0)
    wb_map = lambda i, ea, eb, nu: (eb[last(i, nu)], 0, 0)
    row_map = lambda i, ea, eb, nu: (last(i, nu), 0)
    up = (1, D_MODEL, D_EXPERT)
    down = (1, D_EXPERT, D_MODEL)
    grid_spec = pltpu.PrefetchScalarGridSpec(
        num_scalar_prefetch=3,
        grid=(n_tiles,),
        in_specs=[pl.BlockSpec((tm, D_MODEL + LANES), row_map),
                  pl.BlockSpec(up, wa_map), pl.BlockSpec(up, wa_map), pl.BlockSpec(down, wa_map),
                  pl.BlockSpec(up, wb_map), pl.BlockSpec(up, wb_map), pl.BlockSpec(down, wb_map)],
        out_specs=pl.BlockSpec((tm, D_MODEL), lambda i, ea, eb, nu: (i, 0)),
    )
    return pl.pallas_call(
        _moe_kernel,
        grid_spec=grid_spec,
        out_shape=jax.ShapeDtypeStruct((n_tiles * tm, D_MODEL), F32),
        compiler_params=_cparams(("arbitrary",)),
        name="moe",
    )(tile_ea, tile_eb, n_used, xs, w1, w3, w2, w1, w3, w2)


def _final_kernel(pos_ref, pos_next_ref, x1_ref, mod_ref, g_ref, ys_ref, o_ref, ybuf, sem, *, n_seq):
    i = pl.program_id(0)
    n = pl.num_programs(0)
    tm = x1_ref.shape[0]
    slot = i % 2

    def start_gather(idx_ref, s):
        def body(r, carry):
            src = idx_ref[0, 0, r]
            pltpu.make_async_copy(ys_ref.at[pl.ds(src, 1), :], ybuf.at[s, pl.ds(r, 1), :], sem.at[s]).start()
            return carry

        lax.fori_loop(0, tm, body, 0, unroll=8)

    @pl.when(i == 0)
    def _():
        start_gather(pos_ref, 0)

    @pl.when(i + 1 < n)
    def _():
        start_gather(pos_next_ref, 1 - slot)

    pltpu.make_async_copy(ys_ref.at[pl.ds(0, tm), :], ybuf.at[slot], sem.at[slot]).wait()
    gt2 = mod_ref[:, 5, :]
    x2 = (x1_ref[...].reshape(n_seq, tm // n_seq, D_MODEL)
          + gt2[:, None, :] * ybuf[slot].reshape(n_seq, tm // n_seq, D_MODEL)).reshape(tm, D_MODEL)
    ms = jnp.mean(x2 * x2, axis=-1, keepdims=True)
    o_ref[...] = x2 * lax.rsqrt(ms + EPS) * g_ref[...]


def _final(x1, y_sorted, pos, mod3, gf, *, seq_len, tm):
    R = x1.shape[0]
    if seq_len >= tm:
        n_seq, tps = 1, seq_len // tm
        mod_map = lambda i: (i // tps, 0, 0)
    else:
        n_seq = tm // seq_len
        mod_map = lambda i: (i, 0, 0)
    n = R // tm
    row = pl.BlockSpec((tm, D_MODEL), lambda i: (i, 0))
    pos3 = pos.reshape(n, 1, tm)
    return pl.pallas_call(
        functools.partial(_final_kernel, n_seq=n_seq),
        grid=(n,),
        in_specs=[pl.BlockSpec((1, 1, tm), lambda i: (i, 0, 0), memory_space=pltpu.SMEM),
                  pl.BlockSpec((1, 1, tm), lambda i: (jnp.minimum(i + 1, n - 1), 0, 0), memory_space=pltpu.SMEM),
                  row, pl.BlockSpec((n_seq, N_MOD, D_MODEL), mod_map), _resident((1, D_MODEL)),
                  pl.BlockSpec(memory_space=pl.ANY)],
        out_specs=row,
        out_shape=jax.ShapeDtypeStruct((R, D_MODEL), F32),
        scratch_shapes=[pltpu.VMEM((2, tm, D_MODEL), F32), pltpu.SemaphoreType.DMA((2,))],
        compiler_params=_cparams(("arbitrary",)),
        name="final",
    )(pos3, pos3, x1, mod3, gf, y_sorted)


def _rope_tables(pos):
    half = RET_DK // 2
    inv = ROPE_BASE ** (-jnp.arange(half, dtype=F32) / half)
    ang = pos.astype(F32)[:, None] * inv[None, :]
    cos = jnp.cos(ang)
    sin = jnp.sin(ang)
    return jnp.concatenate([cos, cos], axis=-1), jnp.concatenate([-sin, sin], axis=-1)


def kernel(x_prompt, x_sample, cache_ret_state, cache_swa_k, cache_swa_v, c_prompt, c_sample,
           norm1_g, norm2_g, ada_w, ada_b, w_in, ret_gn_g, swa_sinks, w_ret_branch, w_swa_branch, w_out,
           router_group_w, router_group_b, router_expert_w, router_expert_b,
           expert_w1, expert_w3, expert_w2, final_norm_g):
    depth = w_in.shape[0]
    assert depth == 1
    bp, tp, _ = x_prompt.shape
    bs, ts, _ = x_sample.shape
    past = WINDOW
    assert cache_swa_k.shape[2] == past and ts == CHUNK and tp % 512 == 0
    tm = 512

    l = 0
    c1 = 4 * RET_W
    c2 = c1 + SWA_Q_W
    c3 = c2 + 2 * SWA_KV_W
    wret = w_in[l, :, :c1].astype(BF16)
    wsq = w_in[l, :, c1:c2].astype(BF16)
    wkv = w_in[l, :, c2:c3].astype(BF16)
    wbg = w_in[l, :, c3:].astype(BF16)
    wrb = w_ret_branch[l].astype(BF16)
    wsb = w_swa_branch[l].astype(BF16)
    wo = w_out[l].astype(BF16)
    n_r = N_GROUPS + N_EXPERTS
    wr = jnp.zeros((D_MODEL, LANES), F32).at[:, :N_GROUPS].set(router_group_w[l]).at[:, N_GROUPS:n_r].set(
        router_expert_w[l]).astype(BF16)
    br = jnp.zeros((1, LANES), F32).at[0, :N_GROUPS].set(router_group_b[l]).at[0, N_GROUPS:n_r].set(
        router_expert_b[l])
    w1 = expert_w1[l].astype(BF16)
    w3 = expert_w3[l].astype(BF16)
    w2 = expert_w2[l].astype(BF16)
    g1 = norm1_g[l].reshape(1, D_MODEL)
    g2 = norm2_g[l].reshape(1, D_MODEL)
    gn = ret_gn_g[l].reshape(1, RET_W)
    gf = final_norm_g.reshape(1, D_MODEL)
    sinks = swa_sinks[l]

    c_all = jnp.concatenate([c_prompt, c_sample], axis=0)
    mod = _ada(c_all, ada_w[l], ada_b[l]).reshape(bp + bs, N_MOD, D_MODEL)
    mod_p, mod_s = mod[:bp], mod[bp:]

    cos_p, sin_p = _rope_tables(jnp.arange(tp))
    cos_s, sin_s = _rope_tables(PAST_LEN + jnp.arange(ts))
    rep = tm // ts
    cos_s, sin_s = jnp.tile(cos_s, (rep, 1)), jnp.tile(sin_s, (rep, 1))

    xp = x_prompt.reshape(bp * tp, D_MODEL)
    xs = x_sample.reshape(bs * ts, D_MODEL)

    lc_p = 128
    ret_p, sq_p, kv_p, gate_p = _inproj(xp, mod_p, g1, cos_p, sin_p, wret, wsq, wkv, wbg, seq_len=tp, tm=tm,
                                        lc=lc_p)
    ret_s, sq_s, kv_s, gate_s = _inproj(xs, mod_s, g1, cos_s, sin_s, wret, wsq, wkv, wbg, seq_len=ts, tm=tm,
                                        lc=ts)

    s0_p = jnp.zeros((bp, RET_HEADS, RET_DK, RET_DV), F32)
    r_p, state_p = _retention(ret_p, s0_p, gn, n_seq=bp, seq_len=tp, lc=lc_p, n_sub=2)
    r_s, state_s = _retention(ret_s, cache_ret_state[l].astype(F32), gn, n_seq=bs, seq_len=ts, lc=ts, n_sub=1)

    nc_p = tp // CHUNK
    kmap = lambda back, colblk: (lambda b, c: (b * nc_p + jnp.maximum(c - back, 0), colblk))
    o_p = _swa(sinks, sq_p, [kv_p] * 3, [kv_p] * 3,
               [kmap(2, 0), kmap(1, 0), kmap(0, 0)], [kmap(2, 1), kmap(1, 1), kmap(0, 1)],
               n_seq=bp, nc=nc_p, masked=True)
    ck = cache_swa_k[l].reshape(bs * past, SWA_KV_W)
    cv = cache_swa_v[l].reshape(bs * past, SWA_KV_W)
    cmap = lambda blk: (lambda b, c: (2 * b + blk, 0))
    o_s = _swa(sinks, sq_s, [ck, ck, kv_s], [cv, cv, kv_s],
               [cmap(0), cmap(1), lambda b, c: (b, 0)], [cmap(0), cmap(1), lambda b, c: (b, 1)],
               n_seq=bs, nc=1, masked=False)

    tm_m = 512
    n_p = bp * tp
    x1_p, meta_p = _merge(xp, r_p, o_p, gate_p, mod_p, g2, wrb, wsb, wo, wr, br, seq_len=tp, tm=tm_m)
    x1_s, meta_s = _merge(xs, r_s, o_s, gate_s, mod_s, g2, wrb, wsb, wo, wr, br, seq_len=ts, tm=tm_m)

    tm_e = 256
    pos, free_slots, tile_ea, tile_eb, n_used, n_slots = _plan(meta_p, meta_s, tm=tm_e)
    pos_ext = jnp.concatenate([pos, free_slots], axis=0)
    xsorted = _scatter_rows(x1_p, meta_p, mod_p, x1_s, meta_s, mod_s, g2, pos_ext, seq_a=tp, seq_b=ts, tb=512)
    y_sorted = _moe(tile_ea, tile_eb, n_used, xsorted, w1, w3, w2, tm=tm_e, n_tiles=n_slots // tm_e)

    tm_f = 512
    out_p = _final(x1_p, y_sorted, pos[:n_p], mod_p, gf, seq_len=tp, tm=tm_f)
    out_s = _final(x1_s, y_sorted, pos[n_p:], mod_s, gf, seq_len=ts, tm=tm_f)

    y_prompt = out_p.reshape(bp, tp, D_MODEL)
    y_sample = out_s.reshape(bs, ts, D_MODEL)
    kvp = kv_p.reshape(bp, tp, 2 * SWA_KV_W)[:, tp - WINDOW:].reshape(bp, WINDOW, 2, SWA_KV_HEADS, SWA_HEAD_DIM)
    kvs = kv_s.reshape(bs, ts, 2, SWA_KV_HEADS, SWA_HEAD_DIM)
    k_s = jnp.concatenate([cache_swa_k[l].astype(F32), kvs[:, :, 0]], axis=1)[:, -WINDOW:]
    v_s = jnp.concatenate([cache_swa_v[l].astype(F32), kvs[:, :, 1]], axis=1)[:, -WINDOW:]
    return (y_prompt, y_sample, state_p[None], kvp[:, :, 0][None], kvp[:, :, 1][None],
            state_s[None], k_s[None], v_s[None])
```

```python
import functools
import math

import jax
import jax.numpy as jnp
from jax import lax
from jax.experimental import pallas as pl
from jax.experimental.pallas import tpu as pltpu

F32 = jnp.float32
BF16 = jnp.bfloat16

D_MODEL = 2048
CHUNK = 64
RET_HEADS = 8
RET_DK = 128
RET_DV = 128
RET_W = RET_HEADS * RET_DK
ROPE_BASE = 10000.0
SWA_Q_HEADS = 16
SWA_KV_HEADS = 2
SWA_GROUP = SWA_Q_HEADS // SWA_KV_HEADS
SWA_HEAD_DIM = 64
SWA_Q_W = SWA_Q_HEADS * SWA_HEAD_DIM
SWA_KV_W = SWA_KV_HEADS * SWA_HEAD_DIM
WINDOW = 128
PAST_LEN = 1024
N_GROUPS = 4
EXPERTS_PER_GROUP = 4
N_EXPERTS = 16
D_EXPERT = 512
N_MOD = 6
EPS = 1e-6
NEG_INF = -1e30
N_PAIRS = 6
N_CLASSES = N_GROUPS * N_PAIRS

LANES = 128
VMEM_LIMIT = 56 * 1024 * 1024

RET_LOG_GAMMA = tuple(math.log1p(-(2.0 ** (-5.0 - h))) for h in range(RET_HEADS))


def _cparams(sem):
    return pltpu.CompilerParams(dimension_semantics=sem, vmem_limit_bytes=VMEM_LIMIT)


def _resident(shape):
    nd = len(shape)
    return pl.BlockSpec(shape, lambda *_: (0,) * nd, pipeline_mode=pl.Buffered(1))


def _ada_kernel(c_ref, w_ref, b_ref, o_ref):
    c = c_ref[...]
    a = c * jax.nn.sigmoid(c)
    o_ref[...] = jnp.dot(a, w_ref[...], preferred_element_type=F32,
                         precision=lax.Precision.HIGHEST) + b_ref[...]


def _ada(c_all, ada_w, ada_b):
    nb = c_all.shape[0]
    n_out = ada_w.shape[1]
    tn = 1024
    return pl.pallas_call(
        _ada_kernel,
        grid=(n_out // tn,),
        in_specs=[pl.BlockSpec((nb, D_MODEL), lambda j: (0, 0)),
                  pl.BlockSpec((D_MODEL, tn), lambda j: (0, j)),
                  pl.BlockSpec((1, tn), lambda j: (0, j))],
        out_specs=pl.BlockSpec((nb, tn), lambda j: (0, j)),
        out_shape=jax.ShapeDtypeStruct((nb, n_out), F32),
        compiler_params=_cparams(("arbitrary",)),
        name="ada",
    )(c_all, ada_w, ada_b.reshape(1, n_out))


def _modulated_norm(x, g, shift, scale, n_seq):
    tm = x.shape[0]
    ms = jnp.mean(x * x, axis=-1, keepdims=True)
    y = x * lax.rsqrt(ms + EPS) * g
    y3 = y.reshape(n_seq, tm // n_seq, D_MODEL)
    h = y3 * (1.0 + scale)[:, None, :] + shift[:, None, :]
    return h.reshape(tm, D_MODEL)


COL_BLK = 1024


def _inproj_ret_kernel(x_ref, mod_ref, g1_ref, cos_ref, sin_ref, wret_ref, ret_ref, h_ref, h_scr, dq_scr, dk_scr,
                       *, n_seq, lc):
    i = pl.program_id(0)
    n = pl.num_programs(0) - 1
    tm = x_ref.shape[0]

    def norm_stage():
        h = _modulated_norm(x_ref[...], g1_ref[...], mod_ref[:, 0, :], mod_ref[:, 1, :], n_seq)
        h_scr[i % 2] = h.astype(BF16)

    def matmul_stage():
        h_prev = h_scr.at[(i + 1) % 2]
        h_ref[...] = h_prev[...]
        cos = cos_ref[...]
        sin = sin_ref[...]
        assert COL_BLK == RET_W
        for blk in range(4):
            c0 = blk * COL_BLK
            acc = jnp.dot(h_prev[...], wret_ref[:, c0:c0 + COL_BLK], preferred_element_type=F32)
            if blk >= 2:
                ret_ref[:, c0:c0 + COL_BLK] = acc.astype(BF16)
                continue
            dec_scr = dq_scr if blk == 0 else dk_scr
            for hh in range(RET_HEADS):
                a = acc[:, hh * RET_DK:(hh + 1) * RET_DK]
                r = a * cos + pltpu.roll(a, RET_DK // 2, 1) * sin
                ret_ref[:, c0 + hh * RET_DK:c0 + (hh + 1) * RET_DK] = (r * dec_scr[hh]).astype(BF16)

    @pl.when(i == 0)
    def _():
        e = ((lax.broadcasted_iota(jnp.int32, (tm, RET_DK), 0) % lc) + 1).astype(F32)
        for hh in range(RET_HEADS):
            dq_scr[hh] = jnp.exp(e * RET_LOG_GAMMA[hh])
            dk_scr[hh] = jnp.exp(-e * RET_LOG_GAMMA[hh]) * (RET_DK ** -0.5)
        norm_stage()

    @pl.when((i > 0) & (i < n))
    def _():
        norm_stage()
        matmul_stage()

    @pl.when(i == n)
    def _():
        matmul_stage()


def _inproj_rest_kernel(h_ref, wsq_ref, wkv_ref, wbg_ref, sq_ref, kv_ref, gate_ref):
    sq_ref[...] = jnp.dot(h_ref[...], wsq_ref[...], preferred_element_type=F32).astype(BF16)
    kv_ref[...] = jnp.dot(h_ref[...], wkv_ref[...], preferred_element_type=F32)
    for blk in range(2 * D_MODEL // COL_BLK):
        c0 = blk * COL_BLK
        acc = jnp.dot(h_ref[...], wbg_ref[:, c0:c0 + COL_BLK], preferred_element_type=F32)
        gate_ref[:, c0:c0 + COL_BLK] = jax.nn.sigmoid(acc).astype(BF16)


def _inproj(x2d, mod3, g1, cos_t, sin_t, wret, wsq, wkv, wbg, *, seq_len, tm, lc):
    R = x2d.shape[0]
    n = R // tm
    if seq_len >= tm:
        n_seq, tps = 1, seq_len // tm
        seq_of = lambda t: t // tps
        tab_of = lambda t: t % tps
    else:
        n_seq = tm // seq_len
        seq_of = lambda t: t
        tab_of = lambda t: 0
    cur = lambda i: jnp.minimum(i, n - 1)
    prev = lambda i: jnp.maximum(i - 1, 0)
    row = lambda w: pl.BlockSpec((tm, w), lambda i: (i, 0))
    row_prev = lambda w: pl.BlockSpec((tm, w), lambda i: (prev(i), 0))
    assert tm % lc == 0
    ret, h = pl.pallas_call(
        functools.partial(_inproj_ret_kernel, n_seq=n_seq, lc=lc),
        grid=(n + 1,),
        in_specs=[pl.BlockSpec((tm, D_MODEL), lambda i: (cur(i), 0)),
                  pl.BlockSpec((n_seq, N_MOD, D_MODEL), lambda i: (seq_of(cur(i)), 0, 0)),
                  _resident((1, D_MODEL)),
                  pl.BlockSpec((tm, LANES), lambda i: (tab_of(prev(i)), 0)),
                  pl.BlockSpec((tm, LANES), lambda i: (tab_of(prev(i)), 0)),
                  _resident((D_MODEL, 4 * RET_W))],
        out_specs=[row_prev(4 * RET_W), row_prev(D_MODEL)],
        out_shape=[jax.ShapeDtypeStruct((R, 4 * RET_W), BF16),
                   jax.ShapeDtypeStruct((R, D_MODEL), BF16)],
        scratch_shapes=[pltpu.VMEM((2, tm, D_MODEL), BF16),
                        pltpu.VMEM((RET_HEADS, tm, RET_DK), F32), pltpu.VMEM((RET_HEADS, tm, RET_DK), F32)],
        compiler_params=_cparams(("arbitrary",)),
        name="inproj_ret",
    )(x2d, mod3, g1, cos_t, sin_t, wret)
    sq, kv, gate = pl.pallas_call(
        _inproj_rest_kernel,
        grid=(R // tm,),
        in_specs=[row(D_MODEL),
                  _resident((D_MODEL, SWA_Q_W)),
                  _resident((D_MODEL, 2 * SWA_KV_W)),
                  _resident((D_MODEL, 2 * D_MODEL))],
        out_specs=[row(SWA_Q_W), row(2 * SWA_KV_W), row(2 * D_MODEL)],
        out_shape=[jax.ShapeDtypeStruct((R, SWA_Q_W), BF16),
                   jax.ShapeDtypeStruct((R, 2 * SWA_KV_W), F32),
                   jax.ShapeDtypeStruct((R, 2 * D_MODEL), BF16)],
        compiler_params=_cparams(("parallel",)),
        name="inproj_rest",
    )(h, wsq, wkv, wbg)
    return ret, sq, kv, gate


def _ret_kernel(blk_ref, s0_ref, gn_ref, r_ref, sout_ref, s_scr, *, lc, n_sub):
    c = pl.program_id(1)

    @pl.when(c == 0)
    def _():
        s_scr[...] = s0_ref[0]

    causal = (lax.broadcasted_iota(jnp.int32, (lc, lc), 0) >= lax.broadcasted_iota(jnp.int32, (lc, lc), 1))
    nt = (((1,), (1,)), ((), ()))
    tn = (((0,), (0,)), ((), ()))
    for h in range(RET_HEADS):
        lo, hi = h * RET_DK, (h + 1) * RET_DK
        state = s_scr[h]
        for sub in range(n_sub):
            rows = slice(sub * lc, (sub + 1) * lc)
            q = blk_ref[rows, lo:hi]
            k = blk_ref[rows, RET_W + lo:RET_W + hi]
            v = blk_ref[rows, 2 * RET_W + lo:2 * RET_W + hi]
            g = blk_ref[rows, 3 * RET_W + lo:3 * RET_W + hi].astype(F32)
            s = jnp.where(causal, lax.dot_general(q, k, nt, preferred_element_type=F32), 0.0)
            o = (jnp.dot(s.astype(BF16), v, preferred_element_type=F32)
                 + jnp.dot(q, state.astype(BF16), preferred_element_type=F32))
            state = math.exp(lc * RET_LOG_GAMMA[h]) * (state + lax.dot_general(k, v, tn, preferred_element_type=F32))
            mu = jnp.mean(o, axis=-1, keepdims=True)
            d = o - mu
            var = jnp.mean(d * d, axis=-1, keepdims=True)
            on = d * lax.rsqrt(var + EPS) * gn_ref[:, lo:hi]
            r_ref[rows, lo:hi] = (on * (g * jax.nn.sigmoid(g))).astype(BF16)
        s_scr[h] = state

    @pl.when(c == pl.num_programs(1) - 1)
    def _():
        sout_ref[0] = s_scr[...]


def _retention(ret_all, s0, gn_g, *, n_seq, seq_len, lc, n_sub):
    R = ret_all.shape[0]
    rows = lc * n_sub
    nc = seq_len // rows
    st_spec = pl.BlockSpec((1, RET_HEADS, RET_DK, RET_DV), lambda b, c: (b, 0, 0, 0))
    return pl.pallas_call(
        functools.partial(_ret_kernel, lc=lc, n_sub=n_sub),
        grid=(n_seq, nc),
        in_specs=[pl.BlockSpec((rows, 4 * RET_W), lambda b, c: (b * nc + c, 0)),
                  st_spec,
                  _resident((1, RET_W))],
        out_specs=[pl.BlockSpec((rows, RET_W), lambda b, c: (b * nc + c, 0)), st_spec],
        out_shape=[jax.ShapeDtypeStruct((R, RET_W), BF16),
                   jax.ShapeDtypeStruct((n_seq, RET_HEADS, RET_DK, RET_DV), F32)],
        scratch_shapes=[pltpu.VMEM((RET_HEADS, RET_DK, RET_DV), F32)],
        compiler_params=_cparams(("parallel", "arbitrary")),
        name="retention",
    )(ret_all, s0, gn_g)


KEYS = WINDOW + CHUNK
KPAD = 256


def _swa_kernel(sink_ref, q_ref, k2_ref, k1_ref, k0_ref, v2_ref, v1_ref, v0_ref, o_ref, *, masked):
    c = pl.program_id(1)
    lane = lax.broadcasted_iota(jnp.int32, (KEYS, LANES), 1)
    zpad = jnp.zeros((KPAD - KEYS, LANES), BF16)

    def block_diag(win, h):
        rolled = pltpu.roll(win, SWA_HEAD_DIM, 1)
        lo_src, hi_src = (win, rolled) if h == 0 else (rolled, win)
        a = jnp.where(lane < SWA_HEAD_DIM, lo_src, 0.0).astype(BF16)
        b = jnp.where(lane >= SWA_HEAD_DIM, hi_src, 0.0).astype(BF16)
        return jnp.concatenate([a, zpad, b, zpad], axis=0)

    kwin = jnp.concatenate([k2_ref[...], k1_ref[...], k0_ref[...]], axis=0)
    vwin = jnp.concatenate([v2_ref[...], v1_ref[...], v0_ref[...]], axis=0)

    col = lax.broadcasted_iota(jnp.int32, (1, KPAD), 1)
    if masked:
        first_ok = jnp.where(c >= 2, 0, jnp.where(c == 1, CHUNK, 2 * CHUNK))
        ok = (col >= first_ok) & (col < KEYS)
    else:
        ok = col < KEYS
    n_pairs = SWA_GROUP // 2
    rows = n_pairs * CHUNK
    row = lax.broadcasted_iota(jnp.int32, (rows, 1), 0)
    out_lane = lax.broadcasted_iota(jnp.int32, (rows, LANES), 1)
    nt = (((1,), (1,)), ((), ()))
    for h in range(SWA_KV_HEADS):
        kk = block_diag(kwin, h)
        vv = block_diag(vwin, h)
        base = h * SWA_GROUP * SWA_HEAD_DIM
        q4 = jnp.concatenate([q_ref[:, base + p * LANES: base + (p + 1) * LANES] for p in range(n_pairs)],
                             axis=0)
        s = lax.dot_general(q4, kk, nt, preferred_element_type=F32) * (SWA_HEAD_DIM ** -0.5)
        ps, invs = [], []
        for half in range(2):
            sh = jnp.where(ok, s[:, half * KPAD:(half + 1) * KPAD], NEG_INF)
            sink = jnp.zeros((rows, 1), F32)
            for p in range(n_pairs):
                sink = jnp.where(row // CHUNK == p, sink_ref[h * SWA_GROUP + 2 * p + half], sink)
            m = jnp.maximum(jnp.max(sh, axis=-1, keepdims=True), sink)
            p_half = jnp.exp(sh - m)
            den = jnp.sum(p_half, axis=-1, keepdims=True) + jnp.exp(sink - m)
            ps.append(p_half.astype(BF16))
            invs.append(1.0 / den)
        pv = jnp.dot(jnp.concatenate(ps, axis=1), vv, preferred_element_type=F32)
        o = pv * jnp.where(out_lane < SWA_HEAD_DIM, invs[0], invs[1])
        for p in range(n_pairs):
            o_ref[:, base + p * LANES: base + (p + 1) * LANES] = o[p * CHUNK:(p + 1) * CHUNK].astype(BF16)


def _swa(sinks, sq, k_arrs, v_arrs, k_maps, v_maps, *, n_seq, nc, masked):
    R = sq.shape[0]
    kv_specs = [pl.BlockSpec((CHUNK, SWA_KV_W), m) for m in (*k_maps, *v_maps)]
    return pl.pallas_call(
        functools.partial(_swa_kernel, masked=masked),
        grid=(n_seq, nc),
        in_specs=[pl.BlockSpec(memory_space=pltpu.SMEM),
                  pl.BlockSpec((CHUNK, SWA_Q_W), lambda b, c: (b * nc + c, 0)),
                  *kv_specs],
        out_specs=pl.BlockSpec((CHUNK, SWA_Q_W), lambda b, c: (b * nc + c, 0)),
        out_shape=jax.ShapeDtypeStruct((R, SWA_Q_W), BF16),
        compiler_params=_cparams(("parallel", "arbitrary")),
        name="swa",
    )(sinks, sq, *k_arrs, *v_arrs)


def _route(logits):
    tm = logits.shape[0]
    lane = lax.broadcasted_iota(jnp.int32, (tm, LANES), 1)
    is_g = lane < N_GROUPS
    gl = jnp.where(is_g, logits, NEG_INF)
    gmax = jnp.max(gl, axis=-1, keepdims=True)
    gidx = jnp.min(jnp.where(gl == gmax, lane, LANES), axis=-1, keepdims=True)
    gsum = jnp.sum(jnp.where(is_g, jnp.exp(gl - gmax), 0.0), axis=-1, keepdims=True)
    g_w = 1.0 / gsum
    base = N_GROUPS + EXPERTS_PER_GROUP * gidx
    el = jnp.where((lane >= base) & (lane < base + EXPERTS_PER_GROUP), logits, NEG_INF)
    v1 = jnp.max(el, axis=-1, keepdims=True)
    i1 = jnp.min(jnp.where(el == v1, lane, LANES), axis=-1, keepdims=True)
    el2 = jnp.where(lane == i1, NEG_INF, el)
    v2 = jnp.max(el2, axis=-1, keepdims=True)
    i2 = jnp.min(jnp.where(el2 == v2, lane, LANES), axis=-1, keepdims=True)
    e2 = jnp.exp(v2 - v1)
    den = 1.0 + e2
    w1 = g_w / den
    w2 = g_w * e2 / den
    l1 = i1 - base
    l2 = i2 - base
    first_lo = l1 < l2
    la = jnp.where(first_lo, l1, l2)
    lb = jnp.where(first_lo, l2, l1)
    wa = jnp.where(first_lo, w1, w2)
    wb = jnp.where(first_lo, w2, w1)
    pair = jnp.where(la == 0, 0, jnp.where(la == 1, 3, 5)) + (lb - la - 1)
    cls = (gidx * N_PAIRS + pair).astype(F32)
    return jnp.where(lane == 0, wa, jnp.where(lane == 1, wb, jnp.where(lane == 2, cls, 0.0)))


ROW_W = D_MODEL + LANES


def _merge_kernel(x_ref, r_ref, o_ref, gate_ref, mod_ref, modp_ref, g2_ref, wrb_ref, wsb_ref, wout_ref, wr_ref,
                  br_ref, x1_ref, text_ref, meta_ref, x1_scr, *, n_seq):
    i = pl.program_id(0)
    n = pl.num_programs(0) - 1
    tm = x_ref.shape[0]

    def matmul_stage():
        g_r = gate_ref[:, :D_MODEL].astype(F32)
        g_s = gate_ref[:, D_MODEL:].astype(F32)
        merged = (g_r * jnp.dot(r_ref[...], wrb_ref[...], preferred_element_type=F32)
                  + g_s * jnp.dot(o_ref[...], wsb_ref[...], preferred_element_type=F32))
        mix = jnp.dot(merged.astype(BF16), wout_ref[...], preferred_element_type=F32)
        gt1 = mod_ref[:, 2, :]
        x1 = (x_ref[...].reshape(n_seq, tm // n_seq, D_MODEL) + gt1[:, None, :]
              * mix.reshape(n_seq, tm // n_seq, D_MODEL)).reshape(tm, D_MODEL)
        x1_ref[...] = x1
        x1_scr[...] = x1

    def vector_stage():
        t = _modulated_norm(x1_scr[...], g2_ref[...], modp_ref[:, 3, :], modp_ref[:, 4, :], n_seq)
        logits = jnp.dot(t.astype(BF16), wr_ref[...], preferred_element_type=F32) + br_ref[...]
        meta = _route(logits)
        text_ref[:, :D_MODEL] = t
        text_ref[:, D_MODEL:] = meta
        meta_ref[...] = meta

    @pl.when(i == 0)
    def _():
        matmul_stage()

    @pl.when((i > 0) & (i < n))
    def _():
        vector_stage()
        matmul_stage()

    @pl.when(i == n)
    def _():
        vector_stage()


def _merge(x2d, r, o_swa, gates, mod3, g2, wrb, wsb, wout, wr, br, *, seq_len, tm):
    R = x2d.shape[0]
    n = R // tm
    if seq_len >= tm:
        n_seq, tps = 1, seq_len // tm
        seq_of = lambda t: t // tps
    else:
        n_seq = tm // seq_len
        seq_of = lambda t: t
    cur = lambda i: jnp.minimum(i, n - 1)
    prev = lambda i: jnp.maximum(i - 1, 0)
    row = lambda w: pl.BlockSpec((tm, w), lambda i: (cur(i), 0))
    row_prev = lambda w: pl.BlockSpec((tm, w), lambda i: (prev(i), 0))
    return pl.pallas_call(
        functools.partial(_merge_kernel, n_seq=n_seq),
        grid=(n + 1,),
        in_specs=[row(D_MODEL), row(RET_W), row(SWA_Q_W), row(2 * D_MODEL),
                  pl.BlockSpec((n_seq, N_MOD, D_MODEL), lambda i: (seq_of(cur(i)), 0, 0)),
                  pl.BlockSpec((n_seq, N_MOD, D_MODEL), lambda i: (seq_of(prev(i)), 0, 0)),
                  _resident((1, D_MODEL)),
                  _resident((RET_W, D_MODEL)), _resident((SWA_Q_W, D_MODEL)), _resident((D_MODEL, D_MODEL)),
                  _resident((D_MODEL, LANES)), _resident((1, LANES))],
        out_specs=[row(D_MODEL), row_prev(ROW_W), row_prev(LANES)],
        out_shape=[jax.ShapeDtypeStruct((R, D_MODEL), F32),
                   jax.ShapeDtypeStruct((R, ROW_W), F32),
                   jax.ShapeDtypeStruct((R, LANES), F32)],
        scratch_shapes=[pltpu.VMEM((tm, D_MODEL), F32)],
        compiler_params=_cparams(("arbitrary",)),
        name="merge",
    )(x2d, r, o_swa, gates, mod3, mod3, g2, wrb, wsb, wout, wr, br)


PLAN_BLK = 2048
TILE_ROWS = 256


def _plan_kernel(meta_a_ref, meta_b_ref, pos_ref, tile_ref, pad_ref, cnt_scr, offs_scr, carry_scr, tri_scr,
                 *, tm, nb_a):
    ph = pl.program_id(0)
    b = pl.program_id(1)
    blk = meta_a_ref.shape[0]
    lane = lax.broadcasted_iota(jnp.int32, (blk, LANES), 1)
    cls_col = jnp.where(b < nb_a, meta_a_ref[:, 2:3], meta_b_ref[:, 2:3])
    oh = jnp.where(cls_col == lane.astype(F32), 1.0, 0.0)

    @pl.when((ph == 0) & (b == 0))
    def _():
        cnt_scr[...] = jnp.zeros_like(cnt_scr)
        ri = lax.broadcasted_iota(jnp.int32, (blk, blk), 0)
        ci = lax.broadcasted_iota(jnp.int32, (blk, blk), 1)
        tri_scr[...] = jnp.where(ci <= ri, 1.0, 0.0).astype(BF16)

    @pl.when(ph == 0)
    def _():
        cnt_scr[...] += jnp.sum(oh, axis=0, keepdims=True)

    @pl.when((ph == 1) & (b == 0))
    def _():
        cnt = cnt_scr[...]
        ptiles = jnp.floor((cnt + (tm - 1)) * (1.0 / tm))
        ri = lax.broadcasted_iota(jnp.int32, (LANES, LANES), 0)
        ci = lax.broadcasted_iota(jnp.int32, (LANES, LANES), 1)
        before = jnp.where(ri < ci, 1.0, 0.0).astype(BF16)
        offs = jnp.dot(ptiles.astype(BF16), before, preferred_element_type=F32) * tm
        offs_scr[...] = offs
        carry_scr[...] = jnp.zeros_like(carry_scr)
        padded = ptiles * tm
        ends = offs + padded
        tl = lax.broadcasted_iota(jnp.int32, (TILE_ROWS, LANES), 1)
        tstart = lax.broadcasted_iota(jnp.int32, (TILE_ROWS, LANES), 0).astype(F32) * tm
        tcls = jnp.sum(jnp.where((ends[0:1, :] <= tstart) & (tl < N_CLASSES), 1.0, 0.0), axis=1, keepdims=True)
        tcls = jnp.minimum(tcls, N_CLASSES - 1.0)
        total = jnp.max(ends[0:1, :], axis=1, keepdims=True)
        grp = (jnp.where(tcls >= N_PAIRS, 1.0, 0.0) + jnp.where(tcls >= 2 * N_PAIRS, 1.0, 0.0)
               + jnp.where(tcls >= 3 * N_PAIRS, 1.0, 0.0))
        pair = tcls - N_PAIRS * grp
        la = jnp.where(pair >= 3, 1.0, 0.0) + jnp.where(pair >= 5, 1.0, 0.0)
        lb = jnp.where(pair == 0, 1.0, jnp.where((pair == 1) | (pair == 3), 2.0, 3.0))
        ea = EXPERTS_PER_GROUP * grp + la
        eb = EXPERTS_PER_GROUP * grp + lb
        n_used = total * (1.0 / tm)
        tile_ref[...] = jnp.where(tl == 0, ea, jnp.where(tl == 1, eb, jnp.where(tl == 2, n_used, 0.0))
                                  ).astype(jnp.int32)
        npad = padded - cnt
        pstart = jnp.dot(npad.astype(BF16), before, preferred_element_type=F32)
        n_class_pad = jnp.sum(npad[0:1, :], axis=1, keepdims=True)
        rows = pad_ref.shape[0]
        v = (lax.broadcasted_iota(jnp.int32, (rows, LANES), 0) * LANES
             + lax.broadcasted_iota(jnp.int32, (rows, LANES), 1)).astype(F32)
        slot = jnp.where(v >= n_class_pad, total - n_class_pad + v, 0.0)
        for c in range(N_CLASSES):
            ps = pstart[0:1, c:c + 1]
            inside = (v >= ps) & (v < ps + npad[0:1, c:c + 1])
            slot = jnp.where(inside, offs[0:1, c:c + 1] + cnt[0:1, c:c + 1] - ps + v, slot)
        pad_ref[...] = slot.astype(jnp.int32)

    @pl.when(ph == 1)
    def _():
        incl = jnp.dot(tri_scr[...], oh.astype(BF16), preferred_element_type=F32)
        base = offs_scr[0:1, :] + carry_scr[0:1, :]
        pos = jnp.sum(oh * (base + incl - oh), axis=1, keepdims=True)
        pos_ref[...] = jnp.broadcast_to(pos, (blk, LANES)).astype(jnp.int32)
        carry_scr[...] += jnp.sum(oh, axis=0, keepdims=True)


def _plan(meta_a, meta_b, *, tm):
    na, nb = meta_a.shape[0], meta_b.shape[0]
    n = na + nb
    n_tiles = n // tm + N_CLASSES
    n_free = N_CLASSES * tm
    assert n_tiles <= TILE_ROWS and na % PLAN_BLK == 0 and nb % PLAN_BLK == 0 and n_free % LANES == 0
    n_slots = n_tiles * tm
    nb_a = na // PLAN_BLK
    nb_b = nb // PLAN_BLK
    pos2d, tile2d, pad2d = pl.pallas_call(
        functools.partial(_plan_kernel, tm=tm, nb_a=nb_a),
        grid=(2, nb_a + nb_b),
        in_specs=[pl.BlockSpec((PLAN_BLK, LANES), lambda ph, b: (jnp.minimum(b, nb_a - 1), 0)),
                  pl.BlockSpec((PLAN_BLK, LANES), lambda ph, b: (jnp.maximum(b - nb_a, 0), 0))],
        out_specs=[pl.BlockSpec((PLAN_BLK, LANES), lambda ph, b: (b * ph, 0)),
                   pl.BlockSpec((TILE_ROWS, LANES), lambda ph, b: (0, 0)),
                   pl.BlockSpec((n_free // LANES, LANES), lambda ph, b: (0, 0))],
        out_shape=[jax.ShapeDtypeStruct((n, LANES), jnp.int32),
                   jax.ShapeDtypeStruct((TILE_ROWS, LANES), jnp.int32),
                   jax.ShapeDtypeStruct((n_free // LANES, LANES), jnp.int32)],
        scratch_shapes=[pltpu.VMEM((8, LANES), F32), pltpu.VMEM((8, LANES), F32), pltpu.VMEM((8, LANES), F32),
                        pltpu.VMEM((PLAN_BLK, PLAN_BLK), BF16)],
        compiler_params=_cparams(("arbitrary", "arbitrary")),
        name="plan",
    )(meta_a, meta_b)
    return pos2d[:, 0], pad2d.reshape(-1), tile2d[:n_tiles, 0], tile2d[:n_tiles, 1], tile2d[0:1, 2], n_slots


N_STAGE = 3


def _scatter_kernel(pos_ref, text_a_ref, text_b_ref, out_ref, stage, zero_scr, sem_in, sem_out, *, nb_a, nb_b):
    i = pl.program_id(0)
    n = pl.num_programs(0)
    n_tok = nb_a + nb_b
    tb = pos_ref.shape[2]
    slot = i % N_STAGE

    def load(step, s):
        def from_a():
            return pltpu.make_async_copy(text_a_ref.at[pl.ds(step * tb, tb), :], stage.at[s], sem_in.at[s])

        def from_b():
            return pltpu.make_async_copy(text_b_ref.at[pl.ds((step - nb_a) * tb, tb), :], stage.at[s], sem_in.at[s])

        return from_a, from_b

    def start_load(step, s):
        from_a, from_b = load(step, s)

        @pl.when(step < nb_a)
        def _():
            from_a().start()

        @pl.when((step >= nb_a) & (step < n_tok))
        def _():
            from_b().start()

    def wait_rows(s):
        pltpu.make_async_copy(stage.at[s], out_ref.at[pl.ds(0, tb), :], sem_out.at[s]).wait()

    def scatter_rows(src_row):
        def body(r, carry):
            dst = pos_ref[0, 0, r]
            pltpu.make_async_copy(src_row(r), out_ref.at[pl.ds(dst, 1), :], sem_out.at[slot]).start()
            return carry

        lax.fori_loop(0, tb, body, 0, unroll=8)

    @pl.when(i == 0)
    def _():
        zero_scr[...] = jnp.zeros_like(zero_scr)
        start_load(i, slot)

    @pl.when(i >= 2)
    def _():
        wait_rows((i + 1) % N_STAGE)

    start_load(i + 1, (i + 1) % N_STAGE)

    @pl.when(i < n_tok)
    def _():
        pltpu.make_async_copy(text_a_ref.at[pl.ds(0, tb), :], stage.at[slot], sem_in.at[slot]).wait()
        scatter_rows(lambda r: stage.at[slot, pl.ds(r, 1), :])

    @pl.when(i >= n_tok)
    def _():
        scatter_rows(lambda r: zero_scr.at[pl.ds(0, 1), :])

    @pl.when(i == n - 1)
    def _():
        wait_rows((i + 2) % N_STAGE)
        wait_rows(slot)


def _scatter_rows(text_a, text_b, pos_ext, *, tb):
    nb_a = text_a.shape[0] // tb
    nb_b = text_b.shape[0] // tb
    n_steps = pos_ext.shape[0] // tb
    assert n_steps >= 2 and nb_a >= 1 and nb_b >= 1
    return pl.pallas_call(
        functools.partial(_scatter_kernel, nb_a=nb_a, nb_b=nb_b),
        grid=(n_steps,),
        in_specs=[pl.BlockSpec((1, 1, tb), lambda i: (i, 0, 0), memory_space=pltpu.SMEM),
                  pl.BlockSpec(memory_space=pl.ANY),
                  pl.BlockSpec(memory_space=pl.ANY)],
        out_specs=pl.BlockSpec(memory_space=pl.ANY),
        out_shape=jax.ShapeDtypeStruct((pos_ext.shape[0], ROW_W), F32),
        scratch_shapes=[pltpu.VMEM((N_STAGE, tb, ROW_W), F32), pltpu.VMEM((8, ROW_W), F32),
                        pltpu.SemaphoreType.DMA((N_STAGE,)), pltpu.SemaphoreType.DMA((N_STAGE,))],
        compiler_params=_cparams(("arbitrary",)),
        name="scatter_rows",
    )(pos_ext.reshape(n_steps, 1, tb), text_a, text_b)


def _moe_kernel(ea_ref, eb_ref, nused_ref, xs_ref, w1a_ref, w3a_ref, w2a_ref, w1b_ref, w3b_ref, w2b_ref, y_ref):
    i = pl.program_id(0)

    @pl.when(i >= nused_ref[0])
    def _():
        y_ref[...] = jnp.zeros_like(y_ref)

    @pl.when(i < nused_ref[0])
    def _():
        x = xs_ref[:, :D_MODEL].astype(BF16)
        wa = xs_ref[:, D_MODEL:D_MODEL + 1]
        wb = xs_ref[:, D_MODEL + 1:D_MODEL + 2]

        def hidden(w1_ref, w3_ref, gate):
            a = jnp.dot(x, w1_ref[0], preferred_element_type=F32)
            b = jnp.dot(x, w3_ref[0], preferred_element_type=F32)
            return (a * jax.nn.sigmoid(a) * b * gate).astype(BF16)

        ha = hidden(w1a_ref, w3a_ref, wa)
        hb = hidden(w1b_ref, w3b_ref, wb)
        y_ref[...] = (jnp.dot(ha, w2a_ref[0], preferred_element_type=F32)
                      + jnp.dot(hb, w2b_ref[0], preferred_element_type=F32))


def _moe(tile_ea, tile_eb, n_used, xs, w1, w3, w2, *, tm, n_tiles):
    last = lambda i, nu: jnp.minimum(i, nu[0] - 1)
    wa_map = lambda i, ea, eb, nu: (ea[last(i, nu)], 0, 0)
    wb_map = lambda i, ea, eb, nu: (eb[last(i, nu)], 0, 0)
    row_map = lambda i, ea, eb, nu: (last(i, nu), 0)
    up = (1, D_MODEL, D_EXPERT)
    down = (1, D_EXPERT, D_MODEL)
    grid_spec = pltpu.PrefetchScalarGridSpec(
        num_scalar_prefetch=3,
        grid=(n_tiles,),
        in_specs=[pl.BlockSpec((tm, D_MODEL + LANES), row_map),
                  pl.BlockSpec(up, wa_map), pl.BlockSpec(up, wa_map), pl.BlockSpec(down, wa_map),
                  pl.BlockSpec(up, wb_map), pl.BlockSpec(up, wb_map), pl.BlockSpec(down, wb_map)],
        out_specs=pl.BlockSpec((tm, D_MODEL), lambda i, ea, eb, nu: (i, 0)),
    )
    return pl.pallas_call(
        _moe_kernel,
        grid_spec=grid_spec,
        out_shape=jax.ShapeDtypeStruct((n_tiles * tm, D_MODEL), F32),
        compiler_params=_cparams(("arbitrary",)),
        name="moe",
    )(tile_ea, tile_eb, n_used, xs, w1, w3, w2, w1, w3, w2)


def _final_kernel(pos_ref, pos_next_ref, x1_ref, mod_ref, g_ref, ys_ref, o_ref, ybuf, sem, *, n_seq):
    i = pl.program_id(0)
    n = pl.num_programs(0)
    tm = x1_ref.shape[0]
    slot = i % 2

    def start_gather(idx_ref, s):
        def body(r, carry):
            src = idx_ref[0, 0, r]
            pltpu.make_async_copy(ys_ref.at[pl.ds(src, 1), :], ybuf.at[s, pl.ds(r, 1), :], sem.at[s]).start()
            return carry

        lax.fori_loop(0, tm, body, 0, unroll=8)

    @pl.when(i == 0)
    def _():
        start_gather(pos_ref, 0)

    @pl.when(i + 1 < n)
    def _():
        start_gather(pos_next_ref, 1 - slot)

    pltpu.make_async_copy(ys_ref.at[pl.ds(0, tm), :], ybuf.at[slot], sem.at[slot]).wait()
    gt2 = mod_ref[:, 5, :]
    x2 = (x1_ref[...].reshape(n_seq, tm // n_seq, D_MODEL)
          + gt2[:, None, :] * ybuf[slot].reshape(n_seq, tm // n_seq, D_MODEL)).reshape(tm, D_MODEL)
    ms = jnp.mean(x2 * x2, axis=-1, keepdims=True)
    o_ref[...] = x2 * lax.rsqrt(ms + EPS) * g_ref[...]


def _final(x1, y_sorted, pos, mod3, gf, *, seq_len, tm):
    R = x1.shape[0]
    if seq_len >= tm:
        n_seq, tps = 1, seq_len // tm
        mod_map = lambda i: (i // tps, 0, 0)
    else:
        n_seq = tm // seq_len
        mod_map = lambda i: (i, 0, 0)
    n = R // tm
    row = pl.BlockSpec((tm, D_MODEL), lambda i: (i, 0))
    pos3 = pos.reshape(n, 1, tm)
    return pl.pallas_call(
        functools.partial(_final_kernel, n_seq=n_seq),
        grid=(n,),
        in_specs=[pl.BlockSpec((1, 1, tm), lambda i: (i, 0, 0), memory_space=pltpu.SMEM),
                  pl.BlockSpec((1, 1, tm), lambda i: (jnp.minimum(i + 1, n - 1), 0, 0), memory_space=pltpu.SMEM),
                  row, pl.BlockSpec((n_seq, N_MOD, D_MODEL), mod_map), _resident((1, D_MODEL)),
                  pl.BlockSpec(memory_space=pl.ANY)],
        out_specs=row,
        out_shape=jax.ShapeDtypeStruct((R, D_MODEL), F32),
        scratch_shapes=[pltpu.VMEM((2, tm, D_MODEL), F32), pltpu.SemaphoreType.DMA((2,))],
        compiler_params=_cparams(("arbitrary",)),
        name="final",
    )(pos3, pos3, x1, mod3, gf, y_sorted)


def _rope_tables(pos):
    half = RET_DK // 2
    inv = ROPE_BASE ** (-jnp.arange(half, dtype=F32) / half)
    ang = pos.astype(F32)[:, None] * inv[None, :]
    cos = jnp.cos(ang)
    sin = jnp.sin(ang)
    return jnp.concatenate([cos, cos], axis=-1), jnp.concatenate([-sin, sin], axis=-1)


def kernel(x_prompt, x_sample, cache_ret_state, cache_swa_k, cache_swa_v, c_prompt, c_sample,
           norm1_g, norm2_g, ada_w, ada_b, w_in, ret_gn_g, swa_sinks, w_ret_branch, w_swa_branch, w_out,
           router_group_w, router_group_b, router_expert_w, router_expert_b,
           expert_w1, expert_w3, expert_w2, final_norm_g):
    depth = w_in.shape[0]
    assert depth == 1
    bp, tp, _ = x_prompt.shape
    bs, ts, _ = x_sample.shape
    past = WINDOW
    assert cache_swa_k.shape[2] == past and ts == CHUNK and tp % 512 == 0
    tm = 512

    l = 0
    c1 = 4 * RET_W
    c2 = c1 + SWA_Q_W
    c3 = c2 + 2 * SWA_KV_W
    wret = w_in[l, :, :c1].astype(BF16)
    wsq = w_in[l, :, c1:c2].astype(BF16)
    wkv = w_in[l, :, c2:c3].astype(BF16)
    wbg = w_in[l, :, c3:].astype(BF16)
    wrb = w_ret_branch[l].astype(BF16)
    wsb = w_swa_branch[l].astype(BF16)
    wo = w_out[l].astype(BF16)
    n_r = N_GROUPS + N_EXPERTS
    wr = jnp.zeros((D_MODEL, LANES), F32).at[:, :N_GROUPS].set(router_group_w[l]).at[:, N_GROUPS:n_r].set(
        router_expert_w[l]).astype(BF16)
    br = jnp.zeros((1, LANES), F32).at[0, :N_GROUPS].set(router_group_b[l]).at[0, N_GROUPS:n_r].set(
        router_expert_b[l])
    w1 = expert_w1[l].astype(BF16)
    w3 = expert_w3[l].astype(BF16)
    w2 = expert_w2[l].astype(BF16)
    g1 = norm1_g[l].reshape(1, D_MODEL)
    g2 = norm2_g[l].reshape(1, D_MODEL)
    gn = ret_gn_g[l].reshape(1, RET_W)
    gf = final_norm_g.reshape(1, D_MODEL)
    sinks = swa_sinks[l]

    c_all = jnp.concatenate([c_prompt, c_sample], axis=0)
    mod = _ada(c_all, ada_w[l], ada_b[l]).reshape(bp + bs, N_MOD, D_MODEL)
    mod_p, mod_s = mod[:bp], mod[bp:]

    cos_p, sin_p = _rope_tables(jnp.arange(tp))
    cos_s, sin_s = _rope_tables(PAST_LEN + jnp.arange(ts))
    rep = tm // ts
    cos_s, sin_s = jnp.tile(cos_s, (rep, 1)), jnp.tile(sin_s, (rep, 1))

    xp = x_prompt.reshape(bp * tp, D_MODEL)
    xs = x_sample.reshape(bs * ts, D_MODEL)

    lc_p = 128
    ret_p, sq_p, kv_p, gate_p = _inproj(xp, mod_p, g1, cos_p, sin_p, wret, wsq, wkv, wbg, seq_len=tp, tm=tm,
                                        lc=lc_p)
    ret_s, sq_s, kv_s, gate_s = _inproj(xs, mod_s, g1, cos_s, sin_s, wret, wsq, wkv, wbg, seq_len=ts, tm=tm,
                                        lc=ts)

    s0_p = jnp.zeros((bp, RET_HEADS, RET_DK, RET_DV), F32)
    r_p, state_p = _retention(ret_p, s0_p, gn, n_seq=bp, seq_len=tp, lc=lc_p, n_sub=2)
    r_s, state_s = _retention(ret_s, cache_ret_state[l].astype(F32), gn, n_seq=bs, seq_len=ts, lc=ts, n_sub=1)

    nc_p = tp // CHUNK
    kmap = lambda back, colblk: (lambda b, c: (b * nc_p + jnp.maximum(c - back, 0), colblk))
    o_p = _swa(sinks, sq_p, [kv_p] * 3, [kv_p] * 3,
               [kmap(2, 0), kmap(1, 0), kmap(0, 0)], [kmap(2, 1), kmap(1, 1), kmap(0, 1)],
               n_seq=bp, nc=nc_p, masked=True)
    ck = cache_swa_k[l].reshape(bs * past, SWA_KV_W)
    cv = cache_swa_v[l].reshape(bs * past, SWA_KV_W)
    cmap = lambda blk: (lambda b, c: (2 * b + blk, 0))
    o_s = _swa(sinks, sq_s, [ck, ck, kv_s], [cv, cv, kv_s],
               [cmap(0), cmap(1), lambda b, c: (b, 0)], [cmap(0), cmap(1), lambda b, c: (b, 1)],
               n_seq=bs, nc=1, masked=False)

    tm_m = 256
    n_p = bp * tp
    x1_p, text_p, meta_p = _merge(xp, r_p, o_p, gate_p, mod_p, g2, wrb, wsb, wo, wr, br, seq_len=tp, tm=tm_m)
    x1_s, text_s, meta_s = _merge(xs, r_s, o_s, gate_s, mod_s, g2, wrb, wsb, wo, wr, br, seq_len=ts, tm=tm_m)

    tm_e = 256
    pos, free_slots, tile_ea, tile_eb, n_used, n_slots = _plan(meta_p, meta_s, tm=tm_e)
    pos_ext = jnp.concatenate([pos, free_slots], axis=0)
    xsorted = _scatter_rows(text_p, text_s, pos_ext, tb=512)
    y_sorted = _moe(tile_ea, tile_eb, n_used, xsorted, w1, w3, w2, tm=tm_e, n_tiles=n_slots // tm_e)

    tm_f = 512
    out_p = _final(x1_p, y_sorted, pos[:n_p], mod_p, gf, seq_len=tp, tm=tm_f)
    out_s = _final(x1_s, y_sorted, pos[n_p:], mod_s, gf, seq_len=ts, tm=tm_f)

    y_prompt = out_p.reshape(bp, tp, D_MODEL)
    y_sample = out_s.reshape(bs, ts, D_MODEL)
    kvp = kv_p.reshape(bp, tp, 2 * SWA_KV_W)[:, tp - WINDOW:].reshape(bp, WINDOW, 2, SWA_KV_HEADS, SWA_HEAD_DIM)
    kvs = kv_s.reshape(bs, ts, 2, SWA_KV_HEADS, SWA_HEAD_DIM)
    k_s = jnp.concatenate([cache_swa_k[l].astype(F32), kvs[:, :, 0]], axis=1)[:, -WINDOW:]
    v_s = jnp.concatenate([cache_swa_v[l].astype(F32), kvs[:, :, 1]], axis=1)[:, -WINDOW:]
    return (y_prompt, y_sample, state_p[None], kvp[:, :, 0][None], kvp[:, :, 1][None],
            state_s[None], k_s[None], v_s[None])
```

```python
import functools
import math

import jax
import jax.numpy as jnp
from jax import lax
from jax.experimental import pallas as pl
from jax.experimental.pallas import tpu as pltpu

F32 = jnp.float32
BF16 = jnp.bfloat16

D_MODEL = 2048
CHUNK = 64
RET_HEADS = 8
RET_DK = 128
RET_DV = 128
RET_W = RET_HEADS * RET_DK
ROPE_BASE = 10000.0
SWA_Q_HEADS = 16
SWA_KV_HEADS = 2
SWA_GROUP = SWA_Q_HEADS // SWA_KV_HEADS
SWA_HEAD_DIM = 64
SWA_Q_W = SWA_Q_HEADS * SWA_HEAD_DIM
SWA_KV_W = SWA_KV_HEADS * SWA_HEAD_DIM
WINDOW = 128
PAST_LEN = 1024
N_GROUPS = 4
EXPERTS_PER_GROUP = 4
N_EXPERTS = 16
D_EXPERT = 512
N_MOD = 6
EPS = 1e-6
NEG_INF = -1e30
N_PAIRS = 6
N_CLASSES = N_GROUPS * N_PAIRS

LANES = 128
VMEM_LIMIT = 56 * 1024 * 1024

RET_LOG_GAMMA = tuple(math.log1p(-(2.0 ** (-5.0 - h))) for h in range(RET_HEADS))


def _cparams(sem):
    return pltpu.CompilerParams(dimension_semantics=sem, vmem_limit_bytes=VMEM_LIMIT)


def _resident(shape):
    nd = len(shape)
    return pl.BlockSpec(shape, lambda *_: (0,) * nd, pipeline_mode=pl.Buffered(1))


def _ada_kernel(c_ref, w_ref, b_ref, o_ref):
    c = c_ref[...]
    a = c * jax.nn.sigmoid(c)
    o_ref[...] = jnp.dot(a, w_ref[...], preferred_element_type=F32,
                         precision=lax.Precision.HIGHEST) + b_ref[...]


def _ada(c_all, ada_w, ada_b):
    nb = c_all.shape[0]
    n_out = ada_w.shape[1]
    tn = 1024
    return pl.pallas_call(
        _ada_kernel,
        grid=(n_out // tn,),
        in_specs=[pl.BlockSpec((nb, D_MODEL), lambda j: (0, 0)),
                  pl.BlockSpec((D_MODEL, tn), lambda j: (0, j)),
                  pl.BlockSpec((1, tn), lambda j: (0, j))],
        out_specs=pl.BlockSpec((nb, tn), lambda j: (0, j)),
        out_shape=jax.ShapeDtypeStruct((nb, n_out), F32),
        compiler_params=_cparams(("arbitrary",)),
        name="ada",
    )(c_all, ada_w, ada_b.reshape(1, n_out))


def _modulated_norm(x, g, shift, scale, n_seq):
    tm = x.shape[0]
    ms = jnp.mean(x * x, axis=-1, keepdims=True)
    y = x * lax.rsqrt(ms + EPS) * g
    y3 = y.reshape(n_seq, tm // n_seq, D_MODEL)
    h = y3 * (1.0 + scale)[:, None, :] + shift[:, None, :]
    return h.reshape(tm, D_MODEL)


COL_BLK = 1024


def _inproj_ret_kernel(x_ref, mod_ref, g1_ref, cos_ref, sin_ref, wret_ref, ret_ref, h_ref, dq_scr, dk_scr,
                       *, n_seq, lc):
    tm = x_ref.shape[0]

    @pl.when(pl.program_id(0) == 0)
    def _():
        e = ((lax.broadcasted_iota(jnp.int32, (tm, RET_DK), 0) % lc) + 1).astype(F32)
        for hh in range(RET_HEADS):
            dq_scr[hh] = jnp.exp(e * RET_LOG_GAMMA[hh])
            dk_scr[hh] = jnp.exp(-e * RET_LOG_GAMMA[hh]) * (RET_DK ** -0.5)

    h = _modulated_norm(x_ref[...], g1_ref[...], mod_ref[:, 0, :], mod_ref[:, 1, :], n_seq)
    h_ref[...] = h.astype(BF16)
    cos = cos_ref[...]
    sin = sin_ref[...]
    assert COL_BLK == RET_W
    for blk in range(4):
        c0 = blk * COL_BLK
        acc = jnp.dot(h_ref[...], wret_ref[:, c0:c0 + COL_BLK], preferred_element_type=F32)
        if blk >= 2:
            ret_ref[:, c0:c0 + COL_BLK] = acc.astype(BF16)
            continue
        dec_scr = dq_scr if blk == 0 else dk_scr
        for hh in range(RET_HEADS):
            a = acc[:, hh * RET_DK:(hh + 1) * RET_DK]
            r = a * cos + pltpu.roll(a, RET_DK // 2, 1) * sin
            ret_ref[:, c0 + hh * RET_DK:c0 + (hh + 1) * RET_DK] = (r * dec_scr[hh]).astype(BF16)


def _inproj_rest_kernel(h_ref, wsq_ref, wkv_ref, wbg_ref, *rest, n_cast):
    cast_in, (sq_ref, kv_ref, gate_ref), cast_out = rest[:n_cast], rest[n_cast:n_cast + 3], rest[n_cast + 3:]
    for src, dst in zip(cast_in, cast_out):
        dst[...] = src[...].astype(BF16)
    sq_ref[...] = jnp.dot(h_ref[...], wsq_ref[...], preferred_element_type=F32).astype(BF16)
    kv_ref[...] = jnp.dot(h_ref[...], wkv_ref[...], preferred_element_type=F32)
    for blk in range(2 * D_MODEL // COL_BLK):
        c0 = blk * COL_BLK
        acc = jnp.dot(h_ref[...], wbg_ref[:, c0:c0 + COL_BLK], preferred_element_type=F32)
        gate_ref[:, c0:c0 + COL_BLK] = jax.nn.sigmoid(acc).astype(BF16)


def _inproj(x2d, mod3, g1, cos_t, sin_t, wret, wsq, wkv, wbg, *, seq_len, tm, lc, cast=()):
    R = x2d.shape[0]
    if seq_len >= tm:
        n_seq, tps = 1, seq_len // tm
        mod_map = lambda i: (i // tps, 0, 0)
        tab_map = lambda i: (i % tps, 0)
    else:
        n_seq = tm // seq_len
        mod_map = lambda i: (i, 0, 0)
        tab_map = lambda i: (0, 0)
    row = lambda w: pl.BlockSpec((tm, w), lambda i: (i, 0))
    assert tm % lc == 0
    ret, h = pl.pallas_call(
        functools.partial(_inproj_ret_kernel, n_seq=n_seq, lc=lc),
        grid=(R // tm,),
        in_specs=[row(D_MODEL),
                  pl.BlockSpec((n_seq, N_MOD, D_MODEL), mod_map),
                  _resident((1, D_MODEL)),
                  pl.BlockSpec((tm, LANES), tab_map),
                  pl.BlockSpec((tm, LANES), tab_map),
                  _resident((D_MODEL, 4 * RET_W))],
        out_specs=[row(4 * RET_W), row(D_MODEL)],
        out_shape=[jax.ShapeDtypeStruct((R, 4 * RET_W), BF16),
                   jax.ShapeDtypeStruct((R, D_MODEL), BF16)],
        scratch_shapes=[pltpu.VMEM((RET_HEADS, tm, RET_DK), F32), pltpu.VMEM((RET_HEADS, tm, RET_DK), F32)],
        compiler_params=_cparams(("arbitrary",)),
        name="inproj_ret",
    )(x2d, mod3, g1, cos_t, sin_t, wret)
    n_steps = R // tm
    cast_2d = [w.reshape(-1, w.shape[-1]) for w in cast]
    cast_specs = []
    for w in cast_2d:
        slab = w.shape[0] // n_steps
        assert slab * n_steps == w.shape[0] and slab % 16 == 0
        cast_specs.append(pl.BlockSpec((slab, w.shape[1]), lambda i: (i, 0)))
    outs = pl.pallas_call(
        functools.partial(_inproj_rest_kernel, n_cast=len(cast)),
        grid=(n_steps,),
        in_specs=[row(D_MODEL),
                  _resident((D_MODEL, SWA_Q_W)),
                  _resident((D_MODEL, 2 * SWA_KV_W)),
                  _resident((D_MODEL, 2 * D_MODEL))] + cast_specs,
        out_specs=[row(SWA_Q_W), row(2 * SWA_KV_W), row(2 * D_MODEL)] + cast_specs,
        out_shape=[jax.ShapeDtypeStruct((R, SWA_Q_W), BF16),
                   jax.ShapeDtypeStruct((R, 2 * SWA_KV_W), F32),
                   jax.ShapeDtypeStruct((R, 2 * D_MODEL), BF16)]
                  + [jax.ShapeDtypeStruct(w.shape, BF16) for w in cast_2d],
        compiler_params=_cparams(("parallel",)),
        name="inproj_rest",
    )(h, wsq, wkv, wbg, *cast_2d)
    sq, kv, gate = outs[:3]
    return ret, sq, kv, gate, [o.reshape(w.shape) for o, w in zip(outs[3:], cast)]


def _ret_kernel(blk_ref, s0_ref, gn_ref, r_ref, sout_ref, s_scr, *, lc, n_sub):
    c = pl.program_id(1)

    @pl.when(c == 0)
    def _():
        s_scr[...] = s0_ref[0]

    causal = (lax.broadcasted_iota(jnp.int32, (lc, lc), 0) >= lax.broadcasted_iota(jnp.int32, (lc, lc), 1))
    nt = (((1,), (1,)), ((), ()))
    tn = (((0,), (0,)), ((), ()))
    for h in range(RET_HEADS):
        lo, hi = h * RET_DK, (h + 1) * RET_DK
        state = s_scr[h]
        for sub in range(n_sub):
            rows = slice(sub * lc, (sub + 1) * lc)
            q = blk_ref[rows, lo:hi]
            k = blk_ref[rows, RET_W + lo:RET_W + hi]
            v = blk_ref[rows, 2 * RET_W + lo:2 * RET_W + hi]
            g = blk_ref[rows, 3 * RET_W + lo:3 * RET_W + hi].astype(F32)
            s = jnp.where(causal, lax.dot_general(q, k, nt, preferred_element_type=F32), 0.0)
            o = (jnp.dot(s.astype(BF16), v, preferred_element_type=F32)
                 + jnp.dot(q, state.astype(BF16), preferred_element_type=F32))
            state = math.exp(lc * RET_LOG_GAMMA[h]) * (state + lax.dot_general(k, v, tn, preferred_element_type=F32))
            mu = jnp.mean(o, axis=-1, keepdims=True)
            d = o - mu
            var = jnp.mean(d * d, axis=-1, keepdims=True)
            on = d * lax.rsqrt(var + EPS) * gn_ref[:, lo:hi]
            r_ref[rows, lo:hi] = (on * (g * jax.nn.sigmoid(g))).astype(BF16)
        s_scr[h] = state

    @pl.when(c == pl.num_programs(1) - 1)
    def _():
        sout_ref[0] = s_scr[...]


def _retention(ret_all, s0, gn_g, *, n_seq, seq_len, lc, n_sub):
    R = ret_all.shape[0]
    rows = lc * n_sub
    nc = seq_len // rows
    st_spec = pl.BlockSpec((1, RET_HEADS, RET_DK, RET_DV), lambda b, c: (b, 0, 0, 0))
    return pl.pallas_call(
        functools.partial(_ret_kernel, lc=lc, n_sub=n_sub),
        grid=(n_seq, nc),
        in_specs=[pl.BlockSpec((rows, 4 * RET_W), lambda b, c: (b * nc + c, 0)),
                  st_spec,
                  _resident((1, RET_W))],
        out_specs=[pl.BlockSpec((rows, RET_W), lambda b, c: (b * nc + c, 0)), st_spec],
        out_shape=[jax.ShapeDtypeStruct((R, RET_W), BF16),
                   jax.ShapeDtypeStruct((n_seq, RET_HEADS, RET_DK, RET_DV), F32)],
        scratch_shapes=[pltpu.VMEM((RET_HEADS, RET_DK, RET_DV), F32)],
        compiler_params=_cparams(("parallel", "arbitrary")),
        name="retention",
    )(ret_all, s0, gn_g)


KEYS = WINDOW + CHUNK
KPAD = 256


def _swa_kernel(sink_ref, q_ref, k2_ref, k1_ref, k0_ref, v2_ref, v1_ref, v0_ref, o_ref, *, masked, n_q):
    j = pl.program_id(1)
    kall = jnp.concatenate([k2_ref[...], k1_ref[...], k0_ref[...]], axis=0)
    vall = jnp.concatenate([v2_ref[...], v1_ref[...], v0_ref[...]], axis=0)
    lane = lax.broadcasted_iota(jnp.int32, kall.shape, 1)
    zpad = jnp.zeros((KPAD - KEYS, LANES), BF16)

    def lane_halves(win, h):
        rolled = pltpu.roll(win, SWA_HEAD_DIM, 1)
        lo_src, hi_src = (win, rolled) if h == 0 else (rolled, win)
        return (jnp.where(lane < SWA_HEAD_DIM, lo_src, 0.0).astype(BF16),
                jnp.where(lane >= SWA_HEAD_DIM, hi_src, 0.0).astype(BF16))

    col = lax.broadcasted_iota(jnp.int32, (1, KPAD), 1)
    n_pairs = SWA_GROUP // 2
    rows = n_pairs * CHUNK
    row = lax.broadcasted_iota(jnp.int32, (rows, 1), 0)
    out_lane = lax.broadcasted_iota(jnp.int32, (rows, LANES), 1)
    nt = (((1,), (1,)), ((), ()))
    q_scale = jnp.asarray(SWA_HEAD_DIM ** -0.5, BF16)
    for h in range(SWA_KV_HEADS):
        k_lo, k_hi = lane_halves(kall, h)
        v_lo, v_hi = lane_halves(vall, h)
        base = h * SWA_GROUP * SWA_HEAD_DIM
        sinks = []
        for half in range(2):
            sink = jnp.zeros((rows, 1), F32)
            for p in range(n_pairs):
                sink = jnp.where(row // CHUNK == p, sink_ref[h * SWA_GROUP + 2 * p + half], sink)
            sinks.append(sink)
        for u in range(n_q):
            r0 = u * CHUNK
            c = j * n_q + u
            if masked:
                first_ok = jnp.where(c >= 2, 0, jnp.where(c == 1, CHUNK, 2 * CHUNK))
                ok = (col >= first_ok) & (col < KEYS)
            else:
                ok = col < KEYS
            kk = jnp.concatenate([k_lo[r0:r0 + KEYS], zpad, k_hi[r0:r0 + KEYS], zpad], axis=0)
            vv = jnp.concatenate([v_lo[r0:r0 + KEYS], zpad, v_hi[r0:r0 + KEYS], zpad], axis=0)
            q4 = jnp.concatenate([q_ref[r0:r0 + CHUNK, base + p * LANES: base + (p + 1) * LANES]
                                  for p in range(n_pairs)], axis=0)
            s = lax.dot_general(q4 * q_scale, kk, nt, preferred_element_type=F32)
            ps, invs = [], []
            for half in range(2):
                sh = jnp.where(ok, s[:, half * KPAD:(half + 1) * KPAD], NEG_INF)
                m = jnp.maximum(jnp.max(sh, axis=-1, keepdims=True), sinks[half])
                p_half = jnp.exp(sh - m)
                den = jnp.sum(p_half, axis=-1, keepdims=True) + jnp.exp(sinks[half] - m)
                ps.append(p_half.astype(BF16))
                invs.append(1.0 / den)
            pv = jnp.dot(jnp.concatenate(ps, axis=1), vv, preferred_element_type=F32)
            o = pv * jnp.where(out_lane < SWA_HEAD_DIM, invs[0], invs[1])
            for p in range(n_pairs):
                o_ref[r0:r0 + CHUNK, base + p * LANES: base + (p + 1) * LANES] = (
                    o[p * CHUNK:(p + 1) * CHUNK].astype(BF16))


def _swa(sinks, sq, k_arrs, v_arrs, k_maps, v_maps, *, n_seq, nc, masked, n_q):
    R = sq.shape[0]
    kv_rows = (CHUNK, CHUNK, n_q * CHUNK)
    kv_specs = [pl.BlockSpec((kv_rows[t % 3], SWA_KV_W), m) for t, m in enumerate((*k_maps, *v_maps))]
    return pl.pallas_call(
        functools.partial(_swa_kernel, masked=masked, n_q=n_q),
        grid=(n_seq, nc),
        in_specs=[pl.BlockSpec(memory_space=pltpu.SMEM),
                  pl.BlockSpec((n_q * CHUNK, SWA_Q_W), lambda b, c: (b * nc + c, 0)),
                  *kv_specs],
        out_specs=pl.BlockSpec((n_q * CHUNK, SWA_Q_W), lambda b, c: (b * nc + c, 0)),
        out_shape=jax.ShapeDtypeStruct((R, SWA_Q_W), BF16),
        compiler_params=_cparams(("parallel", "arbitrary")),
        name="swa",
    )(sinks, sq, *k_arrs, *v_arrs)


def _route(logits):
    tm = logits.shape[0]
    lane = lax.broadcasted_iota(jnp.int32, (tm, LANES), 1)
    is_g = lane < N_GROUPS
    gl = jnp.where(is_g, logits, NEG_INF)
    gmax = jnp.max(gl, axis=-1, keepdims=True)
    gidx = jnp.min(jnp.where(gl == gmax, lane, LANES), axis=-1, keepdims=True)
    gsum = jnp.sum(jnp.where(is_g, jnp.exp(gl - gmax), 0.0), axis=-1, keepdims=True)
    g_w = 1.0 / gsum
    base = N_GROUPS + EXPERTS_PER_GROUP * gidx
    el = jnp.where((lane >= base) & (lane < base + EXPERTS_PER_GROUP), logits, NEG_INF)
    v1 = jnp.max(el, axis=-1, keepdims=True)
    i1 = jnp.min(jnp.where(el == v1, lane, LANES), axis=-1, keepdims=True)
    el2 = jnp.where(lane == i1, NEG_INF, el)
    v2 = jnp.max(el2, axis=-1, keepdims=True)
    i2 = jnp.min(jnp.where(el2 == v2, lane, LANES), axis=-1, keepdims=True)
    e2 = jnp.exp(v2 - v1)
    den = 1.0 + e2
    w1 = g_w / den
    w2 = g_w * e2 / den
    l1 = i1 - base
    l2 = i2 - base
    first_lo = l1 < l2
    la = jnp.where(first_lo, l1, l2)
    lb = jnp.where(first_lo, l2, l1)
    wa = jnp.where(first_lo, w1, w2)
    wb = jnp.where(first_lo, w2, w1)
    pair = jnp.where(la == 0, lb - 1, jnp.where(la == 1, jnp.where(lb == 3, 3, 4), 5))
    swapped = la == 2
    w_slot_a = jnp.where(swapped, wb, wa)
    w_slot_b = jnp.where(swapped, wa, wb)
    cls = (gidx * N_PAIRS + pair).astype(F32)
    return jnp.where(lane == 0, w_slot_a, jnp.where(lane == 1, w_slot_b, jnp.where(lane == 2, cls, 0.0)))


ROW_W = D_MODEL + LANES


def _merge_kernel(x_ref, r_ref, o_ref, gate_ref, mod_ref, modp_ref, g2_ref, wrb_ref, wsb_ref, wout_ref, wr_ref,
                  br_ref, x1_ref, text_ref, meta_ref, x1_scr, *, n_seq):
    i = pl.program_id(0)
    n = pl.num_programs(0) - 1
    tm = x_ref.shape[0]

    def matmul_stage():
        g_r = gate_ref[:, :D_MODEL].astype(F32)
        g_s = gate_ref[:, D_MODEL:].astype(F32)
        merged = (g_r * jnp.dot(r_ref[...], wrb_ref[...], preferred_element_type=F32)
                  + g_s * jnp.dot(o_ref[...], wsb_ref[...], preferred_element_type=F32))
        mix = jnp.dot(merged.astype(BF16), wout_ref[...], preferred_element_type=F32)
        gt1 = mod_ref[:, 2, :]
        x1 = (x_ref[...].reshape(n_seq, tm // n_seq, D_MODEL) + gt1[:, None, :]
              * mix.reshape(n_seq, tm // n_seq, D_MODEL)).reshape(tm, D_MODEL)
        x1_ref[...] = x1
        x1_scr[...] = x1

    def vector_stage():
        t = _modulated_norm(x1_scr[...], g2_ref[...], modp_ref[:, 3, :], modp_ref[:, 4, :], n_seq)
        logits = jnp.dot(t.astype(BF16), wr_ref[...], preferred_element_type=F32) + br_ref[...]
        meta = _route(logits)
        text_ref[:, :D_MODEL] = t
        text_ref[:, D_MODEL:] = meta
        meta_ref[...] = meta

    @pl.when(i == 0)
    def _():
        matmul_stage()

    @pl.when((i > 0) & (i < n))
    def _():
        vector_stage()
        matmul_stage()

    @pl.when(i == n)
    def _():
        vector_stage()


def _merge(x2d, r, o_swa, gates, mod3, g2, wrb, wsb, wout, wr, br, *, seq_len, tm):
    R = x2d.shape[0]
    n = R // tm
    if seq_len >= tm:
        n_seq, tps = 1, seq_len // tm
        seq_of = lambda t: t // tps
    else:
        n_seq = tm // seq_len
        seq_of = lambda t: t
    cur = lambda i: jnp.minimum(i, n - 1)
    prev = lambda i: jnp.maximum(i - 1, 0)
    row = lambda w: pl.BlockSpec((tm, w), lambda i: (cur(i), 0))
    row_prev = lambda w: pl.BlockSpec((tm, w), lambda i: (prev(i), 0))
    return pl.pallas_call(
        functools.partial(_merge_kernel, n_seq=n_seq),
        grid=(n + 1,),
        in_specs=[row(D_MODEL), row(RET_W), row(SWA_Q_W), row(2 * D_MODEL),
                  pl.BlockSpec((n_seq, N_MOD, D_MODEL), lambda i: (seq_of(cur(i)), 0, 0)),
                  pl.BlockSpec((n_seq, N_MOD, D_MODEL), lambda i: (seq_of(prev(i)), 0, 0)),
                  _resident((1, D_MODEL)),
                  _resident((RET_W, D_MODEL)), _resident((SWA_Q_W, D_MODEL)), _resident((D_MODEL, D_MODEL)),
                  _resident((D_MODEL, LANES)), _resident((1, LANES))],
        out_specs=[row(D_MODEL), row_prev(ROW_W), row_prev(LANES)],
        out_shape=[jax.ShapeDtypeStruct((R, D_MODEL), F32),
                   jax.ShapeDtypeStruct((R, ROW_W), F32),
                   jax.ShapeDtypeStruct((R, LANES), F32)],
        scratch_shapes=[pltpu.VMEM((tm, D_MODEL), F32)],
        compiler_params=_cparams(("arbitrary",)),
        name="merge",
    )(x2d, r, o_swa, gates, mod3, mod3, g2, wrb, wsb, wout, wr, br)


PLAN_BLK = 2048
TILE_ROWS = 256


def _plan_kernel(meta_a_ref, meta_b_ref, pos_ref, tile_ref, pad_ref, cnt_scr, offs_scr, carry_scr, tri_scr,
                 *, tm, nb_a):
    ph = pl.program_id(0)
    b = pl.program_id(1)
    blk = meta_a_ref.shape[0]
    lane = lax.broadcasted_iota(jnp.int32, (blk, LANES), 1)
    cls_col = jnp.where(b < nb_a, meta_a_ref[:, 2:3], meta_b_ref[:, 2:3])
    oh = jnp.where(cls_col == lane.astype(F32), 1.0, 0.0)

    @pl.when((ph == 0) & (b == 0))
    def _():
        cnt_scr[...] = jnp.zeros_like(cnt_scr)
        ri = lax.broadcasted_iota(jnp.int32, (blk, blk), 0)
        ci = lax.broadcasted_iota(jnp.int32, (blk, blk), 1)
        tri_scr[...] = jnp.where(ci <= ri, 1.0, 0.0).astype(BF16)

    @pl.when(ph == 0)
    def _():
        cnt_scr[...] += jnp.sum(oh, axis=0, keepdims=True)

    @pl.when((ph == 1) & (b == 0))
    def _():
        cnt = cnt_scr[...]
        ptiles = jnp.floor((cnt + (tm - 1)) * (1.0 / tm))
        ri = lax.broadcasted_iota(jnp.int32, (LANES, LANES), 0)
        ci = lax.broadcasted_iota(jnp.int32, (LANES, LANES), 1)
        before = jnp.where(ri < ci, 1.0, 0.0).astype(BF16)
        offs = jnp.dot(ptiles.astype(BF16), before, preferred_element_type=F32) * tm
        offs_scr[...] = offs
        carry_scr[...] = jnp.zeros_like(carry_scr)
        padded = ptiles * tm
        ends = offs + padded
        tl = lax.broadcasted_iota(jnp.int32, (TILE_ROWS, LANES), 1)
        tstart = lax.broadcasted_iota(jnp.int32, (TILE_ROWS, LANES), 0).astype(F32) * tm
        tcls = jnp.sum(jnp.where((ends[0:1, :] <= tstart) & (tl < N_CLASSES), 1.0, 0.0), axis=1, keepdims=True)
        tcls = jnp.minimum(tcls, N_CLASSES - 1.0)
        total = jnp.max(ends[0:1, :], axis=1, keepdims=True)
        grp = (jnp.where(tcls >= N_PAIRS, 1.0, 0.0) + jnp.where(tcls >= 2 * N_PAIRS, 1.0, 0.0)
               + jnp.where(tcls >= 3 * N_PAIRS, 1.0, 0.0))
        pair = tcls - N_PAIRS * grp
        la = jnp.where(pair < 3, 0.0, jnp.where(pair < 5, 1.0, 3.0))
        lb = jnp.where(pair == 0, 1.0, jnp.where((pair == 1) | (pair >= 4), 2.0, 3.0))
        ea = EXPERTS_PER_GROUP * grp + la
        eb = EXPERTS_PER_GROUP * grp + lb
        n_used = total * (1.0 / tm)
        tile_ref[...] = jnp.where(tl == 0, ea, jnp.where(tl == 1, eb, jnp.where(tl == 2, n_used, 0.0))
                                  ).astype(jnp.int32)
        npad = padded - cnt
        pstart = jnp.dot(npad.astype(BF16), before, preferred_element_type=F32)
        n_class_pad = jnp.sum(npad[0:1, :], axis=1, keepdims=True)
        rows = pad_ref.shape[0]
        v = (lax.broadcasted_iota(jnp.int32, (rows, LANES), 0) * LANES
             + lax.broadcasted_iota(jnp.int32, (rows, LANES), 1)).astype(F32)
        slot = jnp.where(v >= n_class_pad, total - n_class_pad + v, 0.0)
        for c in range(N_CLASSES):
            ps = pstart[0:1, c:c + 1]
            inside = (v >= ps) & (v < ps + npad[0:1, c:c + 1])
            slot = jnp.where(inside, offs[0:1, c:c + 1] + cnt[0:1, c:c + 1] - ps + v, slot)
        pad_ref[...] = slot.astype(jnp.int32)

    @pl.when(ph == 1)
    def _():
        incl = jnp.dot(tri_scr[...], oh.astype(BF16), preferred_element_type=F32)
        base = offs_scr[0:1, :] + carry_scr[0:1, :]
        pos = jnp.sum(oh * (base + incl - oh), axis=1, keepdims=True)
        pos_ref[...] = jnp.broadcast_to(pos, (blk, LANES)).astype(jnp.int32)
        carry_scr[...] += jnp.sum(oh, axis=0, keepdims=True)


def _plan(meta_a, meta_b, *, tm):
    na, nb = meta_a.shape[0], meta_b.shape[0]
    n = na + nb
    n_tiles = n // tm + N_CLASSES
    n_free = N_CLASSES * tm
    assert n_tiles <= TILE_ROWS and na % PLAN_BLK == 0 and nb % PLAN_BLK == 0 and n_free % LANES == 0
    n_slots = n_tiles * tm
    nb_a = na // PLAN_BLK
    nb_b = nb // PLAN_BLK
    pos2d, tile2d, pad2d = pl.pallas_call(
        functools.partial(_plan_kernel, tm=tm, nb_a=nb_a),
        grid=(2, nb_a + nb_b),
        in_specs=[pl.BlockSpec((PLAN_BLK, LANES), lambda ph, b: (jnp.minimum(b, nb_a - 1), 0)),
                  pl.BlockSpec((PLAN_BLK, LANES), lambda ph, b: (jnp.maximum(b - nb_a, 0), 0))],
        out_specs=[pl.BlockSpec((PLAN_BLK, LANES), lambda ph, b: (b * ph, 0)),
                   pl.BlockSpec((TILE_ROWS, LANES), lambda ph, b: (0, 0)),
                   pl.BlockSpec((n_free // LANES, LANES), lambda ph, b: (0, 0))],
        out_shape=[jax.ShapeDtypeStruct((n, LANES), jnp.int32),
                   jax.ShapeDtypeStruct((TILE_ROWS, LANES), jnp.int32),
                   jax.ShapeDtypeStruct((n_free // LANES, LANES), jnp.int32)],
        scratch_shapes=[pltpu.VMEM((8, LANES), F32), pltpu.VMEM((8, LANES), F32), pltpu.VMEM((8, LANES), F32),
                        pltpu.VMEM((PLAN_BLK, PLAN_BLK), BF16)],
        compiler_params=_cparams(("arbitrary", "arbitrary")),
        name="plan",
    )(meta_a, meta_b)
    return pos2d[:, 0], pad2d.reshape(-1), tile2d[:n_tiles, 0], tile2d[:n_tiles, 1], tile2d[0:1, 2], n_slots


N_STAGE = 3


def _scatter_kernel(pos_ref, text_a_ref, text_b_ref, out_ref, stage, zero_scr, sem_in, sem_out, *, nb_a, nb_b):
    i = pl.program_id(0)
    n = pl.num_programs(0)
    n_tok = nb_a + nb_b
    tb = pos_ref.shape[2]
    slot = i % N_STAGE

    def load(step, s):
        def from_a():
            return pltpu.make_async_copy(text_a_ref.at[pl.ds(step * tb, tb), :], stage.at[s], sem_in.at[s])

        def from_b():
            return pltpu.make_async_copy(text_b_ref.at[pl.ds((step - nb_a) * tb, tb), :], stage.at[s], sem_in.at[s])

        return from_a, from_b

    def start_load(step, s):
        from_a, from_b = load(step, s)

        @pl.when(step < nb_a)
        def _():
            from_a().start()

        @pl.when((step >= nb_a) & (step < n_tok))
        def _():
            from_b().start()

    def wait_rows(s):
        pltpu.make_async_copy(stage.at[s], out_ref.at[pl.ds(0, tb), :], sem_out.at[s]).wait()

    def scatter_rows(src_row):
        def body(r, carry):
            dst = pos_ref[0, 0, r]
            pltpu.make_async_copy(src_row(r), out_ref.at[pl.ds(dst, 1), :], sem_out.at[slot]).start()
            return carry

        lax.fori_loop(0, tb, body, 0, unroll=8)

    @pl.when(i == 0)
    def _():
        zero_scr[...] = jnp.zeros_like(zero_scr)
        start_load(i, slot)

    @pl.when(i >= 2)
    def _():
        wait_rows((i + 1) % N_STAGE)

    start_load(i + 1, (i + 1) % N_STAGE)

    @pl.when(i < n_tok)
    def _():
        pltpu.make_async_copy(text_a_ref.at[pl.ds(0, tb), :], stage.at[slot], sem_in.at[slot]).wait()
        scatter_rows(lambda r: stage.at[slot, pl.ds(r, 1), :])

    @pl.when(i >= n_tok)
    def _():
        scatter_rows(lambda r: zero_scr.at[pl.ds(0, 1), :])

    @pl.when(i == n - 1)
    def _():
        wait_rows((i + 2) % N_STAGE)
        wait_rows(slot)


def _scatter_rows(text_a, text_b, pos_ext, *, tb):
    nb_a = text_a.shape[0] // tb
    nb_b = text_b.shape[0] // tb
    n_steps = pos_ext.shape[0] // tb
    assert n_steps >= 2 and nb_a >= 1 and nb_b >= 1
    return pl.pallas_call(
        functools.partial(_scatter_kernel, nb_a=nb_a, nb_b=nb_b),
        grid=(n_steps,),
        in_specs=[pl.BlockSpec((1, 1, tb), lambda i: (i, 0, 0), memory_space=pltpu.SMEM),
                  pl.BlockSpec(memory_space=pl.ANY),
                  pl.BlockSpec(memory_space=pl.ANY)],
        out_specs=pl.BlockSpec(memory_space=pl.ANY),
        out_shape=jax.ShapeDtypeStruct((pos_ext.shape[0], ROW_W), F32),
        scratch_shapes=[pltpu.VMEM((N_STAGE, tb, ROW_W), F32), pltpu.VMEM((8, ROW_W), F32),
                        pltpu.SemaphoreType.DMA((N_STAGE,)), pltpu.SemaphoreType.DMA((N_STAGE,))],
        compiler_params=_cparams(("arbitrary",)),
        name="scatter_rows",
    )(pos_ext.reshape(n_steps, 1, tb), text_a, text_b)


def _moe_kernel(ea_ref, eb_ref, nused_ref, xs_ref, w1a_ref, w3a_ref, w2a_ref, w1b_ref, w3b_ref, w2b_ref, y_ref):
    i = pl.program_id(0)

    @pl.when(i >= nused_ref[0])
    def _():
        y_ref[...] = jnp.zeros_like(y_ref)

    @pl.when(i < nused_ref[0])
    def _():
        x = xs_ref[:, :D_MODEL].astype(BF16)
        wa = xs_ref[:, D_MODEL:D_MODEL + 1]
        wb = xs_ref[:, D_MODEL + 1:D_MODEL + 2]

        def hidden(w1_ref, w3_ref, gate):
            a = jnp.dot(x, w1_ref[0], preferred_element_type=F32)
            b = jnp.dot(x, w3_ref[0], preferred_element_type=F32)
            return (a * jax.nn.sigmoid(a) * b * gate).astype(BF16)

        ha = hidden(w1a_ref, w3a_ref, wa)
        hb = hidden(w1b_ref, w3b_ref, wb)
        y_ref[...] = (jnp.dot(ha, w2a_ref[0], preferred_element_type=F32)
                      + jnp.dot(hb, w2b_ref[0], preferred_element_type=F32))


def _moe(tile_ea, tile_eb, n_used, xs, w1, w3, w2, *, tm, n_tiles):
    last = lambda i, nu: jnp.minimum(i, nu[0] - 1)
    wa_map = lambda i, ea, eb, nu: (ea[last(i, nu)], 0, 0)
    wb_map = lambda i, ea, eb, nu: (eb[last(i, nu)], 0, 0)
    row_map = lambda i, ea, eb, nu: (last(i, nu), 0)
    up = (1, D_MODEL, D_EXPERT)
    down = (1, D_EXPERT, D_MODEL)
    grid_spec = pltpu.PrefetchScalarGridSpec(
        num_scalar_prefetch=3,
        grid=(n_tiles,),
        in_specs=[pl.BlockSpec((tm, D_MODEL + LANES), row_map),
                  pl.BlockSpec(up, wa_map), pl.BlockSpec(up, wa_map), pl.BlockSpec(down, wa_map),
                  pl.BlockSpec(up, wb_map), pl.BlockSpec(up, wb_map), pl.BlockSpec(down, wb_map)],
        out_specs=pl.BlockSpec((tm, D_MODEL), lambda i, ea, eb, nu: (i, 0)),
    )
    return pl.pallas_call(
        _moe_kernel,
        grid_spec=grid_spec,
        out_shape=jax.ShapeDtypeStruct((n_tiles * tm, D_MODEL), F32),
        compiler_params=_cparams(("arbitrary",)),
        name="moe",
    )(tile_ea, tile_eb, n_used, xs, w1, w3, w2, w1, w3, w2)


def _final_kernel(pos_ref, pos_next_ref, x1_ref, mod_ref, g_ref, ys_ref, o_ref, ybuf, sem, *, n_seq):
    i = pl.program_id(0)
    n = pl.num_programs(0)
    tm = x1_ref.shape[0]
    slot = i % 2

    def start_gather(idx_ref, s):
        def body(r, carry):
            src = idx_ref[0, 0, r]
            pltpu.make_async_copy(ys_ref.at[pl.ds(src, 1), :], ybuf.at[s, pl.ds(r, 1), :], sem.at[s]).start()
            return carry

        lax.fori_loop(0, tm, body, 0, unroll=8)

    @pl.when(i == 0)
    def _():
        start_gather(pos_ref, 0)

    @pl.when(i + 1 < n)
    def _():
        start_gather(pos_next_ref, 1 - slot)

    pltpu.make_async_copy(ys_ref.at[pl.ds(0, tm), :], ybuf.at[slot], sem.at[slot]).wait()
    gt2 = mod_ref[:, 5, :]
    x2 = (x1_ref[...].reshape(n_seq, tm // n_seq, D_MODEL)
          + gt2[:, None, :] * ybuf[slot].reshape(n_seq, tm // n_seq, D_MODEL)).reshape(tm, D_MODEL)
    ms = jnp.mean(x2 * x2, axis=-1, keepdims=True)
    o_ref[...] = x2 * lax.rsqrt(ms + EPS) * g_ref[...]


def _final(x1, y_sorted, pos, mod3, gf, *, seq_len, tm):
    R = x1.shape[0]
    if seq_len >= tm:
        n_seq, tps = 1, seq_len // tm
        mod_map = lambda i: (i // tps, 0, 0)
    else:
        n_seq = tm // seq_len
        mod_map = lambda i: (i, 0, 0)
    n = R // tm
    row = pl.BlockSpec((tm, D_MODEL), lambda i: (i, 0))
    pos3 = pos.reshape(n, 1, tm)
    return pl.pallas_call(
        functools.partial(_final_kernel, n_seq=n_seq),
        grid=(n,),
        in_specs=[pl.BlockSpec((1, 1, tm), lambda i: (i, 0, 0), memory_space=pltpu.SMEM),
                  pl.BlockSpec((1, 1, tm), lambda i: (jnp.minimum(i + 1, n - 1), 0, 0), memory_space=pltpu.SMEM),
                  row, pl.BlockSpec((n_seq, N_MOD, D_MODEL), mod_map), _resident((1, D_MODEL)),
                  pl.BlockSpec(memory_space=pl.ANY)],
        out_specs=row,
        out_shape=jax.ShapeDtypeStruct((R, D_MODEL), F32),
        scratch_shapes=[pltpu.VMEM((2, tm, D_MODEL), F32), pltpu.SemaphoreType.DMA((2,))],
        compiler_params=_cparams(("arbitrary",)),
        name="final",
    )(pos3, pos3, x1, mod3, gf, y_sorted)


def _rope_tables(pos):
    half = RET_DK // 2
    inv = ROPE_BASE ** (-jnp.arange(half, dtype=F32) / half)
    ang = pos.astype(F32)[:, None] * inv[None, :]
    cos = jnp.cos(ang)
    sin = jnp.sin(ang)
    return jnp.concatenate([cos, cos], axis=-1), jnp.concatenate([-sin, sin], axis=-1)


def kernel(x_prompt, x_sample, cache_ret_state, cache_swa_k, cache_swa_v, c_prompt, c_sample,
           norm1_g, norm2_g, ada_w, ada_b, w_in, ret_gn_g, swa_sinks, w_ret_branch, w_swa_branch, w_out,
           router_group_w, router_group_b, router_expert_w, router_expert_b,
           expert_w1, expert_w3, expert_w2, final_norm_g):
    depth = w_in.shape[0]
    assert depth == 1
    bp, tp, _ = x_prompt.shape
    bs, ts, _ = x_sample.shape
    past = WINDOW
    assert cache_swa_k.shape[2] == past and ts == CHUNK and tp % 512 == 0
    tm = 512

    l = 0
    c1 = 4 * RET_W
    c2 = c1 + SWA_Q_W
    c3 = c2 + 2 * SWA_KV_W
    wret = w_in[l, :, :c1].astype(BF16)
    wsq = w_in[l, :, c1:c2].astype(BF16)
    wkv = w_in[l, :, c2:c3].astype(BF16)
    wbg = w_in[l, :, c3:].astype(BF16)
    wrb = w_ret_branch[l].astype(BF16)
    wsb = w_swa_branch[l].astype(BF16)
    wo = w_out[l].astype(BF16)
    n_r = N_GROUPS + N_EXPERTS
    wr = jnp.zeros((D_MODEL, LANES), F32).at[:, :N_GROUPS].set(router_group_w[l]).at[:, N_GROUPS:n_r].set(
        router_expert_w[l]).astype(BF16)
    br = jnp.zeros((1, LANES), F32).at[0, :N_GROUPS].set(router_group_b[l]).at[0, N_GROUPS:n_r].set(
        router_expert_b[l])
    g1 =norm1_g[l].reshape(1, D_MODEL)
    g2 = norm2_g[l].reshape(1, D_MODEL)
    gn = ret_gn_g[l].reshape(1, RET_W)
    gf = final_norm_g.reshape(1, D_MODEL)
    sinks = swa_sinks[l]

    c_all = jnp.concatenate([c_prompt, c_sample], axis=0)
    mod = _ada(c_all, ada_w[l], ada_b[l]).reshape(bp + bs, N_MOD, D_MODEL)
    mod_p, mod_s = mod[:bp], mod[bp:]

    cos_p, sin_p = _rope_tables(jnp.arange(tp))
    cos_s, sin_s = _rope_tables(PAST_LEN + jnp.arange(ts))
    rep = tm // ts
    cos_s, sin_s = jnp.tile(cos_s, (rep, 1)), jnp.tile(sin_s, (rep, 1))

    xp = x_prompt.reshape(bp * tp, D_MODEL)
    xs = x_sample.reshape(bs * ts, D_MODEL)

    lc_p = 128
    ret_p, sq_p, kv_p, gate_p, (w1, w3, w2) = _inproj(
        xp, mod_p, g1, cos_p, sin_p, wret, wsq, wkv, wbg, seq_len=tp, tm=tm, lc=lc_p,
        cast=(expert_w1[l], expert_w3[l], expert_w2[l]))
    ret_s, sq_s, kv_s, gate_s, _ = _inproj(xs, mod_s, g1, cos_s, sin_s, wret, wsq, wkv, wbg, seq_len=ts, tm=tm,
                                           lc=ts)

    s0_p = jnp.zeros((bp, RET_HEADS, RET_DK, RET_DV), F32)
    r_p, state_p = _retention(ret_p, s0_p, gn, n_seq=bp, seq_len=tp, lc=lc_p, n_sub=2)
    r_s, state_s = _retention(ret_s, cache_ret_state[l].astype(F32), gn, n_seq=bs, seq_len=ts, lc=ts, n_sub=1)

    n_q = 2
    nc_p = tp // CHUNK
    ns_p = nc_p // n_q
    prev_map = lambda back, colblk: (lambda b, s: (b * nc_p + jnp.maximum(s * n_q - back, 0), colblk))
    own_map = lambda colblk: (lambda b, s: (b * ns_p + s, colblk))
    o_p = _swa(sinks, sq_p, [kv_p] * 3, [kv_p] * 3,
               [prev_map(2, 0), prev_map(1, 0), own_map(0)], [prev_map(2, 1), prev_map(1, 1), own_map(1)],
               n_seq=bp, nc=ns_p, masked=True, n_q=n_q)
    ck = cache_swa_k[l].reshape(bs * past, SWA_KV_W)
    cv = cache_swa_v[l].reshape(bs * past, SWA_KV_W)
    cmap = lambda blk: (lambda b, c: (2 * b + blk, 0))
    o_s = _swa(sinks, sq_s, [ck, ck, kv_s], [cv, cv, kv_s],
               [cmap(0), cmap(1), lambda b, c: (b, 0)], [cmap(0), cmap(1), lambda b, c: (b, 1)],
               n_seq=bs, nc=1, masked=False, n_q=1)

    tm_m = 256
    n_p = bp * tp
    x1_p, text_p, meta_p = _merge(xp, r_p, o_p, gate_p, mod_p, g2, wrb, wsb, wo, wr, br, seq_len=tp, tm=tm_m)
    x1_s, text_s, meta_s = _merge(xs, r_s, o_s, gate_s, mod_s, g2, wrb, wsb, wo, wr, br, seq_len=ts, tm=tm_m)

    tm_e = 256
    pos, free_slots, tile_ea, tile_eb, n_used, n_slots = _plan(meta_p, meta_s, tm=tm_e)
    pos_ext = jnp.concatenate([pos, free_slots], axis=0)
    xsorted = _scatter_rows(text_p, text_s, pos_ext, tb=512)
    y_sorted = _moe(tile_ea, tile_eb, n_used, xsorted, w1, w3, w2, tm=tm_e, n_tiles=n_slots // tm_e)

    tm_f = 512
    out_p = _final(x1_p, y_sorted, pos[:n_p], mod_p, gf, seq_len=tp, tm=tm_f)
    out_s = _final(x1_s, y_sorted, pos[n_p:], mod_s, gf, seq_len=ts, tm=tm_f)

    y_prompt = out_p.reshape(bp, tp, D_MODEL)
    y_sample = out_s.reshape(bs, ts, D_MODEL)
    kvp = kv_p.reshape(bp, tp, 2 * SWA_KV_W)[:, tp - WINDOW:].reshape(bp, WINDOW, 2, SWA_KV_HEADS, SWA_HEAD_DIM)
    kvs = kv_s.reshape(bs, ts, 2, SWA_KV_HEADS, SWA_HEAD_DIM)
    k_s = jnp.concatenate([cache_swa_k[l].astype(F32), kvs[:, :, 0]], axis=1)[:, -WINDOW:]
    v_s = jnp.concatenate([cache_swa_v[l].astype(F32), kvs[:, :, 1]], axis=1)[:, -WINDOW:]
    return (y_prompt, y_sample, state_p[None], kvp[:, :, 0][None], kvp[:, :, 1][None],
            state_s[None], k_s[None], v_s[None])
```

```python
import functools
import math

import jax
import jax.numpy as jnp
from jax import lax
from jax.experimental import pallas as pl
from jax.experimental.pallas import tpu as pltpu

F32 = jnp.float32
BF16 = jnp.bfloat16

D_MODEL = 2048
CHUNK = 64
RET_HEADS = 8
RET_DK = 128
RET_DV = 128
RET_W = RET_HEADS * RET_DK
ROPE_BASE = 10000.0
SWA_Q_HEADS = 16
SWA_KV_HEADS = 2
SWA_GROUP = SWA_Q_HEADS // SWA_KV_HEADS
SWA_HEAD_DIM = 64
SWA_Q_W = SWA_Q_HEADS * SWA_HEAD_DIM
SWA_KV_W = SWA_KV_HEADS * SWA_HEAD_DIM
WINDOW = 128
PAST_LEN = 1024
N_GROUPS = 4
EXPERTS_PER_GROUP = 4
N_EXPERTS = 16
D_EXPERT = 512
N_MOD = 6
EPS = 1e-6
NEG_INF = -1e30
N_PAIRS = 6
N_CLASSES = N_GROUPS * N_PAIRS

LANES = 128
VMEM_LIMIT = 56 * 1024 * 1024

RET_LOG_GAMMA = tuple(math.log1p(-(2.0 ** (-5.0 - h))) for h in range(RET_HEADS))


def _cparams(sem):
    return pltpu.CompilerParams(dimension_semantics=sem, vmem_limit_bytes=VMEM_LIMIT)


def _resident(shape):
    nd = len(shape)
    return pl.BlockSpec(shape, lambda *_: (0,) * nd, pipeline_mode=pl.Buffered(1))


def _ada_kernel(c_ref, w_ref, b_ref, o_ref):
    c = c_ref[...]
    a = c * jax.nn.sigmoid(c)
    w = w_ref[...]

    def split(v):
        hi = v.astype(BF16)
        return hi, (v - hi.astype(F32)).astype(BF16)

    a_hi, a_lo = split(a)
    w_hi, w_lo = split(w)
    dot = functools.partial(jnp.dot, preferred_element_type=F32)
    o_ref[...] = dot(a_hi, w_hi) + (dot(a_lo, w_hi) + dot(a_hi, w_lo)) + b_ref[...]


def _ada(c_all, ada_w, ada_b):
    nb = c_all.shape[0]
    n_out = ada_w.shape[1]
    tn = 1024
    return pl.pallas_call(
        _ada_kernel,
        grid=(n_out // tn,),
        in_specs=[pl.BlockSpec((nb, D_MODEL), lambda j: (0, 0)),
                  pl.BlockSpec((D_MODEL, tn), lambda j: (0, j)),
                  pl.BlockSpec((1, tn), lambda j: (0, j))],
        out_specs=pl.BlockSpec((nb, tn), lambda j: (0, j)),
        out_shape=jax.ShapeDtypeStruct((nb, n_out), F32),
        compiler_params=_cparams(("arbitrary",)),
        name="ada",
    )(c_all, ada_w, ada_b.reshape(1, n_out))


def _modulated_norm(x, g, shift, scale, n_seq):
    tm = x.shape[0]
    ms = jnp.mean(x * x, axis=-1, keepdims=True)
    y = x * lax.rsqrt(ms + EPS) * g
    y3 = y.reshape(n_seq, tm // n_seq, D_MODEL)
    h = y3 * (1.0 + scale)[:, None, :] + shift[:, None, :]
    return h.reshape(tm, D_MODEL)


COL_BLK = 1024


def _inproj_ret_kernel(x_ref, mod_ref, g1_ref, cos_ref, sin_ref, wret_ref, ret_ref, h_ref, dq_scr, dk_scr,
                       *, n_seq, lc):
    tm = x_ref.shape[0]

    @pl.when(pl.program_id(0) == 0)
    def _():
        e = ((lax.broadcasted_iota(jnp.int32, (tm, RET_DK), 0) % lc) + 1).astype(F32)
        for hh in range(RET_HEADS):
            dq_scr[hh] = jnp.exp(e * RET_LOG_GAMMA[hh])
            dk_scr[hh] = jnp.exp(-e * RET_LOG_GAMMA[hh]) * (RET_DK ** -0.5)

    h = _modulated_norm(x_ref[...], g1_ref[...], mod_ref[:, 0, :], mod_ref[:, 1, :], n_seq)
    h_ref[...] = h.astype(BF16)
    cos = cos_ref[...]
    sin = sin_ref[...]
    assert COL_BLK == RET_W
    for blk in range(4):
        c0 = blk * COL_BLK
        acc = jnp.dot(h_ref[...], wret_ref[:, c0:c0 + COL_BLK], preferred_element_type=F32)
        if blk >= 2:
            ret_ref[:, c0:c0 + COL_BLK] = acc.astype(BF16)
            continue
        dec_scr = dq_scr if blk == 0 else dk_scr
        for hh in range(RET_HEADS):
            a = acc[:, hh * RET_DK:(hh + 1) * RET_DK]
            r = a * cos + pltpu.roll(a, RET_DK // 2, 1) * sin
            ret_ref[:, c0 + hh * RET_DK:c0 + (hh + 1) * RET_DK] = (r * dec_scr[hh]).astype(BF16)


def _inproj_rest_kernel(h_ref, wsq_ref, wkv_ref, wbg_ref, *rest, n_cast):
    cast_in, (sq_ref, kv_ref, gate_ref), cast_out = rest[:n_cast], rest[n_cast:n_cast + 3], rest[n_cast + 3:]
    for src, dst in zip(cast_in, cast_out):
        dst[...] = src[...].astype(BF16)
    sq_ref[...] = jnp.dot(h_ref[...], wsq_ref[...], preferred_element_type=F32).astype(BF16)
    kv_ref[...] = jnp.dot(h_ref[...], wkv_ref[...], preferred_element_type=F32)
    for blk in range(2 * D_MODEL // COL_BLK):
        c0 = blk * COL_BLK
        acc = jnp.dot(h_ref[...], wbg_ref[:, c0:c0 + COL_BLK], preferred_element_type=F32)
        gate_ref[:, c0:c0 + COL_BLK] = jax.nn.sigmoid(acc).astype(BF16)


def _inproj(x2d, mod3, g1, cos_t, sin_t, wret, wsq, wkv, wbg, *, seq_len, tm, lc, cast=()):
    R = x2d.shape[0]
    if seq_len >= tm:
        n_seq, tps = 1, seq_len // tm
        mod_map = lambda i: (i // tps, 0, 0)
        tab_map = lambda i: (i % tps, 0)
    else:
        n_seq = tm // seq_len
        mod_map = lambda i: (i, 0, 0)
        tab_map = lambda i: (0, 0)
    row = lambda w: pl.BlockSpec((tm, w), lambda i: (i, 0))
    assert tm % lc == 0
    ret, h = pl.pallas_call(
        functools.partial(_inproj_ret_kernel, n_seq=n_seq, lc=lc),
        grid=(R // tm,),
        in_specs=[row(D_MODEL),
                  pl.BlockSpec((n_seq, N_MOD, D_MODEL), mod_map),
                  _resident((1, D_MODEL)),
                  pl.BlockSpec((tm, LANES), tab_map),
                  pl.BlockSpec((tm, LANES), tab_map),
                  _resident((D_MODEL, 4 * RET_W))],
        out_specs=[row(4 * RET_W), row(D_MODEL)],
        out_shape=[jax.ShapeDtypeStruct((R, 4 * RET_W), BF16),
                   jax.ShapeDtypeStruct((R, D_MODEL), BF16)],
        scratch_shapes=[pltpu.VMEM((RET_HEADS, tm, RET_DK), F32), pltpu.VMEM((RET_HEADS, tm, RET_DK), F32)],
        compiler_params=_cparams(("arbitrary",)),
        name="inproj_ret",
    )(x2d, mod3, g1, cos_t, sin_t, wret)
    n_steps = R // tm
    cast_2d = [w.reshape(-1, w.shape[-1]) for w in cast]
    cast_specs = []
    for w in cast_2d:
        slab = w.shape[0] // n_steps
        assert slab * n_steps == w.shape[0] and slab % 16 == 0
        cast_specs.append(pl.BlockSpec((slab, w.shape[1]), lambda i: (i, 0)))
    outs = pl.pallas_call(
        functools.partial(_inproj_rest_kernel, n_cast=len(cast)),
        grid=(n_steps,),
        in_specs=[row(D_MODEL),
                  _resident((D_MODEL, SWA_Q_W)),
                  _resident((D_MODEL, 2 * SWA_KV_W)),
                  _resident((D_MODEL, 2 * D_MODEL))] + cast_specs,
        out_specs=[row(SWA_Q_W), row(2 * SWA_KV_W), row(2 * D_MODEL)] + cast_specs,
        out_shape=[jax.ShapeDtypeStruct((R, SWA_Q_W), BF16),
                   jax.ShapeDtypeStruct((R, 2 * SWA_KV_W), F32),
                   jax.ShapeDtypeStruct((R, 2 * D_MODEL), BF16)]
                  + [jax.ShapeDtypeStruct(w.shape, BF16) for w in cast_2d],
        compiler_params=_cparams(("parallel",)),
        name="inproj_rest",
    )(h, wsq, wkv, wbg, *cast_2d)
    sq, kv, gate = outs[:3]
    return ret, sq, kv, gate, [o.reshape(w.shape) for o, w in zip(outs[3:], cast)]


def _ret_kernel(blk_ref, s0_ref, gn_ref, r_ref, sout_ref, s_scr, *, lc, n_sub):
    c = pl.program_id(1)

    @pl.when(c == 0)
    def _():
        s_scr[...] = s0_ref[0]

    causal = (lax.broadcasted_iota(jnp.int32, (lc, lc), 0) >= lax.broadcasted_iota(jnp.int32, (lc, lc), 1))
    nt = (((1,), (1,)), ((), ()))
    tn = (((0,), (0,)), ((), ()))
    heads = range(RET_HEADS)
    cols = [slice(h * RET_DK, (h + 1) * RET_DK) for h in heads]
    for sub in range(n_sub):
        rows = slice(sub * lc, (sub + 1) * lc)
        part = lambda p, h: blk_ref[rows, p * RET_W + h * RET_DK:p * RET_W + (h + 1) * RET_DK]
        scores = [lax.dot_general(part(0, h), part(1, h), nt, preferred_element_type=F32) for h in heads]
        masked = [jnp.where(causal, s, 0.0).astype(BF16) for s in scores]
        states = [s_scr[h] for h in heads]
        outs = [jnp.dot(masked[h], part(2, h), preferred_element_type=F32)
                + jnp.dot(part(0, h), states[h].astype(BF16), preferred_element_type=F32) for h in heads]
        for h in heads:
            kv = lax.dot_general(part(1, h), part(2, h), tn, preferred_element_type=F32)
            s_scr[h] = math.exp(lc * RET_LOG_GAMMA[h]) * (states[h] + kv)
        for h in heads:
            o = outs[h]
            mu = jnp.mean(o, axis=-1, keepdims=True)
            d = o - mu
            var = jnp.mean(d * d, axis=-1, keepdims=True)
            on = d * lax.rsqrt(var + EPS) * gn_ref[:, cols[h]]
            g = part(3, h).astype(F32)
            r_ref[rows, cols[h]] = (on * (g * jax.nn.sigmoid(g))).astype(BF16)

    @pl.when(c == pl.num_programs(1) - 1)
    def _():
        sout_ref[0] = s_scr[...]


def _retention(ret_all, s0, gn_g, *, n_seq, seq_len, lc, n_sub):
    R = ret_all.shape[0]
    rows = lc * n_sub
    nc = seq_len // rows
    st_spec = pl.BlockSpec((1, RET_HEADS, RET_DK, RET_DV), lambda b, c: (b, 0, 0, 0))
    return pl.pallas_call(
        functools.partial(_ret_kernel, lc=lc, n_sub=n_sub),
        grid=(n_seq, nc),
        in_specs=[pl.BlockSpec((rows, 4 * RET_W), lambda b, c: (b * nc + c, 0)),
                  st_spec,
                  _resident((1, RET_W))],
        out_specs=[pl.BlockSpec((rows, RET_W), lambda b, c: (b * nc + c, 0)), st_spec],
        out_shape=[jax.ShapeDtypeStruct((R, RET_W), BF16),
                   jax.ShapeDtypeStruct((n_seq, RET_HEADS, RET_DK, RET_DV), F32)],
        scratch_shapes=[pltpu.VMEM((RET_HEADS, RET_DK, RET_DV), F32)],
        compiler_params=_cparams(("parallel", "arbitrary")),
        name="retention",
    )(ret_all, s0, gn_g)


KEYS = WINDOW + CHUNK
KPAD = 256


def _swa_kernel(sink_ref, q_ref, k2_ref, k1_ref, k0_ref, v2_ref, v1_ref, v0_ref, o_ref, *, masked, n_q):
    j = pl.program_id(1)
    kall = jnp.concatenate([k2_ref[...], k1_ref[...], k0_ref[...]], axis=0)
    vall = jnp.concatenate([v2_ref[...], v1_ref[...], v0_ref[...]], axis=0)
    lane = lax.broadcasted_iota(jnp.int32, kall.shape, 1)
    zpad = jnp.zeros((KPAD - KEYS, LANES), BF16)

    def lane_halves(win, h):
        rolled = pltpu.roll(win, SWA_HEAD_DIM, 1)
        lo_src, hi_src = (win, rolled) if h == 0 else (rolled, win)
        return (jnp.where(lane < SWA_HEAD_DIM, lo_src, 0.0).astype(BF16),
                jnp.where(lane >= SWA_HEAD_DIM, hi_src, 0.0).astype(BF16))

    col = lax.broadcasted_iota(jnp.int32, (1, KPAD), 1)
    n_pairs = SWA_GROUP // 2
    rows = n_pairs * CHUNK
    row = lax.broadcasted_iota(jnp.int32, (rows, 1), 0)
    out_lane = lax.broadcasted_iota(jnp.int32, (rows, LANES), 1)
    nt = (((1,), (1,)), ((), ()))
    q_scale = jnp.asarray(SWA_HEAD_DIM ** -0.5, BF16)
    for h in range(SWA_KV_HEADS):
        k_lo, k_hi = lane_halves(kall, h)
        v_lo, v_hi = lane_halves(vall, h)
        base = h * SWA_GROUP * SWA_HEAD_DIM
        sinks = []
        for half in range(2):
            sink = jnp.zeros((rows, 1), F32)
            for p in range(n_pairs):
                sink = jnp.where(row // CHUNK == p, sink_ref[h * SWA_GROUP + 2 * p + half], sink)
            sinks.append(sink)
        for u in range(n_q):
            r0 = u * CHUNK
            c = j * n_q + u
            if masked:
                first_ok = jnp.where(c >= 2, 0, jnp.where(c == 1, CHUNK, 2 * CHUNK))
                ok = (col >= first_ok) & (col < KEYS)
            else:
                ok = col < KEYS
            kk = jnp.concatenate([k_lo[r0:r0 + KEYS], zpad, k_hi[r0:r0 + KEYS], zpad], axis=0)
            vv = jnp.concatenate([v_lo[r0:r0 + KEYS], zpad, v_hi[r0:r0 + KEYS], zpad], axis=0)
            q4 = jnp.concatenate([q_ref[r0:r0 + CHUNK, base + p * LANES: base + (p + 1) * LANES]
                                  for p in range(n_pairs)], axis=0)
            s = lax.dot_general(q4 * q_scale, kk, nt, preferred_element_type=F32)
            ps, invs = [], []
            for half in range(2):
                sh = jnp.where(ok, s[:, half * KPAD:(half + 1) * KPAD], NEG_INF)
                m = jnp.maximum(jnp.max(sh, axis=-1, keepdims=True), sinks[half])
                p_half = jnp.exp(sh - m)
                den = jnp.sum(p_half, axis=-1, keepdims=True) + jnp.exp(sinks[half] - m)
                ps.append(p_half.astype(BF16))
                invs.append(1.0 / den)
            pv = jnp.dot(jnp.concatenate(ps, axis=1), vv, preferred_element_type=F32)
            o = pv * jnp.where(out_lane < SWA_HEAD_DIM, invs[0], invs[1])
            for p in range(n_pairs):
                o_ref[r0:r0 + CHUNK, base + p * LANES: base + (p + 1) * LANES] = (
                    o[p * CHUNK:(p + 1) * CHUNK].astype(BF16))


def _swa(sinks, sq, k_arrs, v_arrs, k_maps, v_maps, *, n_seq, nc, masked, n_q):
    R = sq.shape[0]
    kv_rows = (CHUNK, CHUNK, n_q * CHUNK)
    kv_specs = [pl.BlockSpec((kv_rows[t % 3], SWA_KV_W), m) for t, m in enumerate((*k_maps, *v_maps))]
    return pl.pallas_call(
        functools.partial(_swa_kernel, masked=masked, n_q=n_q),
        grid=(n_seq, nc),
        in_specs=[pl.BlockSpec(memory_space=pltpu.SMEM),
                  pl.BlockSpec((n_q * CHUNK, SWA_Q_W), lambda b, c: (b * nc + c, 0)),
                  *kv_specs],
        out_specs=pl.BlockSpec((n_q * CHUNK, SWA_Q_W), lambda b, c: (b * nc + c, 0)),
        out_shape=jax.ShapeDtypeStruct((R, SWA_Q_W), BF16),
        compiler_params=_cparams(("parallel", "arbitrary")),
        name="swa",
    )(sinks, sq, *k_arrs, *v_arrs)


def _route(logits):
    tm = logits.shape[0]
    lane = lax.broadcasted_iota(jnp.int32, (tm, LANES), 1)
    is_g = lane < N_GROUPS
    gl = jnp.where(is_g, logits, NEG_INF)
    gmax = jnp.max(gl, axis=-1, keepdims=True)
    gidx = jnp.min(jnp.where(gl == gmax, lane, LANES), axis=-1, keepdims=True)
    gsum = jnp.sum(jnp.where(is_g, jnp.exp(gl - gmax), 0.0), axis=-1, keepdims=True)
    g_w = 1.0 / gsum
    base = N_GROUPS + EXPERTS_PER_GROUP * gidx
    el = jnp.where((lane >= base) & (lane < base + EXPERTS_PER_GROUP), logits, NEG_INF)
    v1 = jnp.max(el, axis=-1, keepdims=True)
    i1 = jnp.min(jnp.where(el == v1, lane, LANES), axis=-1, keepdims=True)
    el2 = jnp.where(lane == i1, NEG_INF, el)
    v2 = jnp.max(el2, axis=-1, keepdims=True)
    i2 = jnp.min(jnp.where(el2 == v2, lane, LANES), axis=-1, keepdims=True)
    e2 = jnp.exp(v2 - v1)
    den = 1.0 + e2
    w1 = g_w / den
    w2 = g_w * e2 / den
    l1 = i1 - base
    l2 = i2 - base
    first_lo = l1 < l2
    la = jnp.where(first_lo, l1, l2)
    lb = jnp.where(first_lo, l2, l1)
    wa = jnp.where(first_lo, w1, w2)
    wb = jnp.where(first_lo, w2, w1)
    pair = jnp.where(la == 0, lb - 1, jnp.where(la == 1, jnp.where(lb == 3, 3, 4), 5))
    swapped = la == 2
    w_slot_a = jnp.where(swapped, wb, wa)
    w_slot_b = jnp.where(swapped, wa, wb)
    cls = (gidx * N_PAIRS + pair).astype(F32)
    return jnp.where(lane == 0, w_slot_a, jnp.where(lane == 1, w_slot_b, jnp.where(lane == 2, cls, 0.0)))


ROW_W = D_MODEL + LANES


def _merge_kernel(x_ref, r_ref, o_ref, gate_ref, mod_ref, modp_ref, g2_ref, wrb_ref, wsb_ref, wout_ref, wr_ref,
                  br_ref, x1_ref, text_ref, meta_ref, x1_scr, *, n_seq):
    i = pl.program_id(0)
    n = pl.num_programs(0) - 1
    tm = x_ref.shape[0]

    def matmul_stage():
        g_r = gate_ref[:, :D_MODEL].astype(F32)
        g_s = gate_ref[:, D_MODEL:].astype(F32)
        merged = (g_r * jnp.dot(r_ref[...], wrb_ref[...], preferred_element_type=F32)
                  + g_s * jnp.dot(o_ref[...], wsb_ref[...], preferred_element_type=F32))
        mix = jnp.dot(merged.astype(BF16), wout_ref[...], preferred_element_type=F32)
        gt1 = mod_ref[:, 2, :]
        x1 = (x_ref[...].reshape(n_seq, tm // n_seq, D_MODEL) + gt1[:, None, :]
              * mix.reshape(n_seq, tm // n_seq, D_MODEL)).reshape(tm, D_MODEL)
        x1_ref[...] = x1
        x1_scr[...] = x1

    def vector_stage():
        t = _modulated_norm(x1_scr[...], g2_ref[...], modp_ref[:, 3, :], modp_ref[:, 4, :], n_seq)
        logits = jnp.dot(t.astype(BF16), wr_ref[...], preferred_element_type=F32) + br_ref[...]
        meta = _route(logits)
        text_ref[:, :D_MODEL] = t
        text_ref[:, D_MODEL:] = meta
        meta_ref[...] = meta

    @pl.when(i == 0)
    def _():
        matmul_stage()

    @pl.when((i > 0) & (i < n))
    def _():
        vector_stage()
        matmul_stage()

    @pl.when(i == n)
    def _():
        vector_stage()


def _merge(x2d, r, o_swa, gates, mod3, g2, wrb, wsb, wout, wr, br, *, seq_len, tm):
    R = x2d.shape[0]
    n = R // tm
    if seq_len >= tm:
        n_seq, tps = 1, seq_len // tm
        seq_of = lambda t: t // tps
    else:
        n_seq = tm // seq_len
        seq_of = lambda t: t
    cur = lambda i: jnp.minimum(i, n - 1)
    prev = lambda i: jnp.maximum(i - 1, 0)
    row = lambda w: pl.BlockSpec((tm, w), lambda i: (cur(i), 0))
    row_prev = lambda w: pl.BlockSpec((tm, w), lambda i: (prev(i), 0))
    return pl.pallas_call(
        functools.partial(_merge_kernel, n_seq=n_seq),
        grid=(n + 1,),
        in_specs=[row(D_MODEL), row(RET_W), row(SWA_Q_W), row(2 * D_MODEL),
                  pl.BlockSpec((n_seq, N_MOD, D_MODEL), lambda i: (seq_of(cur(i)), 0, 0)),
                  pl.BlockSpec((n_seq, N_MOD, D_MODEL), lambda i: (seq_of(prev(i)), 0, 0)),
                  _resident((1, D_MODEL)),
                  _resident((RET_W, D_MODEL)), _resident((SWA_Q_W, D_MODEL)), _resident((D_MODEL, D_MODEL)),
                  _resident((D_MODEL, LANES)), _resident((1, LANES))],
        out_specs=[row(D_MODEL), row_prev(ROW_W), row_prev(LANES)],
        out_shape=[jax.ShapeDtypeStruct((R, D_MODEL), F32),
                   jax.ShapeDtypeStruct((R, ROW_W), F32),
                   jax.ShapeDtypeStruct((R, LANES), F32)],
        scratch_shapes=[pltpu.VMEM((tm, D_MODEL), F32)],
        compiler_params=_cparams(("arbitrary",)),
        name="merge",
    )(x2d, r, o_swa, gates, mod3, mod3, g2, wrb, wsb, wout, wr, br)


PLAN_BLK = 2048
TILE_ROWS = 256


def _plan_kernel(meta_a_ref, meta_b_ref, pos_ref, tile_ref, pad_ref, cnt_scr, offs_scr, carry_scr, tri_scr,
                 *, tm, nb_a):
    ph = pl.program_id(0)
    b = pl.program_id(1)
    blk = meta_a_ref.shape[0]
    lane = lax.broadcasted_iota(jnp.int32, (blk, LANES), 1)
    cls_col = jnp.where(b < nb_a, meta_a_ref[:, 2:3], meta_b_ref[:, 2:3])
    oh = jnp.where(cls_col == lane.astype(F32), 1.0, 0.0)

    @pl.when((ph == 0) & (b == 0))
    def _():
        cnt_scr[...] = jnp.zeros_like(cnt_scr)
        ri = lax.broadcasted_iota(jnp.int32, (blk, blk), 0)
        ci = lax.broadcasted_iota(jnp.int32, (blk, blk), 1)
        tri_scr[...] = jnp.where(ci <= ri, 1.0, 0.0).astype(BF16)

    @pl.when(ph == 0)
    def _():
        cnt_scr[...] += jnp.sum(oh, axis=0, keepdims=True)

    @pl.when((ph == 1) & (b == 0))
    def _():
        cnt = cnt_scr[...]
        ptiles = jnp.floor((cnt + (tm - 1)) * (1.0 / tm))
        ri = lax.broadcasted_iota(jnp.int32, (LANES, LANES), 0)
        ci = lax.broadcasted_iota(jnp.int32, (LANES, LANES), 1)
        before = jnp.where(ri < ci, 1.0, 0.0).astype(BF16)
        offs = jnp.dot(ptiles.astype(BF16), before, preferred_element_type=F32) * tm
        offs_scr[...] = offs
        carry_scr[...] = jnp.zeros_like(carry_scr)
        padded = ptiles * tm
        ends = offs + padded
        tl = lax.broadcasted_iota(jnp.int32, (TILE_ROWS, LANES), 1)
        tstart = lax.broadcasted_iota(jnp.int32, (TILE_ROWS, LANES), 0).astype(F32) * tm
        tcls = jnp.sum(jnp.where((ends[0:1, :] <= tstart) & (tl < N_CLASSES), 1.0, 0.0), axis=1, keepdims=True)
        tcls = jnp.minimum(tcls, N_CLASSES - 1.0)
        total = jnp.max(ends[0:1, :], axis=1, keepdims=True)
        grp = (jnp.where(tcls >= N_PAIRS, 1.0, 0.0) + jnp.where(tcls >= 2 * N_PAIRS, 1.0, 0.0)
               + jnp.where(tcls >= 3 * N_PAIRS, 1.0, 0.0))
        pair = tcls - N_PAIRS * grp
        la = jnp.where(pair < 3, 0.0, jnp.where(pair < 5, 1.0, 3.0))
        lb = jnp.where(pair == 0, 1.0, jnp.where((pair == 1) | (pair >= 4), 2.0, 3.0))
        ea = EXPERTS_PER_GROUP * grp + la
        eb = EXPERTS_PER_GROUP * grp + lb
        n_used = total * (1.0 / tm)
        tile_ref[...] = jnp.where(tl == 0, ea, jnp.where(tl == 1, eb, jnp.where(tl == 2, n_used, 0.0))
                                  ).astype(jnp.int32)
        npad = padded - cnt
        pstart = jnp.dot(npad.astype(BF16), before, preferred_element_type=F32)
        n_class_pad = jnp.sum(npad[0:1, :], axis=1, keepdims=True)
        rows = pad_ref.shape[0]
        v = (lax.broadcasted_iota(jnp.int32, (rows, LANES), 0) * LANES
             + lax.broadcasted_iota(jnp.int32, (rows, LANES), 1)).astype(F32)
        slot = jnp.where(v >= n_class_pad, total - n_class_pad + v, 0.0)
        for c in range(N_CLASSES):
            ps = pstart[0:1, c:c + 1]
            inside = (v >= ps) & (v < ps + npad[0:1, c:c + 1])
            slot = jnp.where(inside, offs[0:1, c:c + 1] + cnt[0:1, c:c + 1] - ps + v, slot)
        pad_ref[...] = slot.astype(jnp.int32)

    @pl.when(ph == 1)
    def _():
        incl = jnp.dot(tri_scr[...], oh.astype(BF16), preferred_element_type=F32)
        base = offs_scr[0:1, :] + carry_scr[0:1, :]
        pos = jnp.sum(oh * (base + incl - oh), axis=1, keepdims=True)
        pos_ref[...] = jnp.broadcast_to(pos, (blk, LANES)).astype(jnp.int32)
        carry_scr[...] += jnp.sum(oh, axis=0, keepdims=True)


def _plan(meta_a, meta_b, *, tm):
    na, nb = meta_a.shape[0], meta_b.shape[0]
    n = na + nb
    n_tiles = n // tm + N_CLASSES
    n_free = N_CLASSES * tm
    assert n_tiles <= TILE_ROWS and na % PLAN_BLK == 0 and nb % PLAN_BLK == 0 and n_free % LANES == 0
    n_slots = n_tiles * tm
    nb_a = na // PLAN_BLK
    nb_b = nb // PLAN_BLK
    pos2d, tile2d, pad2d = pl.pallas_call(
        functools.partial(_plan_kernel, tm=tm, nb_a=nb_a),
        grid=(2, nb_a + nb_b),
        in_specs=[pl.BlockSpec((PLAN_BLK, LANES), lambda ph, b: (jnp.minimum(b, nb_a - 1), 0)),
                  pl.BlockSpec((PLAN_BLK, LANES), lambda ph, b: (jnp.maximum(b - nb_a, 0), 0))],
        out_specs=[pl.BlockSpec((PLAN_BLK, LANES), lambda ph, b: (b * ph, 0)),
                   pl.BlockSpec((TILE_ROWS, LANES), lambda ph, b: (0, 0)),
                   pl.BlockSpec((n_free // LANES, LANES), lambda ph, b: (0, 0))],
        out_shape=[jax.ShapeDtypeStruct((n, LANES), jnp.int32),
                   jax.ShapeDtypeStruct((TILE_ROWS, LANES), jnp.int32),
                   jax.ShapeDtypeStruct((n_free // LANES, LANES), jnp.int32)],
        scratch_shapes=[pltpu.VMEM((8, LANES), F32), pltpu.VMEM((8, LANES), F32), pltpu.VMEM((8, LANES), F32),
                        pltpu.VMEM((PLAN_BLK, PLAN_BLK), BF16)],
        compiler_params=_cparams(("arbitrary", "arbitrary")),
        name="plan",
    )(meta_a, meta_b)
    return pos2d[:, 0], pad2d.reshape(-1), tile2d[:n_tiles, 0], tile2d[:n_tiles, 1], tile2d[0:1, 2], n_slots


N_STAGE = 3


def _scatter_kernel(pos_ref, text_a_ref, text_b_ref, out_ref, stage, zero_scr, sem_in, sem_out, *, nb_a, nb_b):
    i = pl.program_id(0)
    n = pl.num_programs(0)
    n_tok = nb_a + nb_b
    tb = pos_ref.shape[2]
    slot = i % N_STAGE

    def load(step, s):
        def from_a():
            return pltpu.make_async_copy(text_a_ref.at[pl.ds(step * tb, tb), :], stage.at[s], sem_in.at[s])

        def from_b():
            return pltpu.make_async_copy(text_b_ref.at[pl.ds((step - nb_a) * tb, tb), :], stage.at[s], sem_in.at[s])

        return from_a, from_b

    def start_load(step, s):
        from_a, from_b = load(step, s)

        @pl.when(step < nb_a)
        def _():
            from_a().start()

        @pl.when((step >= nb_a) & (step < n_tok))
        def _():
            from_b().start()

    def wait_rows(s):
        pltpu.make_async_copy(stage.at[s], out_ref.at[pl.ds(0, tb), :], sem_out.at[s]).wait()

    def scatter_rows(src_row):
        def body(r, carry):
            dst = pos_ref[0, 0, r]
            pltpu.make_async_copy(src_row(r), out_ref.at[pl.ds(dst, 1), :], sem_out.at[slot]).start()
            return carry

        lax.fori_loop(0, tb, body, 0, unroll=8)

    @pl.when(i == 0)
    def _():
        zero_scr[...] = jnp.zeros_like(zero_scr)
        start_load(i, slot)

    @pl.when(i >= 2)
    def _():
        wait_rows((i + 1) % N_STAGE)

    start_load(i + 1, (i + 1) % N_STAGE)

    @pl.when(i < n_tok)
    def _():
        pltpu.make_async_copy(text_a_ref.at[pl.ds(0, tb), :], stage.at[slot], sem_in.at[slot]).wait()
        scatter_rows(lambda r: stage.at[slot, pl.ds(r, 1), :])

    @pl.when(i >= n_tok)
    def _():
        scatter_rows(lambda r: zero_scr.at[pl.ds(0, 1), :])

    @pl.when(i == n - 1)
    def _():
        wait_rows((i + 2) % N_STAGE)
        wait_rows(slot)


def _scatter_rows(text_a, text_b, pos_ext, *, tb):
    nb_a = text_a.shape[0] // tb
    nb_b = text_b.shape[0] // tb
    n_steps = pos_ext.shape[0] // tb
    assert n_steps >= 2 and nb_a >= 1 and nb_b >= 1
    return pl.pallas_call(
        functools.partial(_scatter_kernel, nb_a=nb_a, nb_b=nb_b),
        grid=(n_steps,),
        in_specs=[pl.BlockSpec((1, 1, tb), lambda i: (i, 0, 0), memory_space=pltpu.SMEM),
                  pl.BlockSpec(memory_space=pl.ANY),
                  pl.BlockSpec(memory_space=pl.ANY)],
        out_specs=pl.BlockSpec(memory_space=pl.ANY),
        out_shape=jax.ShapeDtypeStruct((pos_ext.shape[0], ROW_W), F32),
        scratch_shapes=[pltpu.VMEM((N_STAGE, tb, ROW_W), F32), pltpu.VMEM((8, ROW_W), F32),
                        pltpu.SemaphoreType.DMA((N_STAGE,)), pltpu.SemaphoreType.DMA((N_STAGE,))],
        compiler_params=_cparams(("arbitrary",)),
        name="scatter_rows",
    )(pos_ext.reshape(n_steps, 1, tb), text_a, text_b)


def _moe_kernel(ea_ref, eb_ref, nused_ref, xs_ref, w1a_ref, w3a_ref, w2a_ref, w1b_ref, w3b_ref, w2b_ref, y_ref):
    i = pl.program_id(0)

    @pl.when(i >= nused_ref[0])
    def _():
        y_ref[...] = jnp.zeros_like(y_ref)

    @pl.when(i < nused_ref[0])
    def _():
        x = xs_ref[:, :D_MODEL].astype(BF16)
        wa = xs_ref[:, D_MODEL:D_MODEL + 1]
        wb = xs_ref[:, D_MODEL + 1:D_MODEL + 2]

        def hidden(w1_ref, w3_ref, gate):
            a = jnp.dot(x, w1_ref[0], preferred_element_type=F32)
            b = jnp.dot(x, w3_ref[0], preferred_element_type=F32)
            return (a * jax.nn.sigmoid(a) * b * gate).astype(BF16)

        ha = hidden(w1a_ref, w3a_ref, wa)
        hb = hidden(w1b_ref, w3b_ref, wb)
        y_ref[...] = (jnp.dot(ha, w2a_ref[0], preferred_element_type=F32)
                      + jnp.dot(hb, w2b_ref[0], preferred_element_type=F32))


def _moe(tile_ea, tile_eb, n_used, xs, w1, w3, w2, *, tm, n_tiles):
    last = lambda i, nu: jnp.minimum(i, nu[0] - 1)
    wa_map = lambda i, ea, eb, nu: (ea[last(i, nu)], 0, 0)
    wb_map = lambda i, ea, eb, nu: (eb[last(i, nu)], 0, 0)
    row_map = lambda i, ea, eb, nu: (last(i, nu), 0)
    up = (1, D_MODEL, D_EXPERT)
    down = (1, D_EXPERT, D_MODEL)
    grid_spec = pltpu.PrefetchScalarGridSpec(
        num_scalar_prefetch=3,
        grid=(n_tiles,),
        in_specs=[pl.BlockSpec((tm, D_MODEL + LANES), row_map),
                  pl.BlockSpec(up, wa_map), pl.BlockSpec(up, wa_map), pl.BlockSpec(down, wa_map),
                  pl.BlockSpec(up, wb_map), pl.BlockSpec(up, wb_map), pl.BlockSpec(down, wb_map)],
        out_specs=pl.BlockSpec((tm, D_MODEL), lambda i, ea, eb, nu: (i, 0)),
    )
    return pl.pallas_call(
        _moe_kernel,
        grid_spec=grid_spec,
        out_shape=jax.ShapeDtypeStruct((n_tiles * tm, D_MODEL), F32),
        compiler_params=_cparams(("arbitrary",)),
        name="moe",
    )(tile_ea, tile_eb, n_used, xs, w1, w3, w2, w1, w3, w2)


def _final_kernel(pos_ref, pos_next_ref, x1_ref, mod_ref, g_ref, ys_ref, o_ref, ybuf, sem, *, n_seq):
    i = pl.program_id(0)
    n = pl.num_programs(0)
    tm = x1_ref.shape[0]
    slot = i % 2

    def start_gather(idx_ref, s):
        def body(r, carry):
            src = idx_ref[0, 0, r]
            pltpu.make_async_copy(ys_ref.at[pl.ds(src, 1), :], ybuf.at[s, pl.ds(r, 1), :], sem.at[s]).start()
            return carry

        lax.fori_loop(0, tm, body, 0, unroll=8)

    @pl.when(i == 0)
    def _():
        start_gather(pos_ref, 0)

    @pl.when(i + 1 < n)
    def _():
        start_gather(pos_next_ref, 1 - slot)

    pltpu.make_async_copy(ys_ref.at[pl.ds(0, tm), :], ybuf.at[slot], sem.at[slot]).wait()
    gt2 = mod_ref[:, 5, :]
    x2 = (x1_ref[...].reshape(n_seq, tm // n_seq, D_MODEL)
          + gt2[:, None, :] * ybuf[slot].reshape(n_seq, tm // n_seq, D_MODEL)).reshape(tm, D_MODEL)
    ms = jnp.mean(x2 * x2, axis=-1, keepdims=True)
    o_ref[...] = x2 * lax.rsqrt(ms + EPS) * g_ref[...]


def _final(x1, y_sorted, pos, mod3, gf, *, seq_len, tm):
    R = x1.shape[0]
    if seq_len >= tm:
        n_seq, tps = 1, seq_len // tm
        mod_map = lambda i: (i // tps, 0, 0)
    else:
        n_seq = tm // seq_len
        mod_map = lambda i: (i, 0, 0)
    n = R // tm
    row = pl.BlockSpec((tm, D_MODEL), lambda i: (i, 0))
    pos3 = pos.reshape(n, 1, tm)
    return pl.pallas_call(
        functools.partial(_final_kernel, n_seq=n_seq),
        grid=(n,),
        in_specs=[pl.BlockSpec((1, 1, tm), lambda i: (i, 0, 0), memory_space=pltpu.SMEM),
                  pl.BlockSpec((1, 1, tm), lambda i: (jnp.minimum(i + 1, n - 1), 0, 0), memory_space=pltpu.SMEM),
                  row, pl.BlockSpec((n_seq, N_MOD, D_MODEL), mod_map), _resident((1, D_MODEL)),
                  pl.BlockSpec(memory_space=pl.ANY)],
        out_specs=row,
        out_shape=jax.ShapeDtypeStruct((R, D_MODEL), F32),
        scratch_shapes=[pltpu.VMEM((2, tm, D_MODEL), F32), pltpu.SemaphoreType.DMA((2,))],
        compiler_params=_cparams(("arbitrary",)),
        name="final",
    )(pos3, pos3, x1, mod3, gf, y_sorted)


def _rope_tables(pos):
    half = RET_DK // 2
    inv = ROPE_BASE ** (-jnp.arange(half, dtype=F32) / half)
    ang = pos.astype(F32)[:, None] * inv[None, :]
    cos = jnp.cos(ang)
    sin = jnp.sin(ang)
    return jnp.concatenate([cos, cos], axis=-1), jnp.concatenate([-sin, sin], axis=-1)


def kernel(x_prompt, x_sample, cache_ret_state, cache_swa_k, cache_swa_v, c_prompt, c_sample,
           norm1_g, norm2_g, ada_w, ada_b, w_in, ret_gn_g, swa_sinks, w_ret_branch, w_swa_branch, w_out,
           router_group_w, router_group_b, router_expert_w, router_expert_b,
           expert_w1, expert_w3, expert_w2, final_norm_g):
    depth = w_in.shape[0]
    assert depth == 1
    bp, tp, _ = x_prompt.shape
    bs, ts, _ = x_sample.shape
    past = WINDOW
    assert cache_swa_k.shape[2] == past and ts == CHUNK and tp % 512 == 0
    tm = 512

    l = 0
    c1 = 4 * RET_W
    c2 = c1 + SWA_Q_W
    c3 = c2 + 2 * SWA_KV_W
    wret = w_in[l, :, :c1].astype(BF16)
    wsq = w_in[l, :, c1:c2].astype(BF16)
    wkv = w_in[l, :, c2:c3].astype(BF16)
    wbg = w_in[l, :, c3:].astype(BF16)
    wrb = w_ret_branch[l].astype(BF16)
    wsb = w_swa_branch[l].astype(BF16)
    wo = w_out[l].astype(BF16)
    n_r = N_GROUPS + N_EXPERTS
    wr = jnp.zeros((D_MODEL, LANES), F32).at[:, :N_GROUPS].set(router_group_w[l]).at[:, N_GROUPS:n_r].set(
        router_expert_w[l]).astype(BF16)
    br = jnp.zeros((1, LANES), F32).at[0, :N_GROUPS].set(router_group_b[l]).at[0, N_GROUPS:n_r].set(
        router_expert_b[l])
    g1 =norm1_g[l].reshape(1, D_MODEL)
    g2 = norm2_g[l].reshape(1, D_MODEL)
    gn = ret_gn_g[l].reshape(1, RET_W)
    gf = final_norm_g.reshape(1, D_MODEL)
    sinks = swa_sinks[l]

    c_all = jnp.concatenate([c_prompt, c_sample], axis=0)
    mod = _ada(c_all, ada_w[l], ada_b[l]).reshape(bp + bs, N_MOD, D_MODEL)
    mod_p, mod_s = mod[:bp], mod[bp:]

    cos_p, sin_p = _rope_tables(jnp.arange(tp))
    cos_s, sin_s = _rope_tables(PAST_LEN + jnp.arange(ts))
    rep = tm // ts
    cos_s, sin_s = jnp.tile(cos_s, (rep, 1)), jnp.tile(sin_s, (rep, 1))

    xp = x_prompt.reshape(bp * tp, D_MODEL)
    xs = x_sample.reshape(bs * ts, D_MODEL)

    lc_p = 128
    ret_p, sq_p, kv_p, gate_p, (w1, w3, w2) = _inproj(
        xp, mod_p, g1, cos_p, sin_p, wret, wsq, wkv, wbg, seq_len=tp, tm=tm, lc=lc_p,
        cast=(expert_w1[l], expert_w3[l], expert_w2[l]))
    ret_s, sq_s, kv_s, gate_s, _ = _inproj(xs, mod_s, g1, cos_s, sin_s, wret, wsq, wkv, wbg, seq_len=ts, tm=tm,
                                           lc=ts)

    s0_p = jnp.zeros((bp, RET_HEADS, RET_DK, RET_DV), F32)
    r_p, state_p = _retention(ret_p, s0_p, gn, n_seq=bp, seq_len=tp, lc=lc_p, n_sub=2)
    r_s, state_s = _retention(ret_s, cache_ret_state[l].astype(F32), gn, n_seq=bs, seq_len=ts, lc=ts, n_sub=1)

    n_q = 4
    nc_p = tp // CHUNK
    ns_p = nc_p // n_q
    prev_map = lambda back, colblk: (lambda b, s: (b * nc_p + jnp.maximum(s * n_q - back, 0), colblk))
    own_map = lambda colblk: (lambda b, s: (b * ns_p + s, colblk))
    o_p = _swa(sinks, sq_p, [kv_p] * 3, [kv_p] * 3,
               [prev_map(2, 0), prev_map(1, 0), own_map(0)], [prev_map(2, 1), prev_map(1, 1), own_map(1)],
               n_seq=bp, nc=ns_p, masked=True, n_q=n_q)
    ck = cache_swa_k[l].reshape(bs * past, SWA_KV_W)
    cv = cache_swa_v[l].reshape(bs * past, SWA_KV_W)
    cmap = lambda blk: (lambda b, c: (2 * b + blk, 0))
    o_s = _swa(sinks, sq_s, [ck, ck, kv_s], [cv, cv, kv_s],
               [cmap(0), cmap(1), lambda b, c: (b, 0)], [cmap(0), cmap(1), lambda b, c: (b, 1)],
               n_seq=bs, nc=1, masked=False, n_q=1)

    tm_m = 256
    n_p = bp * tp
    x1_p, text_p, meta_p = _merge(xp, r_p, o_p, gate_p, mod_p, g2, wrb, wsb, wo, wr, br, seq_len=tp, tm=tm_m)
    x1_s, text_s, meta_s = _merge(xs, r_s, o_s, gate_s, mod_s, g2, wrb, wsb, wo, wr, br, seq_len=ts, tm=tm_m)

    tm_e = 256
    pos, free_slots, tile_ea, tile_eb, n_used, n_slots = _plan(meta_p, meta_s, tm=tm_e)
    pos_ext = jnp.concatenate([pos, free_slots], axis=0)
    xsorted = _scatter_rows(text_p, text_s, pos_ext, tb=512)
    y_sorted = _moe(tile_ea, tile_eb, n_used, xsorted, w1, w3, w2, tm=tm_e, n_tiles=n_slots // tm_e)

    tm_f = 512
    out_p = _final(x1_p, y_sorted, pos[:n_p], mod_p, gf, seq_len=tp, tm=tm_f)
    out_s = _final(x1_s, y_sorted, pos[n_p:], mod_s, gf, seq_len=ts, tm=tm_f)

    y_prompt = out_p.reshape(bp, tp, D_MODEL)
    y_sample = out_s.reshape(bs, ts, D_MODEL)
    kvp = kv_p.reshape(bp, tp, 2 * SWA_KV_W)[:, tp - WINDOW:].reshape(bp, WINDOW, 2, SWA_KV_HEADS, SWA_HEAD_DIM)
    kvs = kv_s.reshape(bs, ts, 2, SWA_KV_HEADS, SWA_HEAD_DIM)
    k_s = jnp.concatenate([cache_swa_k[l].astype(F32), kvs[:, :, 0]], axis=1)[:, -WINDOW:]
    v_s = jnp.concatenate([cache_swa_v[l].astype(F32), kvs[:, :, 1]], axis=1)[:, -WINDOW:]
    return (y_prompt, y_sample, state_p[None], kvp[:, :, 0][None], kvp[:, :, 1][None],
            state_s[None], k_s[None], v_s[None])
```

```python
import functools
import math

import jax
import jax.numpy as jnp
from jax import lax
from jax.experimental import pallas as pl
from jax.experimental.pallas import tpu as pltpu

F32 = jnp.float32
BF16 = jnp.bfloat16

D_MODEL = 2048
CHUNK = 64
RET_HEADS = 8
RET_DK = 128
RET_DV = 128
RET_W = RET_HEADS * RET_DK
ROPE_BASE = 10000.0
SWA_Q_HEADS = 16
SWA_KV_HEADS = 2
SWA_GROUP = SWA_Q_HEADS // SWA_KV_HEADS
SWA_HEAD_DIM = 64
SWA_Q_W = SWA_Q_HEADS * SWA_HEAD_DIM
SWA_KV_W = SWA_KV_HEADS * SWA_HEAD_DIM
WINDOW = 128
PAST_LEN = 1024
N_GROUPS = 4
EXPERTS_PER_GROUP = 4
N_EXPERTS = 16
D_EXPERT = 512
N_MOD = 6
EPS = 1e-6
NEG_INF = -1e30
N_PAIRS = 6
N_CLASSES = N_GROUPS * N_PAIRS

LANES = 128
VMEM_LIMIT = 56 * 1024 * 1024

RET_LOG_GAMMA = tuple(math.log1p(-(2.0 ** (-5.0 - h))) for h in range(RET_HEADS))


def _cparams(sem):
    return pltpu.CompilerParams(dimension_semantics=sem, vmem_limit_bytes=VMEM_LIMIT)


def _sigmoid(x):
    return 0.5 * jnp.tanh(0.5 * x) + 0.5


def _resident(shape):
    nd = len(shape)
    return pl.BlockSpec(shape, lambda *_: (0,) * nd, pipeline_mode=pl.Buffered(1))


def _ada_kernel(c_ref, w_ref, b_ref, o_ref):
    c = c_ref[...]
    a = c * jax.nn.sigmoid(c)
    w = w_ref[...]

    def split(v):
        hi = v.astype(BF16)
        return hi, (v - hi.astype(F32)).astype(BF16)

    a_hi, a_lo = split(a)
    w_hi, w_lo = split(w)
    dot = functools.partial(jnp.dot, preferred_element_type=F32)
    o_ref[...] = dot(a_hi, w_hi) + (dot(a_lo, w_hi) + dot(a_hi, w_lo)) + b_ref[...]


def _ada(c_all, ada_w, ada_b):
    nb = c_all.shape[0]
    n_out = ada_w.shape[1]
    tn = 1024
    return pl.pallas_call(
        _ada_kernel,
        grid=(n_out // tn,),
        in_specs=[pl.BlockSpec((nb, D_MODEL), lambda j: (0, 0)),
                  pl.BlockSpec((D_MODEL, tn), lambda j: (0, j)),
                  pl.BlockSpec((1, tn), lambda j: (0, j))],
        out_specs=pl.BlockSpec((nb, tn), lambda j: (0, j)),
        out_shape=jax.ShapeDtypeStruct((nb, n_out), F32),
        compiler_params=_cparams(("arbitrary",)),
        name="ada",
    )(c_all, ada_w, ada_b.reshape(1, n_out))


def _modulated_norm(x, g, shift, scale, n_seq):
    tm = x.shape[0]
    ms = jnp.mean(x * x, axis=-1, keepdims=True)
    y = x * lax.rsqrt(ms + EPS) * g
    y3 = y.reshape(n_seq, tm // n_seq, D_MODEL)
    h = y3 * (1.0 + scale)[:, None, :] + shift[:, None, :]
    return h.reshape(tm, D_MODEL)


COL_BLK = 1024


def _cast_plan(casts, n_steps):
    in_specs, out_specs, out_shapes, splits = [], [], [], []
    for w, ranges in casts:
        slab = w.shape[0] // n_steps
        assert slab * n_steps == w.shape[0] and slab % 16 == 0
        in_specs.append(pl.BlockSpec((slab, w.shape[1]), lambda i: (i, 0)))
        for c0, c1 in ranges:
            out_specs.append(pl.BlockSpec((slab, c1 - c0), lambda i: (i, 0)))
            out_shapes.append(jax.ShapeDtypeStruct((w.shape[0], c1 - c0), BF16))
        splits.append(tuple(ranges))
    return in_specs, out_specs, out_shapes, tuple(splits)


def _cast_slabs(in_refs, out_refs, splits):
    k = 0
    for src, ranges in zip(in_refs, splits):
        for c0, c1 in ranges:
            out_refs[k][...] = src[:, c0:c1].astype(BF16)
            k += 1


def _inproj_ret_kernel(x_ref, mod_ref, g1_ref, cos_ref, sin_ref, wret_ref, *rest, n_seq, lc, splits):
    n_in = len(splits)
    cast_in, (ret_ref, h_ref) = rest[:n_in], rest[n_in:n_in + 2]
    cast_out, (dq_scr, dk_scr) = rest[n_in + 2:-2], rest[-2:]
    _cast_slabs(cast_in, cast_out, splits)
    tm = x_ref.shape[0]

    @pl.when(pl.program_id(0) == 0)
    def _():
        e = ((lax.broadcasted_iota(jnp.int32, (tm, RET_DK), 0) % lc) + 1).astype(F32)
        for hh in range(RET_HEADS):
            dq_scr[hh] = jnp.exp(e * RET_LOG_GAMMA[hh])
            dk_scr[hh] = jnp.exp(-e * RET_LOG_GAMMA[hh]) * (RET_DK ** -0.5)

    h = _modulated_norm(x_ref[...], g1_ref[...], mod_ref[:, 0, :], mod_ref[:, 1, :], n_seq)
    h_ref[...] = h.astype(BF16)
    cos = cos_ref[...]
    sin = sin_ref[...]
    assert COL_BLK == RET_W
    for blk in range(4):
        c0 = blk * COL_BLK
        acc = jnp.dot(h_ref[...], wret_ref[:, c0:c0 + COL_BLK], preferred_element_type=F32)
        if blk >= 2:
            ret_ref[:, c0:c0 + COL_BLK] = acc.astype(BF16)
            continue
        dec_scr = dq_scr if blk == 0 else dk_scr
        for hh in range(RET_HEADS):
            a = acc[:, hh * RET_DK:(hh + 1) * RET_DK]
            r = a * cos + pltpu.roll(a, RET_DK // 2, 1) * sin
            ret_ref[:, c0 + hh * RET_DK:c0 + (hh + 1) * RET_DK] = (r * dec_scr[hh]).astype(BF16)


def _inproj_rest_kernel(h_ref, wsq_ref, wkv_ref, wbg_ref, *rest, splits):
    n_in = len(splits)
    cast_in, (sq_ref, kv_ref, gate_ref), cast_out = rest[:n_in], rest[n_in:n_in + 3], rest[n_in + 3:]
    _cast_slabs(cast_in, cast_out, splits)
    sq_ref[...] = jnp.dot(h_ref[...], wsq_ref[...], preferred_element_type=F32).astype(BF16)
    kv_ref[...] = jnp.dot(h_ref[...], wkv_ref[...], preferred_element_type=F32)
    for blk in range(2 * D_MODEL // COL_BLK):
        c0 = blk * COL_BLK
        acc = jnp.dot(h_ref[...], wbg_ref[:, c0:c0 + COL_BLK], preferred_element_type=F32)
        gate_ref[:, c0:c0 + COL_BLK] = _sigmoid(acc).astype(BF16)


def _inproj_ret(x2d, mod3, g1, cos_t, sin_t, wret, *, seq_len, tm, lc, casts=()):
    R = x2d.shape[0]
    if seq_len >= tm:
        n_seq, tps = 1, seq_len // tm
        mod_map = lambda i: (i // tps, 0, 0)
        tab_map = lambda i: (i % tps, 0)
    else:
        n_seq = tm // seq_len
        mod_map = lambda i: (i, 0, 0)
        tab_map = lambda i: (0, 0)
    row = lambda w: pl.BlockSpec((tm, w), lambda i: (i, 0))
    assert tm % lc == 0
    n_steps = R // tm
    c_in, c_out, c_shapes, splits = _cast_plan(casts, n_steps)
    outs = pl.pallas_call(
        functools.partial(_inproj_ret_kernel, n_seq=n_seq, lc=lc, splits=splits),
        grid=(n_steps,),
        in_specs=[row(D_MODEL),
                  pl.BlockSpec((n_seq, N_MOD, D_MODEL), mod_map),
                  _resident((1, D_MODEL)),
                  pl.BlockSpec((tm, LANES), tab_map),
                  pl.BlockSpec((tm, LANES), tab_map),
                  _resident((D_MODEL, 4 * RET_W))] + c_in,
        out_specs=[row(4 * RET_W), row(D_MODEL)] + c_out,
        out_shape=[jax.ShapeDtypeStruct((R, 4 * RET_W), BF16),
                   jax.ShapeDtypeStruct((R, D_MODEL), BF16)] + c_shapes,
        scratch_shapes=[pltpu.VMEM((RET_HEADS, tm, RET_DK), F32), pltpu.VMEM((RET_HEADS, tm, RET_DK), F32)],
        compiler_params=_cparams(("arbitrary",)),
        name="inproj_ret",
    )(x2d, mod3, g1, cos_t, sin_t, wret, *[w for w, _ in casts])
    return outs[0], outs[1], outs[2:]


def _inproj_rest(h, wsq, wkv, wbg, *, tm, casts=()):
    R = h.shape[0]
    n_steps = R // tm
    row = lambda w: pl.BlockSpec((tm, w), lambda i: (i, 0))
    c_in, c_out, c_shapes, splits = _cast_plan(casts, n_steps)
    outs = pl.pallas_call(
        functools.partial(_inproj_rest_kernel, splits=splits),
        grid=(n_steps,),
        in_specs=[row(D_MODEL),
                  _resident((D_MODEL, SWA_Q_W)),
                  _resident((D_MODEL, 2 * SWA_KV_W)),
                  _resident((D_MODEL, 2 * D_MODEL))] + c_in,
        out_specs=[row(SWA_Q_W), row(2 * SWA_KV_W), row(2 * D_MODEL)] + c_out,
        out_shape=[jax.ShapeDtypeStruct((R, SWA_Q_W), BF16),
                   jax.ShapeDtypeStruct((R, 2 * SWA_KV_W), F32),
                   jax.ShapeDtypeStruct((R, 2 * D_MODEL), BF16)] + c_shapes,
        compiler_params=_cparams(("parallel",)),
        name="inproj_rest",
    )(h, wsq, wkv, wbg, *[w for w, _ in casts])
    return outs[0], outs[1], outs[2], outs[3:]


def _ret_kernel(blk_ref, s0_ref, gn_ref, r_ref, sout_ref, s_scr, *, lc, n_sub):
    c = pl.program_id(1)

    @pl.when(c == 0)
    def _():
        s_scr[...] = s0_ref[0]

    causal = (lax.broadcasted_iota(jnp.int32, (lc, lc), 0) >= lax.broadcasted_iota(jnp.int32, (lc, lc), 1))
    nt = (((1,), (1,)), ((), ()))
    tn = (((0,), (0,)), ((), ()))
    heads = range(RET_HEADS)
    cols = [slice(h * RET_DK, (h + 1) * RET_DK) for h in heads]
    for sub in range(n_sub):
        rows = slice(sub * lc, (sub + 1) * lc)
        part = lambda p, h: blk_ref[rows, p * RET_W + h * RET_DK:p * RET_W + (h + 1) * RET_DK]
        scores = [lax.dot_general(part(0, h), part(1, h), nt, preferred_element_type=F32) for h in heads]
        masked = [jnp.where(causal, s, 0.0).astype(BF16) for s in scores]
        states = [s_scr[h] for h in heads]
        outs = [jnp.dot(masked[h], part(2, h), preferred_element_type=F32)
                + jnp.dot(part(0, h), states[h].astype(BF16), preferred_element_type=F32) for h in heads]
        for h in heads:
            kv = lax.dot_general(part(1, h), part(2, h), tn, preferred_element_type=F32)
            s_scr[h] = math.exp(lc * RET_LOG_GAMMA[h]) * (states[h] + kv)
        for h in heads:
            o = outs[h]
            mu = jnp.mean(o, axis=-1, keepdims=True)
            d = o - mu
            var = jnp.mean(d * d, axis=-1, keepdims=True)
            on = d * lax.rsqrt(var + EPS) * gn_ref[:, cols[h]]
            g = part(3, h).astype(F32)
            r_ref[rows, cols[h]] = (on * (g * _sigmoid(g))).astype(BF16)

    @pl.when(c == pl.num_programs(1) - 1)
    def _():
        sout_ref[0] = s_scr[...]


def _retention(ret_all, s0, gn_g, *, n_seq, seq_len, lc, n_sub):
    R = ret_all.shape[0]
    rows = lc * n_sub
    nc = seq_len // rows
    st_spec = pl.BlockSpec((1, RET_HEADS, RET_DK, RET_DV), lambda b, c: (b, 0, 0, 0))
    return pl.pallas_call(
        functools.partial(_ret_kernel, lc=lc, n_sub=n_sub),
        grid=(n_seq, nc),
        in_specs=[pl.BlockSpec((rows, 4 * RET_W), lambda b, c: (b * nc + c, 0)),
                  st_spec,
                  _resident((1, RET_W))],
        out_specs=[pl.BlockSpec((rows, RET_W), lambda b, c: (b * nc + c, 0)), st_spec],
        out_shape=[jax.ShapeDtypeStruct((R, RET_W), BF16),
                   jax.ShapeDtypeStruct((n_seq, RET_HEADS, RET_DK, RET_DV), F32)],
        scratch_shapes=[pltpu.VMEM((RET_HEADS, RET_DK, RET_DV), F32)],
        compiler_params=_cparams(("parallel", "arbitrary")),
        name="retention",
    )(ret_all, s0, gn_g)


KEYS = WINDOW + CHUNK
KPAD = 256


def _swa_kernel(sink_ref, q_ref, k2_ref, k1_ref, k0_ref, v2_ref, v1_ref, v0_ref, o_ref, *, masked, n_q):
    j = pl.program_id(1)
    kall = jnp.concatenate([k2_ref[...], k1_ref[...], k0_ref[...]], axis=0)
    vall = jnp.concatenate([v2_ref[...], v1_ref[...], v0_ref[...]], axis=0)
    lane = lax.broadcasted_iota(jnp.int32, kall.shape, 1)
    zpad = jnp.zeros((KPAD - KEYS, LANES), BF16)

    def lane_halves(win, h):
        rolled = pltpu.roll(win, SWA_HEAD_DIM, 1)
        lo_src, hi_src = (win, rolled) if h == 0 else (rolled, win)
        return (jnp.where(lane < SWA_HEAD_DIM, lo_src, 0.0).astype(BF16),
                jnp.where(lane >= SWA_HEAD_DIM, hi_src, 0.0).astype(BF16))

    col = lax.broadcasted_iota(jnp.int32, (1, KPAD), 1)
    n_pairs = SWA_GROUP // 2
    rows = n_pairs * CHUNK
    row = lax.broadcasted_iota(jnp.int32, (rows, 1), 0)
    out_lane = lax.broadcasted_iota(jnp.int32, (rows, LANES), 1)
    nt = (((1,), (1,)), ((), ()))
    q_scale = jnp.asarray(SWA_HEAD_DIM ** -0.5, BF16)
    for h in range(SWA_KV_HEADS):
        k_lo, k_hi = lane_halves(kall, h)
        v_lo, v_hi = lane_halves(vall, h)
        base = h * SWA_GROUP * SWA_HEAD_DIM
        sinks = []
        for half in range(2):
            sink = jnp.zeros((rows, 1), F32)
            for p in range(n_pairs):
                sink = jnp.where(row // CHUNK == p, sink_ref[h * SWA_GROUP + 2 * p + half], sink)
            sinks.append(sink)
        for u in range(n_q):
            r0 = u * CHUNK
            c = j * n_q + u
            if masked:
                first_ok = jnp.where(c >= 2, 0, jnp.where(c == 1, CHUNK, 2 * CHUNK))
                ok = (col >= first_ok) & (col < KEYS)
            else:
                ok = col < KEYS
            kk = jnp.concatenate([k_lo[r0:r0 + KEYS], zpad, k_hi[r0:r0 + KEYS], zpad], axis=0)
            vv = jnp.concatenate([v_lo[r0:r0 + KEYS], zpad, v_hi[r0:r0 + KEYS], zpad], axis=0)
            q4 = jnp.concatenate([q_ref[r0:r0 + CHUNK, base + p * LANES: base + (p + 1) * LANES]
                                  for p in range(n_pairs)], axis=0)
            s = lax.dot_general(q4 * q_scale, kk, nt, preferred_element_type=F32)
            ps, invs = [], []
            for half in range(2):
                sh = jnp.where(ok, s[:, half * KPAD:(half + 1) * KPAD], NEG_INF)
                m = jnp.maximum(jnp.max(sh, axis=-1, keepdims=True), sinks[half])
                p_half = jnp.exp(sh - m)
                den = jnp.sum(p_half, axis=-1, keepdims=True) + jnp.exp(sinks[half] - m)
                ps.append(p_half.astype(BF16))
                invs.append(1.0 / den)
            pv = jnp.dot(jnp.concatenate(ps, axis=1), vv, preferred_element_type=F32)
            o = pv * jnp.where(out_lane < SWA_HEAD_DIM, invs[0], invs[1])
            for p in range(n_pairs):
                o_ref[r0:r0 + CHUNK, base + p * LANES: base + (p + 1) * LANES] = (
                    o[p * CHUNK:(p + 1) * CHUNK].astype(BF16))


def _swa(sinks, sq, k_arrs, v_arrs, k_maps, v_maps, *, n_seq, nc, masked, n_q):
    R = sq.shape[0]
    kv_rows = (CHUNK, CHUNK, n_q * CHUNK)
    kv_specs = [pl.BlockSpec((kv_rows[t % 3], SWA_KV_W), m) for t, m in enumerate((*k_maps, *v_maps))]
    return pl.pallas_call(
        functools.partial(_swa_kernel, masked=masked, n_q=n_q),
        grid=(n_seq, nc),
        in_specs=[pl.BlockSpec(memory_space=pltpu.SMEM),
                  pl.BlockSpec((n_q * CHUNK, SWA_Q_W), lambda b, c: (b * nc + c, 0)),
                  *kv_specs],
        out_specs=pl.BlockSpec((n_q * CHUNK, SWA_Q_W), lambda b, c: (b * nc + c, 0)),
        out_shape=jax.ShapeDtypeStruct((R, SWA_Q_W), BF16),
        compiler_params=_cparams(("parallel", "arbitrary")),
        name="swa",
    )(sinks, sq, *k_arrs, *v_arrs)


def _route(logits):
    tm = logits.shape[0]
    lane = lax.broadcasted_iota(jnp.int32, (tm, LANES), 1)
    is_g = lane < N_GROUPS
    gl = jnp.where(is_g, logits, NEG_INF)
    gmax = jnp.max(gl, axis=-1, keepdims=True)
    gidx = jnp.min(jnp.where(gl == gmax, lane, LANES), axis=-1, keepdims=True)
    gsum = jnp.sum(jnp.where(is_g, jnp.exp(gl - gmax), 0.0), axis=-1, keepdims=True)
    g_w = 1.0 / gsum
    base = N_GROUPS + EXPERTS_PER_GROUP * gidx
    el = jnp.where((lane >= base) & (lane < base + EXPERTS_PER_GROUP), logits, NEG_INF)
    v1 = jnp.max(el, axis=-1, keepdims=True)
    i1 = jnp.min(jnp.where(el == v1, lane, LANES), axis=-1, keepdims=True)
    el2 = jnp.where(lane == i1, NEG_INF, el)
    v2 = jnp.max(el2, axis=-1, keepdims=True)
    i2 = jnp.min(jnp.where(el2 == v2, lane, LANES), axis=-1, keepdims=True)
    e2 = jnp.exp(v2 - v1)
    den = 1.0 + e2
    w1 = g_w / den
    w2 = g_w * e2 / den
    l1 = i1 - base
    l2 = i2 - base
    first_lo = l1 < l2
    la = jnp.where(first_lo, l1, l2)
    lb = jnp.where(first_lo, l2, l1)
    wa = jnp.where(first_lo, w1, w2)
    wb = jnp.where(first_lo, w2, w1)
    pair = jnp.where(la == 0, lb - 1, jnp.where(la == 1, jnp.where(lb == 3, 3, 4), 5))
    swapped = la == 2
    w_slot_a = jnp.where(swapped, wb, wa)
    w_slot_b = jnp.where(swapped, wa, wb)
    cls = (gidx * N_PAIRS + pair).astype(F32)
    return jnp.where(lane == 0, w_slot_a, jnp.where(lane == 1, w_slot_b, jnp.where(lane == 2, cls, 0.0)))


ROW_W = D_MODEL + LANES


def _merge_kernel(x_ref, r_ref, o_ref, gate_ref, mod_ref, modp_ref, g2_ref, wrb_ref, wsb_ref, wout_ref, wr_ref,
                  br_ref, x1_ref, text_ref, meta_ref, x1_scr, *, n_seq):
    i = pl.program_id(0)
    n = pl.num_programs(0) - 1
    tm = x_ref.shape[0]

    def matmul_stage():
        g_r = gate_ref[:, :D_MODEL].astype(F32)
        g_s = gate_ref[:, D_MODEL:].astype(F32)
        merged = (g_r * jnp.dot(r_ref[...], wrb_ref[...], preferred_element_type=F32)
                  + g_s * jnp.dot(o_ref[...], wsb_ref[...], preferred_element_type=F32))
        mix = jnp.dot(merged.astype(BF16), wout_ref[...], preferred_element_type=F32)
        gt1 = mod_ref[:, 2, :]
        x1 = (x_ref[...].reshape(n_seq, tm // n_seq, D_MODEL) + gt1[:, None, :]
              * mix.reshape(n_seq, tm // n_seq, D_MODEL)).reshape(tm, D_MODEL)
        x1_ref[...] = x1
        x1_scr[...] = x1

    def vector_stage():
        t = _modulated_norm(x1_scr[...], g2_ref[...], modp_ref[:, 3, :], modp_ref[:, 4, :], n_seq)
        logits = jnp.dot(t.astype(BF16), wr_ref[...], preferred_element_type=F32) + br_ref[...]
        meta = _route(logits)
        text_ref[:, :D_MODEL] = t
        text_ref[:, D_MODEL:] = meta
        meta_ref[...] = meta

    @pl.when(i == 0)
    def _():
        matmul_stage()

    @pl.when((i > 0) & (i < n))
    def _():
        vector_stage()
        matmul_stage()

    @pl.when(i == n)
    def _():
        vector_stage()


def _merge(x2d, r, o_swa, gates, mod3, g2, wrb, wsb, wout, wr, br, *, seq_len, tm):
    R = x2d.shape[0]
    n = R // tm
    if seq_len >= tm:
        n_seq, tps = 1, seq_len // tm
        seq_of = lambda t: t // tps
    else:
        n_seq = tm // seq_len
        seq_of = lambda t: t
    cur = lambda i: jnp.minimum(i, n - 1)
    prev = lambda i: jnp.maximum(i - 1, 0)
    row = lambda w: pl.BlockSpec((tm, w), lambda i: (cur(i), 0))
    row_prev = lambda w: pl.BlockSpec((tm, w), lambda i: (prev(i), 0))
    return pl.pallas_call(
        functools.partial(_merge_kernel, n_seq=n_seq),
        grid=(n + 1,),
        in_specs=[row(D_MODEL), row(RET_W), row(SWA_Q_W), row(2 * D_MODEL),
                  pl.BlockSpec((n_seq, N_MOD, D_MODEL), lambda i: (seq_of(cur(i)), 0, 0)),
                  pl.BlockSpec((n_seq, N_MOD, D_MODEL), lambda i: (seq_of(prev(i)), 0, 0)),
                  _resident((1, D_MODEL)),
                  _resident((RET_W, D_MODEL)), _resident((SWA_Q_W, D_MODEL)), _resident((D_MODEL, D_MODEL)),
                  _resident((D_MODEL, LANES)), _resident((1, LANES))],
        out_specs=[row(D_MODEL), row_prev(ROW_W), row_prev(LANES)],
        out_shape=[jax.ShapeDtypeStruct((R, D_MODEL), F32),
                   jax.ShapeDtypeStruct((R, ROW_W), F32),
                   jax.ShapeDtypeStruct((R, LANES), F32)],
        scratch_shapes=[pltpu.VMEM((tm, D_MODEL), F32)],
        compiler_params=_cparams(("arbitrary",)),
        name="merge",
    )(x2d, r, o_swa, gates, mod3, mod3, g2, wrb, wsb, wout, wr, br)


PLAN_BLK = 2048
TILE_ROWS = 256


def _plan_kernel(meta_a_ref, meta_b_ref, pos_ref, tile_ref, pad_ref, cnt_scr, offs_scr, carry_scr, tri_scr,
                 *, tm, nb_a):
    ph = pl.program_id(0)
    b = pl.program_id(1)
    blk = meta_a_ref.shape[0]
    lane = lax.broadcasted_iota(jnp.int32, (blk, LANES), 1)
    cls_col = jnp.where(b < nb_a, meta_a_ref[:, 2:3], meta_b_ref[:, 2:3])
    oh = jnp.where(cls_col == lane.astype(F32), 1.0, 0.0)

    @pl.when((ph == 0) & (b == 0))
    def _():
        cnt_scr[...] = jnp.zeros_like(cnt_scr)
        ri = lax.broadcasted_iota(jnp.int32, (blk, blk), 0)
        ci = lax.broadcasted_iota(jnp.int32, (blk, blk), 1)
        tri_scr[...] = jnp.where(ci <= ri, 1.0, 0.0).astype(BF16)

    @pl.when(ph == 0)
    def _():
        cnt_scr[...] += jnp.sum(oh, axis=0, keepdims=True)

    @pl.when((ph == 1) & (b == 0))
    def _():
        cnt = cnt_scr[...]
        ptiles = jnp.floor((cnt + (tm - 1)) * (1.0 / tm))
        ri = lax.broadcasted_iota(jnp.int32, (LANES, LANES), 0)
        ci = lax.broadcasted_iota(jnp.int32, (LANES, LANES), 1)
        before = jnp.where(ri < ci, 1.0, 0.0).astype(BF16)
        offs = jnp.dot(ptiles.astype(BF16), before, preferred_element_type=F32) * tm
        offs_scr[...] = offs
        carry_scr[...] = jnp.zeros_like(carry_scr)
        padded = ptiles * tm
        ends = offs + padded
        tl = lax.broadcasted_iota(jnp.int32, (TILE_ROWS, LANES), 1)
        tstart = lax.broadcasted_iota(jnp.int32, (TILE_ROWS, LANES), 0).astype(F32) * tm
        tcls = jnp.sum(jnp.where((ends[0:1, :] <= tstart) & (tl < N_CLASSES), 1.0, 0.0), axis=1, keepdims=True)
        tcls = jnp.minimum(tcls, N_CLASSES - 1.0)
        total = jnp.max(ends[0:1, :], axis=1, keepdims=True)
        grp = (jnp.where(tcls >= N_PAIRS, 1.0, 0.0) + jnp.where(tcls >= 2 * N_PAIRS, 1.0, 0.0)
               + jnp.where(tcls >= 3 * N_PAIRS, 1.0, 0.0))
        pair = tcls - N_PAIRS * grp
        la = jnp.where(pair < 3, 0.0, jnp.where(pair < 5, 1.0, 3.0))
        lb = jnp.where(pair == 0, 1.0, jnp.where((pair == 1) | (pair >= 4), 2.0, 3.0))
        ea = EXPERTS_PER_GROUP * grp + la
        eb = EXPERTS_PER_GROUP * grp + lb
        n_used = total * (1.0 / tm)
        tile_ref[...] = jnp.where(tl == 0, ea, jnp.where(tl == 1, eb, jnp.where(tl == 2, n_used, 0.0))
                                  ).astype(jnp.int32)
        npad = padded - cnt
        pstart = jnp.dot(npad.astype(BF16), before, preferred_element_type=F32)
        n_class_pad = jnp.sum(npad[0:1, :], axis=1, keepdims=True)
        rows = pad_ref.shape[0]
        v = (lax.broadcasted_iota(jnp.int32, (rows, LANES), 0) * LANES
             + lax.broadcasted_iota(jnp.int32, (rows, LANES), 1)).astype(F32)
        slot = jnp.where(v >= n_class_pad, total - n_class_pad + v, 0.0)
        for c in range(N_CLASSES):
            ps = pstart[0:1, c:c + 1]
            inside = (v >= ps) & (v < ps + npad[0:1, c:c + 1])
            slot = jnp.where(inside, offs[0:1, c:c + 1] + cnt[0:1, c:c + 1] - ps + v, slot)
        pad_ref[...] = slot.astype(jnp.int32)

    @pl.when(ph == 1)
    def _():
        incl = jnp.dot(tri_scr[...], oh.astype(BF16), preferred_element_type=F32)
        base = offs_scr[0:1, :] + carry_scr[0:1, :]
        pos = jnp.sum(oh * (base + incl - oh), axis=1, keepdims=True)
        pos_ref[...] = jnp.broadcast_to(pos, (blk, LANES)).astype(jnp.int32)
        carry_scr[...] += jnp.sum(oh, axis=0, keepdims=True)


def _plan(meta_a, meta_b, *, tm):
    na, nb = meta_a.shape[0], meta_b.shape[0]
    n = na + nb
    n_tiles = n // tm + N_CLASSES
    n_free = N_CLASSES * tm
    assert n_tiles <= TILE_ROWS and na % PLAN_BLK == 0 and nb % PLAN_BLK == 0 and n_free % LANES == 0
    n_slots = n_tiles * tm
    nb_a = na // PLAN_BLK
    nb_b = nb // PLAN_BLK
    pos2d, tile2d, pad2d = pl.pallas_call(
        functools.partial(_plan_kernel, tm=tm, nb_a=nb_a),
        grid=(2, nb_a + nb_b),
        in_specs=[pl.BlockSpec((PLAN_BLK, LANES), lambda ph, b: (jnp.minimum(b, nb_a - 1), 0)),
                  pl.BlockSpec((PLAN_BLK, LANES), lambda ph, b: (jnp.maximum(b - nb_a, 0), 0))],
        out_specs=[pl.BlockSpec((PLAN_BLK, LANES), lambda ph, b: (b * ph, 0)),
                   pl.BlockSpec((TILE_ROWS, LANES), lambda ph, b: (0, 0)),
                   pl.BlockSpec((n_free // LANES, LANES), lambda ph, b: (0, 0))],
        out_shape=[jax.ShapeDtypeStruct((n, LANES), jnp.int32),
                   jax.ShapeDtypeStruct((TILE_ROWS, LANES), jnp.int32),
                   jax.ShapeDtypeStruct((n_free // LANES, LANES), jnp.int32)],
        scratch_shapes=[pltpu.VMEM((8, LANES), F32), pltpu.VMEM((8, LANES), F32), pltpu.VMEM((8, LANES), F32),
                        pltpu.VMEM((PLAN_BLK, PLAN_BLK), BF16)],
        compiler_params=_cparams(("arbitrary", "arbitrary")),
        name="plan",
    )(meta_a, meta_b)
    return pos2d[:, 0], pad2d.reshape(-1), tile2d[:n_tiles, 0], tile2d[:n_tiles, 1], tile2d[0:1, 2], n_slots


N_STAGE = 3


def _scatter_kernel(pos_ref, text_a_ref, text_b_ref, out_ref, stage, zero_scr, sem_in, sem_out, *, nb_a, nb_b):
    i = pl.program_id(0)
    n = pl.num_programs(0)
    n_tok = nb_a + nb_b
    tb = pos_ref.shape[2]
    slot = i % N_STAGE

    def load(step, s):
        def from_a():
            return pltpu.make_async_copy(text_a_ref.at[pl.ds(step * tb, tb), :], stage.at[s], sem_in.at[s])

        def from_b():
            return pltpu.make_async_copy(text_b_ref.at[pl.ds((step - nb_a) * tb, tb), :], stage.at[s], sem_in.at[s])

        return from_a, from_b

    def start_load(step, s):
        from_a, from_b = load(step, s)

        @pl.when(step < nb_a)
        def _():
            from_a().start()

        @pl.when((step >= nb_a) & (step < n_tok))
        def _():
            from_b().start()

    def wait_rows(s):
        pltpu.make_async_copy(stage.at[s], out_ref.at[pl.ds(0, tb), :], sem_out.at[s]).wait()

    def scatter_rows(src_row):
        def body(r, carry):
            dst = pos_ref[0, 0, r]
            pltpu.make_async_copy(src_row(r), out_ref.at[pl.ds(dst, 1), :], sem_out.at[slot]).start()
            return carry

        lax.fori_loop(0, tb, body, 0, unroll=8)

    @pl.when(i == 0)
    def _():
        zero_scr[...] = jnp.zeros_like(zero_scr)
        start_load(i, slot)

    @pl.when(i >= 2)
    def _():
        wait_rows((i + 1) % N_STAGE)

    start_load(i + 1, (i + 1) % N_STAGE)

    @pl.when(i < n_tok)
    def _():
        pltpu.make_async_copy(text_a_ref.at[pl.ds(0, tb), :], stage.at[slot], sem_in.at[slot]).wait()
        scatter_rows(lambda r: stage.at[slot, pl.ds(r, 1), :])

    @pl.when(i >= n_tok)
    def _():
        scatter_rows(lambda r: zero_scr.at[pl.ds(0, 1), :])

    @pl.when(i == n - 1)
    def _():
        wait_rows((i + 2) % N_STAGE)
        wait_rows(slot)


def _scatter_rows(text_a, text_b, pos_ext, *, tb):
    nb_a = text_a.shape[0] // tb
    nb_b = text_b.shape[0] // tb
    n_steps = pos_ext.shape[0] // tb
    assert n_steps >= 2 and nb_a >= 1 and nb_b >= 1
    return pl.pallas_call(
        functools.partial(_scatter_kernel, nb_a=nb_a, nb_b=nb_b),
        grid=(n_steps,),
        in_specs=[pl.BlockSpec((1, 1, tb), lambda i: (i, 0, 0), memory_space=pltpu.SMEM),
                  pl.BlockSpec(memory_space=pl.ANY),
                  pl.BlockSpec(memory_space=pl.ANY)],
        out_specs=pl.BlockSpec(memory_space=pl.ANY),
        out_shape=jax.ShapeDtypeStruct((pos_ext.shape[0], ROW_W), F32),
        scratch_shapes=[pltpu.VMEM((N_STAGE, tb, ROW_W), F32), pltpu.VMEM((8, ROW_W), F32),
                        pltpu.SemaphoreType.DMA((N_STAGE,)), pltpu.SemaphoreType.DMA((N_STAGE,))],
        compiler_params=_cparams(("arbitrary",)),
        name="scatter_rows",
    )(pos_ext.reshape(n_steps, 1, tb), text_a, text_b)


def _moe_kernel(ea_ref, eb_ref, nused_ref, xs_ref, w1a_ref, w3a_ref, w2a_ref, w1b_ref, w3b_ref, w2b_ref, y_ref):
    i = pl.program_id(0)

    @pl.when(i >= nused_ref[0])
    def _():
        y_ref[...] = jnp.zeros_like(y_ref)

    @pl.when(i < nused_ref[0])
    def _():
        x = xs_ref[:, :D_MODEL].astype(BF16)
        wa = xs_ref[:, D_MODEL:D_MODEL + 1]
        wb = xs_ref[:, D_MODEL + 1:D_MODEL + 2]

        def hidden(w1_ref, w3_ref, gate):
            a = jnp.dot(x, w1_ref[0], preferred_element_type=F32)
            b = jnp.dot(x, w3_ref[0], preferred_element_type=F32)
            return (a * _sigmoid(a) * b * gate).astype(BF16)

        ha = hidden(w1a_ref, w3a_ref, wa)
        hb = hidden(w1b_ref, w3b_ref, wb)
        y_ref[...] = (jnp.dot(ha, w2a_ref[0], preferred_element_type=F32)
                      + jnp.dot(hb, w2b_ref[0], preferred_element_type=F32))


def _moe(tile_ea, tile_eb, n_used, xs, w1, w3, w2, *, tm, n_tiles):
    last = lambda i, nu: jnp.minimum(i, nu[0] - 1)
    wa_map = lambda i, ea, eb, nu: (ea[last(i, nu)], 0, 0)
    wb_map = lambda i, ea, eb, nu: (eb[last(i, nu)], 0, 0)
    row_map = lambda i, ea, eb, nu: (last(i, nu), 0)
    up = (1, D_MODEL, D_EXPERT)
    down = (1, D_EXPERT, D_MODEL)
    grid_spec = pltpu.PrefetchScalarGridSpec(
        num_scalar_prefetch=3,
        grid=(n_tiles,),
        in_specs=[pl.BlockSpec((tm, D_MODEL + LANES), row_map),
                  pl.BlockSpec(up, wa_map), pl.BlockSpec(up, wa_map), pl.BlockSpec(down, wa_map),
                  pl.BlockSpec(up, wb_map), pl.BlockSpec(up, wb_map), pl.BlockSpec(down, wb_map)],
        out_specs=pl.BlockSpec((tm, D_MODEL), lambda i, ea, eb, nu: (i, 0)),
    )
    return pl.pallas_call(
        _moe_kernel,
        grid_spec=grid_spec,
        out_shape=jax.ShapeDtypeStruct((n_tiles * tm, D_MODEL), F32),
        compiler_params=_cparams(("arbitrary",)),
        name="moe",
    )(tile_ea, tile_eb, n_used, xs, w1, w3, w2, w1, w3, w2)


def _final_kernel(pos_ref, pos_next_ref, x1_ref, mod_ref, g_ref, ys_ref, o_ref, ybuf, sem, *, n_seq):
    i = pl.program_id(0)
    n = pl.num_programs(0)
    tm = x1_ref.shape[0]
    slot = i % 2

    def start_gather(idx_ref, s):
        def body(r, carry):
            src = idx_ref[0, 0, r]
            pltpu.make_async_copy(ys_ref.at[pl.ds(src, 1), :], ybuf.at[s, pl.ds(r, 1), :], sem.at[s]).start()
            return carry

        lax.fori_loop(0, tm, body, 0, unroll=8)

    @pl.when(i == 0)
    def _():
        start_gather(pos_ref, 0)

    @pl.when(i + 1 < n)
    def _():
        start_gather(pos_next_ref, 1 - slot)

    pltpu.make_async_copy(ys_ref.at[pl.ds(0, tm), :], ybuf.at[slot], sem.at[slot]).wait()
    gt2 = mod_ref[:, 5, :]
    x2 = (x1_ref[...].reshape(n_seq, tm // n_seq, D_MODEL)
          + gt2[:, None, :] * ybuf[slot].reshape(n_seq, tm // n_seq, D_MODEL)).reshape(tm, D_MODEL)
    ms = jnp.mean(x2 * x2, axis=-1, keepdims=True)
    o_ref[...] = x2 * lax.rsqrt(ms + EPS) * g_ref[...]


def _final(x1, y_sorted, pos, mod3, gf, *, seq_len, tm):
    R = x1.shape[0]
    if seq_len >= tm:
        n_seq, tps = 1, seq_len // tm
        mod_map = lambda i: (i // tps, 0, 0)
    else:
        n_seq = tm // seq_len
        mod_map = lambda i: (i, 0, 0)
    n = R // tm
    row = pl.BlockSpec((tm, D_MODEL), lambda i: (i, 0))
    pos3 = pos.reshape(n, 1, tm)
    return pl.pallas_call(
        functools.partial(_final_kernel, n_seq=n_seq),
        grid=(n,),
        in_specs=[pl.BlockSpec((1, 1, tm), lambda i: (i, 0, 0), memory_space=pltpu.SMEM),
                  pl.BlockSpec((1, 1, tm), lambda i: (jnp.minimum(i + 1, n - 1), 0, 0), memory_space=pltpu.SMEM),
                  row, pl.BlockSpec((n_seq, N_MOD, D_MODEL), mod_map), _resident((1, D_MODEL)),
                  pl.BlockSpec(memory_space=pl.ANY)],
        out_specs=row,
        out_shape=jax.ShapeDtypeStruct((R, D_MODEL), F32),
        scratch_shapes=[pltpu.VMEM((2, tm, D_MODEL), F32), pltpu.SemaphoreType.DMA((2,))],
        compiler_params=_cparams(("arbitrary",)),
        name="final",
    )(pos3, pos3, x1, mod3, gf, y_sorted)


def _rope_tables(pos):
    half = RET_DK // 2
    inv = ROPE_BASE ** (-jnp.arange(half, dtype=F32) / half)
    ang = pos.astype(F32)[:, None] * inv[None, :]
    cos = jnp.cos(ang)
    sin = jnp.sin(ang)
    return jnp.concatenate([cos, cos], axis=-1), jnp.concatenate([-sin, sin], axis=-1)


def kernel(x_prompt, x_sample, cache_ret_state, cache_swa_k, cache_swa_v, c_prompt, c_sample,
           norm1_g, norm2_g, ada_w, ada_b, w_in, ret_gn_g, swa_sinks, w_ret_branch, w_swa_branch, w_out,
           router_group_w, router_group_b, router_expert_w, router_expert_b,
           expert_w1, expert_w3, expert_w2, final_norm_g):
    depth = w_in.shape[0]
    assert depth == 1
    bp, tp, _ = x_prompt.shape
    bs, ts, _ = x_sample.shape
    past = WINDOW
    assert cache_swa_k.shape[2] == past and ts == CHUNK and tp % 512 == 0
    tm = 512

    l = 0
    c1 = 4 * RET_W
    c2 = c1 + SWA_Q_W
    c3 = c2 + 2 * SWA_KV_W
    wret = w_in[l, :, :c1].astype(BF16)
    n_r = N_GROUPS + N_EXPERTS
    wr = jnp.zeros((D_MODEL, LANES), F32).at[:, :N_GROUPS].set(router_group_w[l]).at[:, N_GROUPS:n_r].set(
        router_expert_w[l]).astype(BF16)
    br = jnp.zeros((1, LANES), F32).at[0, :N_GROUPS].set(router_group_b[l]).at[0, N_GROUPS:n_r].set(
        router_expert_b[l])
    g1 = norm1_g[l].reshape(1, D_MODEL)
    g2 = norm2_g[l].reshape(1, D_MODEL)
    gn = ret_gn_g[l].reshape(1, RET_W)
    gf = final_norm_g.reshape(1, D_MODEL)
    sinks = swa_sinks[l]

    c_all = jnp.concatenate([c_prompt, c_sample], axis=0)
    mod = _ada(c_all, ada_w[l], ada_b[l]).reshape(bp + bs, N_MOD, D_MODEL)
    mod_p, mod_s = mod[:bp], mod[bp:]

    cos_p, sin_p = _rope_tables(jnp.arange(tp))
    cos_s, sin_s = _rope_tables(PAST_LEN + jnp.arange(ts))
    rep = tm // ts
    cos_s, sin_s = jnp.tile(cos_s, (rep, 1)), jnp.tile(sin_s, (rep, 1))

    xp = x_prompt.reshape(bp * tp, D_MODEL)
    xs = x_sample.reshape(bs * ts, D_MODEL)

    lc_p = 128
    whole = lambda w: (w.reshape(-1, w.shape[-1]), [(0, w.shape[-1])])
    ret_p, h_p, (wsq, wkv, wbg, wrb, wsb, wo) = _inproj_ret(
        xp, mod_p, g1, cos_p, sin_p, wret, seq_len=tp, tm=tm, lc=lc_p,
        casts=[(w_in[l], [(c1, c2), (c2, c3), (c3, w_in.shape[2])]),
               whole(w_ret_branch[l]), whole(w_swa_branch[l]), whole(w_out[l])])
    sq_p, kv_p, gate_p, (w1, w3, w2) = _inproj_rest(
        h_p, wsq, wkv, wbg, tm=tm, casts=[whole(expert_w1[l]), whole(expert_w3[l]), whole(expert_w2[l])])
    w1, w3, w2 = (w.reshape(e.shape[1:]) for w, e in zip((w1, w3, w2), (expert_w1, expert_w3, expert_w2)))
    ret_s, h_s, _ = _inproj_ret(xs, mod_s, g1, cos_s, sin_s, wret, seq_len=ts, tm=tm, lc=ts)
    sq_s, kv_s, gate_s, _ = _inproj_rest(h_s, wsq, wkv, wbg, tm=tm)

    s0_p = jnp.zeros((bp, RET_HEADS, RET_DK, RET_DV), F32)
    r_p, state_p = _retention(ret_p, s0_p, gn, n_seq=bp, seq_len=tp, lc=lc_p, n_sub=2)
    r_s, state_s = _retention(ret_s, cache_ret_state[l].astype(F32), gn, n_seq=bs, seq_len=ts, lc=ts, n_sub=1)

    n_q = 4
    nc_p = tp // CHUNK
    ns_p = nc_p // n_q
    prev_map = lambda back, colblk: (lambda b, s: (b * nc_p + jnp.maximum(s * n_q - back, 0), colblk))
    own_map = lambda colblk: (lambda b, s: (b * ns_p + s, colblk))
    o_p = _swa(sinks, sq_p, [kv_p] * 3, [kv_p] * 3,
               [prev_map(2, 0), prev_map(1, 0), own_map(0)], [prev_map(2, 1), prev_map(1, 1), own_map(1)],
               n_seq=bp, nc=ns_p, masked=True, n_q=n_q)
    ck = cache_swa_k[l].reshape(bs * past, SWA_KV_W)
    cv = cache_swa_v[l].reshape(bs * past, SWA_KV_W)
    cmap = lambda blk: (lambda b, c: (2 * b + blk, 0))
    o_s = _swa(sinks, sq_s, [ck, ck, kv_s], [cv, cv, kv_s],
               [cmap(0), cmap(1), lambda b, c: (b, 0)], [cmap(0), cmap(1), lambda b, c: (b, 1)],
               n_seq=bs, nc=1, masked=False, n_q=1)

    tm_m = 256
    n_p = bp * tp
    x1_p, text_p, meta_p = _merge(xp, r_p, o_p, gate_p, mod_p, g2, wrb, wsb, wo, wr, br, seq_len=tp, tm=tm_m)
    x1_s, text_s, meta_s = _merge(xs, r_s, o_s, gate_s, mod_s, g2, wrb, wsb, wo, wr, br, seq_len=ts, tm=tm_m)

    tm_e = 256
    pos, free_slots, tile_ea, tile_eb, n_used, n_slots = _plan(meta_p, meta_s, tm=tm_e)
    pos_ext = jnp.concatenate([pos, free_slots], axis=0)
    xsorted = _scatter_rows(text_p, text_s, pos_ext, tb=512)
    y_sorted = _moe(tile_ea, tile_eb, n_used, xsorted, w1, w3, w2, tm=tm_e, n_tiles=n_slots // tm_e)

    tm_f = 512
    out_p = _final(x1_p, y_sorted, pos[:n_p], mod_p, gf, seq_len=tp, tm=tm_f)
    out_s = _final(x1_s, y_sorted, pos[n_p:], mod_s, gf, seq_len=ts, tm=tm_f)

    y_prompt = out_p.reshape(bp, tp, D_MODEL)
    y_sample = out_s.reshape(bs, ts, D_MODEL)
    kvp = kv_p.reshape(bp, tp, 2 * SWA_KV_W)[:, tp - WINDOW:].reshape(bp, WINDOW, 2, SWA_KV_HEADS, SWA_HEAD_DIM)
    kvs = kv_s.reshape(bs, ts, 2, SWA_KV_HEADS, SWA_HEAD_DIM)
    k_s = jnp.concatenate([cache_swa_k[l].astype(F32), kvs[:, :, 0]], axis=1)[:, -WINDOW:]
    v_s = jnp.concatenate([cache_swa_v[l].astype(F32), kvs[:, :, 1]], axis=1)[:, -WINDOW:]
    return (y_prompt, y_sample, state_p[None], kvp[:, :, 0][None], kvp[:, :, 1][None],
            state_s[None], k_s[None], v_s[None])
```

```python
import functools
import math

import jax
import jax.numpy as jnp
from jax import lax
from jax.experimental import pallas as pl
from jax.experimental.pallas import tpu as pltpu

F32 = jnp.float32
BF16 = jnp.bfloat16

D_MODEL = 2048
CHUNK = 64
RET_HEADS = 8
RET_DK = 128
RET_DV = 128
RET_W = RET_HEADS * RET_DK
ROPE_BASE = 10000.0
SWA_Q_HEADS = 16
SWA_KV_HEADS = 2
SWA_GROUP = SWA_Q_HEADS // SWA_KV_HEADS
SWA_HEAD_DIM = 64
SWA_Q_W = SWA_Q_HEADS * SWA_HEAD_DIM
SWA_KV_W = SWA_KV_HEADS * SWA_HEAD_DIM
WINDOW = 128
PAST_LEN = 1024
N_GROUPS = 4
EXPERTS_PER_GROUP = 4
N_EXPERTS = 16
D_EXPERT = 512
N_MOD = 6
EPS = 1e-6
NEG_INF = -1e30
N_PAIRS = 6
N_CLASSES = N_GROUPS * N_PAIRS

LANES = 128
VMEM_LIMIT = 56 * 1024 * 1024

RET_LOG_GAMMA = tuple(math.log1p(-(2.0 ** (-5.0 - h))) for h in range(RET_HEADS))


def _cparams(sem):
    return pltpu.CompilerParams(dimension_semantics=sem, vmem_limit_bytes=VMEM_LIMIT)


def _sigmoid(x):
    return 0.5 * jnp.tanh(0.5 * x) + 0.5


def _resident(shape):
    nd = len(shape)
    return pl.BlockSpec(shape, lambda *_: (0,) * nd, pipeline_mode=pl.Buffered(1))


def _ada_kernel(c_ref, w_ref, b_ref, o_ref):
    c = c_ref[...]
    a = c * jax.nn.sigmoid(c)
    w = w_ref[...]

    def split(v):
        hi = v.astype(BF16)
        return hi, (v - hi.astype(F32)).astype(BF16)

    a_hi, a_lo = split(a)
    w_hi, w_lo = split(w)
    dot = functools.partial(jnp.dot, preferred_element_type=F32)
    o_ref[...] = dot(a_hi, w_hi) + (dot(a_lo, w_hi) + dot(a_hi, w_lo)) + b_ref[...]


def _ada(c_all, ada_w, ada_b):
    nb = c_all.shape[0]
    n_out = ada_w.shape[1]
    tn = 1024
    return pl.pallas_call(
        _ada_kernel,
        grid=(n_out // tn,),
        in_specs=[pl.BlockSpec((nb, D_MODEL), lambda j: (0, 0)),
                  pl.BlockSpec((D_MODEL, tn), lambda j: (0, j)),
                  pl.BlockSpec((1, tn), lambda j: (0, j))],
        out_specs=pl.BlockSpec((nb, tn), lambda j: (0, j)),
        out_shape=jax.ShapeDtypeStruct((nb, n_out), F32),
        compiler_params=_cparams(("arbitrary",)),
        name="ada",
    )(c_all, ada_w, ada_b.reshape(1, n_out))


def _modulated_norm(x, g, shift, scale, n_seq):
    tm = x.shape[0]
    ms = jnp.mean(x * x, axis=-1, keepdims=True)
    y3 = (x * lax.rsqrt(ms + EPS)).reshape(n_seq, tm // n_seq, D_MODEL)
    gain = g * (1.0 + scale)
    h = y3 * gain[:, None, :] + shift[:, None, :]
    return h.reshape(tm, D_MODEL)


COL_BLK = 1024


def _cast_plan(casts, n_steps):
    in_specs, out_specs, out_shapes, splits = [], [], [], []
    for w, ranges in casts:
        slab = w.shape[0] // n_steps
        assert slab * n_steps == w.shape[0] and slab % 16 == 0
        in_specs.append(pl.BlockSpec((slab, w.shape[1]), lambda i: (i, 0)))
        for c0, c1 in ranges:
            out_specs.append(pl.BlockSpec((slab, c1 - c0), lambda i: (i, 0)))
            out_shapes.append(jax.ShapeDtypeStruct((w.shape[0], c1 - c0), BF16))
        splits.append(tuple(ranges))
    return in_specs, out_specs, out_shapes, tuple(splits)


def _cast_slabs(in_refs, out_refs, splits):
    k = 0
    for src, ranges in zip(in_refs, splits):
        for c0, c1 in ranges:
            out_refs[k][...] = src[:, c0:c1].astype(BF16)
            k += 1


def _inproj_ret_kernel(x_ref, mod_ref, g1_ref, cos_ref, sin_ref, wret_ref, *rest, n_seq, lc, splits):
    n_in = len(splits)
    cast_in, (ret_ref, h_ref) = rest[:n_in], rest[n_in:n_in + 2]
    cast_out, (dq_scr, dk_scr) = rest[n_in + 2:-2], rest[-2:]
    _cast_slabs(cast_in, cast_out, splits)
    tm = x_ref.shape[0]

    @pl.when(pl.program_id(0) == 0)
    def _():
        e = ((lax.broadcasted_iota(jnp.int32, (tm, RET_DK), 0) % lc) + 1).astype(F32)
        for hh in range(RET_HEADS):
            dq_scr[hh] = jnp.exp(e * RET_LOG_GAMMA[hh])
            dk_scr[hh] = jnp.exp(-e * RET_LOG_GAMMA[hh]) * (RET_DK ** -0.5)

    h = _modulated_norm(x_ref[...], g1_ref[...], mod_ref[:, 0, :], mod_ref[:, 1, :], n_seq)
    h_ref[...] = h.astype(BF16)
    cos = cos_ref[...]
    sin = sin_ref[...]
    assert COL_BLK == RET_W
    for blk in range(4):
        c0 = blk * COL_BLK
        acc = jnp.dot(h_ref[...], wret_ref[:, c0:c0 + COL_BLK], preferred_element_type=F32)
        if blk >= 2:
            ret_ref[:, c0:c0 + COL_BLK] = acc.astype(BF16)
            continue
        dec_scr = dq_scr if blk == 0 else dk_scr
        for hh in range(RET_HEADS):
            a = acc[:, hh * RET_DK:(hh + 1) * RET_DK]
            r = a * cos + pltpu.roll(a, RET_DK // 2, 1) * sin
            ret_ref[:, c0 + hh * RET_DK:c0 + (hh + 1) * RET_DK] = (r * dec_scr[hh]).astype(BF16)


def _inproj_rest_kernel(h_ref, wsq_ref, wkv_ref, wbg_ref, *rest, splits):
    n_in = len(splits)
    cast_in, (sq_ref, kv_ref, gate_ref), cast_out = rest[:n_in], rest[n_in:n_in + 3], rest[n_in + 3:]
    _cast_slabs(cast_in, cast_out, splits)
    sq_ref[...] = jnp.dot(h_ref[...], wsq_ref[...], preferred_element_type=F32).astype(BF16)
    kv_ref[...] = jnp.dot(h_ref[...], wkv_ref[...], preferred_element_type=F32)
    for blk in range(2 * D_MODEL // COL_BLK):
        c0 = blk * COL_BLK
        acc = jnp.dot(h_ref[...], wbg_ref[:, c0:c0 + COL_BLK], preferred_element_type=F32)
        gate_ref[:, c0:c0 + COL_BLK] = _sigmoid(acc).astype(BF16)


def _inproj_ret(x2d, mod3, g1, cos_t, sin_t, wret, *, seq_len, tm, lc, casts=()):
    R = x2d.shape[0]
    if seq_len >= tm:
        n_seq, tps = 1, seq_len // tm
        mod_map = lambda i: (i // tps, 0, 0)
        tab_map = lambda i: (i % tps, 0)
    else:
        n_seq = tm // seq_len
        mod_map = lambda i: (i, 0, 0)
        tab_map = lambda i: (0, 0)
    row = lambda w: pl.BlockSpec((tm, w), lambda i: (i, 0))
    assert tm % lc == 0
    n_steps = R // tm
    c_in, c_out, c_shapes, splits = _cast_plan(casts, n_steps)
    outs = pl.pallas_call(
        functools.partial(_inproj_ret_kernel, n_seq=n_seq, lc=lc, splits=splits),
        grid=(n_steps,),
        in_specs=[row(D_MODEL),
                  pl.BlockSpec((n_seq, N_MOD, D_MODEL), mod_map),
                  _resident((1, D_MODEL)),
                  pl.BlockSpec((tm, LANES), tab_map),
                  pl.BlockSpec((tm, LANES), tab_map),
                  _resident((D_MODEL, 4 * RET_W))] + c_in,
        out_specs=[row(4 * RET_W), row(D_MODEL)] + c_out,
        out_shape=[jax.ShapeDtypeStruct((R, 4 * RET_W), BF16),
                   jax.ShapeDtypeStruct((R, D_MODEL), BF16)] + c_shapes,
        scratch_shapes=[pltpu.VMEM((RET_HEADS, tm, RET_DK), F32), pltpu.VMEM((RET_HEADS, tm, RET_DK), F32)],
        compiler_params=_cparams(("arbitrary",)),
        name="inproj_ret",
    )(x2d, mod3, g1, cos_t, sin_t, wret, *[w for w, _ in casts])
    return outs[0], outs[1], outs[2:]


def _inproj_rest(h, wsq, wkv, wbg, *, tm, casts=()):
    R = h.shape[0]
    n_steps = R // tm
    row = lambda w: pl.BlockSpec((tm, w), lambda i: (i, 0))
    c_in, c_out, c_shapes, splits = _cast_plan(casts, n_steps)
    outs = pl.pallas_call(
        functools.partial(_inproj_rest_kernel, splits=splits),
        grid=(n_steps,),
        in_specs=[row(D_MODEL),
                  _resident((D_MODEL, SWA_Q_W)),
                  _resident((D_MODEL, 2 * SWA_KV_W)),
                  _resident((D_MODEL, 2 * D_MODEL))] + c_in,
        out_specs=[row(SWA_Q_W), row(2 * SWA_KV_W), row(2 * D_MODEL)] + c_out,
        out_shape=[jax.ShapeDtypeStruct((R, SWA_Q_W), BF16),
                   jax.ShapeDtypeStruct((R, 2 * SWA_KV_W), F32),
                   jax.ShapeDtypeStruct((R, 2 * D_MODEL), BF16)] + c_shapes,
        compiler_params=_cparams(("parallel",)),
        name="inproj_rest",
    )(h, wsq, wkv, wbg, *[w for w, _ in casts])
    return outs[0], outs[1], outs[2], outs[3:]


def _ret_kernel(blk_ref, s0_ref, gn_ref, r_ref, sout_ref, s_scr, *, lc, n_sub):
    c = pl.program_id(1)

    @pl.when(c == 0)
    def _():
        s_scr[...] = s0_ref[0]

    causal = (lax.broadcasted_iota(jnp.int32, (lc, lc), 0) >= lax.broadcasted_iota(jnp.int32, (lc, lc), 1))
    nt = (((1,), (1,)), ((), ()))
    tn = (((0,), (0,)), ((), ()))
    heads = range(RET_HEADS)
    cols = [slice(h * RET_DK, (h + 1) * RET_DK) for h in heads]
    for sub in range(n_sub):
        rows = slice(sub * lc, (sub + 1) * lc)
        part = lambda p, h: blk_ref[rows, p * RET_W + h * RET_DK:p * RET_W + (h + 1) * RET_DK]
        scores = [lax.dot_general(part(0, h), part(1, h), nt, preferred_element_type=F32) for h in heads]
        masked = [jnp.where(causal, s, 0.0).astype(BF16) for s in scores]
        states = [s_scr[h] for h in heads]
        outs = [jnp.dot(masked[h], part(2, h), preferred_element_type=F32)
                + jnp.dot(part(0, h), states[h].astype(BF16), preferred_element_type=F32) for h in heads]
        for h in heads:
            kv = lax.dot_general(part(1, h), part(2, h), tn, preferred_element_type=F32)
            s_scr[h] = math.exp(lc * RET_LOG_GAMMA[h]) * (states[h] + kv)
        for h in heads:
            o = outs[h]
            mu = jnp.mean(o, axis=-1, keepdims=True)
            d = o - mu
            var = jnp.mean(d * d, axis=-1, keepdims=True)
            on = d * lax.rsqrt(var + EPS) * gn_ref[:, cols[h]]
            g = part(3, h).astype(F32)
            r_ref[rows, cols[h]] = (on * (g * _sigmoid(g))).astype(BF16)

    @pl.when(c == pl.num_programs(1) - 1)
    def _():
        sout_ref[0] = s_scr[...]


def _retention(ret_all, s0, gn_g, *, n_seq, seq_len, lc, n_sub):
    R = ret_all.shape[0]
    rows = lc * n_sub
    nc = seq_len // rows
    st_spec = pl.BlockSpec((1, RET_HEADS, RET_DK, RET_DV), lambda b, c: (b, 0, 0, 0))
    return pl.pallas_call(
        functools.partial(_ret_kernel, lc=lc, n_sub=n_sub),
        grid=(n_seq, nc),
        in_specs=[pl.BlockSpec((rows, 4 * RET_W), lambda b, c: (b * nc + c, 0)),
                  st_spec,
                  _resident((1, RET_W))],
        out_specs=[pl.BlockSpec((rows, RET_W), lambda b, c: (b * nc + c, 0)), st_spec],
        out_shape=[jax.ShapeDtypeStruct((R, RET_W), BF16),
                   jax.ShapeDtypeStruct((n_seq, RET_HEADS, RET_DK, RET_DV), F32)],
        scratch_shapes=[pltpu.VMEM((RET_HEADS, RET_DK, RET_DV), F32)],
        compiler_params=_cparams(("parallel", "arbitrary")),
        name="retention",
    )(ret_all, s0, gn_g)


KEYS = WINDOW + CHUNK
KPAD = 256


def _swa_kernel(sink_ref, q_ref, k2_ref, k1_ref, k0_ref, v2_ref, v1_ref, v0_ref, o_ref, *, masked, n_q):
    j = pl.program_id(1)
    kall = jnp.concatenate([k2_ref[...], k1_ref[...], k0_ref[...]], axis=0)
    vall = jnp.concatenate([v2_ref[...], v1_ref[...], v0_ref[...]], axis=0)
    lane = lax.broadcasted_iota(jnp.int32, kall.shape, 1)
    zpad = jnp.zeros((KPAD - KEYS, LANES), BF16)

    def lane_halves(win, h):
        rolled = pltpu.roll(win, SWA_HEAD_DIM, 1)
        lo_src, hi_src = (win, rolled) if h == 0 else (rolled, win)
        return (jnp.where(lane < SWA_HEAD_DIM, lo_src, 0.0).astype(BF16),
                jnp.where(lane >= SWA_HEAD_DIM, hi_src, 0.0).astype(BF16))

    col = lax.broadcasted_iota(jnp.int32, (1, KPAD), 1)
    n_pairs = SWA_GROUP // 2
    rows = n_pairs * CHUNK
    row = lax.broadcasted_iota(jnp.int32, (rows, 1), 0)
    out_lane = lax.broadcasted_iota(jnp.int32, (rows, LANES), 1)
    nt = (((1,), (1,)), ((), ()))
    q_scale = jnp.asarray(SWA_HEAD_DIM ** -0.5, BF16)
    for h in range(SWA_KV_HEADS):
        k_lo, k_hi = lane_halves(kall, h)
        v_lo, v_hi = lane_halves(vall, h)
        base = h * SWA_GROUP * SWA_HEAD_DIM
        sinks = []
        for half in range(2):
            sink = jnp.zeros((rows, 1), F32)
            for p in range(n_pairs):
                sink = jnp.where(row // CHUNK == p, sink_ref[h * SWA_GROUP + 2 * p + half], sink)
            sinks.append(sink)
        for u in range(n_q):
            r0 = u * CHUNK
            c = j * n_q + u
            if masked:
                first_ok = jnp.where(c >= 2, 0, jnp.where(c == 1, CHUNK, 2 * CHUNK))
                ok = (col >= first_ok) & (col < KEYS)
            else:
                ok = col < KEYS
            kk = jnp.concatenate([k_lo[r0:r0 + KEYS], zpad, k_hi[r0:r0 + KEYS], zpad], axis=0)
            vv = jnp.concatenate([v_lo[r0:r0 + KEYS], zpad, v_hi[r0:r0 + KEYS], zpad], axis=0)
            q4 = jnp.concatenate([q_ref[r0:r0 + CHUNK, base + p * LANES: base + (p + 1) * LANES]
                                  for p in range(n_pairs)], axis=0)
            s = lax.dot_general(q4 * q_scale, kk, nt, preferred_element_type=F32)
            ps, invs = [], []
            for half in range(2):
                sh = jnp.where(ok, s[:, half * KPAD:(half + 1) * KPAD], NEG_INF)
                m = jnp.maximum(jnp.max(sh, axis=-1, keepdims=True), sinks[half])
                p_half = jnp.exp(sh - m)
                den = jnp.sum(p_half, axis=-1, keepdims=True) + jnp.exp(sinks[half] - m)
                ps.append(p_half.astype(BF16))
                invs.append(1.0 / den)
            pv = jnp.dot(jnp.concatenate(ps, axis=1), vv, preferred_element_type=F32)
            o = pv * jnp.where(out_lane < SWA_HEAD_DIM, invs[0], invs[1])
            for p in range(n_pairs):
                o_ref[r0:r0 + CHUNK, base + p * LANES: base + (p + 1) * LANES] = (
                    o[p * CHUNK:(p + 1) * CHUNK].astype(BF16))


def _swa(sinks, sq, k_arrs, v_arrs, k_maps, v_maps, *, n_seq, nc, masked, n_q):
    R = sq.shape[0]
    kv_rows = (CHUNK, CHUNK, n_q * CHUNK)
    kv_specs = [pl.BlockSpec((kv_rows[t % 3], SWA_KV_W), m) for t, m in enumerate((*k_maps, *v_maps))]
    return pl.pallas_call(
        functools.partial(_swa_kernel, masked=masked, n_q=n_q),
        grid=(n_seq, nc),
        in_specs=[pl.BlockSpec(memory_space=pltpu.SMEM),
                  pl.BlockSpec((n_q * CHUNK, SWA_Q_W), lambda b, c: (b * nc + c, 0)),
                  *kv_specs],
        out_specs=pl.BlockSpec((n_q * CHUNK, SWA_Q_W), lambda b, c: (b * nc + c, 0)),
        out_shape=jax.ShapeDtypeStruct((R, SWA_Q_W), BF16),
        compiler_params=_cparams(("parallel", "arbitrary")),
        name="swa",
    )(sinks, sq, *k_arrs, *v_arrs)


def _route(logits):
    tm = logits.shape[0]
    lane = lax.broadcasted_iota(jnp.int32, (tm, LANES), 1)
    is_g = lane < N_GROUPS
    gl = jnp.where(is_g, logits, NEG_INF)
    gmax = jnp.max(gl, axis=-1, keepdims=True)
    gidx = jnp.min(jnp.where(gl == gmax, lane, LANES), axis=-1, keepdims=True)
    gsum = jnp.sum(jnp.where(is_g, jnp.exp(gl - gmax), 0.0), axis=-1, keepdims=True)
    g_w = 1.0 / gsum
    base = N_GROUPS + EXPERTS_PER_GROUP * gidx
    el = jnp.where((lane >= base) & (lane < base + EXPERTS_PER_GROUP), logits, NEG_INF)
    v1 = jnp.max(el, axis=-1, keepdims=True)
    i1 = jnp.min(jnp.where(el == v1, lane, LANES), axis=-1, keepdims=True)
    el2 = jnp.where(lane == i1, NEG_INF, el)
    v2 = jnp.max(el2, axis=-1, keepdims=True)
    i2 = jnp.min(jnp.where(el2 == v2, lane, LANES), axis=-1, keepdims=True)
    e2 = jnp.exp(v2 - v1)
    den = 1.0 + e2
    w1 = g_w / den
    w2 = g_w * e2 / den
    l1 = i1 - base
    l2 = i2 - base
    first_lo = l1 < l2
    la = jnp.where(first_lo, l1, l2)
    lb = jnp.where(first_lo, l2, l1)
    wa = jnp.where(first_lo, w1, w2)
    wb = jnp.where(first_lo, w2, w1)
    pair = jnp.where(la == 0, lb - 1, jnp.where(la == 1, jnp.where(lb == 3, 3, 4), 5))
    swapped = la == 2
    w_slot_a = jnp.where(swapped, wb, wa)
    w_slot_b = jnp.where(swapped, wa, wb)
    cls = (gidx * N_PAIRS + pair).astype(F32)
    return jnp.where(lane == 0, w_slot_a, jnp.where(lane == 1, w_slot_b, jnp.where(lane == 2, cls, 0.0)))


ROW_W = D_MODEL + LANES


def _merge_kernel(x_ref, r_ref, o_ref, gate_ref, mod_ref, modp_ref, g2_ref, wrb_ref, wsb_ref, wout_ref, wr_ref,
                  br_ref, x1_ref, text_ref, meta_ref, cnt_ref, x1_scr, *, n_seq):
    i = pl.program_id(0)
    n = pl.num_programs(0) - 1
    tm = x_ref.shape[0]

    def matmul_stage():
        g_r = gate_ref[:, :D_MODEL].astype(F32)
        g_s = gate_ref[:, D_MODEL:].astype(F32)
        merged = (g_r * jnp.dot(r_ref[...], wrb_ref[...], preferred_element_type=F32)
                  + g_s * jnp.dot(o_ref[...], wsb_ref[...], preferred_element_type=F32))
        mix = jnp.dot(merged.astype(BF16), wout_ref[...], preferred_element_type=F32)
        gt1 = mod_ref[:, 2, :]
        x1 = (x_ref[...].reshape(n_seq, tm // n_seq, D_MODEL) + gt1[:, None, :]
              * mix.reshape(n_seq, tm // n_seq, D_MODEL)).reshape(tm, D_MODEL)
        x1_ref[...] = x1
        x1_scr[...] = x1

    def vector_stage():
        t = _modulated_norm(x1_scr[...], g2_ref[...], modp_ref[:, 3, :], modp_ref[:, 4, :], n_seq)
        logits = jnp.dot(t.astype(BF16), wr_ref[...], preferred_element_type=F32) + br_ref[...]
        meta = _route(logits)
        text_ref[:, :D_MODEL] = t
        text_ref[:, D_MODEL:] = meta
        meta_ref[...] = meta
        lane = lax.broadcasted_iota(jnp.int32, (tm, LANES), 1).astype(F32)
        cnt_ref[...] += jnp.sum(jnp.where(meta[:, 2:3] == lane, 1.0, 0.0), axis=0, keepdims=True)

    @pl.when(i == 0)
    def _():
        cnt_ref[...] = jnp.zeros_like(cnt_ref)
        matmul_stage()

    @pl.when((i > 0) & (i < n))
    def _():
        vector_stage()
        matmul_stage()

    @pl.when(i == n)
    def _():
        vector_stage()


def _merge(x2d, r, o_swa, gates, mod3, g2, wrb, wsb, wout, wr, br, *, seq_len, tm):
    R = x2d.shape[0]
    n = R // tm
    if seq_len >= tm:
        n_seq, tps = 1, seq_len // tm
        seq_of = lambda t: t // tps
    else:
        n_seq = tm // seq_len
        seq_of = lambda t: t
    cur = lambda i: jnp.minimum(i, n - 1)
    prev = lambda i: jnp.maximum(i - 1, 0)
    row = lambda w: pl.BlockSpec((tm, w), lambda i: (cur(i), 0))
    row_prev = lambda w: pl.BlockSpec((tm, w), lambda i: (prev(i), 0))
    return pl.pallas_call(
        functools.partial(_merge_kernel, n_seq=n_seq),
        grid=(n + 1,),
        in_specs=[row(D_MODEL), row(RET_W), row(SWA_Q_W), row(2 * D_MODEL),
                  pl.BlockSpec((n_seq, N_MOD, D_MODEL), lambda i: (seq_of(cur(i)), 0, 0)),
                  pl.BlockSpec((n_seq, N_MOD, D_MODEL), lambda i: (seq_of(prev(i)), 0, 0)),
                  _resident((1, D_MODEL)),
                  _resident((RET_W, D_MODEL)), _resident((SWA_Q_W, D_MODEL)), _resident((D_MODEL, D_MODEL)),
                  _resident((D_MODEL, LANES)), _resident((1, LANES))],
        out_specs=[row(D_MODEL), row_prev(ROW_W), row_prev(LANES), pl.BlockSpec((8, LANES), lambda i: (0, 0))],
        out_shape=[jax.ShapeDtypeStruct((R, D_MODEL), F32),
                   jax.ShapeDtypeStruct((R, ROW_W), F32),
                   jax.ShapeDtypeStruct((R, LANES), F32),
                   jax.ShapeDtypeStruct((8, LANES), F32)],
        scratch_shapes=[pltpu.VMEM((tm, D_MODEL), F32)],
        compiler_params=_cparams(("arbitrary",)),
        name="merge",
    )(x2d, r, o_swa, gates, mod3, mod3, g2, wrb, wsb, wout, wr, br)


PLAN_BLK = 2048
TILE_ROWS = 256


def _plan_kernel(cnt_a_ref, cnt_b_ref, meta_a_ref, meta_b_ref, pos_ref, tile_ref, pad_ref, offs_scr, carry_scr,
                 tri_scr, *, tm, nb_a):
    b = pl.program_id(0)
    blk = meta_a_ref.shape[0]
    lane = lax.broadcasted_iota(jnp.int32, (blk, LANES), 1)
    cls_col = jnp.where(b < nb_a, meta_a_ref[:, 2:3], meta_b_ref[:, 2:3])
    oh = jnp.where(cls_col == lane.astype(F32), 1.0, 0.0)

    @pl.when(b == 0)
    def _():
        ri = lax.broadcasted_iota(jnp.int32, (blk, blk), 0)
        ci = lax.broadcasted_iota(jnp.int32, (blk, blk), 1)
        tri_scr[...] = jnp.where(ci <= ri, 1.0, 0.0).astype(BF16)
        cnt = cnt_a_ref[...] + cnt_b_ref[...]
        ptiles = jnp.floor((cnt + (tm - 1)) * (1.0 / tm))
        ri = lax.broadcasted_iota(jnp.int32, (LANES, LANES), 0)
        ci = lax.broadcasted_iota(jnp.int32, (LANES, LANES), 1)
        before = jnp.where(ri < ci, 1.0, 0.0).astype(BF16)
        offs = jnp.dot(ptiles.astype(BF16), before, preferred_element_type=F32) * tm
        offs_scr[...] = offs
        carry_scr[...] = jnp.zeros_like(carry_scr)
        padded = ptiles * tm
        ends = offs + padded
        tl = lax.broadcasted_iota(jnp.int32, (TILE_ROWS, LANES), 1)
        tstart = lax.broadcasted_iota(jnp.int32, (TILE_ROWS, LANES), 0).astype(F32) * tm
        tcls = jnp.sum(jnp.where((ends[0:1, :] <= tstart) & (tl < N_CLASSES), 1.0, 0.0), axis=1, keepdims=True)
        tcls = jnp.minimum(tcls, N_CLASSES - 1.0)
        total = jnp.max(ends[0:1, :], axis=1, keepdims=True)
        grp = (jnp.where(tcls >= N_PAIRS, 1.0, 0.0) + jnp.where(tcls >= 2 * N_PAIRS, 1.0, 0.0)
               + jnp.where(tcls >= 3 * N_PAIRS, 1.0, 0.0))
        pair = tcls - N_PAIRS * grp
        la = jnp.where(pair < 3, 0.0, jnp.where(pair < 5, 1.0, 3.0))
        lb = jnp.where(pair == 0, 1.0, jnp.where((pair == 1) | (pair >= 4), 2.0, 3.0))
        ea = EXPERTS_PER_GROUP * grp + la
        eb = EXPERTS_PER_GROUP * grp + lb
        n_used = total * (1.0 / tm)
        tile_ref[...] = jnp.where(tl == 0, ea, jnp.where(tl == 1, eb, jnp.where(tl == 2, n_used, 0.0))
                                  ).astype(jnp.int32)
        npad = padded - cnt
        pstart = jnp.dot(npad.astype(BF16), before, preferred_element_type=F32)
        n_class_pad = jnp.sum(npad[0:1, :], axis=1, keepdims=True)
        rows = pad_ref.shape[0]
        v = (lax.broadcasted_iota(jnp.int32, (rows, LANES), 0) * LANES
             + lax.broadcasted_iota(jnp.int32, (rows, LANES), 1)).astype(F32)
        slot = jnp.where(v >= n_class_pad, total - n_class_pad + v, 0.0)
        for c in range(N_CLASSES):
            ps = pstart[0:1, c:c + 1]
            inside = (v >= ps) & (v < ps + npad[0:1, c:c + 1])
            slot = jnp.where(inside, offs[0:1, c:c + 1] + cnt[0:1, c:c + 1] - ps + v, slot)
        pad_ref[...] = slot.astype(jnp.int32)

    incl = jnp.dot(tri_scr[...], oh.astype(BF16), preferred_element_type=F32)
    base = offs_scr[0:1, :] + carry_scr[0:1, :]
    pos = jnp.sum(oh * (base + incl - oh), axis=1, keepdims=True)
    pos_ref[...] = jnp.broadcast_to(pos, (blk, LANES)).astype(jnp.int32)
    carry_scr[...] += jnp.sum(oh, axis=0, keepdims=True)


def _plan(cnt_a, cnt_b, meta_a, meta_b, *, tm):
    na, nb = meta_a.shape[0], meta_b.shape[0]
    n = na + nb
    n_tiles = n // tm + N_CLASSES
    n_free = N_CLASSES * tm
    assert n_tiles <= TILE_ROWS and na % PLAN_BLK == 0 and nb % PLAN_BLK == 0 and n_free % LANES == 0
    n_slots = n_tiles * tm
    nb_a = na // PLAN_BLK
    nb_b = nb // PLAN_BLK
    pos2d, tile2d, pad2d = pl.pallas_call(
        functools.partial(_plan_kernel, tm=tm, nb_a=nb_a),
        grid=(nb_a + nb_b,),
        in_specs=[_resident((8, LANES)), _resident((8, LANES)),
                  pl.BlockSpec((PLAN_BLK, LANES), lambda b: (jnp.minimum(b, nb_a - 1), 0)),
                  pl.BlockSpec((PLAN_BLK, LANES), lambda b: (jnp.maximum(b - nb_a, 0), 0))],
        out_specs=[pl.BlockSpec((PLAN_BLK, LANES), lambda b: (b, 0)),
                   pl.BlockSpec((TILE_ROWS, LANES), lambda b: (0, 0)),
                   pl.BlockSpec((n_free // LANES, LANES), lambda b: (0, 0))],
        out_shape=[jax.ShapeDtypeStruct((n, LANES), jnp.int32),
                   jax.ShapeDtypeStruct((TILE_ROWS, LANES), jnp.int32),
                   jax.ShapeDtypeStruct((n_free // LANES, LANES), jnp.int32)],
        scratch_shapes=[pltpu.VMEM((8, LANES), F32), pltpu.VMEM((8, LANES), F32),
                        pltpu.VMEM((PLAN_BLK, PLAN_BLK), BF16)],
        compiler_params=_cparams(("arbitrary",)),
        name="plan",
    )(cnt_a, cnt_b, meta_a, meta_b)
    return pos2d[:, 0], pad2d.reshape(-1), tile2d[:n_tiles, 0], tile2d[:n_tiles, 1], tile2d[0:1, 2], n_slots


N_STAGE = 3


def _scatter_kernel(pos_ref, text_a_ref, text_b_ref, out_ref, stage, zero_scr, sem_in, sem_out, *, nb_a, nb_b):
    i = pl.program_id(0)
    n = pl.num_programs(0)
    n_tok = nb_a + nb_b
    tb = pos_ref.shape[2]
    slot = i % N_STAGE

    def load(step, s):
        def from_a():
            return pltpu.make_async_copy(text_a_ref.at[pl.ds(step * tb, tb), :], stage.at[s], sem_in.at[s])

        def from_b():
            return pltpu.make_async_copy(text_b_ref.at[pl.ds((step - nb_a) * tb, tb), :], stage.at[s], sem_in.at[s])

        return from_a, from_b

    def start_load(step, s):
        from_a, from_b = load(step, s)

        @pl.when(step < nb_a)
        def _():
            from_a().start()

        @pl.when((step >= nb_a) & (step < n_tok))
        def _():
            from_b().start()

    def wait_rows(s):
        pltpu.make_async_copy(stage.at[s], out_ref.at[pl.ds(0, tb), :], sem_out.at[s]).wait()

    def scatter_rows(src_row):
        def body(r, carry):
            dst = pos_ref[0, 0, r]
            pltpu.make_async_copy(src_row(r), out_ref.at[pl.ds(dst, 1), :], sem_out.at[slot]).start()
            return carry

        lax.fori_loop(0, tb, body, 0, unroll=8)

    @pl.when(i == 0)
    def _():
        zero_scr[...] = jnp.zeros_like(zero_scr)
        start_load(i, slot)

    @pl.when(i >= 2)
    def _():
        wait_rows((i + 1) % N_STAGE)

    start_load(i + 1, (i + 1) % N_STAGE)

    @pl.when(i < n_tok)
    def _():
        pltpu.make_async_copy(text_a_ref.at[pl.ds(0, tb), :], stage.at[slot], sem_in.at[slot]).wait()
        scatter_rows(lambda r: stage.at[slot, pl.ds(r, 1), :])

    @pl.when(i >= n_tok)
    def _():
        scatter_rows(lambda r: zero_scr.at[pl.ds(0, 1), :])

    @pl.when(i == n - 1)
    def _():
        wait_rows((i + 2) % N_STAGE)
        wait_rows(slot)


def _scatter_rows(text_a, text_b, pos_ext, *, tb):
    nb_a = text_a.shape[0] // tb
    nb_b = text_b.shape[0] // tb
    n_steps = pos_ext.shape[0] // tb
    assert n_steps >= 2 and nb_a >= 1 and nb_b >= 1
    return pl.pallas_call(
        functools.partial(_scatter_kernel, nb_a=nb_a, nb_b=nb_b),
        grid=(n_steps,),
        in_specs=[pl.BlockSpec((1, 1, tb), lambda i: (i, 0, 0), memory_space=pltpu.SMEM),
                  pl.BlockSpec(memory_space=pl.ANY),
                  pl.BlockSpec(memory_space=pl.ANY)],
        out_specs=pl.BlockSpec(memory_space=pl.ANY),
        out_shape=jax.ShapeDtypeStruct((pos_ext.shape[0], ROW_W), F32),
        scratch_shapes=[pltpu.VMEM((N_STAGE, tb, ROW_W), F32), pltpu.VMEM((8, ROW_W), F32),
                        pltpu.SemaphoreType.DMA((N_STAGE,)), pltpu.SemaphoreType.DMA((N_STAGE,))],
        compiler_params=_cparams(("arbitrary",)),
        name="scatter_rows",
    )(pos_ext.reshape(n_steps, 1, tb), text_a, text_b)


def _moe_kernel(ea_ref, eb_ref, nused_ref, xs_ref, w1a_ref, w3a_ref, w2a_ref, w1b_ref, w3b_ref, w2b_ref, y_ref):
    i = pl.program_id(0)

    @pl.when(i >= nused_ref[0])
    def _():
        y_ref[...] = jnp.zeros_like(y_ref)

    @pl.when(i < nused_ref[0])
    def _():
        x = xs_ref[:, :D_MODEL].astype(BF16)
        wa = xs_ref[:, D_MODEL:D_MODEL + 1]
        wb = xs_ref[:, D_MODEL + 1:D_MODEL + 2]

        def hidden(w1_ref, w3_ref, gate):
            a = jnp.dot(x, w1_ref[0], preferred_element_type=F32)
            b = jnp.dot(x, w3_ref[0], preferred_element_type=F32)
            return (a * _sigmoid(a) * b * gate).astype(BF16)

        ha = hidden(w1a_ref, w3a_ref, wa)
        hb = hidden(w1b_ref, w3b_ref, wb)
        y_ref[...] = (jnp.dot(ha, w2a_ref[0], preferred_element_type=F32)
                      + jnp.dot(hb, w2b_ref[0], preferred_element_type=F32))


def _moe(tile_ea, tile_eb, n_used, xs, w1, w3, w2, *, tm, n_tiles):
    last = lambda i, nu: jnp.minimum(i, nu[0] - 1)
    wa_map = lambda i, ea, eb, nu: (ea[last(i, nu)], 0, 0)
    wb_map = lambda i, ea, eb, nu: (eb[last(i, nu)], 0, 0)
    row_map = lambda i, ea, eb, nu: (last(i, nu), 0)
    up = (1, D_MODEL, D_EXPERT)
    down = (1, D_EXPERT, D_MODEL)
    grid_spec = pltpu.PrefetchScalarGridSpec(
        num_scalar_prefetch=3,
        grid=(n_tiles,),
        in_specs=[pl.BlockSpec((tm, D_MODEL + LANES), row_map),
                  pl.BlockSpec(up, wa_map), pl.BlockSpec(up, wa_map), pl.BlockSpec(down, wa_map),
                  pl.BlockSpec(up, wb_map), pl.BlockSpec(up, wb_map), pl.BlockSpec(down, wb_map)],
        out_specs=pl.BlockSpec((tm, D_MODEL), lambda i, ea, eb, nu: (i, 0)),
    )
    return pl.pallas_call(
        _moe_kernel,
        grid_spec=grid_spec,
        out_shape=jax.ShapeDtypeStruct((n_tiles * tm, D_MODEL), F32),
        compiler_params=_cparams(("arbitrary",)),
        name="moe",
    )(tile_ea, tile_eb, n_used, xs, w1, w3, w2, w1, w3, w2)


def _final_kernel(pos_ref, pos_next_ref, x1_ref, mod_ref, g_ref, ys_ref, o_ref, ybuf, sem, *, n_seq):
    i = pl.program_id(0)
    n = pl.num_programs(0)
    tm = x1_ref.shape[0]
    slot = i % 2

    def start_gather(idx_ref, s):
        def body(r, carry):
            src = idx_ref[0, 0, r]
            pltpu.make_async_copy(ys_ref.at[pl.ds(src, 1), :], ybuf.at[s, pl.ds(r, 1), :], sem.at[s]).start()
            return carry

        lax.fori_loop(0, tm, body, 0, unroll=8)

    @pl.when(i == 0)
    def _():
        start_gather(pos_ref, 0)

    @pl.when(i + 1 < n)
    def _():
        start_gather(pos_next_ref, 1 - slot)

    pltpu.make_async_copy(ys_ref.at[pl.ds(0, tm), :], ybuf.at[slot], sem.at[slot]).wait()
    gt2 = mod_ref[:, 5, :]
    x2 = (x1_ref[...].reshape(n_seq, tm // n_seq, D_MODEL)
          + gt2[:, None, :] * ybuf[slot].reshape(n_seq, tm // n_seq, D_MODEL)).reshape(tm, D_MODEL)
    ms = jnp.mean(x2 * x2, axis=-1, keepdims=True)
    o_ref[...] = x2 * lax.rsqrt(ms + EPS) * g_ref[...]


def _final(x1, y_sorted, pos, mod3, gf, *, seq_len, tm):
    R = x1.shape[0]
    if seq_len >= tm:
        n_seq, tps = 1, seq_len // tm
        mod_map = lambda i: (i // tps, 0, 0)
    else:
        n_seq = tm // seq_len
        mod_map = lambda i: (i, 0, 0)
    n = R // tm
    row = pl.BlockSpec((tm, D_MODEL), lambda i: (i, 0))
    pos3 = pos.reshape(n, 1, tm)
    return pl.pallas_call(
        functools.partial(_final_kernel, n_seq=n_seq),
        grid=(n,),
        in_specs=[pl.BlockSpec((1, 1, tm), lambda i: (i, 0, 0), memory_space=pltpu.SMEM),
                  pl.BlockSpec((1, 1, tm), lambda i: (jnp.minimum(i + 1, n - 1), 0, 0), memory_space=pltpu.SMEM),
                  row, pl.BlockSpec((n_seq, N_MOD, D_MODEL), mod_map), _resident((1, D_MODEL)),
                  pl.BlockSpec(memory_space=pl.ANY)],
        out_specs=row,
        out_shape=jax.ShapeDtypeStruct((R, D_MODEL), F32),
        scratch_shapes=[pltpu.VMEM((2, tm, D_MODEL), F32), pltpu.SemaphoreType.DMA((2,))],
        compiler_params=_cparams(("arbitrary",)),
        name="final",
    )(pos3, pos3, x1, mod3, gf, y_sorted)


def _rope_tables(pos):
    half = RET_DK // 2
    inv = ROPE_BASE ** (-jnp.arange(half, dtype=F32) / half)
    ang = pos.astype(F32)[:, None] * inv[None, :]
    cos = jnp.cos(ang)
    sin = jnp.sin(ang)
    return jnp.concatenate([cos, cos], axis=-1), jnp.concatenate([-sin, sin], axis=-1)


def kernel(x_prompt, x_sample, cache_ret_state, cache_swa_k, cache_swa_v, c_prompt, c_sample,
           norm1_g, norm2_g, ada_w, ada_b, w_in, ret_gn_g, swa_sinks, w_ret_branch, w_swa_branch, w_out,
           router_group_w, router_group_b, router_expert_w, router_expert_b,
           expert_w1, expert_w3, expert_w2, final_norm_g):
    depth = w_in.shape[0]
    assert depth == 1
    bp, tp, _ = x_prompt.shape
    bs, ts, _ = x_sample.shape
    past = WINDOW
    assert cache_swa_k.shape[2] == past and ts == CHUNK and tp % 512 == 0
    tm = 512

    l = 0
    c1 = 4 * RET_W
    c2 = c1 + SWA_Q_W
    c3 = c2 + 2 * SWA_KV_W
    wret = w_in[l, :, :c1].astype(BF16)
    n_r = N_GROUPS + N_EXPERTS
    wr = jnp.zeros((D_MODEL, LANES), F32).at[:, :N_GROUPS].set(router_group_w[l]).at[:, N_GROUPS:n_r].set(
        router_expert_w[l]).astype(BF16)
    br = jnp.zeros((1, LANES), F32).at[0, :N_GROUPS].set(router_group_b[l]).at[0, N_GROUPS:n_r].set(
        router_expert_b[l])
    g1 = norm1_g[l].reshape(1, D_MODEL)
    g2 = norm2_g[l].reshape(1, D_MODEL)
    gn = ret_gn_g[l].reshape(1, RET_W)
    gf = final_norm_g.reshape(1, D_MODEL)
    sinks = swa_sinks[l]

    c_all = jnp.concatenate([c_prompt, c_sample], axis=0)
    mod = _ada(c_all, ada_w[l], ada_b[l]).reshape(bp + bs, N_MOD, D_MODEL)
    mod_p, mod_s = mod[:bp], mod[bp:]

    cos_p, sin_p = _rope_tables(jnp.arange(tp))
    cos_s, sin_s = _rope_tables(PAST_LEN + jnp.arange(ts))
    rep = tm // ts
    cos_s, sin_s = jnp.tile(cos_s, (rep, 1)), jnp.tile(sin_s, (rep, 1))

    xp = x_prompt.reshape(bp * tp, D_MODEL)
    xs = x_sample.reshape(bs * ts, D_MODEL)

    lc_p = 128
    whole = lambda w: (w.reshape(-1, w.shape[-1]), [(0, w.shape[-1])])
    ret_p, h_p, (wsq, wkv, wbg, wrb, wsb, wo) = _inproj_ret(
        xp, mod_p, g1, cos_p, sin_p, wret, seq_len=tp, tm=tm, lc=lc_p,
        casts=[(w_in[l], [(c1, c2), (c2, c3), (c3, w_in.shape[2])]),
               whole(w_ret_branch[l]), whole(w_swa_branch[l]), whole(w_out[l])])
    sq_p, kv_p, gate_p, (w1, w3, w2) = _inproj_rest(
        h_p, wsq, wkv, wbg, tm=tm, casts=[whole(expert_w1[l]), whole(expert_w3[l]), whole(expert_w2[l])])
    w1, w3, w2 = (w.reshape(e.shape[1:]) for w, e in zip((w1, w3, w2), (expert_w1, expert_w3, expert_w2)))
    ret_s, h_s, _ = _inproj_ret(xs, mod_s, g1, cos_s, sin_s, wret, seq_len=ts, tm=tm, lc=ts)
    sq_s, kv_s, gate_s, _ = _inproj_rest(h_s, wsq, wkv, wbg, tm=tm)

    s0_p = jnp.zeros((bp, RET_HEADS, RET_DK, RET_DV), F32)
    r_p, state_p = _retention(ret_p, s0_p, gn, n_seq=bp, seq_len=tp, lc=lc_p, n_sub=2)
    r_s, state_s = _retention(ret_s, cache_ret_state[l].astype(F32), gn, n_seq=bs, seq_len=ts, lc=ts, n_sub=1)

    n_q = 4
    nc_p = tp // CHUNK
    ns_p = nc_p // n_q
    prev_map = lambda back, colblk: (lambda b, s: (b * nc_p + jnp.maximum(s * n_q - back, 0), colblk))
    own_map = lambda colblk: (lambda b, s: (b * ns_p + s, colblk))
    o_p = _swa(sinks, sq_p, [kv_p] * 3, [kv_p] * 3,
               [prev_map(2, 0), prev_map(1, 0), own_map(0)], [prev_map(2, 1), prev_map(1, 1), own_map(1)],
               n_seq=bp, nc=ns_p, masked=True, n_q=n_q)
    ck = cache_swa_k[l].reshape(bs * past, SWA_KV_W)
    cv = cache_swa_v[l].reshape(bs * past, SWA_KV_W)
    cmap = lambda blk: (lambda b, c: (2 * b + blk, 0))
    o_s = _swa(sinks, sq_s, [ck, ck, kv_s], [cv, cv, kv_s],
               [cmap(0), cmap(1), lambda b, c: (b, 0)], [cmap(0), cmap(1), lambda b, c: (b, 1)],
               n_seq=bs, nc=1, masked=False, n_q=1)

    tm_m = 256
    n_p = bp * tp
    x1_p, text_p, meta_p, cnt_p = _merge(xp, r_p, o_p, gate_p, mod_p, g2, wrb, wsb, wo, wr, br, seq_len=tp, tm=tm_m)
    x1_s, text_s, meta_s, cnt_s = _merge(xs, r_s, o_s, gate_s, mod_s, g2, wrb, wsb, wo, wr, br, seq_len=ts, tm=tm_m)

    tm_e = 256
    pos, free_slots, tile_ea, tile_eb, n_used, n_slots = _plan(cnt_p, cnt_s, meta_p, meta_s, tm=tm_e)
    pos_ext = jnp.concatenate([pos, free_slots], axis=0)
    xsorted = _scatter_rows(text_p, text_s, pos_ext, tb=512)
    y_sorted = _moe(tile_ea, tile_eb, n_used, xsorted, w1, w3, w2, tm=tm_e, n_tiles=n_slots // tm_e)

    tm_f = 512
    out_p = _final(x1_p, y_sorted, pos[:n_p], mod_p, gf, seq_len=tp, tm=tm_f)
    out_s = _final(x1_s, y_sorted, pos[n_p:], mod_s, gf, seq_len=ts, tm=tm_f)

    y_prompt = out_p.reshape(bp, tp, D_MODEL)
    y_sample = out_s.reshape(bs, ts, D_MODEL)
    kvp = kv_p.reshape(bp, tp, 2 * SWA_KV_W)[:, tp - WINDOW:].reshape(bp, WINDOW, 2, SWA_KV_HEADS, SWA_HEAD_DIM)
    kvs = kv_s.reshape(bs, ts, 2, SWA_KV_HEADS, SWA_HEAD_DIM)
    k_s = jnp.concatenate([cache_swa_k[l].astype(F32), kvs[:, :, 0]], axis=1)[:, -WINDOW:]
    v_s = jnp.concatenate([cache_swa_v[l].astype(F32), kvs[:, :, 1]], axis=1)[:, -WINDOW:]
    return (y_prompt, y_sample, state_p[None], kvp[:, :, 0][None], kvp[:, :, 1][None],
            state_s[None], k_s[None], v_s[None])
```

```python
import functools
import math

import jax
import jax.numpy as jnp
from jax import lax
from jax.experimental import pallas as pl
from jax.experimental.pallas import tpu as pltpu

F32 = jnp.float32
BF16 = jnp.bfloat16

D_MODEL = 2048
CHUNK = 64
RET_HEADS = 8
RET_DK = 128
RET_DV = 128
RET_W = RET_HEADS * RET_DK
ROPE_BASE = 10000.0
SWA_Q_HEADS = 16
SWA_KV_HEADS = 2
SWA_GROUP = SWA_Q_HEADS // SWA_KV_HEADS
SWA_HEAD_DIM = 64
SWA_Q_W = SWA_Q_HEADS * SWA_HEAD_DIM
SWA_KV_W = SWA_KV_HEADS * SWA_HEAD_DIM
WINDOW = 128
PAST_LEN = 1024
N_GROUPS = 4
EXPERTS_PER_GROUP = 4
N_EXPERTS = 16
D_EXPERT = 512
N_MOD = 6
EPS = 1e-6
NEG_INF = -1e30
N_PAIRS = 6
N_CLASSES = N_GROUPS * N_PAIRS

LANES = 128
VMEM_LIMIT = 56 * 1024 * 1024

RET_LOG_GAMMA = tuple(math.log1p(-(2.0 ** (-5.0 - h))) for h in range(RET_HEADS))


def _cparams(sem):
    return pltpu.CompilerParams(dimension_semantics=sem, vmem_limit_bytes=VMEM_LIMIT)


def _sigmoid(x):
    return 0.5 * jnp.tanh(0.5 * x) + 0.5


def _resident(shape):
    nd = len(shape)
    return pl.BlockSpec(shape, lambda *_: (0,) * nd, pipeline_mode=pl.Buffered(1))


def _ada_kernel(c_ref, w_ref, b_ref, o_ref):
    c = c_ref[...]
    a = c * jax.nn.sigmoid(c)
    w = w_ref[...]

    def split(v):
        hi = v.astype(BF16)
        return hi, (v - hi.astype(F32)).astype(BF16)

    a_hi, a_lo = split(a)
    w_hi, w_lo = split(w)
    dot = functools.partial(jnp.dot, preferred_element_type=F32)
    o_ref[...] = dot(a_hi, w_hi) + (dot(a_lo, w_hi) + dot(a_hi, w_lo)) + b_ref[...]


def _ada(c_all, ada_w, ada_b):
    nb = c_all.shape[0]
    n_out = ada_w.shape[1]
    tn = 1024
    return pl.pallas_call(
        _ada_kernel,
        grid=(n_out // tn,),
        in_specs=[pl.BlockSpec((nb, D_MODEL), lambda j: (0, 0)),
                  pl.BlockSpec((D_MODEL, tn), lambda j: (0, j)),
                  pl.BlockSpec((1, tn), lambda j: (0, j))],
        out_specs=pl.BlockSpec((nb, tn), lambda j: (0, j)),
        out_shape=jax.ShapeDtypeStruct((nb, n_out), F32),
        compiler_params=_cparams(("arbitrary",)),
        name="ada",
    )(c_all, ada_w, ada_b.reshape(1, n_out))


def _modulated_norm(x, g, shift, scale, n_seq):
    tm = x.shape[0]
    ms = jnp.mean(x * x, axis=-1, keepdims=True)
    y3 = (x * lax.rsqrt(ms + EPS)).reshape(n_seq, tm // n_seq, D_MODEL)
    gain = g * (1.0 + scale)
    h = y3 * gain[:, None, :] + shift[:, None, :]
    return h.reshape(tm, D_MODEL)


COL_BLK = 1024


def _cast_plan(casts, n_steps):
    in_specs, out_specs, out_shapes, splits = [], [], [], []
    for w, ranges in casts:
        slab = w.shape[0] // n_steps
        assert slab * n_steps == w.shape[0] and slab % 16 == 0
        in_specs.append(pl.BlockSpec((slab, w.shape[1]), lambda i: (i, 0)))
        for c0, c1 in ranges:
            out_specs.append(pl.BlockSpec((slab, c1 - c0), lambda i: (i, 0)))
            out_shapes.append(jax.ShapeDtypeStruct((w.shape[0], c1 - c0), BF16))
        splits.append(tuple(ranges))
    return in_specs, out_specs, out_shapes, tuple(splits)


def _cast_slabs(in_refs, out_refs, splits):
    k = 0
    for src, ranges in zip(in_refs, splits):
        for c0, c1 in ranges:
            out_refs[k][...] = src[:, c0:c1].astype(BF16)
            k += 1


def _inproj_ret_kernel(x_ref, mod_ref, g1_ref, cos_ref, sin_ref, wret_ref, *rest, n_seq, lc, splits):
    n_in = len(splits)
    cast_in, (ret_ref, h_ref) = rest[:n_in], rest[n_in:n_in + 2]
    cast_out, (dq_scr, dk_scr) = rest[n_in + 2:-2], rest[-2:]
    _cast_slabs(cast_in, cast_out, splits)
    tm = x_ref.shape[0]

    @pl.when(pl.program_id(0) == 0)
    def _():
        e = ((lax.broadcasted_iota(jnp.int32, (tm, RET_DK), 0) % lc) + 1).astype(F32)
        for hh in range(RET_HEADS):
            dq_scr[hh] = jnp.exp(e * RET_LOG_GAMMA[hh])
            dk_scr[hh] = jnp.exp(-e * RET_LOG_GAMMA[hh]) * (RET_DK ** -0.5)

    h = _modulated_norm(x_ref[...], g1_ref[...], mod_ref[:, 0, :], mod_ref[:, 1, :], n_seq)
    h_ref[...] = h.astype(BF16)
    cos = cos_ref[...]
    sin = sin_ref[...]
    assert COL_BLK == RET_W
    for blk in range(4):
        c0 = blk * COL_BLK
        acc = jnp.dot(h_ref[...], wret_ref[:, c0:c0 + COL_BLK], preferred_element_type=F32)
        if blk >= 2:
            ret_ref[:, c0:c0 + COL_BLK] = acc.astype(BF16)
            continue
        dec_scr = dq_scr if blk == 0 else dk_scr
        for hh in range(RET_HEADS):
            a = acc[:, hh * RET_DK:(hh + 1) * RET_DK]
            r = a * cos + pltpu.roll(a, RET_DK // 2, 1) * sin
            ret_ref[:, c0 + hh * RET_DK:c0 + (hh + 1) * RET_DK] = (r * dec_scr[hh]).astype(BF16)


def _inproj_rest_kernel(h_ref, wsq_ref, wkv_ref, wbg_ref, *rest, splits):
    n_in = len(splits)
    cast_in, (sq_ref, kv_ref, gate_ref), cast_out = rest[:n_in], rest[n_in:n_in + 3], rest[n_in + 3:]
    _cast_slabs(cast_in, cast_out, splits)
    sq_ref[...] = jnp.dot(h_ref[...], wsq_ref[...], preferred_element_type=F32).astype(BF16)
    kv_ref[...] = jnp.dot(h_ref[...], wkv_ref[...], preferred_element_type=F32)
    for blk in range(2 * D_MODEL // COL_BLK):
        c0 = blk * COL_BLK
        acc = jnp.dot(h_ref[...], wbg_ref[:, c0:c0 + COL_BLK], preferred_element_type=F32)
        gate_ref[:, c0:c0 + COL_BLK] = _sigmoid(acc).astype(BF16)


def _inproj_ret(x2d, mod3, g1, cos_t, sin_t, wret, *, seq_len, tm, lc, casts=()):
    R = x2d.shape[0]
    if seq_len >= tm:
        n_seq, tps = 1, seq_len // tm
        mod_map = lambda i: (i // tps, 0, 0)
        tab_map = lambda i: (i % tps, 0)
    else:
        n_seq = tm // seq_len
        mod_map = lambda i: (i, 0, 0)
        tab_map = lambda i: (0, 0)
    row = lambda w: pl.BlockSpec((tm, w), lambda i: (i, 0))
    assert tm % lc == 0
    n_steps = R // tm
    c_in, c_out, c_shapes, splits = _cast_plan(casts, n_steps)
    outs = pl.pallas_call(
        functools.partial(_inproj_ret_kernel, n_seq=n_seq, lc=lc, splits=splits),
        grid=(n_steps,),
        in_specs=[row(D_MODEL),
                  pl.BlockSpec((n_seq, N_MOD, D_MODEL), mod_map),
                  _resident((1, D_MODEL)),
                  pl.BlockSpec((tm, LANES), tab_map),
                  pl.BlockSpec((tm, LANES), tab_map),
                  _resident((D_MODEL, 4 * RET_W))] + c_in,
        out_specs=[row(4 * RET_W), row(D_MODEL)] + c_out,
        out_shape=[jax.ShapeDtypeStruct((R, 4 * RET_W), BF16),
                   jax.ShapeDtypeStruct((R, D_MODEL), BF16)] + c_shapes,
        scratch_shapes=[pltpu.VMEM((RET_HEADS, tm, RET_DK), F32), pltpu.VMEM((RET_HEADS, tm, RET_DK), F32)],
        compiler_params=_cparams(("arbitrary",)),
        name="inproj_ret",
    )(x2d, mod3, g1, cos_t, sin_t, wret, *[w for w, _ in casts])
    return outs[0], outs[1], outs[2:]


def _inproj_rest(h, wsq, wkv, wbg, *, tm, casts=()):
    R = h.shape[0]
    n_steps = R // tm
    row = lambda w: pl.BlockSpec((tm, w), lambda i: (i, 0))
    c_in, c_out, c_shapes, splits = _cast_plan(casts, n_steps)
    outs = pl.pallas_call(
        functools.partial(_inproj_rest_kernel, splits=splits),
        grid=(n_steps,),
        in_specs=[row(D_MODEL),
                  _resident((D_MODEL, SWA_Q_W)),
                  _resident((D_MODEL, 2 * SWA_KV_W)),
                  _resident((D_MODEL, 2 * D_MODEL))] + c_in,
        out_specs=[row(SWA_Q_W), row(2 * SWA_KV_W), row(2 * D_MODEL)] + c_out,
        out_shape=[jax.ShapeDtypeStruct((R, SWA_Q_W), BF16),
                   jax.ShapeDtypeStruct((R, 2 * SWA_KV_W), F32),
                   jax.ShapeDtypeStruct((R, 2 * D_MODEL), BF16)] + c_shapes,
        compiler_params=_cparams(("parallel",)),
        name="inproj_rest",
    )(h, wsq, wkv, wbg, *[w for w, _ in casts])
    return outs[0], outs[1], outs[2], outs[3:]


def _ret_kernel(blk_ref, s0_ref, gn_ref, r_ref, sout_ref, s_scr, *, lc, n_sub):
    c = pl.program_id(1)

    @pl.when(c == 0)
    def _():
        s_scr[...] = s0_ref[0]

    causal = (lax.broadcasted_iota(jnp.int32, (lc, lc), 0) >= lax.broadcasted_iota(jnp.int32, (lc, lc), 1))
    nt = (((1,), (1,)), ((), ()))
    tn = (((0,), (0,)), ((), ()))
    heads = range(RET_HEADS)
    cols = [slice(h * RET_DK, (h + 1) * RET_DK) for h in heads]
    for sub in range(n_sub):
        rows = slice(sub * lc, (sub + 1) * lc)
        part = lambda p, h: blk_ref[rows, p * RET_W + h * RET_DK:p * RET_W + (h + 1) * RET_DK]
        scores = [lax.dot_general(part(0, h), part(1, h), nt, preferred_element_type=F32) for h in heads]
        masked = [jnp.where(causal, s, 0.0).astype(BF16) for s in scores]
        states = [s_scr[h] for h in heads]
        outs = [jnp.dot(masked[h], part(2, h), preferred_element_type=F32)
                + jnp.dot(part(0, h), states[h].astype(BF16), preferred_element_type=F32) for h in heads]
        for h in heads:
            kv = lax.dot_general(part(1, h), part(2, h), tn, preferred_element_type=F32)
            s_scr[h] = math.exp(lc * RET_LOG_GAMMA[h]) * (states[h] + kv)
        for h in heads:
            o = outs[h]
            mu = jnp.mean(o, axis=-1, keepdims=True)
            d = o - mu
            var = jnp.mean(d * d, axis=-1, keepdims=True)
            on = d * lax.rsqrt(var + EPS) * gn_ref[:, cols[h]]
            g = part(3, h).astype(F32)
            r_ref[rows, cols[h]] = (on * (g * _sigmoid(g))).astype(BF16)

    @pl.when(c == pl.num_programs(1) - 1)
    def _():
        sout_ref[0] = s_scr[...]


def _retention(ret_all, s0, gn_g, *, n_seq, seq_len, lc, n_sub):
    R = ret_all.shape[0]
    rows = lc * n_sub
    nc = seq_len // rows
    st_spec = pl.BlockSpec((1, RET_HEADS, RET_DK, RET_DV), lambda b, c: (b, 0, 0, 0))
    return pl.pallas_call(
        functools.partial(_ret_kernel, lc=lc, n_sub=n_sub),
        grid=(n_seq, nc),
        in_specs=[pl.BlockSpec((rows, 4 * RET_W), lambda b, c: (b * nc + c, 0)),
                  st_spec,
                  _resident((1, RET_W))],
        out_specs=[pl.BlockSpec((rows, RET_W), lambda b, c: (b * nc + c, 0)), st_spec],
        out_shape=[jax.ShapeDtypeStruct((R, RET_W), BF16),
                   jax.ShapeDtypeStruct((n_seq, RET_HEADS, RET_DK, RET_DV), F32)],
        scratch_shapes=[pltpu.VMEM((RET_HEADS, RET_DK, RET_DV), F32)],
        compiler_params=_cparams(("parallel", "arbitrary")),
        name="retention",
    )(ret_all, s0, gn_g)


KEYS = WINDOW + CHUNK
KPAD = 256


def _swa_kernel(sink_ref, q_ref, k2_ref, k1_ref, k0_ref, v2_ref, v1_ref, v0_ref, o_ref, *, masked, n_q):
    j = pl.program_id(1)
    kall = jnp.concatenate([k2_ref[...], k1_ref[...], k0_ref[...]], axis=0)
    vall = jnp.concatenate([v2_ref[...], v1_ref[...], v0_ref[...]], axis=0)
    lane = lax.broadcasted_iota(jnp.int32, kall.shape, 1)
    zpad = jnp.zeros((KPAD - KEYS, LANES), BF16)

    def lane_halves(win, h):
        rolled = pltpu.roll(win, SWA_HEAD_DIM, 1)
        lo_src, hi_src = (win, rolled) if h == 0 else (rolled, win)
        return (jnp.where(lane < SWA_HEAD_DIM, lo_src, 0.0).astype(BF16),
                jnp.where(lane >= SWA_HEAD_DIM, hi_src, 0.0).astype(BF16))

    col = lax.broadcasted_iota(jnp.int32, (1, KPAD), 1)
    n_pairs = SWA_GROUP // 2
    rows = n_pairs * CHUNK
    row = lax.broadcasted_iota(jnp.int32, (rows, 1), 0)
    out_lane = lax.broadcasted_iota(jnp.int32, (rows, LANES), 1)
    nt = (((1,), (1,)), ((), ()))
    q_scale = jnp.asarray(SWA_HEAD_DIM ** -0.5, BF16)
    for h in range(SWA_KV_HEADS):
        k_lo, k_hi = lane_halves(kall, h)
        v_lo, v_hi = lane_halves(vall, h)
        base = h * SWA_GROUP * SWA_HEAD_DIM
        sinks = []
        for half in range(2):
            sink = jnp.zeros((rows, 1), F32)
            for p in range(n_pairs):
                sink = jnp.where(row // CHUNK == p, sink_ref[h * SWA_GROUP + 2 * p + half], sink)
            sinks.append(sink)
        for u in range(n_q):
            r0 = u * CHUNK
            c = j * n_q + u
            if masked:
                first_ok = jnp.where(c >= 2, 0, jnp.where(c == 1, CHUNK, 2 * CHUNK))
                ok = (col >= first_ok) & (col < KEYS)
            else:
                ok = col < KEYS
            kk = jnp.concatenate([k_lo[r0:r0 + KEYS], zpad, k_hi[r0:r0 + KEYS], zpad], axis=0)
            vv = jnp.concatenate([v_lo[r0:r0 + KEYS], zpad, v_hi[r0:r0 + KEYS], zpad], axis=0)
            q4 = jnp.concatenate([q_ref[r0:r0 + CHUNK, base + p * LANES: base + (p + 1) * LANES]
                                  for p in range(n_pairs)], axis=0)
            s = lax.dot_general(q4 * q_scale, kk, nt, preferred_element_type=F32)
            ps, invs = [], []
            for half in range(2):
                sh = jnp.where(ok, s[:, half * KPAD:(half + 1) * KPAD], NEG_INF)
                m = jnp.maximum(jnp.max(sh, axis=-1, keepdims=True), sinks[half])
                p_half = jnp.exp(sh - m)
                den = jnp.sum(p_half, axis=-1, keepdims=True) + jnp.exp(sinks[half] - m)
                ps.append(p_half.astype(BF16))
                invs.append(1.0 / den)
            pv = jnp.dot(jnp.concatenate(ps, axis=1), vv, preferred_element_type=F32)
            o = pv * jnp.where(out_lane < SWA_HEAD_DIM, invs[0], invs[1])
            for p in range(n_pairs):
                o_ref[r0:r0 + CHUNK, base + p * LANES: base + (p + 1) * LANES] = (
                    o[p * CHUNK:(p + 1) * CHUNK].astype(BF16))


def _swa(sinks, sq, k_arrs, v_arrs, k_maps, v_maps, *, n_seq, nc, masked, n_q):
    R = sq.shape[0]
    kv_rows = (CHUNK, CHUNK, n_q * CHUNK)
    kv_specs = [pl.BlockSpec((kv_rows[t % 3], SWA_KV_W), m) for t, m in enumerate((*k_maps, *v_maps))]
    return pl.pallas_call(
        functools.partial(_swa_kernel, masked=masked, n_q=n_q),
        grid=(n_seq, nc),
        in_specs=[pl.BlockSpec(memory_space=pltpu.SMEM),
                  pl.BlockSpec((n_q * CHUNK, SWA_Q_W), lambda b, c: (b * nc + c, 0)),
                  *kv_specs],
        out_specs=pl.BlockSpec((n_q * CHUNK, SWA_Q_W), lambda b, c: (b * nc + c, 0)),
        out_shape=jax.ShapeDtypeStruct((R, SWA_Q_W), BF16),
        compiler_params=_cparams(("parallel", "arbitrary")),
        name="swa",
    )(sinks, sq, *k_arrs, *v_arrs)


def _route(logits):
    tm = logits.shape[0]
    lane = lax.broadcasted_iota(jnp.int32, (tm, LANES), 1)
    is_g = lane < N_GROUPS
    gl = jnp.where(is_g, logits, NEG_INF)
    gmax = jnp.max(gl, axis=-1, keepdims=True)
    gidx = jnp.min(jnp.where(gl == gmax, lane, LANES), axis=-1, keepdims=True)
    gsum = jnp.sum(jnp.where(is_g, jnp.exp(gl - gmax), 0.0), axis=-1, keepdims=True)
    g_w = 1.0 / gsum
    base = N_GROUPS + EXPERTS_PER_GROUP * gidx
    el = jnp.where((lane >= base) & (lane < base + EXPERTS_PER_GROUP), logits, NEG_INF)
    v1 = jnp.max(el, axis=-1, keepdims=True)
    i1 = jnp.min(jnp.where(el == v1, lane, LANES), axis=-1, keepdims=True)
    el2 = jnp.where(lane == i1, NEG_INF, el)
    v2 = jnp.max(el2, axis=-1, keepdims=True)
    i2 = jnp.min(jnp.where(el2 == v2, lane, LANES), axis=-1, keepdims=True)
    e2 = jnp.exp(v2 - v1)
    den = 1.0 + e2
    w1 = g_w / den
    w2 = g_w * e2 / den
    l1 = i1 - base
    l2 = i2 - base
    first_lo = l1 < l2
    la = jnp.where(first_lo, l1, l2)
    lb = jnp.where(first_lo, l2, l1)
    wa = jnp.where(first_lo, w1, w2)
    wb = jnp.where(first_lo, w2, w1)
    pair = jnp.where(la == 0, lb - 1, jnp.where(la == 1, jnp.where(lb == 3, 3, 4), 5))
    swapped = la == 2
    w_slot_a = jnp.where(swapped, wb, wa)
    w_slot_b = jnp.where(swapped, wa, wb)
    cls = (gidx * N_PAIRS + pair).astype(F32)
    return jnp.where(lane == 0, w_slot_a, jnp.where(lane == 1, w_slot_b, jnp.where(lane == 2, cls, 0.0)))


ROW_W = D_MODEL + LANES


def _merge_kernel(x_ref, r_ref, o_ref, gate_ref, mod_ref, modp_ref, g2_ref, wrb_ref, wsb_ref, wout_ref, wr_ref,
                  br_ref, x1_ref, text_ref, meta_ref, cnt_ref, x1_scr, *, n_seq):
    i = pl.program_id(0)
    n = pl.num_programs(0) - 1
    tm = x_ref.shape[0]

    def matmul_stage():
        g_r = gate_ref[:, :D_MODEL].astype(F32)
        g_s = gate_ref[:, D_MODEL:].astype(F32)
        merged = (g_r * jnp.dot(r_ref[...], wrb_ref[...], preferred_element_type=F32)
                  + g_s * jnp.dot(o_ref[...], wsb_ref[...], preferred_element_type=F32))
        mix = jnp.dot(merged.astype(BF16), wout_ref[...], preferred_element_type=F32)
        gt1 = mod_ref[:, 2, :]
        x1 = (x_ref[...].reshape(n_seq, tm // n_seq, D_MODEL) + gt1[:, None, :]
              * mix.reshape(n_seq, tm // n_seq, D_MODEL)).reshape(tm, D_MODEL)
        x1_ref[...] = x1
        x1_scr[...] = x1

    def vector_stage():
        t = _modulated_norm(x1_scr[...], g2_ref[...], modp_ref[:, 3, :], modp_ref[:, 4, :], n_seq)
        logits = jnp.dot(t.astype(BF16), wr_ref[...], preferred_element_type=F32) + br_ref[...]
        meta = _route(logits)
        text_ref[:, :D_MODEL] = t
        text_ref[:, D_MODEL:] = meta
        meta_ref[...] = meta
        lane = lax.broadcasted_iota(jnp.int32, (tm, LANES), 1).astype(F32)
        cnt_ref[...] += jnp.sum(jnp.where(meta[:, 2:3] == lane, 1.0, 0.0), axis=0, keepdims=True)

    @pl.when(i == 0)
    def _():
        cnt_ref[...] = jnp.zeros_like(cnt_ref)
        matmul_stage()

    @pl.when((i > 0) & (i < n))
    def _():
        vector_stage()
        matmul_stage()

    @pl.when(i == n)
    def _():
        vector_stage()


def _merge(x2d, r, o_swa, gates, mod3, g2, wrb, wsb, wout, wr, br, *, seq_len, tm):
    R = x2d.shape[0]
    n = R // tm
    if seq_len >= tm:
        n_seq, tps = 1, seq_len // tm
        seq_of = lambda t: t // tps
    else:
        n_seq = tm // seq_len
        seq_of = lambda t: t
    cur = lambda i: jnp.minimum(i, n - 1)
    prev = lambda i: jnp.maximum(i - 1, 0)
    row = lambda w: pl.BlockSpec((tm, w), lambda i: (cur(i), 0))
    row_prev = lambda w: pl.BlockSpec((tm, w), lambda i: (prev(i), 0))
    return pl.pallas_call(
        functools.partial(_merge_kernel, n_seq=n_seq),
        grid=(n + 1,),
        in_specs=[row(D_MODEL), row(RET_W), row(SWA_Q_W), row(2 * D_MODEL),
                  pl.BlockSpec((n_seq, N_MOD, D_MODEL), lambda i: (seq_of(cur(i)), 0, 0)),
                  pl.BlockSpec((n_seq, N_MOD, D_MODEL), lambda i: (seq_of(prev(i)), 0, 0)),
                  _resident((1, D_MODEL)),
                  _resident((RET_W, D_MODEL)), _resident((SWA_Q_W, D_MODEL)), _resident((D_MODEL, D_MODEL)),
                  _resident((D_MODEL, LANES)), _resident((1, LANES))],
        out_specs=[row(D_MODEL), row_prev(ROW_W), row_prev(LANES), pl.BlockSpec((8, LANES), lambda i: (0, 0))],
        out_shape=[jax.ShapeDtypeStruct((R, D_MODEL), F32),
                   jax.ShapeDtypeStruct((R, ROW_W), F32),
                   jax.ShapeDtypeStruct((R, LANES), F32),
                   jax.ShapeDtypeStruct((8, LANES), F32)],
        scratch_shapes=[pltpu.VMEM((tm, D_MODEL), F32)],
        compiler_params=_cparams(("arbitrary",)),
        name="merge",
    )(x2d, r, o_swa, gates, mod3, mod3, g2, wrb, wsb, wout, wr, br)


PLAN_BLK = 2048
TILE_ROWS = 256


def _plan_kernel(cnt_a_ref, cnt_b_ref, meta_a_ref, meta_b_ref, pos_ref, tile_ref, pad_ref, offs_scr, carry_scr,
                 tri_scr, *, tm, nb_a):
    b = pl.program_id(0)
    blk = meta_a_ref.shape[0]
    lane = lax.broadcasted_iota(jnp.int32, (blk, LANES), 1)
    cls_col = jnp.where(b < nb_a, meta_a_ref[:, 2:3], meta_b_ref[:, 2:3])
    oh = jnp.where(cls_col == lane.astype(F32), 1.0, 0.0)

    @pl.when(b == 0)
    def _():
        ri = lax.broadcasted_iota(jnp.int32, (blk, blk), 0)
        ci = lax.broadcasted_iota(jnp.int32, (blk, blk), 1)
        tri_scr[...] = jnp.where(ci <= ri, 1.0, 0.0).astype(BF16)
        cnt = cnt_a_ref[...] + cnt_b_ref[...]
        ptiles = jnp.floor((cnt + (tm - 1)) * (1.0 / tm))
        ri = lax.broadcasted_iota(jnp.int32, (LANES, LANES), 0)
        ci = lax.broadcasted_iota(jnp.int32, (LANES, LANES), 1)
        before = jnp.where(ri < ci, 1.0, 0.0).astype(BF16)
        offs = jnp.dot(ptiles.astype(BF16), before, preferred_element_type=F32) * tm
        offs_scr[...] = offs
        carry_scr[...] = jnp.zeros_like(carry_scr)
        padded = ptiles * tm
        ends = offs + padded
        tl = lax.broadcasted_iota(jnp.int32, (TILE_ROWS, LANES), 1)
        tstart = lax.broadcasted_iota(jnp.int32, (TILE_ROWS, LANES), 0).astype(F32) * tm
        tcls = jnp.sum(jnp.where((ends[0:1, :] <= tstart) & (tl < N_CLASSES), 1.0, 0.0), axis=1, keepdims=True)
        tcls = jnp.minimum(tcls, N_CLASSES - 1.0)
        total = jnp.max(ends[0:1, :], axis=1, keepdims=True)
        grp = (jnp.where(tcls >= N_PAIRS, 1.0, 0.0) + jnp.where(tcls >= 2 * N_PAIRS, 1.0, 0.0)
               + jnp.where(tcls >= 3 * N_PAIRS, 1.0, 0.0))
        pair = tcls - N_PAIRS * grp
        la = jnp.where(pair < 3, 0.0, jnp.where(pair < 5, 1.0, 3.0))
        lb = jnp.where(pair == 0, 1.0, jnp.where((pair == 1) | (pair >= 4), 2.0, 3.0))
        ea = EXPERTS_PER_GROUP * grp + la
        eb = EXPERTS_PER_GROUP * grp + lb
        n_used = total * (1.0 / tm)
        tile_ref[...] = jnp.where(tl == 0, ea, jnp.where(tl == 1, eb, jnp.where(tl == 2, n_used, 0.0))
                                  ).astype(jnp.int32)
        npad = padded - cnt
        pstart = jnp.dot(npad.astype(BF16), before, preferred_element_type=F32)
        n_class_pad = jnp.sum(npad[0:1, :], axis=1, keepdims=True)
        rows = pad_ref.shape[0]
        v = (lax.broadcasted_iota(jnp.int32, (rows, LANES), 0) * LANES
             + lax.broadcasted_iota(jnp.int32, (rows, LANES), 1)).astype(F32)
        slot = jnp.where(v >= n_class_pad, total - n_class_pad + v, 0.0)
        for c in range(N_CLASSES):
            ps = pstart[0:1, c:c + 1]
            inside = (v >= ps) & (v < ps + npad[0:1, c:c + 1])
            slot = jnp.where(inside, offs[0:1, c:c + 1] + cnt[0:1, c:c + 1] - ps + v, slot)
        pad_ref[...] = slot.astype(jnp.int32)

    incl = jnp.dot(tri_scr[...], oh.astype(BF16), preferred_element_type=F32)
    base = offs_scr[0:1, :] + carry_scr[0:1, :]
    slot_oh = oh * (base + incl - oh)
    hi = jnp.floor(slot_oh * (1.0 / 256.0))
    lo = slot_oh - 256.0 * hi
    ones = jnp.ones((8, LANES), BF16)
    nt = (((1,), (1,)), ((), ()))
    pos = (256.0 * lax.dot_general(ones, hi.astype(BF16), nt, preferred_element_type=F32)
           + lax.dot_general(ones, lo.astype(BF16), nt, preferred_element_type=F32))
    pos_ref[...] = pos.astype(jnp.int32)
    carry_scr[...] += jnp.sum(oh, axis=0, keepdims=True)


def _plan(cnt_a, cnt_b, meta_a, meta_b, *, tm):
    na, nb = meta_a.shape[0], meta_b.shape[0]
    n = na + nb
    n_tiles = n // tm + N_CLASSES
    n_free = N_CLASSES * tm
    assert n_tiles <= TILE_ROWS and na % PLAN_BLK == 0 and nb % PLAN_BLK == 0 and n_free % LANES == 0
    n_slots = n_tiles * tm
    nb_a = na // PLAN_BLK
    nb_b = nb // PLAN_BLK
    pos2d, tile2d, pad2d = pl.pallas_call(
        functools.partial(_plan_kernel, tm=tm, nb_a=nb_a),
        grid=(nb_a + nb_b,),
        in_specs=[_resident((8, LANES)), _resident((8, LANES)),
                  pl.BlockSpec((PLAN_BLK, LANES), lambda b: (jnp.minimum(b, nb_a - 1), 0)),
                  pl.BlockSpec((PLAN_BLK, LANES), lambda b: (jnp.maximum(b - nb_a, 0), 0))],
        out_specs=[pl.BlockSpec((8, PLAN_BLK), lambda b: (0, b)),
                   pl.BlockSpec((TILE_ROWS, LANES), lambda b: (0, 0)),
                   pl.BlockSpec((n_free // LANES, LANES), lambda b: (0, 0))],
        out_shape=[jax.ShapeDtypeStruct((8, n), jnp.int32),
                   jax.ShapeDtypeStruct((TILE_ROWS, LANES), jnp.int32),
                   jax.ShapeDtypeStruct((n_free // LANES, LANES), jnp.int32)],
        scratch_shapes=[pltpu.VMEM((8, LANES), F32), pltpu.VMEM((8, LANES), F32),
                        pltpu.VMEM((PLAN_BLK, PLAN_BLK), BF16)],
        compiler_params=_cparams(("arbitrary",)),
        name="plan",
    )(cnt_a, cnt_b, meta_a, meta_b)
    return pos2d[0], pad2d.reshape(-1), tile2d[:n_tiles, 0], tile2d[:n_tiles, 1], tile2d[0:1, 2], n_slots


N_STAGE = 3


def _scatter_kernel(pos_ref, text_a_ref, text_b_ref, out_ref, stage, zero_scr, sem_in, sem_out, *, nb_a, nb_b):
    i = pl.program_id(0)
    n = pl.num_programs(0)
    n_tok = nb_a + nb_b
    tb = pos_ref.shape[2]
    slot = i % N_STAGE

    def load(step, s):
        def from_a():
            return pltpu.make_async_copy(text_a_ref.at[pl.ds(step * tb, tb), :], stage.at[s], sem_in.at[s])

        def from_b():
            return pltpu.make_async_copy(text_b_ref.at[pl.ds((step - nb_a) * tb, tb), :], stage.at[s], sem_in.at[s])

        return from_a, from_b

    def start_load(step, s):
        from_a, from_b = load(step, s)

        @pl.when(step < nb_a)
        def _():
            from_a().start()

        @pl.when((step >= nb_a) & (step < n_tok))
        def _():
            from_b().start()

    def wait_rows(s):
        pltpu.make_async_copy(stage.at[s], out_ref.at[pl.ds(0, tb), :], sem_out.at[s]).wait()

    def scatter_rows(src_row):
        def body(r, carry):
            dst = pos_ref[0, 0, r]
            pltpu.make_async_copy(src_row(r), out_ref.at[pl.ds(dst, 1), :], sem_out.at[slot]).start()
            return carry

        lax.fori_loop(0, tb, body, 0, unroll=16)

    @pl.when(i == 0)
    def _():
        zero_scr[...] = jnp.zeros_like(zero_scr)
        start_load(i, slot)

    @pl.when(i >= 2)
    def _():
        wait_rows((i + 1) % N_STAGE)

    start_load(i + 1, (i + 1) % N_STAGE)

    @pl.when(i < n_tok)
    def _():
        pltpu.make_async_copy(text_a_ref.at[pl.ds(0, tb), :], stage.at[slot], sem_in.at[slot]).wait()
        scatter_rows(lambda r: stage.at[slot, pl.ds(r, 1), :])

    @pl.when(i >= n_tok)
    def _():
        scatter_rows(lambda r: zero_scr.at[pl.ds(0, 1), :])

    @pl.when(i == n - 1)
    def _():
        wait_rows((i + 2) % N_STAGE)
        wait_rows(slot)


def _scatter_rows(text_a, text_b, pos_ext, *, tb):
    nb_a = text_a.shape[0] // tb
    nb_b = text_b.shape[0] // tb
    n_steps = pos_ext.shape[0] // tb
    assert n_steps >= 2 and nb_a >= 1 and nb_b >= 1
    return pl.pallas_call(
        functools.partial(_scatter_kernel, nb_a=nb_a, nb_b=nb_b),
        grid=(n_steps,),
        in_specs=[pl.BlockSpec((1, 1, tb), lambda i: (i, 0, 0), memory_space=pltpu.SMEM),
                  pl.BlockSpec(memory_space=pl.ANY),
                  pl.BlockSpec(memory_space=pl.ANY)],
        out_specs=pl.BlockSpec(memory_space=pl.ANY),
        out_shape=jax.ShapeDtypeStruct((pos_ext.shape[0], ROW_W), F32),
        scratch_shapes=[pltpu.VMEM((N_STAGE, tb, ROW_W), F32), pltpu.VMEM((8, ROW_W), F32),
                        pltpu.SemaphoreType.DMA((N_STAGE,)), pltpu.SemaphoreType.DMA((N_STAGE,))],
        compiler_params=_cparams(("arbitrary",)),
        name="scatter_rows",
    )(pos_ext.reshape(n_steps, 1, tb), text_a, text_b)


def _moe_kernel(ea_ref, eb_ref, nused_ref, xs_ref, w1a_ref, w3a_ref, w2a_ref, w1b_ref, w3b_ref, w2b_ref, y_ref):
    i = pl.program_id(0)

    @pl.when(i >= nused_ref[0])
    def _():
        y_ref[...] = jnp.zeros_like(y_ref)

    @pl.when(i < nused_ref[0])
    def _():
        x = xs_ref[:, :D_MODEL].astype(BF16)
        wa = xs_ref[:, D_MODEL:D_MODEL + 1]
        wb = xs_ref[:, D_MODEL + 1:D_MODEL + 2]

        def hidden(w1_ref, w3_ref, gate):
            a = jnp.dot(x, w1_ref[0], preferred_element_type=F32)
            b = jnp.dot(x, w3_ref[0], preferred_element_type=F32)
            return (a * _sigmoid(a) * b * gate).astype(BF16)

        ha = hidden(w1a_ref, w3a_ref, wa)
        hb = hidden(w1b_ref, w3b_ref, wb)
        y_ref[...] = (jnp.dot(ha, w2a_ref[0], preferred_element_type=F32)
                      + jnp.dot(hb, w2b_ref[0], preferred_element_type=F32))


def _moe(tile_ea, tile_eb, n_used, xs, w1, w3, w2, *, tm, n_tiles):
    last = lambda i, nu: jnp.minimum(i, nu[0] - 1)
    wa_map = lambda i, ea, eb, nu: (ea[last(i, nu)], 0, 0)
    wb_map = lambda i, ea, eb, nu: (eb[last(i, nu)], 0, 0)
    row_map = lambda i, ea, eb, nu: (last(i, nu), 0)
    up = (1, D_MODEL, D_EXPERT)
    down = (1, D_EXPERT, D_MODEL)
    grid_spec = pltpu.PrefetchScalarGridSpec(
        num_scalar_prefetch=3,
        grid=(n_tiles,),
        in_specs=[pl.BlockSpec((tm, D_MODEL + LANES), row_map),
                  pl.BlockSpec(up, wa_map), pl.BlockSpec(up, wa_map), pl.BlockSpec(down, wa_map),
                  pl.BlockSpec(up, wb_map), pl.BlockSpec(up, wb_map), pl.BlockSpec(down, wb_map)],
        out_specs=pl.BlockSpec((tm, D_MODEL), lambda i, ea, eb, nu: (i, 0)),
    )
    return pl.pallas_call(
        _moe_kernel,
        grid_spec=grid_spec,
        out_shape=jax.ShapeDtypeStruct((n_tiles * tm, D_MODEL), F32),
        compiler_params=_cparams(("arbitrary",)),
        name="moe",
    )(tile_ea, tile_eb, n_used, xs, w1, w3, w2, w1, w3, w2)


def _final_kernel(pos_ref, pos_next_ref, x1_ref, mod_ref, g_ref, ys_ref, o_ref, ybuf, sem, *, n_seq):
    i = pl.program_id(0)
    n = pl.num_programs(0)
    tm = x1_ref.shape[0]
    slot = i % 2

    def start_gather(idx_ref, s):
        def body(r, carry):
            src = idx_ref[0, 0, r]
            pltpu.make_async_copy(ys_ref.at[pl.ds(src, 1), :], ybuf.at[s, pl.ds(r, 1), :], sem.at[s]).start()
            return carry

        lax.fori_loop(0, tm, body, 0, unroll=16)

    @pl.when(i == 0)
    def _():
        start_gather(pos_ref, 0)

    @pl.when(i + 1 < n)
    def _():
        start_gather(pos_next_ref, 1 - slot)

    pltpu.make_async_copy(ys_ref.at[pl.ds(0, tm), :], ybuf.at[slot], sem.at[slot]).wait()
    gt2 = mod_ref[:, 5, :]
    x2 = (x1_ref[...].reshape(n_seq, tm // n_seq, D_MODEL)
          + gt2[:, None, :] * ybuf[slot].reshape(n_seq, tm // n_seq, D_MODEL)).reshape(tm, D_MODEL)
    ms = jnp.mean(x2 * x2, axis=-1, keepdims=True)
    o_ref[...] = x2 * lax.rsqrt(ms + EPS) * g_ref[...]


def _final(x1, y_sorted, pos, mod3, gf, *, seq_len, tm):
    R = x1.shape[0]
    if seq_len >= tm:
        n_seq, tps = 1, seq_len // tm
        mod_map = lambda i: (i // tps, 0, 0)
    else:
        n_seq = tm // seq_len
        mod_map = lambda i: (i, 0, 0)
    n = R // tm
    row = pl.BlockSpec((tm, D_MODEL), lambda i: (i, 0))
    pos3 = pos.reshape(n, 1, tm)
    return pl.pallas_call(
        functools.partial(_final_kernel, n_seq=n_seq),
        grid=(n,),
        in_specs=[pl.BlockSpec((1, 1, tm), lambda i: (i, 0, 0), memory_space=pltpu.SMEM),
                  pl.BlockSpec((1, 1, tm), lambda i: (jnp.minimum(i + 1, n - 1), 0, 0), memory_space=pltpu.SMEM),
                  row, pl.BlockSpec((n_seq, N_MOD, D_MODEL), mod_map), _resident((1, D_MODEL)),
                  pl.BlockSpec(memory_space=pl.ANY)],
        out_specs=row,
        out_shape=jax.ShapeDtypeStruct((R, D_MODEL), F32),
        scratch_shapes=[pltpu.VMEM((2, tm, D_MODEL), F32), pltpu.SemaphoreType.DMA((2,))],
        compiler_params=_cparams(("arbitrary",)),
        name="final",
    )(pos3, pos3, x1, mod3, gf, y_sorted)


def _rope_tables(pos):
    half = RET_DK // 2
    inv = ROPE_BASE ** (-jnp.arange(half, dtype=F32) / half)
    ang = pos.astype(F32)[:, None] * inv[None, :]
    cos = jnp.cos(ang)
    sin = jnp.sin(ang)
    return jnp.concatenate([cos, cos], axis=-1), jnp.concatenate([-sin, sin], axis=-1)


def kernel(x_prompt, x_sample, cache_ret_state, cache_swa_k, cache_swa_v, c_prompt, c_sample,
           norm1_g, norm2_g, ada_w, ada_b, w_in, ret_gn_g, swa_sinks, w_ret_branch, w_swa_branch, w_out,
           router_group_w, router_group_b, router_expert_w, router_expert_b,
           expert_w1, expert_w3, expert_w2, final_norm_g):
    depth = w_in.shape[0]
    assert depth == 1
    bp, tp, _ = x_prompt.shape
    bs, ts, _ = x_sample.shape
    past = WINDOW
    assert cache_swa_k.shape[2] == past and ts == CHUNK and tp % 512 == 0
    tm = 512

    l = 0
    c1 = 4 * RET_W
    c2 = c1 + SWA_Q_W
    c3 = c2 + 2 * SWA_KV_W
    wret = w_in[l, :, :c1].astype(BF16)
    n_r = N_GROUPS + N_EXPERTS
    wr = jnp.zeros((D_MODEL, LANES), F32).at[:, :N_GROUPS].set(router_group_w[l]).at[:, N_GROUPS:n_r].set(
        router_expert_w[l]).astype(BF16)
    br = jnp.zeros((1, LANES), F32).at[0, :N_GROUPS].set(router_group_b[l]).at[0, N_GROUPS:n_r].set(
        router_expert_b[l])
    g1 = norm1_g[l].reshape(1, D_MODEL)
    g2 = norm2_g[l].reshape(1, D_MODEL)
    gn = ret_gn_g[l].reshape(1, RET_W)
    gf = final_norm_g.reshape(1, D_MODEL)
    sinks = swa_sinks[l]

    c_all = jnp.concatenate([c_prompt, c_sample], axis=0)
    mod = _ada(c_all, ada_w[l], ada_b[l]).reshape(bp + bs, N_MOD, D_MODEL)
    mod_p, mod_s = mod[:bp], mod[bp:]

    cos_p, sin_p = _rope_tables(jnp.arange(tp))
    cos_s, sin_s = _rope_tables(PAST_LEN + jnp.arange(ts))
    rep = tm // ts
    cos_s, sin_s = jnp.tile(cos_s, (rep, 1)), jnp.tile(sin_s, (rep, 1))

    xp = x_prompt.reshape(bp * tp, D_MODEL)
    xs = x_sample.reshape(bs * ts, D_MODEL)

    lc_p = 128
    whole = lambda w: (w.reshape(-1, w.shape[-1]), [(0, w.shape[-1])])
    ret_p, h_p, (wsq, wkv, wbg, wrb, wsb, wo) = _inproj_ret(
        xp, mod_p, g1, cos_p, sin_p, wret, seq_len=tp, tm=tm, lc=lc_p,
        casts=[(w_in[l], [(c1, c2), (c2, c3), (c3, w_in.shape[2])]),
               whole(w_ret_branch[l]), whole(w_swa_branch[l]), whole(w_out[l])])
    sq_p, kv_p, gate_p, (w1, w3, w2) = _inproj_rest(
        h_p, wsq, wkv, wbg, tm=tm, casts=[whole(expert_w1[l]), whole(expert_w3[l]), whole(expert_w2[l])])
    w1, w3, w2 = (w.reshape(e.shape[1:]) for w, e in zip((w1, w3, w2), (expert_w1, expert_w3, expert_w2)))
    ret_s, h_s, _ = _inproj_ret(xs, mod_s, g1, cos_s, sin_s, wret, seq_len=ts, tm=tm, lc=ts)
    sq_s, kv_s, gate_s, _ = _inproj_rest(h_s, wsq, wkv, wbg, tm=tm)

    s0_p = jnp.zeros((bp, RET_HEADS, RET_DK, RET_DV), F32)
    r_p, state_p = _retention(ret_p, s0_p, gn, n_seq=bp, seq_len=tp, lc=lc_p, n_sub=2)
    r_s, state_s = _retention(ret_s, cache_ret_state[l].astype(F32), gn, n_seq=bs, seq_len=ts, lc=ts, n_sub=1)

    n_q = 8
    nc_p = tp // CHUNK
    ns_p = nc_p // n_q
    prev_map = lambda back, colblk: (lambda b, s: (b * nc_p + jnp.maximum(s * n_q - back, 0), colblk))
    own_map = lambda colblk: (lambda b, s: (b * ns_p + s, colblk))
    o_p = _swa(sinks, sq_p, [kv_p] * 3, [kv_p] * 3,
               [prev_map(2, 0), prev_map(1, 0), own_map(0)], [prev_map(2, 1), prev_map(1, 1), own_map(1)],
               n_seq=bp, nc=ns_p, masked=True, n_q=n_q)
    ck = cache_swa_k[l].reshape(bs * past, SWA_KV_W)
    cv = cache_swa_v[l].reshape(bs * past, SWA_KV_W)
    cmap = lambda blk: (lambda b, c: (2 * b + blk, 0))
    o_s = _swa(sinks, sq_s, [ck, ck, kv_s], [cv, cv, kv_s],
               [cmap(0), cmap(1), lambda b, c: (b, 0)], [cmap(0), cmap(1), lambda b, c: (b, 1)],
               n_seq=bs, nc=1, masked=False, n_q=1)

    tm_m = 256
    n_p = bp * tp
    x1_p, text_p, meta_p, cnt_p = _merge(xp, r_p, o_p, gate_p, mod_p, g2, wrb, wsb, wo, wr, br, seq_len=tp, tm=tm_m)
    x1_s, text_s, meta_s, cnt_s = _merge(xs, r_s, o_s, gate_s, mod_s, g2, wrb, wsb, wo, wr, br, seq_len=ts, tm=tm_m)

    tm_e = 256
    pos, free_slots, tile_ea, tile_eb, n_used, n_slots = _plan(cnt_p, cnt_s, meta_p, meta_s, tm=tm_e)
    pos_ext = jnp.concatenate([pos, free_slots], axis=0)
    xsorted = _scatter_rows(text_p, text_s, pos_ext, tb=1024)
    y_sorted = _moe(tile_ea, tile_eb, n_used, xsorted, w1, w3, w2, tm=tm_e, n_tiles=n_slots // tm_e)

    tm_f = 512
    out_p = _final(x1_p, y_sorted, pos[:n_p], mod_p, gf, seq_len=tp, tm=tm_f)
    out_s = _final(x1_s, y_sorted, pos[n_p:], mod_s, gf, seq_len=ts, tm=tm_f)

    y_prompt = out_p.reshape(bp, tp, D_MODEL)
    y_sample = out_s.reshape(bs, ts, D_MODEL)
    kvp = kv_p.reshape(bp, tp, 2 * SWA_KV_W)[:, tp - WINDOW:].reshape(bp, WINDOW, 2, SWA_KV_HEADS, SWA_HEAD_DIM)
    kvs = kv_s.reshape(bs, ts, 2, SWA_KV_HEADS, SWA_HEAD_DIM)
    k_s = jnp.concatenate([cache_swa_k[l].astype(F32), kvs[:, :, 0]], axis=1)[:, -WINDOW:]
    v_s = jnp.concatenate([cache_swa_v[l].astype(F32), kvs[:, :, 1]], axis=1)[:, -WINDOW:]
    return (y_prompt, y_sample, state_p[None], kvp[:, :, 0][None], kvp[:, :, 1][None],
            state_s[None], k_s[None], v_s[None])
```

```python
import functools
import math

import jax
import jax.numpy as jnp
from jax import lax
from jax.experimental import pallas as pl
from jax.experimental.pallas import tpu as pltpu

F32 = jnp.float32
BF16 = jnp.bfloat16

D_MODEL = 2048
CHUNK = 64
RET_HEADS = 8
RET_DK = 128
RET_DV = 128
RET_W = RET_HEADS * RET_DK
ROPE_BASE = 10000.0
SWA_Q_HEADS = 16
SWA_KV_HEADS = 2
SWA_GROUP = SWA_Q_HEADS // SWA_KV_HEADS
SWA_HEAD_DIM = 64
SWA_Q_W = SWA_Q_HEADS * SWA_HEAD_DIM
SWA_KV_W = SWA_KV_HEADS * SWA_HEAD_DIM
WINDOW = 128
PAST_LEN = 1024
N_GROUPS = 4
EXPERTS_PER_GROUP = 4
N_EXPERTS = 16
D_EXPERT = 512
N_MOD = 6
EPS = 1e-6
NEG_INF = -1e30
LOG2_E = math.log2(math.e)
N_PAIRS = 6
N_CLASSES = N_GROUPS * N_PAIRS

LANES = 128
VMEM_LIMIT = 56 * 1024 * 1024

RET_LOG_GAMMA = tuple(math.log1p(-(2.0 ** (-5.0 - h))) for h in range(RET_HEADS))


def _cparams(sem):
    return pltpu.CompilerParams(dimension_semantics=sem, vmem_limit_bytes=VMEM_LIMIT)


def _sigmoid(x):
    return 0.5 * jnp.tanh(0.5 * x) + 0.5


def _resident(shape):
    nd = len(shape)
    return pl.BlockSpec(shape, lambda *_: (0,) * nd, pipeline_mode=pl.Buffered(1))


def _ada_kernel(c_ref, w_ref, b_ref, o_ref):
    c = c_ref[...]
    a = c * jax.nn.sigmoid(c)
    w = w_ref[...]

    def split(v):
        hi = v.astype(BF16)
        return hi, (v - hi.astype(F32)).astype(BF16)

    a_hi, a_lo = split(a)
    w_hi, w_lo = split(w)
    dot = functools.partial(jnp.dot, preferred_element_type=F32)
    o_ref[...] = dot(a_hi, w_hi) + (dot(a_lo, w_hi) + dot(a_hi, w_lo)) + b_ref[...]


def _ada(c_all, ada_w, ada_b):
    nb = c_all.shape[0]
    n_out = ada_w.shape[1]
    tn = 1024
    return pl.pallas_call(
        _ada_kernel,
        grid=(n_out // tn,),
        in_specs=[pl.BlockSpec((nb, D_MODEL), lambda j: (0, 0)),
                  pl.BlockSpec((D_MODEL, tn), lambda j: (0, j)),
                  pl.BlockSpec((1, tn), lambda j: (0, j))],
        out_specs=pl.BlockSpec((nb, tn), lambda j: (0, j)),
        out_shape=jax.ShapeDtypeStruct((nb, n_out), F32),
        compiler_params=_cparams(("arbitrary",)),
        name="ada",
    )(c_all, ada_w, ada_b.reshape(1, n_out))


def _modulated_norm(x, g, shift, scale, n_seq):
    tm = x.shape[0]
    ms = jnp.mean(x * x, axis=-1, keepdims=True)
    y3 = (x * lax.rsqrt(ms + EPS)).reshape(n_seq, tm // n_seq, D_MODEL)
    gain = g * (1.0 + scale)
    h = y3 * gain[:, None, :] + shift[:, None, :]
    return h.reshape(tm, D_MODEL)


COL_BLK = 1024


def _cast_plan(casts, n_steps):
    in_specs, out_specs, out_shapes, splits = [], [], [], []
    for w, ranges in casts:
        slab = w.shape[0] // n_steps
        assert slab * n_steps == w.shape[0] and slab % 16 == 0
        in_specs.append(pl.BlockSpec((slab, w.shape[1]), lambda i: (i, 0)))
        for c0, c1 in ranges:
            out_specs.append(pl.BlockSpec((slab, c1 - c0), lambda i: (i, 0)))
            out_shapes.append(jax.ShapeDtypeStruct((w.shape[0], c1 - c0), BF16))
        splits.append(tuple(ranges))
    return in_specs, out_specs, out_shapes, tuple(splits)


def _cast_slabs(in_refs, out_refs, splits):
    k = 0
    for src, ranges in zip(in_refs, splits):
        for c0, c1 in ranges:
            out_refs[k][...] = src[:, c0:c1].astype(BF16)
            k += 1


def _inproj_ret_kernel(x_ref, mod_ref, g1_ref, cos_ref, sin_ref, wret_ref, *rest, n_seq, lc, splits):
    n_in = len(splits)
    cast_in, (ret_ref, h_ref) = rest[:n_in], rest[n_in:n_in + 2]
    cast_out, (dq_scr, dk_scr) = rest[n_in + 2:-2], rest[-2:]
    _cast_slabs(cast_in, cast_out, splits)
    tm = x_ref.shape[0]

    @pl.when(pl.program_id(0) == 0)
    def _():
        e = ((lax.broadcasted_iota(jnp.int32, (tm, RET_DK), 0) % lc) + 1).astype(F32)
        for hh in range(RET_HEADS):
            dq_scr[hh] = jnp.exp(e * RET_LOG_GAMMA[hh])
            dk_scr[hh] = jnp.exp(-e * RET_LOG_GAMMA[hh]) * (RET_DK ** -0.5)

    h = _modulated_norm(x_ref[...], g1_ref[...], mod_ref[:, 0, :], mod_ref[:, 1, :], n_seq)
    h_ref[...] = h.astype(BF16)
    cos = cos_ref[...]
    sin = sin_ref[...]
    assert COL_BLK == RET_W
    for blk in range(4):
        c0 = blk * COL_BLK
        acc = jnp.dot(h_ref[...], wret_ref[:, c0:c0 + COL_BLK], preferred_element_type=F32)
        if blk >= 2:
            ret_ref[:, c0:c0 + COL_BLK] = acc.astype(BF16)
            continue
        dec_scr = dq_scr if blk == 0 else dk_scr
        for hh in range(RET_HEADS):
            a = acc[:, hh * RET_DK:(hh + 1) * RET_DK]
            r = a * cos + pltpu.roll(a, RET_DK // 2, 1) * sin
            ret_ref[:, c0 + hh * RET_DK:c0 + (hh + 1) * RET_DK] = (r * dec_scr[hh]).astype(BF16)


def _inproj_rest_kernel(h_ref, wsq_ref, wkv_ref, wbg_ref, *rest, splits):
    n_in = len(splits)
    cast_in, (sq_ref, kv_ref, gate_ref), cast_out = rest[:n_in], rest[n_in:n_in + 3], rest[n_in + 3:]
    _cast_slabs(cast_in, cast_out, splits)
    sq_ref[...] = jnp.dot(h_ref[...], wsq_ref[...], preferred_element_type=F32).astype(BF16)
    kv_ref[...] = jnp.dot(h_ref[...], wkv_ref[...], preferred_element_type=F32)
    for blk in range(2 * D_MODEL // COL_BLK):
        c0 = blk * COL_BLK
        acc = jnp.dot(h_ref[...], wbg_ref[:, c0:c0 + COL_BLK], preferred_element_type=F32)
        gate_ref[:, c0:c0 + COL_BLK] = _sigmoid(acc).astype(BF16)


def _inproj_ret(x2d, mod3, g1, cos_t, sin_t, wret, *, seq_len, tm, lc, casts=()):
    R = x2d.shape[0]
    if seq_len >= tm:
        n_seq, tps = 1, seq_len // tm
        mod_map = lambda i: (i // tps, 0, 0)
        tab_map = lambda i: (i % tps, 0)
    else:
        n_seq = tm // seq_len
        mod_map = lambda i: (i, 0, 0)
        tab_map = lambda i: (0, 0)
    row = lambda w: pl.BlockSpec((tm, w), lambda i: (i, 0))
    assert tm % lc == 0
    n_steps = R // tm
    c_in, c_out, c_shapes, splits = _cast_plan(casts, n_steps)
    outs = pl.pallas_call(
        functools.partial(_inproj_ret_kernel, n_seq=n_seq, lc=lc, splits=splits),
        grid=(n_steps,),
        in_specs=[row(D_MODEL),
                  pl.BlockSpec((n_seq, N_MOD, D_MODEL), mod_map),
                  _resident((1, D_MODEL)),
                  pl.BlockSpec((tm, LANES), tab_map),
                  pl.BlockSpec((tm, LANES), tab_map),
                  _resident((D_MODEL, 4 * RET_W))] + c_in,
        out_specs=[row(4 * RET_W), row(D_MODEL)] + c_out,
        out_shape=[jax.ShapeDtypeStruct((R, 4 * RET_W), BF16),
                   jax.ShapeDtypeStruct((R, D_MODEL), BF16)] + c_shapes,
        scratch_shapes=[pltpu.VMEM((RET_HEADS, tm, RET_DK), F32), pltpu.VMEM((RET_HEADS, tm, RET_DK), F32)],
        compiler_params=_cparams(("arbitrary",)),
        name="inproj_ret",
    )(x2d, mod3, g1, cos_t, sin_t, wret, *[w for w, _ in casts])
    return outs[0], outs[1], outs[2:]


def _inproj_rest(h, wsq, wkv, wbg, *, tm, casts=()):
    R = h.shape[0]
    n_steps = R // tm
    row = lambda w: pl.BlockSpec((tm, w), lambda i: (i, 0))
    c_in, c_out, c_shapes, splits = _cast_plan(casts, n_steps)
    outs = pl.pallas_call(
        functools.partial(_inproj_rest_kernel, splits=splits),
        grid=(n_steps,),
        in_specs=[row(D_MODEL),
                  _resident((D_MODEL, SWA_Q_W)),
                  _resident((D_MODEL, 2 * SWA_KV_W)),
                  _resident((D_MODEL, 2 * D_MODEL))] + c_in,
        out_specs=[row(SWA_Q_W), row(2 * SWA_KV_W), row(2 * D_MODEL)] + c_out,
        out_shape=[jax.ShapeDtypeStruct((R, SWA_Q_W), BF16),
                   jax.ShapeDtypeStruct((R, 2 * SWA_KV_W), F32),
                   jax.ShapeDtypeStruct((R, 2 * D_MODEL), BF16)] + c_shapes,
        compiler_params=_cparams(("parallel",)),
        name="inproj_rest",
    )(h, wsq, wkv, wbg, *[w for w, _ in casts])
    return outs[0], outs[1], outs[2], outs[3:]


def _ret_kernel(blk_ref, s0_ref, gn_ref, r_ref, sout_ref, s_scr, *, lc, n_sub):
    c = pl.program_id(1)

    @pl.when(c == 0)
    def _():
        s_scr[...] = s0_ref[0]

    causal = (lax.broadcasted_iota(jnp.int32, (lc, lc), 0) >= lax.broadcasted_iota(jnp.int32, (lc, lc), 1))
    nt = (((1,), (1,)), ((), ()))
    tn = (((0,), (0,)), ((), ()))
    heads = range(RET_HEADS)
    cols = [slice(h * RET_DK, (h + 1) * RET_DK) for h in heads]
    for sub in range(n_sub):
        rows = slice(sub * lc, (sub + 1) * lc)
        part = lambda p, h: blk_ref[rows, p * RET_W + h * RET_DK:p * RET_W + (h + 1) * RET_DK]
        scores = [lax.dot_general(part(0, h), part(1, h), nt, preferred_element_type=F32) for h in heads]
        masked = [jnp.where(causal, s, 0.0).astype(BF16) for s in scores]
        states = [s_scr[h] for h in heads]
        outs = [jnp.dot(masked[h], part(2, h), preferred_element_type=F32)
                + jnp.dot(part(0, h), states[h].astype(BF16), preferred_element_type=F32) for h in heads]
        for h in heads:
            kv = lax.dot_general(part(1, h), part(2, h), tn, preferred_element_type=F32)
            s_scr[h] = math.exp(lc * RET_LOG_GAMMA[h]) * (states[h] + kv)
        for h in heads:
            o = outs[h]
            mu = jnp.mean(o, axis=-1, keepdims=True)
            d = o - mu
            var = jnp.mean(d * d, axis=-1, keepdims=True)
            on = d * lax.rsqrt(var + EPS) * gn_ref[:, cols[h]]
            g = part(3, h).astype(F32)
            r_ref[rows, cols[h]] = (on * (g * _sigmoid(g))).astype(BF16)

    @pl.when(c == pl.num_programs(1) - 1)
    def _():
        sout_ref[0] = s_scr[...]


def _retention(ret_all, s0, gn_g, *, n_seq, seq_len, lc, n_sub):
    R = ret_all.shape[0]
    rows = lc * n_sub
    nc = seq_len // rows
    st_spec = pl.BlockSpec((1, RET_HEADS, RET_DK, RET_DV), lambda b, c: (b, 0, 0, 0))
    return pl.pallas_call(
        functools.partial(_ret_kernel, lc=lc, n_sub=n_sub),
        grid=(n_seq, nc),
        in_specs=[pl.BlockSpec((rows, 4 * RET_W), lambda b, c: (b * nc + c, 0)),
                  st_spec,
                  _resident((1, RET_W))],
        out_specs=[pl.BlockSpec((rows, RET_W), lambda b, c: (b * nc + c, 0)), st_spec],
        out_shape=[jax.ShapeDtypeStruct((R, RET_W), BF16),
                   jax.ShapeDtypeStruct((n_seq, RET_HEADS, RET_DK, RET_DV), F32)],
        scratch_shapes=[pltpu.VMEM((RET_HEADS, RET_DK, RET_DV), F32)],
        compiler_params=_cparams(("parallel", "arbitrary")),
        name="retention",
    )(ret_all, s0, gn_g)


KEYS = WINDOW + CHUNK
KPAD = 256


def _swa_kernel(sink_ref, q_ref, k2_ref, k1_ref, k0_ref, v2_ref, v1_ref, v0_ref, o_ref, *, masked, n_q):
    j = pl.program_id(1)
    kall = jnp.concatenate([k2_ref[...], k1_ref[...], k0_ref[...]], axis=0) * (
        SWA_HEAD_DIM ** -0.5 * LOG2_E)
    vall = jnp.concatenate([v2_ref[...], v1_ref[...], v0_ref[...]], axis=0)
    lane = lax.broadcasted_iota(jnp.int32, kall.shape, 1)
    zpad = jnp.zeros((KPAD - KEYS, LANES), BF16)

    def lane_halves(win, h):
        rolled = pltpu.roll(win, SWA_HEAD_DIM, 1)
        lo_src, hi_src = (win, rolled) if h == 0 else (rolled, win)
        return (jnp.where(lane < SWA_HEAD_DIM, lo_src, 0.0).astype(BF16),
                jnp.where(lane >= SWA_HEAD_DIM, hi_src, 0.0).astype(BF16))

    col = lax.broadcasted_iota(jnp.int32, (1, KPAD), 1)
    n_pairs = SWA_GROUP // 2
    rows = n_pairs * CHUNK
    row = lax.broadcasted_iota(jnp.int32, (rows, 1), 0)
    out_lane = lax.broadcasted_iota(jnp.int32, (rows, LANES), 1)
    nt = (((1,), (1,)), ((), ()))
    for h in range(SWA_KV_HEADS):
        k_lo, k_hi = lane_halves(kall, h)
        v_lo, v_hi = lane_halves(vall, h)
        base = h * SWA_GROUP * SWA_HEAD_DIM
        sinks = []
        for half in range(2):
            sink = jnp.zeros((rows, 1), F32)
            for p in range(n_pairs):
                sink = jnp.where(row // CHUNK == p, sink_ref[h * SWA_GROUP + 2 * p + half], sink)
            sinks.append(sink * LOG2_E)
        for u in range(n_q):
            r0 = u * CHUNK
            c = j * n_q + u
            if masked:
                first_ok = jnp.where(c >= 2, 0, jnp.where(c == 1, CHUNK, 2 * CHUNK))
                ok = (col >= first_ok) & (col < KEYS)
            else:
                ok = col < KEYS
            kk = jnp.concatenate([k_lo[r0:r0 + KEYS], zpad, k_hi[r0:r0 + KEYS], zpad], axis=0)
            vv = jnp.concatenate([v_lo[r0:r0 + KEYS], zpad, v_hi[r0:r0 + KEYS], zpad], axis=0)
            q4 = jnp.concatenate([q_ref[r0:r0 + CHUNK, base + p * LANES: base + (p + 1) * LANES]
                                  for p in range(n_pairs)], axis=0)
            s = lax.dot_general(q4, kk, nt, preferred_element_type=F32)
            ps, invs = [], []
            for half in range(2):
                sh = jnp.where(ok, s[:, half * KPAD:(half + 1) * KPAD], NEG_INF)
                m = jnp.maximum(jnp.max(sh, axis=-1, keepdims=True), sinks[half])
                p_half = jnp.exp2(sh - m)
                den = jnp.sum(p_half, axis=-1, keepdims=True) + jnp.exp2(sinks[half] - m)
                ps.append(p_half.astype(BF16))
                invs.append(1.0 / den)
            pv = jnp.dot(jnp.concatenate(ps, axis=1), vv, preferred_element_type=F32)
            o = pv * jnp.where(out_lane < SWA_HEAD_DIM, invs[0], invs[1])
            for p in range(n_pairs):
                o_ref[r0:r0 + CHUNK, base + p * LANES: base + (p + 1) * LANES] = (
                    o[p * CHUNK:(p + 1) * CHUNK].astype(BF16))


def _swa(sinks, sq, k_arrs, v_arrs, k_maps, v_maps, *, n_seq, nc, masked, n_q):
    R = sq.shape[0]
    kv_rows = (CHUNK, CHUNK, n_q * CHUNK)
    kv_specs = [pl.BlockSpec((kv_rows[t % 3], SWA_KV_W), m) for t, m in enumerate((*k_maps, *v_maps))]
    return pl.pallas_call(
        functools.partial(_swa_kernel, masked=masked, n_q=n_q),
        grid=(n_seq, nc),
        in_specs=[pl.BlockSpec(memory_space=pltpu.SMEM),
                  pl.BlockSpec((n_q * CHUNK, SWA_Q_W), lambda b, c: (b * nc + c, 0)),
                  *kv_specs],
        out_specs=pl.BlockSpec((n_q * CHUNK, SWA_Q_W), lambda b, c: (b * nc + c, 0)),
        out_shape=jax.ShapeDtypeStruct((R, SWA_Q_W), BF16),
        compiler_params=_cparams(("parallel", "arbitrary")),
        name="swa",
    )(sinks, sq, *k_arrs, *v_arrs)


def _route(logits):
    tm = logits.shape[0]
    lane = lax.broadcasted_iota(jnp.int32, (tm, LANES), 1)
    is_g = lane < N_GROUPS
    gl = jnp.where(is_g, logits, NEG_INF)
    gmax = jnp.max(gl, axis=-1, keepdims=True)
    gidx = jnp.min(jnp.where(gl == gmax, lane, LANES), axis=-1, keepdims=True)
    gsum = jnp.sum(jnp.where(is_g, jnp.exp(gl - gmax), 0.0), axis=-1, keepdims=True)
    g_w = 1.0 / gsum
    base = N_GROUPS + EXPERTS_PER_GROUP * gidx
    el = jnp.where((lane >= base) & (lane < base + EXPERTS_PER_GROUP), logits, NEG_INF)
    v1 = jnp.max(el, axis=-1, keepdims=True)
    i1 = jnp.min(jnp.where(el == v1, lane, LANES), axis=-1, keepdims=True)
    el2 = jnp.where(lane == i1, NEG_INF, el)
    v2 = jnp.max(el2, axis=-1, keepdims=True)
    i2 = jnp.min(jnp.where(el2 == v2, lane, LANES), axis=-1, keepdims=True)
    e2 = jnp.exp(v2 - v1)
    den = 1.0 + e2
    w1 = g_w / den
    w2 = g_w * e2 / den
    l1 = i1 - base
    l2 = i2 - base
    first_lo = l1 < l2
    la = jnp.where(first_lo, l1, l2)
    lb = jnp.where(first_lo, l2, l1)
    wa = jnp.where(first_lo, w1, w2)
    wb = jnp.where(first_lo, w2, w1)
    pair = jnp.where(la == 0, lb - 1, jnp.where(la == 1, jnp.where(lb == 3, 3, 4), 5))
    swapped = la == 2
    w_slot_a = jnp.where(swapped, wb, wa)
    w_slot_b = jnp.where(swapped, wa, wb)
    cls = (gidx * N_PAIRS + pair).astype(F32)
    return jnp.where(lane == 0, w_slot_a, jnp.where(lane == 1, w_slot_b, jnp.where(lane == 2, cls, 0.0)))


ROW_W = D_MODEL + LANES


def _merge_kernel(x_ref, r_ref, o_ref, gate_ref, mod_ref, modp_ref, g2_ref, wrb_ref, wsb_ref, wout_ref, wr_ref,
                  br_ref, x1_ref, text_ref, meta_ref, cnt_ref, x1_scr, *, n_seq):
    i = pl.program_id(0)
    n = pl.num_programs(0) - 1
    tm = x_ref.shape[0]

    def matmul_stage():
        g_r = gate_ref[:, :D_MODEL].astype(F32)
        g_s = gate_ref[:, D_MODEL:].astype(F32)
        merged = (g_r * jnp.dot(r_ref[...], wrb_ref[...], preferred_element_type=F32)
                  + g_s * jnp.dot(o_ref[...], wsb_ref[...], preferred_element_type=F32))
        mix = jnp.dot(merged.astype(BF16), wout_ref[...], preferred_element_type=F32)
        gt1 = mod_ref[:, 2, :]
        x1 = (x_ref[...].reshape(n_seq, tm // n_seq, D_MODEL) + gt1[:, None, :]
              * mix.reshape(n_seq, tm // n_seq, D_MODEL)).reshape(tm, D_MODEL)
        x1_ref[...] = x1
        x1_scr[...] = x1

    def vector_stage():
        t = _modulated_norm(x1_scr[...], g2_ref[...], modp_ref[:, 3, :], modp_ref[:, 4, :], n_seq)
        logits = jnp.dot(t.astype(BF16), wr_ref[...], preferred_element_type=F32) + br_ref[...]
        meta = _route(logits)
        text_ref[:, :D_MODEL] = t
        text_ref[:, D_MODEL:] = meta
        meta_ref[...] = meta
        lane = lax.broadcasted_iota(jnp.int32, (tm, LANES), 1).astype(F32)
        cnt_ref[...] += jnp.sum(jnp.where(meta[:, 2:3] == lane, 1.0, 0.0), axis=0, keepdims=True)

    @pl.when(i == 0)
    def _():
        cnt_ref[...] = jnp.zeros_like(cnt_ref)
        matmul_stage()

    @pl.when((i > 0) & (i < n))
    def _():
        vector_stage()
        matmul_stage()

    @pl.when(i == n)
    def _():
        vector_stage()


def _merge(x2d, r, o_swa, gates, mod3, g2, wrb, wsb, wout, wr, br, *, seq_len, tm):
    R = x2d.shape[0]
    n = R // tm
    if seq_len >= tm:
        n_seq, tps = 1, seq_len // tm
        seq_of = lambda t: t // tps
    else:
        n_seq = tm // seq_len
        seq_of = lambda t: t
    cur = lambda i: jnp.minimum(i, n - 1)
    prev = lambda i: jnp.maximum(i - 1, 0)
    row = lambda w: pl.BlockSpec((tm, w), lambda i: (cur(i), 0))
    row_prev = lambda w: pl.BlockSpec((tm, w), lambda i: (prev(i), 0))
    return pl.pallas_call(
        functools.partial(_merge_kernel, n_seq=n_seq),
        grid=(n + 1,),
        in_specs=[row(D_MODEL), row(RET_W), row(SWA_Q_W), row(2 * D_MODEL),
                  pl.BlockSpec((n_seq, N_MOD, D_MODEL), lambda i: (seq_of(cur(i)), 0, 0)),
                  pl.BlockSpec((n_seq, N_MOD, D_MODEL), lambda i: (seq_of(prev(i)), 0, 0)),
                  _resident((1, D_MODEL)),
                  _resident((RET_W, D_MODEL)), _resident((SWA_Q_W, D_MODEL)), _resident((D_MODEL, D_MODEL)),
                  _resident((D_MODEL, LANES)), _resident((1, LANES))],
        out_specs=[row(D_MODEL), row_prev(ROW_W), row_prev(LANES), pl.BlockSpec((8, LANES), lambda i: (0, 0))],
        out_shape=[jax.ShapeDtypeStruct((R, D_MODEL), F32),
                   jax.ShapeDtypeStruct((R, ROW_W), F32),
                   jax.ShapeDtypeStruct((R, LANES), F32),
                   jax.ShapeDtypeStruct((8, LANES), F32)],
        scratch_shapes=[pltpu.VMEM((tm, D_MODEL), F32)],
        compiler_params=_cparams(("arbitrary",)),
        name="merge",
    )(x2d, r, o_swa, gates, mod3, mod3, g2, wrb, wsb, wout, wr, br)


PLAN_BLK = 2048
TILE_ROWS = 256


def _plan_kernel(cnt_a_ref, cnt_b_ref, meta_a_ref, meta_b_ref, pos_ref, tile_ref, pad_ref, offs_scr, carry_scr,
                 tri_scr, *, tm, nb_a):
    b = pl.program_id(0)
    blk = meta_a_ref.shape[0]
    lane = lax.broadcasted_iota(jnp.int32, (blk, LANES), 1)
    cls_col = jnp.where(b < nb_a, meta_a_ref[:, 2:3], meta_b_ref[:, 2:3])
    oh = jnp.where(cls_col == lane.astype(F32), 1.0, 0.0)

    @pl.when(b == 0)
    def _():
        ri = lax.broadcasted_iota(jnp.int32, (blk, blk), 0)
        ci = lax.broadcasted_iota(jnp.int32, (blk, blk), 1)
        tri_scr[...] = jnp.where(ci <= ri, 1.0, 0.0).astype(BF16)
        cnt = cnt_a_ref[...] + cnt_b_ref[...]
        ptiles = jnp.floor((cnt + (tm - 1)) * (1.0 / tm))
        ri = lax.broadcasted_iota(jnp.int32, (LANES, LANES), 0)
        ci = lax.broadcasted_iota(jnp.int32, (LANES, LANES), 1)
        before = jnp.where(ri < ci, 1.0, 0.0).astype(BF16)
        offs = jnp.dot(ptiles.astype(BF16), before, preferred_element_type=F32) * tm
        offs_scr[...] = offs
        carry_scr[...] = jnp.zeros_like(carry_scr)
        padded = ptiles * tm
        ends = offs + padded
        tl = lax.broadcasted_iota(jnp.int32, (TILE_ROWS, LANES), 1)
        tstart = lax.broadcasted_iota(jnp.int32, (TILE_ROWS, LANES), 0).astype(F32) * tm
        tcls = jnp.sum(jnp.where((ends[0:1, :] <= tstart) & (tl < N_CLASSES), 1.0, 0.0), axis=1, keepdims=True)
        tcls = jnp.minimum(tcls, N_CLASSES - 1.0)
        total = jnp.max(ends[0:1, :], axis=1, keepdims=True)
        grp = (jnp.where(tcls >= N_PAIRS, 1.0, 0.0) + jnp.where(tcls >= 2 * N_PAIRS, 1.0, 0.0)
               + jnp.where(tcls >= 3 * N_PAIRS, 1.0, 0.0))
        pair = tcls - N_PAIRS * grp
        la = jnp.where(pair < 3, 0.0, jnp.where(pair < 5, 1.0, 3.0))
        lb = jnp.where(pair == 0, 1.0, jnp.where((pair == 1) | (pair >= 4), 2.0, 3.0))
        ea = EXPERTS_PER_GROUP * grp + la
        eb = EXPERTS_PER_GROUP * grp + lb
        n_used = total * (1.0 / tm)
        tile_ref[...] = jnp.where(tl == 0, ea, jnp.where(tl == 1, eb, jnp.where(tl == 2, n_used, 0.0))
                                  ).astype(jnp.int32)
        npad = padded - cnt
        pstart = jnp.dot(npad.astype(BF16), before, preferred_element_type=F32)
        n_class_pad = jnp.sum(npad[0:1, :], axis=1, keepdims=True)
        rows = pad_ref.shape[0]
        v = (lax.broadcasted_iota(jnp.int32, (rows, LANES), 0) * LANES
             + lax.broadcasted_iota(jnp.int32, (rows, LANES), 1)).astype(F32)
        slot = jnp.where(v >= n_class_pad, total - n_class_pad + v, 0.0)
        for c in range(N_CLASSES):
            ps = pstart[0:1, c:c + 1]
            inside = (v >= ps) & (v < ps + npad[0:1, c:c + 1])
            slot = jnp.where(inside, offs[0:1, c:c + 1] + cnt[0:1, c:c + 1] - ps + v, slot)
        pad_ref[...] = slot.astype(jnp.int32)

    incl = jnp.dot(tri_scr[...], oh.astype(BF16), preferred_element_type=F32)
    base = offs_scr[0:1, :] + carry_scr[0:1, :]
    slot_oh = oh * (base + incl - oh)
    hi = jnp.floor(slot_oh * (1.0 / 256.0))
    lo = slot_oh - 256.0 * hi
    ones = jnp.ones((8, LANES), BF16)
    nt = (((1,), (1,)), ((), ()))
    pos = (256.0 * lax.dot_general(ones, hi.astype(BF16), nt, preferred_element_type=F32)
           + lax.dot_general(ones, lo.astype(BF16), nt, preferred_element_type=F32))
    pos_ref[...] = pos.astype(jnp.int32)
    carry_scr[...] += jnp.sum(oh, axis=0, keepdims=True)


def _plan(cnt_a, cnt_b, meta_a, meta_b, *, tm):
    na, nb = meta_a.shape[0], meta_b.shape[0]
    n = na + nb
    n_tiles = n // tm + N_CLASSES
    n_free = N_CLASSES * tm
    assert n_tiles <= TILE_ROWS and na % PLAN_BLK == 0 and nb % PLAN_BLK == 0 and n_free % LANES == 0
    n_slots = n_tiles * tm
    nb_a = na // PLAN_BLK
    nb_b = nb // PLAN_BLK
    pos2d, tile2d, pad2d = pl.pallas_call(
        functools.partial(_plan_kernel, tm=tm, nb_a=nb_a),
        grid=(nb_a + nb_b,),
        in_specs=[_resident((8, LANES)), _resident((8, LANES)),
                  pl.BlockSpec((PLAN_BLK, LANES), lambda b: (jnp.minimum(b, nb_a - 1), 0)),
                  pl.BlockSpec((PLAN_BLK, LANES), lambda b: (jnp.maximum(b - nb_a, 0), 0))],
        out_specs=[pl.BlockSpec((8, PLAN_BLK), lambda b: (0, b)),
                   pl.BlockSpec((TILE_ROWS, LANES), lambda b: (0, 0)),
                   pl.BlockSpec((n_free // LANES, LANES), lambda b: (0, 0))],
        out_shape=[jax.ShapeDtypeStruct((8, n), jnp.int32),
                   jax.ShapeDtypeStruct((TILE_ROWS, LANES), jnp.int32),
                   jax.ShapeDtypeStruct((n_free // LANES, LANES), jnp.int32)],
        scratch_shapes=[pltpu.VMEM((8, LANES), F32), pltpu.VMEM((8, LANES), F32),
                        pltpu.VMEM((PLAN_BLK, PLAN_BLK), BF16)],
        compiler_params=_cparams(("arbitrary",)),
        name="plan",
    )(cnt_a, cnt_b, meta_a, meta_b)
    return pos2d[0], pad2d.reshape(-1), tile2d[:n_tiles, 0], tile2d[:n_tiles, 1], tile2d[0:1, 2], n_slots


N_STAGE = 3


def _scatter_kernel(pos_ref, text_a_ref, text_b_ref, out_ref, stage, zero_scr, sem_in, sem_out, *, nb_a, nb_b):
    i = pl.program_id(0)
    n = pl.num_programs(0)
    n_tok = nb_a + nb_b
    tb = pos_ref.shape[2]
    slot = i % N_STAGE

    def load(step, s):
        def from_a():
            return pltpu.make_async_copy(text_a_ref.at[pl.ds(step * tb, tb), :], stage.at[s], sem_in.at[s])

        def from_b():
            return pltpu.make_async_copy(text_b_ref.at[pl.ds((step - nb_a) * tb, tb), :], stage.at[s], sem_in.at[s])

        return from_a, from_b

    def start_load(step, s):
        from_a, from_b = load(step, s)

        @pl.when(step < nb_a)
        def _():
            from_a().start()

        @pl.when((step >= nb_a) & (step < n_tok))
        def _():
            from_b().start()

    def wait_rows(s):
        pltpu.make_async_copy(stage.at[s], out_ref.at[pl.ds(0, tb), :], sem_out.at[s]).wait()

    def scatter_rows(src_row):
        def body(r, carry):
            dst = pos_ref[0, 0, r]
            pltpu.make_async_copy(src_row(r), out_ref.at[pl.ds(dst, 1), :], sem_out.at[slot]).start()
            return carry

        lax.fori_loop(0, tb, body, 0, unroll=16)

    @pl.when(i == 0)
    def _():
        zero_scr[...] = jnp.zeros_like(zero_scr)
        start_load(i, slot)

    @pl.when(i >= 2)
    def _():
        wait_rows((i + 1) % N_STAGE)

    start_load(i + 1, (i + 1) % N_STAGE)

    @pl.when(i < n_tok)
    def _():
        pltpu.make_async_copy(text_a_ref.at[pl.ds(0, tb), :], stage.at[slot], sem_in.at[slot]).wait()
        scatter_rows(lambda r: stage.at[slot, pl.ds(r, 1), :])

    @pl.when(i >= n_tok)
    def _():
        scatter_rows(lambda r: zero_scr.at[pl.ds(0, 1), :])

    @pl.when(i == n - 1)
    def _():
        wait_rows((i + 2) % N_STAGE)
        wait_rows(slot)


def _scatter_rows(text_a, text_b, pos_ext, *, tb):
    nb_a = text_a.shape[0] // tb
    nb_b = text_b.shape[0] // tb
    n_steps = pos_ext.shape[0] // tb
    assert n_steps >= 2 and nb_a >= 1 and nb_b >= 1
    return pl.pallas_call(
        functools.partial(_scatter_kernel, nb_a=nb_a, nb_b=nb_b),
        grid=(n_steps,),
        in_specs=[pl.BlockSpec((1, 1, tb), lambda i: (i, 0, 0), memory_space=pltpu.SMEM),
                  pl.BlockSpec(memory_space=pl.ANY),
                  pl.BlockSpec(memory_space=pl.ANY)],
        out_specs=pl.BlockSpec(memory_space=pl.ANY),
        out_shape=jax.ShapeDtypeStruct((pos_ext.shape[0], ROW_W), F32),
        scratch_shapes=[pltpu.VMEM((N_STAGE, tb, ROW_W), F32), pltpu.VMEM((8, ROW_W), F32),
                        pltpu.SemaphoreType.DMA((N_STAGE,)), pltpu.SemaphoreType.DMA((N_STAGE,))],
        compiler_params=_cparams(("arbitrary",)),
        name="scatter_rows",
    )(pos_ext.reshape(n_steps, 1, tb), text_a, text_b)


def _moe_kernel(ea_ref, eb_ref, nused_ref, xs_ref, w1a_ref, w3a_ref, w2a_ref, w1b_ref, w3b_ref, w2b_ref, y_ref):
    i = pl.program_id(0)

    @pl.when(i >= nused_ref[0])
    def _():
        y_ref[...] = jnp.zeros_like(y_ref)

    @pl.when(i < nused_ref[0])
    def _():
        x = xs_ref[:, :D_MODEL].astype(BF16)
        wa = xs_ref[:, D_MODEL:D_MODEL + 1]
        wb = xs_ref[:, D_MODEL + 1:D_MODEL + 2]

        def hidden(w1_ref, w3_ref, gate):
            a = jnp.dot(x, w1_ref[0], preferred_element_type=F32)
            b = jnp.dot(x, w3_ref[0], preferred_element_type=F32)
            return (a * _sigmoid(a) * b * gate).astype(BF16)

        ha = hidden(w1a_ref, w3a_ref, wa)
        hb = hidden(w1b_ref, w3b_ref, wb)
        y_ref[...] = (jnp.dot(ha, w2a_ref[0], preferred_element_type=F32)
                      + jnp.dot(hb, w2b_ref[0], preferred_element_type=F32))


def _moe(tile_ea, tile_eb, n_used, xs, w1, w3, w2, *, tm, n_tiles):
    last = lambda i, nu: jnp.minimum(i, nu[0] - 1)
    wa_map = lambda i, ea, eb, nu: (ea[last(i, nu)], 0, 0)
    wb_map = lambda i, ea, eb, nu: (eb[last(i, nu)], 0, 0)
    row_map = lambda i, ea, eb, nu: (last(i, nu), 0)
    up = (1, D_MODEL, D_EXPERT)
    down = (1, D_EXPERT, D_MODEL)
    grid_spec = pltpu.PrefetchScalarGridSpec(
        num_scalar_prefetch=3,
        grid=(n_tiles,),
        in_specs=[pl.BlockSpec((tm, D_MODEL + LANES), row_map),
                  pl.BlockSpec(up, wa_map), pl.BlockSpec(up, wa_map), pl.BlockSpec(down, wa_map),
                  pl.BlockSpec(up, wb_map), pl.BlockSpec(up, wb_map), pl.BlockSpec(down, wb_map)],
        out_specs=pl.BlockSpec((tm, D_MODEL), lambda i, ea, eb, nu: (i, 0)),
    )
    return pl.pallas_call(
        _moe_kernel,
        grid_spec=grid_spec,
        out_shape=jax.ShapeDtypeStruct((n_tiles * tm, D_MODEL), F32),
        compiler_params=_cparams(("arbitrary",)),
        name="moe",
    )(tile_ea, tile_eb, n_used, xs, w1, w3, w2, w1, w3, w2)


def _final_kernel(pos_ref, pos_next_ref, x1_ref, mod_ref, g_ref, ys_ref, o_ref, ybuf, sem, *, n_seq):
    i = pl.program_id(0)
    n = pl.num_programs(0)
    tm = x1_ref.shape[0]
    slot = i % 2

    def start_gather(idx_ref, s):
        def body(r, carry):
            src = idx_ref[0, 0, r]
            pltpu.make_async_copy(ys_ref.at[pl.ds(src, 1), :], ybuf.at[s, pl.ds(r, 1), :], sem.at[s]).start()
            return carry

        lax.fori_loop(0, tm, body, 0, unroll=16)

    @pl.when(i == 0)
    def _():
        start_gather(pos_ref, 0)

    @pl.when(i + 1 < n)
    def _():
        start_gather(pos_next_ref, 1 - slot)

    pltpu.make_async_copy(ys_ref.at[pl.ds(0, tm), :], ybuf.at[slot], sem.at[slot]).wait()
    gt2 = mod_ref[:, 5, :]
    x2 = (x1_ref[...].reshape(n_seq, tm // n_seq, D_MODEL)
          + gt2[:, None, :] * ybuf[slot].reshape(n_seq, tm // n_seq, D_MODEL)).reshape(tm, D_MODEL)
    ms = jnp.mean(x2 * x2, axis=-1, keepdims=True)
    o_ref[...] = x2 * lax.rsqrt(ms + EPS) * g_ref[...]


def _final(x1, y_sorted, pos, mod3, gf, *, seq_len, tm):
    R = x1.shape[0]
    if seq_len >= tm:
        n_seq, tps = 1, seq_len // tm
        mod_map = lambda i: (i // tps, 0, 0)
    else:
        n_seq = tm // seq_len
        mod_map = lambda i: (i, 0, 0)
    n = R // tm
    row = pl.BlockSpec((tm, D_MODEL), lambda i: (i, 0))
    pos3 = pos.reshape(n, 1, tm)
    return pl.pallas_call(
        functools.partial(_final_kernel, n_seq=n_seq),
        grid=(n,),
        in_specs=[pl.BlockSpec((1, 1, tm), lambda i: (i, 0, 0), memory_space=pltpu.SMEM),
                  pl.BlockSpec((1, 1, tm), lambda i: (jnp.minimum(i + 1, n - 1), 0, 0), memory_space=pltpu.SMEM),
                  row, pl.BlockSpec((n_seq, N_MOD, D_MODEL), mod_map), _resident((1, D_MODEL)),
                  pl.BlockSpec(memory_space=pl.ANY)],
        out_specs=row,
        out_shape=jax.ShapeDtypeStruct((R, D_MODEL), F32),
        scratch_shapes=[pltpu.VMEM((2, tm, D_MODEL), F32), pltpu.SemaphoreType.DMA((2,))],
        compiler_params=_cparams(("arbitrary",)),
        name="final",
    )(pos3, pos3, x1, mod3, gf, y_sorted)


def _rope_tables(pos):
    half = RET_DK // 2
    inv = ROPE_BASE ** (-jnp.arange(half, dtype=F32) / half)
    ang = pos.astype(F32)[:, None] * inv[None, :]
    cos = jnp.cos(ang)
    sin = jnp.sin(ang)
    return jnp.concatenate([cos, cos], axis=-1), jnp.concatenate([-sin, sin], axis=-1)


def kernel(x_prompt, x_sample, cache_ret_state, cache_swa_k, cache_swa_v, c_prompt, c_sample,
           norm1_g, norm2_g, ada_w, ada_b, w_in, ret_gn_g, swa_sinks, w_ret_branch, w_swa_branch, w_out,
           router_group_w, router_group_b, router_expert_w, router_expert_b,
           expert_w1, expert_w3, expert_w2, final_norm_g):
    depth = w_in.shape[0]
    assert depth == 1
    bp, tp, _ = x_prompt.shape
    bs, ts, _ = x_sample.shape
    past = WINDOW
    assert cache_swa_k.shape[2] == past and ts == CHUNK and tp % 512 == 0
    tm = 512

    l = 0
    c1 = 4 * RET_W
    c2 = c1 + SWA_Q_W
    c3 = c2 + 2 * SWA_KV_W
    wret = w_in[l, :, :c1].astype(BF16)
    n_r = N_GROUPS + N_EXPERTS
    wr = jnp.zeros((D_MODEL, LANES), F32).at[:, :N_GROUPS].set(router_group_w[l]).at[:, N_GROUPS:n_r].set(
        router_expert_w[l]).astype(BF16)
    br = jnp.zeros((1, LANES), F32).at[0, :N_GROUPS].set(router_group_b[l]).at[0, N_GROUPS:n_r].set(
        router_expert_b[l])
    g1 = norm1_g[l].reshape(1, D_MODEL)
    g2 = norm2_g[l].reshape(1, D_MODEL)
    gn = ret_gn_g[l].reshape(1, RET_W)
    gf = final_norm_g.reshape(1, D_MODEL)
    sinks = swa_sinks[l]

    c_all = jnp.concatenate([c_prompt, c_sample], axis=0)
    mod = _ada(c_all, ada_w[l], ada_b[l]).reshape(bp + bs, N_MOD, D_MODEL)
    mod_p, mod_s = mod[:bp], mod[bp:]

    cos_p, sin_p = _rope_tables(jnp.arange(tp))
    cos_s, sin_s = _rope_tables(PAST_LEN + jnp.arange(ts))
    rep = tm // ts
    cos_s, sin_s = jnp.tile(cos_s, (rep, 1)), jnp.tile(sin_s, (rep, 1))

    xp = x_prompt.reshape(bp * tp, D_MODEL)
    xs = x_sample.reshape(bs * ts, D_MODEL)

    lc_p = 128
    whole = lambda w: (w.reshape(-1, w.shape[-1]), [(0, w.shape[-1])])
    ret_p, h_p, (wsq, wkv, wbg, wrb, wsb, wo) = _inproj_ret(
        xp, mod_p, g1, cos_p, sin_p, wret, seq_len=tp, tm=tm, lc=lc_p,
        casts=[(w_in[l], [(c1, c2), (c2, c3), (c3, w_in.shape[2])]),
               whole(w_ret_branch[l]), whole(w_swa_branch[l]), whole(w_out[l])])
    sq_p, kv_p, gate_p, (w1, w3, w2) = _inproj_rest(
        h_p, wsq, wkv, wbg, tm=tm, casts=[whole(expert_w1[l]), whole(expert_w3[l]), whole(expert_w2[l])])
    w1, w3, w2 = (w.reshape(e.shape[1:]) for w, e in zip((w1, w3, w2), (expert_w1, expert_w3, expert_w2)))
    ret_s, h_s, _ = _inproj_ret(xs, mod_s, g1, cos_s, sin_s, wret, seq_len=ts, tm=tm, lc=ts)
    sq_s, kv_s, gate_s, _ = _inproj_rest(h_s, wsq, wkv, wbg, tm=tm)

    s0_p = jnp.zeros((bp, RET_HEADS, RET_DK, RET_DV), F32)
    r_p, state_p = _retention(ret_p, s0_p, gn, n_seq=bp, seq_len=tp, lc=lc_p, n_sub=2)
    r_s, state_s = _retention(ret_s, cache_ret_state[l].astype(F32), gn, n_seq=bs, seq_len=ts, lc=ts, n_sub=1)

    n_q = 8
    nc_p = tp // CHUNK
    ns_p = nc_p // n_q
    prev_map = lambda back, colblk: (lambda b, s: (b * nc_p + jnp.maximum(s * n_q - back, 0), colblk))
    own_map = lambda colblk: (lambda b, s: (b * ns_p + s, colblk))
    o_p = _swa(sinks, sq_p, [kv_p] * 3, [kv_p] * 3,
               [prev_map(2, 0), prev_map(1, 0), own_map(0)], [prev_map(2, 1), prev_map(1, 1), own_map(1)],
               n_seq=bp, nc=ns_p, masked=True, n_q=n_q)
    ck = cache_swa_k[l].reshape(bs * past, SWA_KV_W)
    cv = cache_swa_v[l].reshape(bs * past, SWA_KV_W)
    cmap = lambda blk: (lambda b, c: (2 * b + blk, 0))
    o_s = _swa(sinks, sq_s, [ck, ck, kv_s], [cv, cv, kv_s],
               [cmap(0), cmap(1), lambda b, c: (b, 0)], [cmap(0), cmap(1), lambda b, c: (b, 1)],
               n_seq=bs, nc=1, masked=False, n_q=1)

    tm_m = 256
    n_p = bp * tp
    x1_p, text_p, meta_p, cnt_p = _merge(xp, r_p, o_p, gate_p, mod_p, g2, wrb, wsb, wo, wr, br, seq_len=tp, tm=tm_m)
    x1_s, text_s, meta_s, cnt_s = _merge(xs, r_s, o_s, gate_s, mod_s, g2, wrb, wsb, wo, wr, br, seq_len=ts, tm=tm_m)

    tm_e = 256
    pos, free_slots, tile_ea, tile_eb, n_used, n_slots = _plan(cnt_p, cnt_s, meta_p, meta_s, tm=tm_e)
    pos_ext = jnp.concatenate([pos, free_slots], axis=0)
    xsorted = _scatter_rows(text_p, text_s, pos_ext, tb=1024)
    y_sorted = _moe(tile_ea, tile_eb, n_used, xsorted, w1, w3, w2, tm=tm_e, n_tiles=n_slots // tm_e)

    tm_f = 512
    out_p = _final(x1_p, y_sorted, pos[:n_p], mod_p, gf, seq_len=tp, tm=tm_f)
    out_s = _final(x1_s, y_sorted, pos[n_p:], mod_s, gf, seq_len=ts, tm=tm_f)

    y_prompt = out_p.reshape(bp, tp, D_MODEL)
    y_sample = out_s.reshape(bs, ts, D_MODEL)
    kvp = kv_p.reshape(bp, tp, 2 * SWA_KV_W)[:, tp - WINDOW:].reshape(bp, WINDOW, 2, SWA_KV_HEADS, SWA_HEAD_DIM)
    kvs = kv_s.reshape(bs, ts, 2, SWA_KV_HEADS, SWA_HEAD_DIM)
    k_s = jnp.concatenate([cache_swa_k[l].astype(F32), kvs[:, :, 0]], axis=1)[:, -WINDOW:]
    v_s = jnp.concatenate([cache_swa_v[l].astype(F32), kvs[:, :, 1]], axis=1)[:, -WINDOW:]
    return (y_prompt, y_sample, state_p[None], kvp[:, :, 0][None], kvp[:, :, 1][None],
            state_s[None], k_s[None], v_s[None])
```

```python
import functools
import math

import jax
import jax.numpy as jnp
from jax import lax
from jax.experimental import pallas as pl
from jax.experimental.pallas import tpu as pltpu

F32 = jnp.float32
BF16 = jnp.bfloat16

D_MODEL = 2048
CHUNK = 64
RET_HEADS = 8
RET_DK = 128
RET_DV = 128
RET_W = RET_HEADS * RET_DK
ROPE_BASE = 10000.0
SWA_Q_HEADS = 16
SWA_KV_HEADS = 2
SWA_GROUP = SWA_Q_HEADS // SWA_KV_HEADS
SWA_HEAD_DIM = 64
SWA_Q_W = SWA_Q_HEADS * SWA_HEAD_DIM
SWA_KV_W = SWA_KV_HEADS * SWA_HEAD_DIM
WINDOW = 128
PAST_LEN = 1024
N_GROUPS = 4
EXPERTS_PER_GROUP = 4
N_EXPERTS = 16
D_EXPERT = 512
N_MOD = 6
EPS = 1e-6
NEG_INF = -1e30
LOG2_E = math.log2(math.e)
N_PAIRS = 6
N_CLASSES = N_GROUPS * N_PAIRS

LANES = 128
VMEM_LIMIT = 56 * 1024 * 1024

RET_LOG_GAMMA = tuple(math.log1p(-(2.0 ** (-5.0 - h))) for h in range(RET_HEADS))


def _cparams(sem):
    return pltpu.CompilerParams(dimension_semantics=sem, vmem_limit_bytes=VMEM_LIMIT)


def _sigmoid(x):
    return 0.5 * jnp.tanh(0.5 * x) + 0.5


def _resident(shape):
    nd = len(shape)
    return pl.BlockSpec(shape, lambda *_: (0,) * nd, pipeline_mode=pl.Buffered(1))


def _ada_kernel(c_ref, w_ref, b_ref, o_ref):
    c = c_ref[...]
    a = c * jax.nn.sigmoid(c)
    w = w_ref[...]

    def split(v):
        hi = v.astype(BF16)
        return hi, (v - hi.astype(F32)).astype(BF16)

    a_hi, a_lo = split(a)
    w_hi, w_lo = split(w)
    dot = functools.partial(jnp.dot, preferred_element_type=F32)
    o_ref[...] = dot(a_hi, w_hi) + (dot(a_lo, w_hi) + dot(a_hi, w_lo)) + b_ref[...]


def _ada(c_all, ada_w, ada_b):
    nb = c_all.shape[0]
    n_out = ada_w.shape[1]
    tn = 1024
    return pl.pallas_call(
        _ada_kernel,
        grid=(n_out // tn,),
        in_specs=[pl.BlockSpec((nb, D_MODEL), lambda j: (0, 0)),
                  pl.BlockSpec((D_MODEL, tn), lambda j: (0, j)),
                  pl.BlockSpec((1, tn), lambda j: (0, j))],
        out_specs=pl.BlockSpec((nb, tn), lambda j: (0, j)),
        out_shape=jax.ShapeDtypeStruct((nb, n_out), F32),
        compiler_params=_cparams(("arbitrary",)),
        name="ada",
    )(c_all, ada_w, ada_b.reshape(1, n_out))


def _modulated_norm(x, g, shift, scale, n_seq):
    tm = x.shape[0]
    ms = jnp.mean(x * x, axis=-1, keepdims=True)
    y3 = (x * lax.rsqrt(ms + EPS)).reshape(n_seq, tm // n_seq, D_MODEL)
    gain = g * (1.0 + scale)
    h = y3 * gain[:, None, :] + shift[:, None, :]
    return h.reshape(tm, D_MODEL)


COL_BLK = 1024


def _cast_plan(casts, n_steps):
    in_specs, out_specs, out_shapes, splits = [], [], [], []
    for w, ranges in casts:
        slab = w.shape[0] // n_steps
        assert slab * n_steps == w.shape[0] and slab % 16 == 0
        in_specs.append(pl.BlockSpec((slab, w.shape[1]), lambda i: (i, 0)))
        for c0, c1 in ranges:
            out_specs.append(pl.BlockSpec((slab, c1 - c0), lambda i: (i, 0)))
            out_shapes.append(jax.ShapeDtypeStruct((w.shape[0], c1 - c0), BF16))
        splits.append(tuple(ranges))
    return in_specs, out_specs, out_shapes, tuple(splits)


def _cast_slabs(in_refs, out_refs, splits):
    k = 0
    for src, ranges in zip(in_refs, splits):
        for c0, c1 in ranges:
            out_refs[k][...] = src[:, c0:c1].astype(BF16)
            k += 1


def _inproj_ret_kernel(x_ref, mod_ref, g1_ref, cos_ref, sin_ref, wret_ref, *rest, n_seq, lc, splits):
    n_in = len(splits)
    cast_in, (ret_ref, h_ref) = rest[:n_in], rest[n_in:n_in + 2]
    cast_out, (dq_scr, dk_scr) = rest[n_in + 2:-2], rest[-2:]
    _cast_slabs(cast_in, cast_out, splits)
    tm = x_ref.shape[0]

    @pl.when(pl.program_id(0) == 0)
    def _():
        e = ((lax.broadcasted_iota(jnp.int32, (tm, RET_DK), 0) % lc) + 1).astype(F32)
        for hh in range(RET_HEADS):
            dq_scr[hh] = jnp.exp(e * RET_LOG_GAMMA[hh])
            dk_scr[hh] = jnp.exp(-e * RET_LOG_GAMMA[hh]) * (RET_DK ** -0.5)

    h = _modulated_norm(x_ref[...], g1_ref[...], mod_ref[:, 0, :], mod_ref[:, 1, :], n_seq)
    h_ref[...] = h.astype(BF16)
    cos = cos_ref[...]
    sin = sin_ref[...]
    assert COL_BLK == RET_W
    for blk in range(4):
        c0 = blk * COL_BLK
        acc = jnp.dot(h_ref[...], wret_ref[:, c0:c0 + COL_BLK], preferred_element_type=F32)
        if blk >= 2:
            ret_ref[:, c0:c0 + COL_BLK] = acc.astype(BF16)
            continue
        dec_scr = dq_scr if blk == 0 else dk_scr
        for hh in range(RET_HEADS):
            a = acc[:, hh * RET_DK:(hh + 1) * RET_DK]
            r = a * cos + pltpu.roll(a, RET_DK // 2, 1) * sin
            ret_ref[:, c0 + hh * RET_DK:c0 + (hh + 1) * RET_DK] = (r * dec_scr[hh]).astype(BF16)


def _inproj_rest_kernel(h_ref, wsq_ref, wkv_ref, wbg_ref, *rest, splits):
    n_in = len(splits)
    cast_in, (sq_ref, kv_ref, gate_ref), cast_out = rest[:n_in], rest[n_in:n_in + 3], rest[n_in + 3:]
    _cast_slabs(cast_in, cast_out, splits)
    sq_ref[...] = jnp.dot(h_ref[...], wsq_ref[...], preferred_element_type=F32).astype(BF16)
    kv_ref[...] = jnp.dot(h_ref[...], wkv_ref[...], preferred_element_type=F32)
    for blk in range(2 * D_MODEL // COL_BLK):
        c0 = blk * COL_BLK
        acc = jnp.dot(h_ref[...], wbg_ref[:, c0:c0 + COL_BLK], preferred_element_type=F32)
        gate_ref[:, c0:c0 + COL_BLK] = _sigmoid(acc).astype(BF16)


def _inproj_ret(x2d, mod3, g1, cos_t, sin_t, wret, *, seq_len, tm, lc, casts=()):
    R = x2d.shape[0]
    if seq_len >= tm:
        n_seq, tps = 1, seq_len // tm
        mod_map = lambda i: (i // tps, 0, 0)
        tab_map = lambda i: (i % tps, 0)
    else:
        n_seq = tm // seq_len
        mod_map = lambda i: (i, 0, 0)
        tab_map = lambda i: (0, 0)
    row = lambda w: pl.BlockSpec((tm, w), lambda i: (i, 0))
    assert tm % lc == 0
    n_steps = R // tm
    c_in, c_out, c_shapes, splits = _cast_plan(casts, n_steps)
    outs = pl.pallas_call(
        functools.partial(_inproj_ret_kernel, n_seq=n_seq, lc=lc, splits=splits),
        grid=(n_steps,),
        in_specs=[row(D_MODEL),
                  pl.BlockSpec((n_seq, N_MOD, D_MODEL), mod_map),
                  _resident((1, D_MODEL)),
                  pl.BlockSpec((tm, LANES), tab_map),
                  pl.BlockSpec((tm, LANES), tab_map),
                  _resident((D_MODEL, 4 * RET_W))] + c_in,
        out_specs=[row(4 * RET_W), row(D_MODEL)] + c_out,
        out_shape=[jax.ShapeDtypeStruct((R, 4 * RET_W), BF16),
                   jax.ShapeDtypeStruct((R, D_MODEL), BF16)] + c_shapes,
        scratch_shapes=[pltpu.VMEM((RET_HEADS, tm, RET_DK), F32), pltpu.VMEM((RET_HEADS, tm, RET_DK), F32)],
        compiler_params=_cparams(("arbitrary",)),
        name="inproj_ret",
    )(x2d, mod3, g1, cos_t, sin_t, wret, *[w for w, _ in casts])
    return outs[0], outs[1], outs[2:]


def _inproj_rest(h, wsq, wkv, wbg, *, tm, casts=()):
    R = h.shape[0]
    n_steps = R // tm
    row = lambda w: pl.BlockSpec((tm, w), lambda i: (i, 0))
    c_in, c_out, c_shapes, splits = _cast_plan(casts, n_steps)
    outs = pl.pallas_call(
        functools.partial(_inproj_rest_kernel, splits=splits),
        grid=(n_steps,),
        in_specs=[row(D_MODEL),
                  _resident((D_MODEL, SWA_Q_W)),
                  _resident((D_MODEL, 2 * SWA_KV_W)),
                  _resident((D_MODEL, 2 * D_MODEL))] + c_in,
        out_specs=[row(SWA_Q_W), row(2 * SWA_KV_W), row(2 * D_MODEL)] + c_out,
        out_shape=[jax.ShapeDtypeStruct((R, SWA_Q_W), BF16),
                   jax.ShapeDtypeStruct((R, 2 * SWA_KV_W), F32),
                   jax.ShapeDtypeStruct((R, 2 * D_MODEL), BF16)] + c_shapes,
        compiler_params=_cparams(("parallel",)),
        name="inproj_rest",
    )(h, wsq, wkv, wbg, *[w for w, _ in casts])
    return outs[0], outs[1], outs[2], outs[3:]


def _ret_kernel(blk_ref, s0_ref, gn_ref, r_ref, sout_ref, s_scr, *, lc, n_sub):
    c = pl.program_id(1)

    @pl.when(c == 0)
    def _():
        s_scr[...] = s0_ref[0]

    causal = (lax.broadcasted_iota(jnp.int32, (lc, lc), 0) >= lax.broadcasted_iota(jnp.int32, (lc, lc), 1))
    nt = (((1,), (1,)), ((), ()))
    tn = (((0,), (0,)), ((), ()))
    heads = range(RET_HEADS)
    cols = [slice(h * RET_DK, (h + 1) * RET_DK) for h in heads]
    for sub in range(n_sub):
        rows = slice(sub * lc, (sub + 1) * lc)
        part = lambda p, h: blk_ref[rows, p * RET_W + h * RET_DK:p * RET_W + (h + 1) * RET_DK]
        scores = [lax.dot_general(part(0, h), part(1, h), nt, preferred_element_type=F32) for h in heads]
        masked = [jnp.where(causal, s, 0.0).astype(BF16) for s in scores]
        states = [s_scr[h] for h in heads]
        outs = [jnp.dot(masked[h], part(2, h), preferred_element_type=F32)
                + jnp.dot(part(0, h), states[h].astype(BF16), preferred_element_type=F32) for h in heads]
        for h in heads:
            kv = lax.dot_general(part(1, h), part(2, h), tn, preferred_element_type=F32)
            s_scr[h] = math.exp(lc * RET_LOG_GAMMA[h]) * (states[h] + kv)
        for h in heads:
            o = outs[h]
            mu = jnp.mean(o, axis=-1, keepdims=True)
            d = o - mu
            var = jnp.mean(d * d, axis=-1, keepdims=True)
            on = d * lax.rsqrt(var + EPS) * gn_ref[:, cols[h]]
            g = part(3, h).astype(F32)
            r_ref[rows, cols[h]] = (on * (g * _sigmoid(g))).astype(BF16)

    @pl.when(c == pl.num_programs(1) - 1)
    def _():
        sout_ref[0] = s_scr[...]


def _retention(ret_all, s0, gn_g, *, n_seq, seq_len, lc, n_sub):
    R = ret_all.shape[0]
    rows = lc * n_sub
    nc = seq_len // rows
    st_spec = pl.BlockSpec((1, RET_HEADS, RET_DK, RET_DV), lambda b, c: (b, 0, 0, 0))
    return pl.pallas_call(
        functools.partial(_ret_kernel, lc=lc, n_sub=n_sub),
        grid=(n_seq, nc),
        in_specs=[pl.BlockSpec((rows, 4 * RET_W), lambda b, c: (b * nc + c, 0)),
                  st_spec,
                  _resident((1, RET_W))],
        out_specs=[pl.BlockSpec((rows, RET_W), lambda b, c: (b * nc + c, 0)), st_spec],
        out_shape=[jax.ShapeDtypeStruct((R, RET_W), BF16),
                   jax.ShapeDtypeStruct((n_seq, RET_HEADS, RET_DK, RET_DV), F32)],
        scratch_shapes=[pltpu.VMEM((RET_HEADS, RET_DK, RET_DV), F32)],
        compiler_params=_cparams(("parallel", "arbitrary")),
        name="retention",
    )(ret_all, s0, gn_g)


KEYS = WINDOW + CHUNK
KPAD = 256


def _swa_kernel(sink_ref, q_ref, k2_ref, k1_ref, k0_ref, v2_ref, v1_ref, v0_ref, o_ref, *, masked, n_q):
    j = pl.program_id(1)
    kall = jnp.concatenate([k2_ref[...], k1_ref[...], k0_ref[...]], axis=0) * (
        SWA_HEAD_DIM ** -0.5 * LOG2_E)
    vall = jnp.concatenate([v2_ref[...], v1_ref[...], v0_ref[...]], axis=0)
    lane = lax.broadcasted_iota(jnp.int32, kall.shape, 1)
    zpad = jnp.zeros((KPAD - KEYS, LANES), BF16)

    def lane_halves(win, h):
        rolled = pltpu.roll(win, SWA_HEAD_DIM, 1)
        lo_src, hi_src = (win, rolled) if h == 0 else (rolled, win)
        return (jnp.where(lane < SWA_HEAD_DIM, lo_src, 0.0).astype(BF16),
                jnp.where(lane >= SWA_HEAD_DIM, hi_src, 0.0).astype(BF16))

    col = lax.broadcasted_iota(jnp.int32, (1, KPAD), 1)
    n_pairs = SWA_GROUP // 2
    rows = n_pairs * CHUNK
    row = lax.broadcasted_iota(jnp.int32, (rows, 1), 0)
    out_lane = lax.broadcasted_iota(jnp.int32, (rows, LANES), 1)
    nt = (((1,), (1,)), ((), ()))
    for h in range(SWA_KV_HEADS):
        k_lo, k_hi = lane_halves(kall, h)
        v_lo, v_hi = lane_halves(vall, h)
        base = h * SWA_GROUP * SWA_HEAD_DIM
        sinks = []
        for half in range(2):
            sink = jnp.zeros((rows, 1), F32)
            for p in range(n_pairs):
                sink = jnp.where(row // CHUNK == p, sink_ref[h * SWA_GROUP + 2 * p + half], sink)
            sinks.append(sink * LOG2_E)
        for u in range(n_q):
            r0 = u * CHUNK
            c = j * n_q + u
            if masked:
                first_ok = jnp.where(c >= 2, 0, jnp.where(c == 1, CHUNK, 2 * CHUNK))
                ok = (col >= first_ok) & (col < KEYS)
            else:
                ok = col < KEYS
            kk = jnp.concatenate([k_lo[r0:r0 + KEYS], zpad, k_hi[r0:r0 + KEYS], zpad], axis=0)
            vv = jnp.concatenate([v_lo[r0:r0 + KEYS], zpad, v_hi[r0:r0 + KEYS], zpad], axis=0)
            q4 = jnp.concatenate([q_ref[r0:r0 + CHUNK, base + p * LANES: base + (p + 1) * LANES]
                                  for p in range(n_pairs)], axis=0)
            s = lax.dot_general(q4, kk, nt, preferred_element_type=F32)
            ps, invs = [], []
            for half in range(2):
                sh = jnp.where(ok, s[:, half * KPAD:(half + 1) * KPAD], NEG_INF)
                m = jnp.maximum(jnp.max(sh, axis=-1, keepdims=True), sinks[half])
                p_half = jnp.exp2(sh - m)
                den = jnp.sum(p_half, axis=-1, keepdims=True) + jnp.exp2(sinks[half] - m)
                ps.append(p_half.astype(BF16))
                invs.append(1.0 / den)
            pv = jnp.dot(jnp.concatenate(ps, axis=1), vv, preferred_element_type=F32)
            o = pv * jnp.where(out_lane < SWA_HEAD_DIM, invs[0], invs[1])
            for p in range(n_pairs):
                o_ref[r0:r0 + CHUNK, base + p * LANES: base + (p + 1) * LANES] = (
                    o[p * CHUNK:(p + 1) * CHUNK].astype(BF16))


def _swa(sinks, sq, k_arrs, v_arrs, k_maps, v_maps, *, n_seq, nc, masked, n_q):
    R = sq.shape[0]
    kv_rows = (CHUNK, CHUNK, n_q * CHUNK)
    kv_specs = [pl.BlockSpec((kv_rows[t % 3], SWA_KV_W), m) for t, m in enumerate((*k_maps, *v_maps))]
    return pl.pallas_call(
        functools.partial(_swa_kernel, masked=masked, n_q=n_q),
        grid=(n_seq, nc),
        in_specs=[pl.BlockSpec(memory_space=pltpu.SMEM),
                  pl.BlockSpec((n_q * CHUNK, SWA_Q_W), lambda b, c: (b * nc + c, 0)),
                  *kv_specs],
        out_specs=pl.BlockSpec((n_q * CHUNK, SWA_Q_W), lambda b, c: (b * nc + c, 0)),
        out_shape=jax.ShapeDtypeStruct((R, SWA_Q_W), BF16),
        compiler_params=_cparams(("parallel", "arbitrary")),
        name="swa",
    )(sinks, sq, *k_arrs, *v_arrs)


def _route(logits):
    tm = logits.shape[0]
    lane = lax.broadcasted_iota(jnp.int32, (tm, LANES), 1)
    is_g = lane < N_GROUPS
    gl = jnp.where(is_g, logits, NEG_INF)
    gmax = jnp.max(gl, axis=-1, keepdims=True)
    gidx = jnp.min(jnp.where(gl == gmax, lane, LANES), axis=-1, keepdims=True)
    gsum = jnp.sum(jnp.where(is_g, jnp.exp(gl - gmax), 0.0), axis=-1, keepdims=True)
    g_w = 1.0 / gsum
    base = N_GROUPS + EXPERTS_PER_GROUP * gidx
    el = jnp.where((lane >= base) & (lane < base + EXPERTS_PER_GROUP), logits, NEG_INF)
    v1 = jnp.max(el, axis=-1, keepdims=True)
    i1 = jnp.min(jnp.where(el == v1, lane, LANES), axis=-1, keepdims=True)
    el2 = jnp.where(lane == i1, NEG_INF, el)
    v2 = jnp.max(el2, axis=-1, keepdims=True)
    i2 = jnp.min(jnp.where(el2 == v2, lane, LANES), axis=-1, keepdims=True)
    e2 = jnp.exp(v2 - v1)
    den = 1.0 + e2
    w1 = g_w / den
    w2 = g_w * e2 / den
    l1 = i1 - base
    l2 = i2 - base
    first_lo = l1 < l2
    la = jnp.where(first_lo, l1, l2)
    lb = jnp.where(first_lo, l2, l1)
    wa = jnp.where(first_lo, w1, w2)
    wb = jnp.where(first_lo, w2, w1)
    pair = jnp.where(la == 0, lb - 1, jnp.where(la == 1, jnp.where(lb == 3, 3, 4), 5))
    swapped = la == 2
    w_slot_a = jnp.where(swapped, wb, wa)
    w_slot_b = jnp.where(swapped, wa, wb)
    cls = (gidx * N_PAIRS + pair).astype(F32)
    return jnp.where(lane == 0, w_slot_a, jnp.where(lane == 1, w_slot_b, jnp.where(lane == 2, cls, 0.0)))


ROW_W = D_MODEL + LANES


def _merge_kernel(x_ref, r_ref, o_ref, gate_ref, mod_ref, modp_ref, g2_ref, wrb_ref, wsb_ref, wout_ref, wr_ref,
                  br_ref, x1_ref, text_ref, meta_ref, cnt_ref, x1_scr, *, n_seq):
    i = pl.program_id(0)
    n = pl.num_programs(0) - 1
    tm = x_ref.shape[0]

    def matmul_stage():
        g_r = gate_ref[:, :D_MODEL].astype(F32)
        g_s = gate_ref[:, D_MODEL:].astype(F32)
        merged = (g_r * jnp.dot(r_ref[...], wrb_ref[...], preferred_element_type=F32)
                  + g_s * jnp.dot(o_ref[...], wsb_ref[...], preferred_element_type=F32))
        mix = jnp.dot(merged.astype(BF16), wout_ref[...], preferred_element_type=F32)
        gt1 = mod_ref[:, 2, :]
        x1 = (x_ref[...].reshape(n_seq, tm // n_seq, D_MODEL) + gt1[:, None, :]
              * mix.reshape(n_seq, tm // n_seq, D_MODEL)).reshape(tm, D_MODEL)
        x1_ref[...] = x1
        x1_scr[...] = x1

    def vector_stage():
        t = _modulated_norm(x1_scr[...], g2_ref[...], modp_ref[:, 3, :], modp_ref[:, 4, :], n_seq)
        logits = jnp.dot(t.astype(BF16), wr_ref[...], preferred_element_type=F32) + br_ref[...]
        meta = _route(logits)
        text_ref[:, :D_MODEL] = t
        text_ref[:, D_MODEL:] = meta
        meta_ref[...] = meta
        lane = lax.broadcasted_iota(jnp.int32, (tm, LANES), 1).astype(F32)
        cnt_ref[...] += jnp.sum(jnp.where(meta[:, 2:3] == lane, 1.0, 0.0), axis=0, keepdims=True)

    @pl.when(i == 0)
    def _():
        cnt_ref[...] = jnp.zeros_like(cnt_ref)
        matmul_stage()

    @pl.when((i > 0) & (i < n))
    def _():
        vector_stage()
        matmul_stage()

    @pl.when(i == n)
    def _():
        vector_stage()


def _merge(x2d, r, o_swa, gates, mod3, g2, wrb, wsb, wout, wr, br, *, seq_len, tm):
    R = x2d.shape[0]
    n = R // tm
    if seq_len >= tm:
        n_seq, tps = 1, seq_len // tm
        seq_of = lambda t: t // tps
    else:
        n_seq = tm // seq_len
        seq_of = lambda t: t
    cur = lambda i: jnp.minimum(i, n - 1)
    prev = lambda i: jnp.maximum(i - 1, 0)
    row = lambda w: pl.BlockSpec((tm, w), lambda i: (cur(i), 0))
    row_prev = lambda w: pl.BlockSpec((tm, w), lambda i: (prev(i), 0))
    return pl.pallas_call(
        functools.partial(_merge_kernel, n_seq=n_seq),
        grid=(n + 1,),
        in_specs=[row(D_MODEL), row(RET_W), row(SWA_Q_W), row(2 * D_MODEL),
                  pl.BlockSpec((n_seq, N_MOD, D_MODEL), lambda i: (seq_of(cur(i)), 0, 0)),
                  pl.BlockSpec((n_seq, N_MOD, D_MODEL), lambda i: (seq_of(prev(i)), 0, 0)),
                  _resident((1, D_MODEL)),
                  _resident((RET_W, D_MODEL)), _resident((SWA_Q_W, D_MODEL)), _resident((D_MODEL, D_MODEL)),
                  _resident((D_MODEL, LANES)), _resident((1, LANES))],
        out_specs=[row(D_MODEL), row_prev(ROW_W), row_prev(LANES), pl.BlockSpec((8, LANES), lambda i: (0, 0))],
        out_shape=[jax.ShapeDtypeStruct((R, D_MODEL), F32),
                   jax.ShapeDtypeStruct((R, ROW_W), F32),
                   jax.ShapeDtypeStruct((R, LANES), F32),
                   jax.ShapeDtypeStruct((8, LANES), F32)],
        scratch_shapes=[pltpu.VMEM((tm, D_MODEL), F32)],
        compiler_params=_cparams(("arbitrary",)),
        name="merge",
    )(x2d, r, o_swa, gates, mod3, mod3, g2, wrb, wsb, wout, wr, br)


PLAN_BLK = 2048
TILE_ROWS = 256


def _plan_kernel(cnt_a_ref, cnt_b_ref, meta_a_ref, meta_b_ref, pos_ref, tile_ref, pad_ref, offs_scr, carry_scr,
                 tri_scr, *, tm, nb_a):
    b = pl.program_id(0)
    blk = meta_a_ref.shape[0]
    lane = lax.broadcasted_iota(jnp.int32, (blk, LANES), 1)
    cls_col = jnp.where(b < nb_a, meta_a_ref[:, 2:3], meta_b_ref[:, 2:3])
    oh = jnp.where(cls_col == lane.astype(F32), 1.0, 0.0)

    @pl.when(b == 0)
    def _():
        ri = lax.broadcasted_iota(jnp.int32, (blk, blk), 0)
        ci = lax.broadcasted_iota(jnp.int32, (blk, blk), 1)
        tri_scr[...] = jnp.where(ci <= ri, 1.0, 0.0).astype(BF16)
        cnt = cnt_a_ref[...] + cnt_b_ref[...]
        ptiles = jnp.floor((cnt + (tm - 1)) * (1.0 / tm))
        ri = lax.broadcasted_iota(jnp.int32, (LANES, LANES), 0)
        ci = lax.broadcasted_iota(jnp.int32, (LANES, LANES), 1)
        before = jnp.where(ri < ci, 1.0, 0.0).astype(BF16)
        offs = jnp.dot(ptiles.astype(BF16), before, preferred_element_type=F32) * tm
        offs_scr[...] = offs
        carry_scr[...] = jnp.zeros_like(carry_scr)
        padded = ptiles * tm
        ends = offs + padded
        tl = lax.broadcasted_iota(jnp.int32, (TILE_ROWS, LANES), 1)
        tstart = lax.broadcasted_iota(jnp.int32, (TILE_ROWS, LANES), 0).astype(F32) * tm
        tcls = jnp.sum(jnp.where((ends[0:1, :] <= tstart) & (tl < N_CLASSES), 1.0, 0.0), axis=1, keepdims=True)
        tcls = jnp.minimum(tcls, N_CLASSES - 1.0)
        total = jnp.max(ends[0:1, :], axis=1, keepdims=True)
        grp = (jnp.where(tcls >= N_PAIRS, 1.0, 0.0) + jnp.where(tcls >= 2 * N_PAIRS, 1.0, 0.0)
               + jnp.where(tcls >= 3 * N_PAIRS, 1.0, 0.0))
        pair = tcls - N_PAIRS * grp
        la = jnp.where(pair < 3, 0.0, jnp.where(pair < 5, 1.0, 3.0))
        lb = jnp.where(pair == 0, 1.0, jnp.where((pair == 1) | (pair >= 4), 2.0, 3.0))
        ea = EXPERTS_PER_GROUP * grp + la
        eb = EXPERTS_PER_GROUP * grp + lb
        n_used = total * (1.0 / tm)
        tile_ref[...] = jnp.where(tl == 0, ea, jnp.where(tl == 1, eb, jnp.where(tl == 2, n_used, 0.0))
                                  ).astype(jnp.int32)
        npad = padded - cnt
        pstart = jnp.dot(npad.astype(BF16), before, preferred_element_type=F32)
        n_class_pad = jnp.sum(npad[0:1, :], axis=1, keepdims=True)
        rows = pad_ref.shape[0]
        v = (lax.broadcasted_iota(jnp.int32, (rows, LANES), 0) * LANES
             + lax.broadcasted_iota(jnp.int32, (rows, LANES), 1)).astype(F32)
        slot = jnp.where(v >= n_class_pad, total - n_class_pad + v, 0.0)
        for c in range(N_CLASSES):
            ps = pstart[0:1, c:c + 1]
            inside = (v >= ps) & (v < ps + npad[0:1, c:c + 1])
            slot = jnp.where(inside, offs[0:1, c:c + 1] + cnt[0:1, c:c + 1] - ps + v, slot)
        pad_ref[...] = slot.astype(jnp.int32)

    incl = jnp.dot(tri_scr[...], oh.astype(BF16), preferred_element_type=F32)
    base = offs_scr[0:1, :] + carry_scr[0:1, :]
    slot_oh = oh * (base + incl - oh)
    hi = jnp.floor(slot_oh * (1.0 / 256.0))
    lo = slot_oh - 256.0 * hi
    ones = jnp.ones((8, LANES), BF16)
    nt = (((1,), (1,)), ((), ()))
    pos = (256.0 * lax.dot_general(ones, hi.astype(BF16), nt, preferred_element_type=F32)
           + lax.dot_general(ones, lo.astype(BF16), nt, preferred_element_type=F32))
    pos_ref[...] = pos.astype(jnp.int32)
    carry_scr[...] += jnp.sum(oh, axis=0, keepdims=True)


def _plan(cnt_a, cnt_b, meta_a, meta_b, *, tm):
    na, nb = meta_a.shape[0], meta_b.shape[0]
    n = na + nb
    n_tiles = n // tm + N_CLASSES
    n_free = N_CLASSES * tm
    assert n_tiles <= TILE_ROWS and na % PLAN_BLK == 0 and nb % PLAN_BLK == 0 and n_free % LANES == 0
    n_slots = n_tiles * tm
    nb_a = na // PLAN_BLK
    nb_b = nb // PLAN_BLK
    pos2d, tile2d, pad2d = pl.pallas_call(
        functools.partial(_plan_kernel, tm=tm, nb_a=nb_a),
        grid=(nb_a + nb_b,),
        in_specs=[_resident((8, LANES)), _resident((8, LANES)),
                  pl.BlockSpec((PLAN_BLK, LANES), lambda b: (jnp.minimum(b, nb_a - 1), 0)),
                  pl.BlockSpec((PLAN_BLK, LANES), lambda b: (jnp.maximum(b - nb_a, 0), 0))],
        out_specs=[pl.BlockSpec((8, PLAN_BLK), lambda b: (0, b)),
                   pl.BlockSpec((TILE_ROWS, LANES), lambda b: (0, 0)),
                   pl.BlockSpec((n_free // LANES, LANES), lambda b: (0, 0))],
        out_shape=[jax.ShapeDtypeStruct((8, n), jnp.int32),
                   jax.ShapeDtypeStruct((TILE_ROWS, LANES), jnp.int32),
                   jax.ShapeDtypeStruct((n_free // LANES, LANES), jnp.int32)],
        scratch_shapes=[pltpu.VMEM((8, LANES), F32), pltpu.VMEM((8, LANES), F32),
                        pltpu.VMEM((PLAN_BLK, PLAN_BLK), BF16)],
        compiler_params=_cparams(("arbitrary",)),
        name="plan",
    )(cnt_a, cnt_b, meta_a, meta_b)
    return pos2d[0], pad2d.reshape(-1), tile2d[:n_tiles, 0], tile2d[:n_tiles, 1], tile2d[0:1, 2], n_slots


N_STAGE = 3


def _scatter_kernel(pos_ref, text_a_ref, text_b_ref, out_ref, stage, zero_scr, sem_in, sem_out, *, nb_a, nb_b):
    i = pl.program_id(0)
    n = pl.num_programs(0)
    n_tok = nb_a + nb_b
    tb = pos_ref.shape[2]
    slot = i % N_STAGE

    def load(step, s):
        def from_a():
            return pltpu.make_async_copy(text_a_ref.at[pl.ds(step * tb, tb), :], stage.at[s], sem_in.at[s])

        def from_b():
            return pltpu.make_async_copy(text_b_ref.at[pl.ds((step - nb_a) * tb, tb), :], stage.at[s], sem_in.at[s])

        return from_a, from_b

    def start_load(step, s):
        from_a, from_b = load(step, s)

        @pl.when(step < nb_a)
        def _():
            from_a().start()

        @pl.when((step >= nb_a) & (step < n_tok))
        def _():
            from_b().start()

    def wait_rows(s):
        pltpu.make_async_copy(stage.at[s], out_ref.at[pl.ds(0, tb), :], sem_out.at[s]).wait()

    def scatter_rows(src_row):
        def body(r, carry):
            dst = pos_ref[0, 0, r]
            pltpu.make_async_copy(src_row(r), out_ref.at[pl.ds(dst, 1), :], sem_out.at[slot]).start()
            return carry

        lax.fori_loop(0, tb, body, 0, unroll=16)

    @pl.when(i == 0)
    def _():
        zero_scr[...] = jnp.zeros_like(zero_scr)
        start_load(i, slot)

    @pl.when(i >= 2)
    def _():
        wait_rows((i + 1) % N_STAGE)

    start_load(i + 1, (i + 1) % N_STAGE)

    @pl.when(i < n_tok)
    def _():
        pltpu.make_async_copy(text_a_ref.at[pl.ds(0, tb), :], stage.at[slot], sem_in.at[slot]).wait()
        scatter_rows(lambda r: stage.at[slot, pl.ds(r, 1), :])

    @pl.when(i >= n_tok)
    def _():
        scatter_rows(lambda r: zero_scr.at[pl.ds(0, 1), :])

    @pl.when(i == n - 1)
    def _():
        wait_rows((i + 2) % N_STAGE)
        wait_rows(slot)


def _scatter_rows(text_a, text_b, pos_ext, *, tb):
    nb_a = text_a.shape[0] // tb
    nb_b = text_b.shape[0] // tb
    n_steps = pos_ext.shape[0] // tb
    assert n_steps >= 2 and nb_a >= 1 and nb_b >= 1
    return pl.pallas_call(
        functools.partial(_scatter_kernel, nb_a=nb_a, nb_b=nb_b),
        grid=(n_steps,),
        in_specs=[pl.BlockSpec((1, 1, tb), lambda i: (i, 0, 0), memory_space=pltpu.SMEM),
                  pl.BlockSpec(memory_space=pl.ANY),
                  pl.BlockSpec(memory_space=pl.ANY)],
        out_specs=pl.BlockSpec(memory_space=pl.ANY),
        out_shape=jax.ShapeDtypeStruct((pos_ext.shape[0], ROW_W), F32),
        scratch_shapes=[pltpu.VMEM((N_STAGE, tb, ROW_W), F32), pltpu.VMEM((8, ROW_W), F32),
                        pltpu.SemaphoreType.DMA((N_STAGE,)), pltpu.SemaphoreType.DMA((N_STAGE,))],
        compiler_params=_cparams(("arbitrary",)),
        name="scatter_rows",
    )(pos_ext.reshape(n_steps, 1, tb), text_a, text_b)


def _moe_kernel(ea_ref, eb_ref, nused_ref, xs_ref, w1a_ref, w3a_ref, w2a_ref, w1b_ref, w3b_ref, w2b_ref, y_ref):
    i = pl.program_id(0)

    @pl.when(i >= nused_ref[0])
    def _():
        y_ref[...] = jnp.zeros_like(y_ref)

    @pl.when(i < nused_ref[0])
    def _():
        x = xs_ref[:, :D_MODEL].astype(BF16)
        wa = xs_ref[:, D_MODEL:D_MODEL + 1]
        wb = xs_ref[:, D_MODEL + 1:D_MODEL + 2]

        def hidden(w1_ref, w3_ref, gate):
            a = jnp.dot(x, w1_ref[0], preferred_element_type=F32)
            b = jnp.dot(x, w3_ref[0], preferred_element_type=F32)
            return (a * _sigmoid(a) * b * gate).astype(BF16)

        ha = hidden(w1a_ref, w3a_ref, wa)
        hb = hidden(w1b_ref, w3b_ref, wb)
        y_ref[...] = (jnp.dot(ha, w2a_ref[0], preferred_element_type=F32)
                      + jnp.dot(hb, w2b_ref[0], preferred_element_type=F32))


def _moe(tile_ea, tile_eb, n_used, xs, w1, w3, w2, *, tm, n_tiles):
    last = lambda i, nu: jnp.minimum(i, nu[0] - 1)
    wa_map = lambda i, ea, eb, nu: (ea[last(i, nu)], 0, 0)
    wb_map = lambda i, ea, eb, nu: (eb[last(i, nu)], 0, 0)
    row_map = lambda i, ea, eb, nu: (last(i, nu), 0)
    up = (1, D_MODEL, D_EXPERT)
    down = (1, D_EXPERT, D_MODEL)
    grid_spec = pltpu.PrefetchScalarGridSpec(
        num_scalar_prefetch=3,
        grid=(n_tiles,),
        in_specs=[pl.BlockSpec((tm, D_MODEL + LANES), row_map),
                  pl.BlockSpec(up, wa_map), pl.BlockSpec(up, wa_map), pl.BlockSpec(down, wa_map),
                  pl.BlockSpec(up, wb_map), pl.BlockSpec(up, wb_map), pl.BlockSpec(down, wb_map)],
        out_specs=pl.BlockSpec((tm, D_MODEL), lambda i, ea, eb, nu: (i, 0)),
    )
    return pl.pallas_call(
        _moe_kernel,
        grid_spec=grid_spec,
        out_shape=jax.ShapeDtypeStruct((n_tiles * tm, D_MODEL), F32),
        compiler_params=_cparams(("arbitrary",)),
        name="moe",
    )(tile_ea, tile_eb, n_used, xs, w1, w3, w2, w1, w3, w2)


def _final_kernel(pos_ref, pos_next_ref, x1_ref, mod_ref, g_ref, ys_ref, o_ref, ybuf, sem, *, n_seq):
    i = pl.program_id(0)
    n = pl.num_programs(0)
    tm = x1_ref.shape[0]
    slot = i % 2

    def start_gather(idx_ref, s):
        def body(r, carry):
            src = idx_ref[0, 0, r]
            pltpu.make_async_copy(ys_ref.at[pl.ds(src, 1), :], ybuf.at[s, pl.ds(r, 1), :], sem.at[s]).start()
            return carry

        lax.fori_loop(0, tm, body, 0, unroll=16)

    @pl.when(i == 0)
    def _():
        start_gather(pos_ref, 0)

    @pl.when(i + 1 < n)
    def _():
        start_gather(pos_next_ref, 1 - slot)

    pltpu.make_async_copy(ys_ref.at[pl.ds(0, tm), :], ybuf.at[slot], sem.at[slot]).wait()
    gt2 = mod_ref[:, 5, :]
    x2 = (x1_ref[...].reshape(n_seq, tm // n_seq, D_MODEL)
          + gt2[:, None, :] * ybuf[slot].reshape(n_seq, tm // n_seq, D_MODEL)).reshape(tm, D_MODEL)
    ms = jnp.mean(x2 * x2, axis=-1, keepdims=True)
    o_ref[...] = x2 * lax.rsqrt(ms + EPS) * g_ref[...]


def _final(x1, y_sorted, pos, mod3, gf, *, seq_len, tm):
    R = x1.shape[0]
    if seq_len >= tm:
        n_seq, tps = 1, seq_len // tm
        mod_map = lambda i: (i // tps, 0, 0)
    else:
        n_seq = tm // seq_len
        mod_map = lambda i: (i, 0, 0)
    n = R // tm
    row = pl.BlockSpec((tm, D_MODEL), lambda i: (i, 0))
    pos3 = pos.reshape(n, 1, tm)
    return pl.pallas_call(
        functools.partial(_final_kernel, n_seq=n_seq),
        grid=(n,),
        in_specs=[pl.BlockSpec((1, 1, tm), lambda i: (i, 0, 0), memory_space=pltpu.SMEM),
                  pl.BlockSpec((1, 1, tm), lambda i: (jnp.minimum(i + 1, n - 1), 0, 0), memory_space=pltpu.SMEM),
                  row, pl.BlockSpec((n_seq, N_MOD, D_MODEL), mod_map), _resident((1, D_MODEL)),
                  pl.BlockSpec(memory_space=pl.ANY)],
        out_specs=row,
        out_shape=jax.ShapeDtypeStruct((R, D_MODEL), F32),
        scratch_shapes=[pltpu.VMEM((2, tm, D_MODEL), F32), pltpu.SemaphoreType.DMA((2,))],
        compiler_params=_cparams(("arbitrary",)),
        name="final",
    )(pos3, pos3, x1, mod3, gf, y_sorted)


def _rope_tables(pos):
    half = RET_DK // 2
    inv = ROPE_BASE ** (-jnp.arange(half, dtype=F32) / half)
    ang = pos.astype(F32)[:, None] * inv[None, :]
    cos = jnp.cos(ang)
    sin = jnp.sin(ang)
    return jnp.concatenate([cos, cos], axis=-1), jnp.concatenate([-sin, sin], axis=-1)


def kernel(x_prompt, x_sample, cache_ret_state, cache_swa_k, cache_swa_v, c_prompt, c_sample,
           norm1_g, norm2_g, ada_w, ada_b, w_in, ret_gn_g, swa_sinks, w_ret_branch, w_swa_branch, w_out,
           router_group_w, router_group_b, router_expert_w, router_expert_b,
           expert_w1, expert_w3, expert_w2, final_norm_g):
    depth = w_in.shape[0]
    assert depth == 1
    bp, tp, _ = x_prompt.shape
    bs, ts, _ = x_sample.shape
    past = WINDOW
    assert cache_swa_k.shape[2] == past and ts == CHUNK and tp % 512 == 0
    tm = 512

    l = 0
    c1 = 4 * RET_W
    c2 = c1 + SWA_Q_W
    c3 = c2 + 2 * SWA_KV_W
    wret = w_in[l, :, :c1].astype(BF16)
    n_r = N_GROUPS + N_EXPERTS
    wr = jnp.zeros((D_MODEL, LANES), F32).at[:, :N_GROUPS].set(router_group_w[l]).at[:, N_GROUPS:n_r].set(
        router_expert_w[l]).astype(BF16)
    br = jnp.zeros((1, LANES), F32).at[0, :N_GROUPS].set(router_group_b[l]).at[0, N_GROUPS:n_r].set(
        router_expert_b[l])
    g1 = norm1_g[l].reshape(1, D_MODEL)
    g2 = norm2_g[l].reshape(1, D_MODEL)
    gn = ret_gn_g[l].reshape(1, RET_W)
    gf = final_norm_g.reshape(1, D_MODEL)
    sinks = swa_sinks[l]

    c_all = jnp.concatenate([c_prompt, c_sample], axis=0)
    mod = _ada(c_all, ada_w[l], ada_b[l]).reshape(bp + bs, N_MOD, D_MODEL)
    mod_p, mod_s = mod[:bp], mod[bp:]

    cos_p, sin_p = _rope_tables(jnp.arange(tp))
    cos_s, sin_s = _rope_tables(PAST_LEN + jnp.arange(ts))
    rep = tm // ts
    cos_s, sin_s = jnp.tile(cos_s, (rep, 1)), jnp.tile(sin_s, (rep, 1))

    xp = x_prompt.reshape(bp * tp, D_MODEL)
    xs = x_sample.reshape(bs * ts, D_MODEL)

    lc_p = 128
    whole = lambda w: (w.reshape(-1, w.shape[-1]), [(0, w.shape[-1])])
    ret_p, h_p, (wsq, wkv, wbg, wrb, wsb, wo) = _inproj_ret(
        xp, mod_p, g1, cos_p, sin_p, wret, seq_len=tp, tm=tm, lc=lc_p,
        casts=[(w_in[l], [(c1, c2), (c2, c3), (c3, w_in.shape[2])]),
               whole(w_ret_branch[l]), whole(w_swa_branch[l]), whole(w_out[l])])
    sq_p, kv_p, gate_p, (w1, w3, w2) = _inproj_rest(
        h_p, wsq, wkv, wbg, tm=tm, casts=[whole(expert_w1[l]), whole(expert_w3[l]), whole(expert_w2[l])])
    w1, w3, w2 = (w.reshape(e.shape[1:]) for w, e in zip((w1, w3, w2), (expert_w1, expert_w3, expert_w2)))
    ret_s, h_s, _ = _inproj_ret(xs, mod_s, g1, cos_s, sin_s, wret, seq_len=ts, tm=tm, lc=ts)
    sq_s, kv_s, gate_s, _ = _inproj_rest(h_s, wsq, wkv, wbg, tm=tm)

    s0_p = jnp.zeros((bp, RET_HEADS, RET_DK, RET_DV), F32)
    r_p, state_p = _retention(ret_p, s0_p, gn, n_seq=bp, seq_len=tp, lc=lc_p, n_sub=2)
    r_s, state_s = _retention(ret_s, cache_ret_state[l].astype(F32), gn, n_seq=bs, seq_len=ts, lc=ts, n_sub=1)

    n_q = 32
    nc_p = tp // CHUNK
    ns_p = nc_p // n_q
    assert ns_p * n_q == nc_p
    prev_map = lambda back, colblk: (lambda b, s: (b * nc_p + jnp.maximum(s * n_q - back, 0), colblk))
    own_map = lambda colblk: (lambda b, s: (b * ns_p + s, colblk))
    o_p = _swa(sinks, sq_p, [kv_p] * 3, [kv_p] * 3,
               [prev_map(2, 0), prev_map(1, 0), own_map(0)], [prev_map(2, 1), prev_map(1, 1), own_map(1)],
               n_seq=bp, nc=ns_p, masked=True, n_q=n_q)
    ck = cache_swa_k[l].reshape(bs * past, SWA_KV_W)
    cv = cache_swa_v[l].reshape(bs * past, SWA_KV_W)
    cmap = lambda blk: (lambda b, c: (2 * b + blk, 0))
    o_s = _swa(sinks, sq_s, [ck, ck, kv_s], [cv, cv, kv_s],
               [cmap(0), cmap(1), lambda b, c: (b, 0)], [cmap(0), cmap(1), lambda b, c: (b, 1)],
               n_seq=bs, nc=1, masked=False, n_q=1)

    tm_m = 256
    n_p = bp * tp
    x1_p, text_p, meta_p, cnt_p = _merge(xp, r_p, o_p, gate_p, mod_p, g2, wrb, wsb, wo, wr, br, seq_len=tp, tm=tm_m)
    x1_s, text_s, meta_s, cnt_s = _merge(xs, r_s, o_s, gate_s, mod_s, g2, wrb, wsb, wo, wr, br, seq_len=ts, tm=tm_m)

    tm_e = 256
    pos, free_slots, tile_ea, tile_eb, n_used, n_slots = _plan(cnt_p, cnt_s, meta_p, meta_s, tm=tm_e)
    pos_ext = jnp.concatenate([pos, free_slots], axis=0)
    xsorted = _scatter_rows(text_p, text_s, pos_ext, tb=1024)
    y_sorted = _moe(tile_ea, tile_eb, n_used, xsorted, w1, w3, w2, tm=tm_e, n_tiles=n_slots // tm_e)

    tm_f = 512
    out_p = _final(x1_p, y_sorted, pos[:n_p], mod_p, gf, seq_len=tp, tm=tm_f)
    out_s = _final(x1_s, y_sorted, pos[n_p:], mod_s, gf, seq_len=ts, tm=tm_f)

    y_prompt = out_p.reshape(bp, tp, D_MODEL)
    y_sample = out_s.reshape(bs, ts, D_MODEL)
    kvp = kv_p.reshape(bp, tp, 2 * SWA_KV_W)[:, tp - WINDOW:].reshape(bp, WINDOW, 2, SWA_KV_HEADS, SWA_HEAD_DIM)
    kvs = kv_s.reshape(bs, ts, 2, SWA_KV_HEADS, SWA_HEAD_DIM)
    k_s = jnp.concatenate([cache_swa_k[l].astype(F32), kvs[:, :, 0]], axis=1)[:, -WINDOW:]
    v_s = jnp.concatenate([cache_swa_v[l].astype(F32), kvs[:, :, 1]], axis=1)[:, -WINDOW:]
    return (y_prompt, y_sample, state_p[None], kvp[:, :, 0][None], kvp[:, :, 1][None],
            state_s[None], k_s[None], v_s[None])
```

```python
import functools
import math

import jax
import jax.numpy as jnp
from jax import lax
from jax.experimental import pallas as pl
from jax.experimental.pallas import tpu as pltpu

F32 = jnp.float32
BF16 = jnp.bfloat16

D_MODEL = 2048
CHUNK = 64
RET_HEADS = 8
RET_DK = 128
RET_DV = 128
RET_W = RET_HEADS * RET_DK
ROPE_BASE = 10000.0
SWA_Q_HEADS = 16
SWA_KV_HEADS = 2
SWA_GROUP = SWA_Q_HEADS // SWA_KV_HEADS
SWA_HEAD_DIM = 64
SWA_Q_W = SWA_Q_HEADS * SWA_HEAD_DIM
SWA_KV_W = SWA_KV_HEADS * SWA_HEAD_DIM
WINDOW = 128
PAST_LEN = 1024
N_GROUPS = 4
EXPERTS_PER_GROUP = 4
N_EXPERTS = 16
D_EXPERT = 512
N_MOD = 6
EPS = 1e-6
NEG_INF = -1e30
LOG2_E = math.log2(math.e)
N_PAIRS = 6
N_CLASSES = N_GROUPS * N_PAIRS

LANES = 128
VMEM_LIMIT = 56 * 1024 * 1024

RET_LOG_GAMMA = tuple(math.log1p(-(2.0 ** (-5.0 - h))) for h in range(RET_HEADS))


def _cparams(sem):
    return pltpu.CompilerParams(dimension_semantics=sem, vmem_limit_bytes=VMEM_LIMIT)


def _sigmoid(x):
    return 0.5 * jnp.tanh(0.5 * x) + 0.5


def _resident(shape):
    nd = len(shape)
    return pl.BlockSpec(shape, lambda *_: (0,) * nd, pipeline_mode=pl.Buffered(1))


def _ada_kernel(c_ref, w_ref, b_ref, o_ref):
    c = c_ref[...]
    a = c * jax.nn.sigmoid(c)
    w = w_ref[...]

    def split(v):
        hi = v.astype(BF16)
        return hi, (v - hi.astype(F32)).astype(BF16)

    a_hi, a_lo = split(a)
    w_hi, w_lo = split(w)
    dot = functools.partial(jnp.dot, preferred_element_type=F32)
    o_ref[...] = dot(a_hi, w_hi) + (dot(a_lo, w_hi) + dot(a_hi, w_lo)) + b_ref[...]


def _ada(c_all, ada_w, ada_b):
    nb = c_all.shape[0]
    n_out = ada_w.shape[1]
    tn = 1024
    return pl.pallas_call(
        _ada_kernel,
        grid=(n_out // tn,),
        in_specs=[pl.BlockSpec((nb, D_MODEL), lambda j: (0, 0)),
                  pl.BlockSpec((D_MODEL, tn), lambda j: (0, j)),
                  pl.BlockSpec((1, tn), lambda j: (0, j))],
        out_specs=pl.BlockSpec((nb, tn), lambda j: (0, j)),
        out_shape=jax.ShapeDtypeStruct((nb, n_out), F32),
        compiler_params=_cparams(("arbitrary",)),
        name="ada",
    )(c_all, ada_w, ada_b.reshape(1, n_out))


def _modulated_norm(x, g, shift, scale, n_seq):
    tm = x.shape[0]
    ms = jnp.mean(x * x, axis=-1, keepdims=True)
    y3 = (x * lax.rsqrt(ms + EPS)).reshape(n_seq, tm // n_seq, D_MODEL)
    gain = g * (1.0 + scale)
    h = y3 * gain[:, None, :] + shift[:, None, :]
    return h.reshape(tm, D_MODEL)


COL_BLK = 1024


def _cast_plan(casts, n_steps):
    in_specs, out_specs, out_shapes, splits = [], [], [], []
    for w, ranges in casts:
        slab = w.shape[0] // n_steps
        assert slab * n_steps == w.shape[0] and slab % 16 == 0
        in_specs.append(pl.BlockSpec((slab, w.shape[1]), lambda i: (i, 0)))
        for c0, c1 in ranges:
            out_specs.append(pl.BlockSpec((slab, c1 - c0), lambda i: (i, 0)))
            out_shapes.append(jax.ShapeDtypeStruct((w.shape[0], c1 - c0), BF16))
        splits.append(tuple(ranges))
    return in_specs, out_specs, out_shapes, tuple(splits)


def _cast_slabs(in_refs, out_refs, splits):
    k = 0
    for src, ranges in zip(in_refs, splits):
        for c0, c1 in ranges:
            out_refs[k][...] = src[:, c0:c1].astype(BF16)
            k += 1


def _inproj_ret_kernel(x_ref, mod_ref, g1_ref, cos_ref, sin_ref, wret_ref, *rest, n_seq, lc, splits):
    n_in = len(splits)
    cast_in, (ret_ref, h_ref) = rest[:n_in], rest[n_in:n_in + 2]
    cast_out, (dq_scr, dk_scr) = rest[n_in + 2:-2], rest[-2:]
    _cast_slabs(cast_in, cast_out, splits)
    tm = x_ref.shape[0]

    @pl.when(pl.program_id(0) == 0)
    def _():
        e = ((lax.broadcasted_iota(jnp.int32, (tm, RET_DK), 0) % lc) + 1).astype(F32)
        for hh in range(RET_HEADS):
            dq_scr[hh] = jnp.exp(e * RET_LOG_GAMMA[hh])
            dk_scr[hh] = jnp.exp(-e * RET_LOG_GAMMA[hh]) * (RET_DK ** -0.5)

    h = _modulated_norm(x_ref[...], g1_ref[...], mod_ref[:, 0, :], mod_ref[:, 1, :], n_seq)
    h_ref[...] = h.astype(BF16)
    cos = cos_ref[...]
    sin = sin_ref[...]
    assert COL_BLK == RET_W
    for blk in range(4):
        c0 = blk * COL_BLK
        acc = jnp.dot(h_ref[...], wret_ref[:, c0:c0 + COL_BLK], preferred_element_type=F32)
        if blk >= 2:
            ret_ref[:, c0:c0 + COL_BLK] = acc.astype(BF16)
            continue
        dec_scr = dq_scr if blk == 0 else dk_scr
        for hh in range(RET_HEADS):
            a = acc[:, hh * RET_DK:(hh + 1) * RET_DK]
            r = a * cos + pltpu.roll(a, RET_DK // 2, 1) * sin
            ret_ref[:, c0 + hh * RET_DK:c0 + (hh + 1) * RET_DK] = (r * dec_scr[hh]).astype(BF16)


def _inproj_rest_kernel(h_ref, wsq_ref, wkv_ref, wbg_ref, *rest, splits):
    n_in = len(splits)
    cast_in, (sq_ref, kv_ref, gate_ref), cast_out = rest[:n_in], rest[n_in:n_in + 3], rest[n_in + 3:]
    _cast_slabs(cast_in, cast_out, splits)
    sq_ref[...] = jnp.dot(h_ref[...], wsq_ref[...], preferred_element_type=F32).astype(BF16)
    kv_ref[...] = jnp.dot(h_ref[...], wkv_ref[...], preferred_element_type=F32)
    for blk in range(2 * D_MODEL // COL_BLK):
        c0 = blk * COL_BLK
        acc = jnp.dot(h_ref[...], wbg_ref[:, c0:c0 + COL_BLK], preferred_element_type=F32)
        gate_ref[:, c0:c0 + COL_BLK] = _sigmoid(acc).astype(BF16)


def _inproj_ret(x2d, mod3, g1, cos_t, sin_t, wret, *, seq_len, tm, lc, casts=()):
    R = x2d.shape[0]
    if seq_len >= tm:
        n_seq, tps = 1, seq_len // tm
        mod_map = lambda i: (i // tps, 0, 0)
        tab_map = lambda i: (i % tps, 0)
    else:
        n_seq = tm // seq_len
        mod_map = lambda i: (i, 0, 0)
        tab_map = lambda i: (0, 0)
    row = lambda w: pl.BlockSpec((tm, w), lambda i: (i, 0))
    assert tm % lc == 0
    n_steps = R // tm
    c_in, c_out, c_shapes, splits = _cast_plan(casts, n_steps)
    outs = pl.pallas_call(
        functools.partial(_inproj_ret_kernel, n_seq=n_seq, lc=lc, splits=splits),
        grid=(n_steps,),
        in_specs=[row(D_MODEL),
                  pl.BlockSpec((n_seq, N_MOD, D_MODEL), mod_map),
                  _resident((1, D_MODEL)),
                  pl.BlockSpec((tm, LANES), tab_map),
                  pl.BlockSpec((tm, LANES), tab_map),
                  _resident((D_MODEL, 4 * RET_W))] + c_in,
        out_specs=[row(4 * RET_W), row(D_MODEL)] + c_out,
        out_shape=[jax.ShapeDtypeStruct((R, 4 * RET_W), BF16),
                   jax.ShapeDtypeStruct((R, D_MODEL), BF16)] + c_shapes,
        scratch_shapes=[pltpu.VMEM((RET_HEADS, tm, RET_DK), F32), pltpu.VMEM((RET_HEADS, tm, RET_DK), F32)],
        compiler_params=_cparams(("arbitrary",)),
        name="inproj_ret",
    )(x2d, mod3, g1, cos_t, sin_t, wret, *[w for w, _ in casts])
    return outs[0], outs[1], outs[2:]


def _inproj_rest(h, wsq, wkv, wbg, *, tm, casts=()):
    R = h.shape[0]
    n_steps = R // tm
    row = lambda w: pl.BlockSpec((tm, w), lambda i: (i, 0))
    c_in, c_out, c_shapes, splits = _cast_plan(casts, n_steps)
    outs = pl.pallas_call(
        functools.partial(_inproj_rest_kernel, splits=splits),
        grid=(n_steps,),
        in_specs=[row(D_MODEL),
                  _resident((D_MODEL, SWA_Q_W)),
                  _resident((D_MODEL, 2 * SWA_KV_W)),
                  _resident((D_MODEL, 2 * D_MODEL))] + c_in,
        out_specs=[row(SWA_Q_W), row(2 * SWA_KV_W), row(2 * D_MODEL)] + c_out,
        out_shape=[jax.ShapeDtypeStruct((R, SWA_Q_W), BF16),
                   jax.ShapeDtypeStruct((R, 2 * SWA_KV_W), F32),
                   jax.ShapeDtypeStruct((R, 2 * D_MODEL), BF16)] + c_shapes,
        compiler_params=_cparams(("parallel",)),
        name="inproj_rest",
    )(h, wsq, wkv, wbg, *[w for w, _ in casts])
    return outs[0], outs[1], outs[2], outs[3:]


def _ret_kernel(blk_ref, s0_ref, gn_ref, r_ref, sout_ref, s_scr, *, lc, n_sub):
    c = pl.program_id(1)

    @pl.when(c == 0)
    def _():
        s_scr[...] = s0_ref[0]

    causal = (lax.broadcasted_iota(jnp.int32, (lc, lc), 0) >= lax.broadcasted_iota(jnp.int32, (lc, lc), 1))
    nt = (((1,), (1,)), ((), ()))
    tn = (((0,), (0,)), ((), ()))
    heads = range(RET_HEADS)
    cols = [slice(h * RET_DK, (h + 1) * RET_DK) for h in heads]
    for sub in range(n_sub):
        rows = slice(sub * lc, (sub + 1) * lc)
        part = lambda p, h: blk_ref[rows, p * RET_W + h * RET_DK:p * RET_W + (h + 1) * RET_DK]
        scores = [lax.dot_general(part(0, h), part(1, h), nt, preferred_element_type=F32) for h in heads]
        masked = [jnp.where(causal, s, 0.0).astype(BF16) for s in scores]
        states = [s_scr[h] for h in heads]
        outs = [jnp.dot(masked[h], part(2, h), preferred_element_type=F32)
                + jnp.dot(part(0, h), states[h].astype(BF16), preferred_element_type=F32) for h in heads]
        for h in heads:
            kv = lax.dot_general(part(1, h), part(2, h), tn, preferred_element_type=F32)
            s_scr[h] = math.exp(lc * RET_LOG_GAMMA[h]) * (states[h] + kv)
        for h in heads:
            o = outs[h]
            mu = jnp.mean(o, axis=-1, keepdims=True)
            d = o - mu
            var = jnp.mean(d * d, axis=-1, keepdims=True)
            on = d * lax.rsqrt(var + EPS) * gn_ref[:, cols[h]]
            g = part(3, h).astype(F32)
            r_ref[rows, cols[h]] = (on * (g * _sigmoid(g))).astype(BF16)

    @pl.when(c == pl.num_programs(1) - 1)
    def _():
        sout_ref[0] = s_scr[...]


def _retention(ret_all, s0, gn_g, *, n_seq, seq_len, lc, n_sub):
    R = ret_all.shape[0]
    rows = lc * n_sub
    nc = seq_len // rows
    st_spec = pl.BlockSpec((1, RET_HEADS, RET_DK, RET_DV), lambda b, c: (b, 0, 0, 0))
    return pl.pallas_call(
        functools.partial(_ret_kernel, lc=lc, n_sub=n_sub),
        grid=(n_seq, nc),
        in_specs=[pl.BlockSpec((rows, 4 * RET_W), lambda b, c: (b * nc + c, 0)),
                  st_spec,
                  _resident((1, RET_W))],
        out_specs=[pl.BlockSpec((rows, RET_W), lambda b, c: (b * nc + c, 0)), st_spec],
        out_shape=[jax.ShapeDtypeStruct((R, RET_W), BF16),
                   jax.ShapeDtypeStruct((n_seq, RET_HEADS, RET_DK, RET_DV), F32)],
        scratch_shapes=[pltpu.VMEM((RET_HEADS, RET_DK, RET_DV), F32)],
        compiler_params=_cparams(("parallel", "arbitrary")),
        name="retention",
    )(ret_all, s0, gn_g)


KEYS = WINDOW + CHUNK
KPAD = 256


def _swa_kernel(sink_ref, q_ref, k2_ref, k1_ref, k0_ref, v2_ref, v1_ref, v0_ref, o_ref, *, masked, n_q):
    j = pl.program_id(1)
    kall = jnp.concatenate([k2_ref[...], k1_ref[...], k0_ref[...]], axis=0) * (
        SWA_HEAD_DIM ** -0.5 * LOG2_E)
    vall = jnp.concatenate([v2_ref[...], v1_ref[...], v0_ref[...]], axis=0)
    lane = lax.broadcasted_iota(jnp.int32, kall.shape, 1)
    zpad = jnp.zeros((KPAD - KEYS, LANES), BF16)

    def lane_halves(win, h):
        rolled = pltpu.roll(win, SWA_HEAD_DIM, 1)
        lo_src, hi_src = (win, rolled) if h == 0 else (rolled, win)
        return (jnp.where(lane < SWA_HEAD_DIM, lo_src, 0.0).astype(BF16),
                jnp.where(lane >= SWA_HEAD_DIM, hi_src, 0.0).astype(BF16))

    col = lax.broadcasted_iota(jnp.int32, (1, KPAD), 1)
    n_pairs = SWA_GROUP // 2
    rows = n_pairs * CHUNK
    row = lax.broadcasted_iota(jnp.int32, (rows, 1), 0)
    out_lane = lax.broadcasted_iota(jnp.int32, (rows, LANES), 1)
    nt = (((1,), (1,)), ((), ()))
    for h in range(SWA_KV_HEADS):
        k_lo, k_hi = lane_halves(kall, h)
        v_lo, v_hi = lane_halves(vall, h)
        base = h * SWA_GROUP * SWA_HEAD_DIM
        sinks = []
        for half in range(2):
            sink = jnp.zeros((rows, 1), F32)
            for p in range(n_pairs):
                sink = jnp.where(row // CHUNK == p, sink_ref[h * SWA_GROUP + 2 * p + half], sink)
            sinks.append(sink * LOG2_E)
        for u in range(n_q):
            r0 = u * CHUNK
            c = j * n_q + u
            if masked:
                first_ok = jnp.where(c >= 2, 0, jnp.where(c == 1, CHUNK, 2 * CHUNK))
                ok = (col >= first_ok) & (col < KEYS)
            else:
                ok = col < KEYS
            kk = jnp.concatenate([k_lo[r0:r0 + KEYS], zpad, k_hi[r0:r0 + KEYS], zpad], axis=0)
            vv = jnp.concatenate([v_lo[r0:r0 + KEYS], zpad, v_hi[r0:r0 + KEYS], zpad], axis=0)
            q4 = jnp.concatenate([q_ref[r0:r0 + CHUNK, base + p * LANES: base + (p + 1) * LANES]
                                  for p in range(n_pairs)], axis=0)
            s = lax.dot_general(q4, kk, nt, preferred_element_type=F32)
            ps, invs = [], []
            for half in range(2):
                sh = jnp.where(ok, s[:, half * KPAD:(half + 1) * KPAD], NEG_INF)
                m = jnp.maximum(jnp.max(sh, axis=-1, keepdims=True), sinks[half])
                p_half = jnp.exp2(sh - m)
                den = jnp.sum(p_half, axis=-1, keepdims=True) + jnp.exp2(sinks[half] - m)
                ps.append(p_half.astype(BF16))
                invs.append(1.0 / den)
            pv = jnp.dot(jnp.concatenate(ps, axis=1), vv, preferred_element_type=F32)
            o = pv * jnp.where(out_lane < SWA_HEAD_DIM, invs[0], invs[1])
            for p in range(n_pairs):
                o_ref[r0:r0 + CHUNK, base + p * LANES: base + (p + 1) * LANES] = (
                    o[p * CHUNK:(p + 1) * CHUNK].astype(BF16))


def _swa(sinks, sq, k_arrs, v_arrs, k_maps, v_maps, *, n_seq, nc, masked, n_q):
    R = sq.shape[0]
    kv_rows = (CHUNK, CHUNK, n_q * CHUNK)
    kv_specs = [pl.BlockSpec((kv_rows[t % 3], SWA_KV_W), m) for t, m in enumerate((*k_maps, *v_maps))]
    return pl.pallas_call(
        functools.partial(_swa_kernel, masked=masked, n_q=n_q),
        grid=(n_seq, nc),
        in_specs=[pl.BlockSpec(memory_space=pltpu.SMEM),
                  pl.BlockSpec((n_q * CHUNK, SWA_Q_W), lambda b, c: (b * nc + c, 0)),
                  *kv_specs],
        out_specs=pl.BlockSpec((n_q * CHUNK, SWA_Q_W), lambda b, c: (b * nc + c, 0)),
        out_shape=jax.ShapeDtypeStruct((R, SWA_Q_W), BF16),
        compiler_params=_cparams(("parallel", "arbitrary")),
        name="swa",
    )(sinks, sq, *k_arrs, *v_arrs)


def _route(logits):
    tm = logits.shape[0]
    lane = lax.broadcasted_iota(jnp.int32, (tm, LANES), 1)
    is_g = lane < N_GROUPS
    gl = jnp.where(is_g, logits, NEG_INF)
    gmax = jnp.max(gl, axis=-1, keepdims=True)
    gidx = jnp.min(jnp.where(gl == gmax, lane, LANES), axis=-1, keepdims=True)
    gsum = jnp.sum(jnp.where(is_g, jnp.exp(gl - gmax), 0.0), axis=-1, keepdims=True)
    g_w = 1.0 / gsum
    base = N_GROUPS + EXPERTS_PER_GROUP * gidx
    el = jnp.where((lane >= base) & (lane < base + EXPERTS_PER_GROUP), logits, NEG_INF)
    v1 = jnp.max(el, axis=-1, keepdims=True)
    i1 = jnp.min(jnp.where(el == v1, lane, LANES), axis=-1, keepdims=True)
    el2 = jnp.where(lane == i1, NEG_INF, el)
    v2 = jnp.max(el2, axis=-1, keepdims=True)
    i2 = jnp.min(jnp.where(el2 == v2, lane, LANES), axis=-1, keepdims=True)
    e2 = jnp.exp(v2 - v1)
    den = 1.0 + e2
    w1 = g_w / den
    w2 = g_w * e2 / den
    l1 = i1 - base
    l2 = i2 - base
    first_lo = l1 < l2
    la = jnp.where(first_lo, l1, l2)
    lb = jnp.where(first_lo, l2, l1)
    wa = jnp.where(first_lo, w1, w2)
    wb = jnp.where(first_lo, w2, w1)
    pair = jnp.where(la == 0, lb - 1, jnp.where(la == 1, jnp.where(lb == 3, 3, 4), 5))
    swapped = la == 2
    w_slot_a = jnp.where(swapped, wb, wa)
    w_slot_b = jnp.where(swapped, wa, wb)
    cls = (gidx * N_PAIRS + pair).astype(F32)
    return jnp.where(lane == 0, w_slot_a, jnp.where(lane == 1, w_slot_b, jnp.where(lane == 2, cls, 0.0)))


ROW_W = D_MODEL + LANES


def _merge_kernel(x_ref, r_ref, o_ref, gate_ref, mod_ref, modp_ref, g2_ref, wrb_ref, wsb_ref, wout_ref, wr_ref,
                  br_ref, x1_ref, text_ref, meta_ref, cnt_ref, x1_scr, *, n_seq):
    i = pl.program_id(0)
    n = pl.num_programs(0) - 1
    tm = x_ref.shape[0]

    def matmul_stage():
        g_r = gate_ref[:, :D_MODEL].astype(F32)
        g_s = gate_ref[:, D_MODEL:].astype(F32)
        merged = (g_r * jnp.dot(r_ref[...], wrb_ref[...], preferred_element_type=F32)
                  + g_s * jnp.dot(o_ref[...], wsb_ref[...], preferred_element_type=F32))
        mix = jnp.dot(merged.astype(BF16), wout_ref[...], preferred_element_type=F32)
        gt1 = mod_ref[:, 2, :]
        x1 = (x_ref[...].reshape(n_seq, tm // n_seq, D_MODEL) + gt1[:, None, :]
              * mix.reshape(n_seq, tm // n_seq, D_MODEL)).reshape(tm, D_MODEL)
        x1_ref[...] = x1
        x1_scr[...] = x1

    def vector_stage():
        t = _modulated_norm(x1_scr[...], g2_ref[...], modp_ref[:, 3, :], modp_ref[:, 4, :], n_seq)
        logits = jnp.dot(t.astype(BF16), wr_ref[...], preferred_element_type=F32) + br_ref[...]
        meta = _route(logits)
        text_ref[:, :D_MODEL] = t
        text_ref[:, D_MODEL:] = meta
        meta_ref[...] = meta
        lane = lax.broadcasted_iota(jnp.int32, (tm, LANES), 1).astype(F32)
        cnt_ref[...] += jnp.sum(jnp.where(meta[:, 2:3] == lane, 1.0, 0.0), axis=0, keepdims=True)

    @pl.when(i == 0)
    def _():
        cnt_ref[...] = jnp.zeros_like(cnt_ref)
        matmul_stage()

    @pl.when((i > 0) & (i < n))
    def _():
        vector_stage()
        matmul_stage()

    @pl.when(i == n)
    def _():
        vector_stage()


def _merge(x2d, r, o_swa, gates, mod3, g2, wrb, wsb, wout, wr, br, *, seq_len, tm):
    R = x2d.shape[0]
    n = R // tm
    if seq_len >= tm:
        n_seq, tps = 1, seq_len // tm
        seq_of = lambda t: t // tps
    else:
        n_seq = tm // seq_len
        seq_of = lambda t: t
    cur = lambda i: jnp.minimum(i, n - 1)
    prev = lambda i: jnp.maximum(i - 1, 0)
    row = lambda w: pl.BlockSpec((tm, w), lambda i: (cur(i), 0))
    row_prev = lambda w: pl.BlockSpec((tm, w), lambda i: (prev(i), 0))
    return pl.pallas_call(
        functools.partial(_merge_kernel, n_seq=n_seq),
        grid=(n + 1,),
        in_specs=[row(D_MODEL), row(RET_W), row(SWA_Q_W), row(2 * D_MODEL),
                  pl.BlockSpec((n_seq, N_MOD, D_MODEL), lambda i: (seq_of(cur(i)), 0, 0)),
                  pl.BlockSpec((n_seq, N_MOD, D_MODEL), lambda i: (seq_of(prev(i)), 0, 0)),
                  _resident((1, D_MODEL)),
                  _resident((RET_W, D_MODEL)), _resident((SWA_Q_W, D_MODEL)), _resident((D_MODEL, D_MODEL)),
                  _resident((D_MODEL, LANES)), _resident((1, LANES))],
        out_specs=[row(D_MODEL), row_prev(ROW_W), row_prev(LANES), pl.BlockSpec((8, LANES), lambda i: (0, 0))],
        out_shape=[jax.ShapeDtypeStruct((R, D_MODEL), F32),
                   jax.ShapeDtypeStruct((R, ROW_W), F32),
                   jax.ShapeDtypeStruct((R, LANES), F32),
                   jax.ShapeDtypeStruct((8, LANES), F32)],
        scratch_shapes=[pltpu.VMEM((tm, D_MODEL), F32)],
        compiler_params=_cparams(("arbitrary",)),
        name="merge",
    )(x2d, r, o_swa, gates, mod3, mod3, g2, wrb, wsb, wout, wr, br)


PLAN_BLK = 2048
TILE_ROWS = 256


def _plan_kernel(cnt_a_ref, cnt_b_ref, meta_a_ref, meta_b_ref, pos_ref, tile_ref, pad_ref, offs_scr, carry_scr,
                 tri_scr, *, tm, nb_a):
    b = pl.program_id(0)
    blk = meta_a_ref.shape[0]
    lane = lax.broadcasted_iota(jnp.int32, (blk, LANES), 1)
    cls_col = jnp.where(b < nb_a, meta_a_ref[:, 2:3], meta_b_ref[:, 2:3])
    oh = jnp.where(cls_col == lane.astype(F32), 1.0, 0.0)

    @pl.when(b == 0)
    def _():
        ri = lax.broadcasted_iota(jnp.int32, (blk, blk), 0)
        ci = lax.broadcasted_iota(jnp.int32, (blk, blk), 1)
        tri_scr[...] = jnp.where(ci <= ri, 1.0, 0.0).astype(BF16)
        cnt = cnt_a_ref[...] + cnt_b_ref[...]
        ptiles = jnp.floor((cnt + (tm - 1)) * (1.0 / tm))
        ri = lax.broadcasted_iota(jnp.int32, (LANES, LANES), 0)
        ci = lax.broadcasted_iota(jnp.int32, (LANES, LANES), 1)
        before = jnp.where(ri < ci, 1.0, 0.0).astype(BF16)
        offs = jnp.dot(ptiles.astype(BF16), before, preferred_element_type=F32) * tm
        offs_scr[...] = offs
        carry_scr[...] = jnp.zeros_like(carry_scr)
        padded = ptiles * tm
        ends = offs + padded
        tl = lax.broadcasted_iota(jnp.int32, (TILE_ROWS, LANES), 1)
        tstart = lax.broadcasted_iota(jnp.int32, (TILE_ROWS, LANES), 0).astype(F32) * tm
        tcls = jnp.sum(jnp.where((ends[0:1, :] <= tstart) & (tl < N_CLASSES), 1.0, 0.0), axis=1, keepdims=True)
        tcls = jnp.minimum(tcls, N_CLASSES - 1.0)
        total = jnp.max(ends[0:1, :], axis=1, keepdims=True)
        grp = (jnp.where(tcls >= N_PAIRS, 1.0, 0.0) + jnp.where(tcls >= 2 * N_PAIRS, 1.0, 0.0)
               + jnp.where(tcls >= 3 * N_PAIRS, 1.0, 0.0))
        pair = tcls - N_PAIRS * grp
        la = jnp.where(pair < 3, 0.0, jnp.where(pair < 5, 1.0, 3.0))
        lb = jnp.where(pair == 0, 1.0, jnp.where((pair == 1) | (pair >= 4), 2.0, 3.0))
        ea = EXPERTS_PER_GROUP * grp + la
        eb = EXPERTS_PER_GROUP * grp + lb
        n_used = total * (1.0 / tm)
        tile_ref[...] = jnp.where(tl == 0, ea, jnp.where(tl == 1, eb, jnp.where(tl == 2, n_used, 0.0))
                                  ).astype(jnp.int32)
        npad = padded - cnt
        pstart = jnp.dot(npad.astype(BF16), before, preferred_element_type=F32)
        n_class_pad = jnp.sum(npad[0:1, :], axis=1, keepdims=True)
        rows = pad_ref.shape[0]
        v = (lax.broadcasted_iota(jnp.int32, (rows, LANES), 0) * LANES
             + lax.broadcasted_iota(jnp.int32, (rows, LANES), 1)).astype(F32)
        slot = jnp.where(v >= n_class_pad, total - n_class_pad + v, 0.0)
        for c in range(N_CLASSES):
            ps = pstart[0:1, c:c + 1]
            inside = (v >= ps) & (v < ps + npad[0:1, c:c + 1])
            slot = jnp.where(inside, offs[0:1, c:c + 1] + cnt[0:1, c:c + 1] - ps + v, slot)
        pad_ref[...] = slot.astype(jnp.int32)

    incl = jnp.dot(tri_scr[...], oh.astype(BF16), preferred_element_type=F32)
    base = offs_scr[0:1, :] + carry_scr[0:1, :]
    slot_oh = oh * (base + incl - oh)
    hi = jnp.floor(slot_oh * (1.0 / 256.0))
    lo = slot_oh - 256.0 * hi
    ones = jnp.ones((8, LANES), BF16)
    nt = (((1,), (1,)), ((), ()))
    pos = (256.0 * lax.dot_general(ones, hi.astype(BF16), nt, preferred_element_type=F32)
           + lax.dot_general(ones, lo.astype(BF16), nt, preferred_element_type=F32))
    pos_ref[...] = pos.astype(jnp.int32)
    carry_scr[...] += jnp.sum(oh, axis=0, keepdims=True)


def _plan(cnt_a, cnt_b, meta_a, meta_b, *, tm):
    na, nb = meta_a.shape[0], meta_b.shape[0]
    n = na + nb
    n_tiles = n // tm + N_CLASSES
    n_free = N_CLASSES * tm
    assert n_tiles <= TILE_ROWS and na % PLAN_BLK == 0 and nb % PLAN_BLK == 0 and n_free % LANES == 0
    n_slots = n_tiles * tm
    nb_a = na // PLAN_BLK
    nb_b = nb // PLAN_BLK
    pos2d, tile2d, pad2d = pl.pallas_call(
        functools.partial(_plan_kernel, tm=tm, nb_a=nb_a),
        grid=(nb_a + nb_b,),
        in_specs=[_resident((8, LANES)), _resident((8, LANES)),
                  pl.BlockSpec((PLAN_BLK, LANES), lambda b: (jnp.minimum(b, nb_a - 1), 0)),
                  pl.BlockSpec((PLAN_BLK, LANES), lambda b: (jnp.maximum(b - nb_a, 0), 0))],
        out_specs=[pl.BlockSpec((8, PLAN_BLK), lambda b: (0, b)),
                   pl.BlockSpec((TILE_ROWS, LANES), lambda b: (0, 0)),
                   pl.BlockSpec((n_free // LANES, LANES), lambda b: (0, 0))],
        out_shape=[jax.ShapeDtypeStruct((8, n), jnp.int32),
                   jax.ShapeDtypeStruct((TILE_ROWS, LANES), jnp.int32),
                   jax.ShapeDtypeStruct((n_free // LANES, LANES), jnp.int32)],
        scratch_shapes=[pltpu.VMEM((8, LANES), F32), pltpu.VMEM((8, LANES), F32),
                        pltpu.VMEM((PLAN_BLK, PLAN_BLK), BF16)],
        compiler_params=_cparams(("arbitrary",)),
        name="plan",
    )(cnt_a, cnt_b, meta_a, meta_b)
    return pos2d[0], pad2d.reshape(-1), tile2d[:n_tiles, 0], tile2d[:n_tiles, 1], tile2d[0:1, 2], n_slots


N_STAGE = 3


def _scatter_kernel(pos_ref, text_a_ref, text_b_ref, out_ref, stage, zero_scr, sem_in, sem_out, *, nb_a, nb_b):
    i = pl.program_id(0)
    n = pl.num_programs(0)
    n_tok = nb_a + nb_b
    tb = pos_ref.shape[2]
    slot = i % N_STAGE

    def load(step, s):
        def from_a():
            return pltpu.make_async_copy(text_a_ref.at[pl.ds(step * tb, tb), :], stage.at[s], sem_in.at[s])

        def from_b():
            return pltpu.make_async_copy(text_b_ref.at[pl.ds((step - nb_a) * tb, tb), :], stage.at[s], sem_in.at[s])

        return from_a, from_b

    def start_load(step, s):
        from_a, from_b = load(step, s)

        @pl.when(step < nb_a)
        def _():
            from_a().start()

        @pl.when((step >= nb_a) & (step < n_tok))
        def _():
            from_b().start()

    def wait_rows(s):
        pltpu.make_async_copy(stage.at[s], out_ref.at[pl.ds(0, tb), :], sem_out.at[s]).wait()

    def scatter_rows(src_row):
        def body(g, carry):
            for prio in range(2):
                r = 2 * g + prio
                dst = pos_ref[0, 0, r]
                pltpu.make_async_copy(src_row(r), out_ref.at[pl.ds(dst, 1), :], sem_out.at[slot]).start(
                    priority=prio)
            return carry

        lax.fori_loop(0, tb // 2, body, 0, unroll=8)

    @pl.when(i == 0)
    def _():
        zero_scr[...] = jnp.zeros_like(zero_scr)
        start_load(i, slot)

    @pl.when(i >= 2)
    def _():
        wait_rows((i + 1) % N_STAGE)

    start_load(i + 1, (i + 1) % N_STAGE)

    @pl.when(i < n_tok)
    def _():
        pltpu.make_async_copy(text_a_ref.at[pl.ds(0, tb), :], stage.at[slot], sem_in.at[slot]).wait()
        scatter_rows(lambda r: stage.at[slot, pl.ds(r, 1), :])

    @pl.when(i >= n_tok)
    def _():
        scatter_rows(lambda r: zero_scr.at[pl.ds(0, 1), :])

    @pl.when(i == n - 1)
    def _():
        wait_rows((i + 2) % N_STAGE)
        wait_rows(slot)


def _scatter_rows(text_a, text_b, pos_ext, *, tb):
    nb_a = text_a.shape[0] // tb
    nb_b = text_b.shape[0] // tb
    n_steps = pos_ext.shape[0] // tb
    assert n_steps >= 2 and nb_a >= 1 and nb_b >= 1
    return pl.pallas_call(
        functools.partial(_scatter_kernel, nb_a=nb_a, nb_b=nb_b),
        grid=(n_steps,),
        in_specs=[pl.BlockSpec((1, 1, tb), lambda i: (i, 0, 0), memory_space=pltpu.SMEM),
                  pl.BlockSpec(memory_space=pl.ANY),
                  pl.BlockSpec(memory_space=pl.ANY)],
        out_specs=pl.BlockSpec(memory_space=pl.ANY),
        out_shape=jax.ShapeDtypeStruct((pos_ext.shape[0], ROW_W), F32),
        scratch_shapes=[pltpu.VMEM((N_STAGE, tb, ROW_W), F32), pltpu.VMEM((8, ROW_W), F32),
                        pltpu.SemaphoreType.DMA((N_STAGE,)), pltpu.SemaphoreType.DMA((N_STAGE,))],
        compiler_params=_cparams(("arbitrary",)),
        name="scatter_rows",
    )(pos_ext.reshape(n_steps, 1, tb), text_a, text_b)


def _moe_kernel(ea_ref, eb_ref, nused_ref, xs_ref, w1a_ref, w3a_ref, w2a_ref, w1b_ref, w3b_ref, w2b_ref, y_ref):
    i = pl.program_id(0)

    @pl.when(i >= nused_ref[0])
    def _():
        y_ref[...] = jnp.zeros_like(y_ref)

    @pl.when(i < nused_ref[0])
    def _():
        x = xs_ref[:, :D_MODEL].astype(BF16)
        wa = xs_ref[:, D_MODEL:D_MODEL + 1]
        wb = xs_ref[:, D_MODEL + 1:D_MODEL + 2]

        def hidden(w1_ref, w3_ref, gate):
            a = jnp.dot(x, w1_ref[0], preferred_element_type=F32)
            b = jnp.dot(x, w3_ref[0], preferred_element_type=F32)
            return (a * _sigmoid(a) * b * gate).astype(BF16)

        ha = hidden(w1a_ref, w3a_ref, wa)
        hb = hidden(w1b_ref, w3b_ref, wb)
        y_ref[...] = (jnp.dot(ha, w2a_ref[0], preferred_element_type=F32)
                      + jnp.dot(hb, w2b_ref[0], preferred_element_type=F32))


def _moe(tile_ea, tile_eb, n_used, xs, w1, w3, w2, *, tm, n_tiles):
    last = lambda i, nu: jnp.minimum(i, nu[0] - 1)
    wa_map = lambda i, ea, eb, nu: (ea[last(i, nu)], 0, 0)
    wb_map = lambda i, ea, eb, nu: (eb[last(i, nu)], 0, 0)
    row_map = lambda i, ea, eb, nu: (last(i, nu), 0)
    up = (1, D_MODEL, D_EXPERT)
    down = (1, D_EXPERT, D_MODEL)
    grid_spec = pltpu.PrefetchScalarGridSpec(
        num_scalar_prefetch=3,
        grid=(n_tiles,),
        in_specs=[pl.BlockSpec((tm, D_MODEL + LANES), row_map),
                  pl.BlockSpec(up, wa_map), pl.BlockSpec(up, wa_map), pl.BlockSpec(down, wa_map),
                  pl.BlockSpec(up, wb_map), pl.BlockSpec(up, wb_map), pl.BlockSpec(down, wb_map)],
        out_specs=pl.BlockSpec((tm, D_MODEL), lambda i, ea, eb, nu: (i, 0)),
    )
    return pl.pallas_call(
        _moe_kernel,
        grid_spec=grid_spec,
        out_shape=jax.ShapeDtypeStruct((n_tiles * tm, D_MODEL), F32),
        compiler_params=_cparams(("arbitrary",)),
        name="moe",
    )(tile_ea, tile_eb, n_used, xs, w1, w3, w2, w1, w3, w2)


def _final_kernel(pos_ref, pos_next_ref, x1_ref, mod_ref, g_ref, ys_ref, o_ref, ybuf, sem, *, n_seq):
    i = pl.program_id(0)
    n = pl.num_programs(0)
    tm = x1_ref.shape[0]
    slot = i % 2

    def start_gather(idx_ref, s):
        def body(g, carry):
            for prio in range(2):
                r = 2 * g + prio
                src = idx_ref[0, 0, r]
                pltpu.make_async_copy(ys_ref.at[pl.ds(src, 1), :], ybuf.at[s, pl.ds(r, 1), :], sem.at[s]).start(
                    priority=prio)
            return carry

        lax.fori_loop(0, tm // 2, body, 0, unroll=8)

    @pl.when(i == 0)
    def _():
        start_gather(pos_ref, 0)

    @pl.when(i + 1 < n)
    def _():
        start_gather(pos_next_ref, 1 - slot)

    pltpu.make_async_copy(ys_ref.at[pl.ds(0, tm), :], ybuf.at[slot], sem.at[slot]).wait()
    gt2 = mod_ref[:, 5, :]
    x2 = (x1_ref[...].reshape(n_seq, tm // n_seq, D_MODEL)
          + gt2[:, None, :] * ybuf[slot].reshape(n_seq, tm // n_seq, D_MODEL)).reshape(tm, D_MODEL)
    ms = jnp.mean(x2 * x2, axis=-1, keepdims=True)
    o_ref[...] = x2 * lax.rsqrt(ms + EPS) * g_ref[...]


def _final(x1, y_sorted, pos, mod3, gf, *, seq_len, tm):
    R = x1.shape[0]
    if seq_len >= tm:
        n_seq, tps = 1, seq_len // tm
        mod_map = lambda i: (i // tps, 0, 0)
    else:
        n_seq = tm // seq_len
        mod_map = lambda i: (i, 0, 0)
    n = R // tm
    row = pl.BlockSpec((tm, D_MODEL), lambda i: (i, 0))
    pos3 = pos.reshape(n, 1, tm)
    return pl.pallas_call(
        functools.partial(_final_kernel, n_seq=n_seq),
        grid=(n,),
        in_specs=[pl.BlockSpec((1, 1, tm), lambda i: (i, 0, 0), memory_space=pltpu.SMEM),
                  pl.BlockSpec((1, 1, tm), lambda i: (jnp.minimum(i + 1, n - 1), 0, 0), memory_space=pltpu.SMEM),
                  row, pl.BlockSpec((n_seq, N_MOD, D_MODEL), mod_map), _resident((1, D_MODEL)),
                  pl.BlockSpec(memory_space=pl.ANY)],
        out_specs=row,
        out_shape=jax.ShapeDtypeStruct((R, D_MODEL), F32),
        scratch_shapes=[pltpu.VMEM((2, tm, D_MODEL), F32), pltpu.SemaphoreType.DMA((2,))],
        compiler_params=_cparams(("arbitrary",)),
        name="final",
    )(pos3, pos3, x1, mod3, gf, y_sorted)


def _rope_tables(pos):
    half = RET_DK // 2
    inv = ROPE_BASE ** (-jnp.arange(half, dtype=F32) / half)
    ang = pos.astype(F32)[:, None] * inv[None, :]
    cos = jnp.cos(ang)
    sin = jnp.sin(ang)
    return jnp.concatenate([cos, cos], axis=-1), jnp.concatenate([-sin, sin], axis=-1)


def kernel(x_prompt, x_sample, cache_ret_state, cache_swa_k, cache_swa_v, c_prompt, c_sample,
           norm1_g, norm2_g, ada_w, ada_b, w_in, ret_gn_g, swa_sinks, w_ret_branch, w_swa_branch, w_out,
           router_group_w, router_group_b, router_expert_w, router_expert_b,
           expert_w1, expert_w3, expert_w2, final_norm_g):
    depth = w_in.shape[0]
    assert depth == 1
    bp, tp, _ = x_prompt.shape
    bs, ts, _ = x_sample.shape
    past = WINDOW
    assert cache_swa_k.shape[2] == past and ts == CHUNK and tp % 512 == 0
    tm = 512

    l = 0
    c1 = 4 * RET_W
    c2 = c1 + SWA_Q_W
    c3 = c2 + 2 * SWA_KV_W
    wret = w_in[l, :, :c1].astype(BF16)
    n_r = N_GROUPS + N_EXPERTS
    wr = jnp.zeros((D_MODEL, LANES), F32).at[:, :N_GROUPS].set(router_group_w[l]).at[:, N_GROUPS:n_r].set(
        router_expert_w[l]).astype(BF16)
    br = jnp.zeros((1, LANES), F32).at[0, :N_GROUPS].set(router_group_b[l]).at[0, N_GROUPS:n_r].set(
        router_expert_b[l])
    g1 = norm1_g[l].reshape(1, D_MODEL)
    g2 = norm2_g[l].reshape(1, D_MODEL)
    gn = ret_gn_g[l].reshape(1, RET_W)
    gf = final_norm_g.reshape(1, D_MODEL)
    sinks = swa_sinks[l]

    c_all = jnp.concatenate([c_prompt, c_sample], axis=0)
    mod = _ada(c_all, ada_w[l], ada_b[l]).reshape(bp + bs, N_MOD, D_MODEL)
    mod_p, mod_s = mod[:bp], mod[bp:]

    cos_p, sin_p = _rope_tables(jnp.arange(tp))
    cos_s, sin_s = _rope_tables(PAST_LEN + jnp.arange(ts))
    rep = tm // ts
    cos_s, sin_s = jnp.tile(cos_s, (rep, 1)), jnp.tile(sin_s, (rep, 1))

    xp = x_prompt.reshape(bp * tp, D_MODEL)
    xs = x_sample.reshape(bs * ts, D_MODEL)

    lc_p = 128
    whole = lambda w: (w.reshape(-1, w.shape[-1]), [(0, w.shape[-1])])
    ret_p, h_p, (wsq, wkv, wbg, wrb, wsb, wo) = _inproj_ret(
        xp, mod_p, g1, cos_p, sin_p, wret, seq_len=tp, tm=tm, lc=lc_p,
        casts=[(w_in[l], [(c1, c2), (c2, c3), (c3, w_in.shape[2])]),
               whole(w_ret_branch[l]), whole(w_swa_branch[l]), whole(w_out[l])])
    sq_p, kv_p, gate_p, (w1, w3, w2) = _inproj_rest(
        h_p, wsq, wkv, wbg, tm=tm, casts=[whole(expert_w1[l]), whole(expert_w3[l]), whole(expert_w2[l])])
    w1, w3, w2 = (w.reshape(e.shape[1:]) for w, e in zip((w1, w3, w2), (expert_w1, expert_w3, expert_w2)))
    ret_s, h_s, _ = _inproj_ret(xs, mod_s, g1, cos_s, sin_s, wret, seq_len=ts, tm=tm, lc=ts)
    sq_s, kv_s, gate_s, _ = _inproj_rest(h_s, wsq, wkv, wbg, tm=tm)

    s0_p = jnp.zeros((bp, RET_HEADS, RET_DK, RET_DV), F32)
    r_p, state_p = _retention(ret_p, s0_p, gn, n_seq=bp, seq_len=tp, lc=lc_p, n_sub=2)
    r_s, state_s = _retention(ret_s, cache_ret_state[l].astype(F32), gn, n_seq=bs, seq_len=ts, lc=ts, n_sub=1)

    n_q = 32
    nc_p = tp // CHUNK
    ns_p = nc_p // n_q
    assert ns_p * n_q == nc_p
    prev_map = lambda back, colblk: (lambda b, s: (b * nc_p + jnp.maximum(s * n_q - back, 0), colblk))
    own_map = lambda colblk: (lambda b, s: (b * ns_p + s, colblk))
    o_p = _swa(sinks, sq_p, [kv_p] * 3, [kv_p] * 3,
               [prev_map(2, 0), prev_map(1, 0), own_map(0)], [prev_map(2, 1), prev_map(1, 1), own_map(1)],
               n_seq=bp, nc=ns_p, masked=True, n_q=n_q)
    ck = cache_swa_k[l].reshape(bs * past, SWA_KV_W)
    cv = cache_swa_v[l].reshape(bs * past, SWA_KV_W)
    cmap = lambda blk: (lambda b, c: (2 * b + blk, 0))
    o_s = _swa(sinks, sq_s, [ck, ck, kv_s], [cv, cv, kv_s],
               [cmap(0), cmap(1), lambda b, c: (b, 0)], [cmap(0), cmap(1), lambda b, c: (b, 1)],
               n_seq=bs, nc=1, masked=False, n_q=1)

    tm_m = 256
    n_p = bp * tp
    x1_p, text_p, meta_p, cnt_p = _merge(xp, r_p, o_p, gate_p, mod_p, g2, wrb, wsb, wo, wr, br, seq_len=tp, tm=tm_m)
    x1_s, text_s, meta_s, cnt_s = _merge(xs, r_s, o_s, gate_s, mod_s, g2, wrb, wsb, wo, wr, br, seq_len=ts, tm=tm_m)

    tm_e = 256
    pos, free_slots, tile_ea, tile_eb, n_used, n_slots = _plan(cnt_p, cnt_s, meta_p, meta_s, tm=tm_e)
    pos_ext = jnp.concatenate([pos, free_slots], axis=0)
    xsorted = _scatter_rows(text_p, text_s, pos_ext, tb=1024)
    y_sorted = _moe(tile_ea, tile_eb, n_used, xsorted, w1, w3, w2, tm=tm_e, n_tiles=n_slots // tm_e)

    tm_f = 512
    out_p = _final(x1_p, y_sorted, pos[:n_p], mod_p, gf, seq_len=tp, tm=tm_f)
    out_s = _final(x1_s, y_sorted, pos[n_p:], mod_s, gf, seq_len=ts, tm=tm_f)

    y_prompt = out_p.reshape(bp, tp, D_MODEL)
    y_sample = out_s.reshape(bs, ts, D_MODEL)
    kvp = kv_p.reshape(bp, tp, 2 * SWA_KV_W)[:, tp - WINDOW:].reshape(bp, WINDOW, 2, SWA_KV_HEADS, SWA_HEAD_DIM)
    kvs = kv_s.reshape(bs, ts, 2, SWA_KV_HEADS, SWA_HEAD_DIM)
    k_s = jnp.concatenate([cache_swa_k[l].astype(F32), kvs[:, :, 0]], axis=1)[:, -WINDOW:]
    v_s = jnp.concatenate([cache_swa_v[l].astype(F32), kvs[:, :, 1]], axis=1)[:, -WINDOW:]
    return (y_prompt, y_sample, state_p[None], kvp[:, :, 0][None], kvp[:, :, 1][None],
            state_s[None], k_s[None], v_s[None])
```
